```python
import math
import jax, jax.numpy as jnp
from jax import lax
import numpy as np

D_MODEL = 1024
BATCH = 4
SEQ = 4096
DEPTH = 2

D_MIX = D_MODEL
CONV_HEADS = 4
CONV_DIM = 256
CONV_WIDTH = 3
POOL_WINDOWS = (2, 4, 8, 16)
N_POOL_GROUPS = 4
POOL_GROUP = 64
POOL_DIM = N_POOL_GROUPS * POOL_GROUP
ATTN_HEADS = 8
HEAD_DIM = 64
ATTN_DIM = ATTN_HEADS * HEAD_DIM
IDX_HEADS = 8
IDX_DIM = 32
INDEX_TOPK_MAX = 256
Q_BLOCK = 128
IN_SPLITS = (CONV_DIM, CONV_DIM, CONV_DIM, POOL_DIM, ATTN_DIM, ATTN_DIM, ATTN_DIM,
             IDX_HEADS * IDX_DIM, IDX_DIM, IDX_HEADS)
D_IN = 3 * CONV_DIM + POOL_DIM + 3 * ATTN_DIM + IDX_HEADS * IDX_DIM + IDX_DIM + IDX_HEADS
V_COL_OFFSET = 3 * CONV_DIM + POOL_DIM + 2 * ATTN_DIM
N_EXPERTS = 32
TOP_K = 4
D_FF = D_MODEL
SWIGLU_ALPHA = 1.702
SWIGLU_LIMIT = 7.0
MOE_BLOCK = 128
LN_EPS = 1e-5
DN_ALPHA = (2.0 * DEPTH) ** 0.25
DN_BETA = (8.0 * DEPTH) ** -0.25

kernel_name = "hybrid_conv_pool_dsa_moe_deepnorm"


def layernorm(x, g, b):
    xf = x.astype(jnp.float32)
    mu = jnp.mean(xf, axis=-1, keepdims=True)
    var = jnp.mean(jnp.square(xf - mu), axis=-1, keepdims=True)
    return ((xf - mu) * lax.rsqrt(var + LN_EPS) * g + b).astype(x.dtype)


def short_gated_conv(h, gate_b, gate_c, conv_w):
    u = gate_c * h
    y = lax.conv_general_dilated(
        u, conv_w[:, None, :].astype(u.dtype), window_strides=(1,),
        padding=[(CONV_WIDTH - 1, 0)], dimension_numbers=("NWC", "WIO", "NWC"),
        feature_group_count=u.shape[-1])
    return gate_b * y


def multiscale_pool(p, w_pool, pool_scale):
    B, S, _ = p.shape
    pg = p.reshape(B, S, N_POOL_GROUPS, POOL_GROUP)
    pf = pg.astype(jnp.float32)
    cs = jnp.cumsum(pf, axis=1)
    t = jnp.arange(1, S + 1, dtype=jnp.float32)
    means = []
    for g, w in enumerate(POOL_WINDOWS):
        c = cs[:, :, g]
        prev = jnp.pad(c, ((0, 0), (w, 0), (0, 0)))[:, :S]
        cnt = jnp.minimum(t, float(w))[None, :, None]
        means.append((c - prev) / cnt)
    mixed = (jnp.stack(means, axis=2) - pf).astype(p.dtype)
    y = jnp.einsum("bsgc,gcd->bsgd", mixed, w_pool)
    return y.reshape(B, S, POOL_DIM) * pool_scale


def indexed_sparse_attention(q, k, v, q_idx, k_idx, w_idx):
    B, S, H, Dh = q.shape
    n_sel = min(INDEX_TOPK_MAX, S // 4)
    nb = S // Q_BLOCK
    slopes = 2.0 ** (-8.0 * jnp.arange(1, H + 1, dtype=jnp.float32) / H)
    k_idx32 = k_idx.astype(jnp.float32)
    key_pos = jnp.arange(S)
    idx_scale = (IDX_HEADS ** -0.5) * (IDX_DIM ** -0.5)

    def to_blocks(a):
        return jnp.moveaxis(a.reshape((B, nb, Q_BLOCK) + a.shape[2:]), 1, 0)

    def block(args):
        qb, qib, wib, tpos = args
        s = jnp.einsum("bqhd,bsd->bqsh", qib.astype(jnp.float32), k_idx32)
        score = jnp.einsum("bqsh,bqh->bqs", jax.nn.relu(s), wib.astype(jnp.float32) * idx_scale)
        causal = key_pos[None, :] <= tpos[:, None]
        score = jnp.where(causal[None], score, -jnp.inf)
        _, sel = lax.top_k(score, n_sel)
        k_sel = jax.vmap(lambda kb, ib: kb[ib])(k, sel)
        v_sel = jax.vmap(lambda vb, ib: vb[ib])(v, sel)
        logits = jnp.einsum("bqhd,bqkhd->bqhk", qb, k_sel).astype(jnp.float32) * (Dh ** -0.5)
        dist = (tpos[None, :, None] - sel).astype(jnp.float32)
        logits = logits - slopes[None, None, :, None] * dist[:, :, None, :]
        valid = sel <= tpos[None, :, None]
        logits = jnp.where(valid[:, :, None, :], logits, -jnp.inf)
        probs = jax.nn.softmax(logits, axis=-1).astype(v.dtype)
        return jnp.einsum("bqhk,bqkhd->bqhd", probs, v_sel)

    tpos_blocks = jnp.arange(S).reshape(nb, Q_BLOCK)
    out = lax.map(block, (to_blocks(q), to_blocks(q_idx), to_blocks(w_idx), tpos_blocks))
    return jnp.moveaxis(out, 0, 1).reshape(B, S, H * Dh)


def hybrid_mixer(x, w_in, conv_w, w_pool, pool_scale, idx_kn_g, idx_kn_b, w_out):
    B, S, _ = x.shape
    proj = jnp.einsum("bsd,dn->bsn", x, w_in)
    cuts = np.cumsum(IN_SPLITS)[:-1].tolist()
    h, gb, gc, p, q, k, v, qi, ki, wi = jnp.split(proj, cuts, axis=-1)
    y_conv = short_gated_conv(h, gb, gc, conv_w)
    y_pool = multiscale_pool(p, w_pool, pool_scale)
    ki = layernorm(ki, idx_kn_g, idx_kn_b)
    y_attn = indexed_sparse_attention(
        q.reshape(B, S, ATTN_HEADS, HEAD_DIM), k.reshape(B, S, ATTN_HEADS, HEAD_DIM),
        v.reshape(B, S, ATTN_HEADS, HEAD_DIM), qi.reshape(B, S, IDX_HEADS, IDX_DIM), ki, wi)
    cat = jnp.concatenate([y_conv, y_pool, y_attn], axis=-1)
    return jnp.einsum("bsm,md->bsd", cat, w_out)


def moe_ffn(x, router_w, router_b, w_gu, b_gu, w_down, b_down):
    B, S, D = x.shape
    T = B * S
    xt = x.reshape(T, D)
    logits = (xt @ router_w + router_b).astype(jnp.float32)
    top_val, top_idx = lax.top_k(logits, TOP_K)
    gates = jax.nn.softmax(top_val, axis=-1).astype(x.dtype)
    n = T * TOP_K
    flat_e = top_idx.reshape(n)
    order = jnp.argsort(flat_e)
    sorted_e = flat_e[order]
    counts = jnp.bincount(flat_e, length=N_EXPERTS)
    starts = jnp.cumsum(counts) - counts
    padded = (counts + MOE_BLOCK - 1) // MOE_BLOCK * MOE_BLOCK
    pad_ends = jnp.cumsum(padded)
    pad_starts = pad_ends - padded
    dest_sorted = pad_starts[sorted_e] + jnp.arange(n) - starts[sorted_e]
    dest = jnp.zeros((n,), jnp.int32).at[order].set(dest_sorted.astype(jnp.int32))
    n_rows = (n + MOE_BLOCK - 1) // MOE_BLOCK * MOE_BLOCK + N_EXPERTS * MOE_BLOCK
    n_blocks = n_rows // MOE_BLOCK
    row_token = jnp.zeros((n_rows,), jnp.int32).at[dest].set(
        jnp.arange(n, dtype=jnp.int32) // TOP_K)
    block_expert = jnp.clip(
        jnp.searchsorted(pad_ends, jnp.arange(n_blocks) * MOE_BLOCK, side="right"),
        0, N_EXPERTS - 1)
    xs = xt[row_token].reshape(n_blocks, MOE_BLOCK, D)

    def expert_block(args):
        xb, e = args
        hcat = xb @ w_gu[e] + b_gu[e]
        gate, up = jnp.split(hcat, 2, axis=-1)
        gate = jnp.minimum(gate, SWIGLU_LIMIT)
        up = jnp.clip(up, -SWIGLU_LIMIT, SWIGLU_LIMIT)
        act = gate * jax.nn.sigmoid(SWIGLU_ALPHA * gate) * (up + 1.0)
        return act @ w_down[e] + b_down[e]

    ys = lax.map(expert_block, (xs, block_expert)).reshape(n_rows, D)
    y = ys[dest].reshape(T, TOP_K, D)
    return jnp.einsum("tk,tkd->td", gates, y).reshape(B, S, D)


def setup_inputs(seed: int = 0) -> dict:
    key = jax.random.key(seed)
    ks = jax.random.split(key, 20)
    f32 = jnp.float32

    def nrm(k, shape, scale):
        return jax.random.normal(k, shape, f32) * scale

    x = nrm(ks[0], (BATCH, SEQ, D_MODEL), 1.0)
    col_scale = jnp.ones((D_IN,), f32).at[V_COL_OFFSET:V_COL_OFFSET + ATTN_DIM].set(DN_BETA)
    w_in = nrm(ks[1], (DEPTH, D_MODEL, D_IN), D_MODEL ** -0.5) * col_scale
    conv_w = nrm(ks[2], (DEPTH, CONV_WIDTH, CONV_DIM), CONV_WIDTH ** -0.5)
    w_pool = nrm(ks[3], (DEPTH, N_POOL_GROUPS, POOL_GROUP, POOL_GROUP), POOL_GROUP ** -0.5)
    pool_scale = 1.0 + nrm(ks[4], (DEPTH, POOL_DIM), 0.02)
    idx_kn_g = 1.0 + nrm(ks[5], (DEPTH, IDX_DIM), 0.02)
    idx_kn_b = nrm(ks[6], (DEPTH, IDX_DIM), 0.02)
    w_out = nrm(ks[7], (DEPTH, D_MIX, D_MODEL), D_MIX ** -0.5) * DN_BETA
    ln1_g = 1.0 + nrm(ks[8], (DEPTH, D_MODEL), 0.02)
    ln1_b = nrm(ks[9], (DEPTH, D_MODEL), 0.02)
    router_w = nrm(ks[10], (DEPTH, D_MODEL, N_EXPERTS), D_MODEL ** -0.5)
    router_b = nrm(ks[11], (DEPTH, N_EXPERTS), 0.01)
    w_gu = nrm(ks[12], (DEPTH, N_EXPERTS, D_MODEL, 2 * D_FF), D_MODEL ** -0.5)
    b_gu = nrm(ks[13], (DEPTH, N_EXPERTS, 2 * D_FF), 0.01)
    w_down = nrm(ks[14], (DEPTH, N_EXPERTS, D_FF, D_MODEL), D_FF ** -0.5) * DN_BETA
    b_down = nrm(ks[15], (DEPTH, N_EXPERTS, D_MODEL), 0.01)
    ln2_g = 1.0 + nrm(ks[16], (DEPTH, D_MODEL), 0.02)
    ln2_b = nrm(ks[17], (DEPTH, D_MODEL), 0.02)
    return {"x": x, "w_in": w_in, "conv_w": conv_w, "w_pool": w_pool,
            "pool_scale": pool_scale, "idx_kn_g": idx_kn_g, "idx_kn_b": idx_kn_b,
            "w_out": w_out, "ln1_g": ln1_g, "ln1_b": ln1_b, "router_w": router_w,
            "router_b": router_b, "w_gu": w_gu, "b_gu": b_gu, "w_down": w_down,
            "b_down": b_down, "ln2_g": ln2_g, "ln2_b": ln2_b}


def reference(x, w_in, conv_w, w_pool, pool_scale, idx_kn_g, idx_kn_b, w_out,
              ln1_g, ln1_b, router_w, router_b, w_gu, b_gu, w_down, b_down,
              ln2_g, ln2_b):
    for l in range(DEPTH):
        mix = hybrid_mixer(x, w_in[l], conv_w[l], w_pool[l], pool_scale[l],
                           idx_kn_g[l], idx_kn_b[l], w_out[l])
        x = layernorm(DN_ALPHA * x + mix, ln1_g[l], ln1_b[l])
        ffn = moe_ffn(x, router_w[l], router_b[l], w_gu[l], b_gu[l], w_down[l], b_down[l])
        x = layernorm(DN_ALPHA * x + ffn, ln2_g[l], ln2_b[l])
    return x
```

```python
import functools

import jax
import jax.numpy as jnp
import numpy as np
from jax import lax
from jax.experimental import pallas as pl
from jax.experimental.pallas import tpu as pltpu

F32 = jnp.float32
BF16 = jnp.bfloat16
I32 = jnp.int32

D_MODEL = 1024
CONV_DIM = 256
CONV_WIDTH = 3
POOL_WINDOWS = (2, 4, 8, 16)
POOL_GROUP = 64
POOL_DIM = 256
ATTN_HEADS = 8
HEAD_DIM = 64
ATTN_DIM = 512
IDX_HEADS = 8
IDX_DIM = 32
INDEX_TOPK_MAX = 256
Q_BLOCK = 128
N_EXPERTS = 32
TOP_K = 4
D_FF = 1024
SWIGLU_ALPHA = 1.702
SWIGLU_LIMIT = 7.0
LN_EPS = 1e-5
DEPTH = 2
DN_ALPHA = (2.0 * DEPTH) ** 0.25

OFF_P = 3 * CONV_DIM
OFF_Q = OFF_P + POOL_DIM
OFF_K = OFF_Q + ATTN_DIM
OFF_V = OFF_K + ATTN_DIM
OFF_QI = OFF_V + ATTN_DIM
OFF_KI = OFF_QI + IDX_HEADS * IDX_DIM
OFF_WI = OFF_KI + IDX_DIM
D_IN = OFF_WI + IDX_HEADS

LANES = 128
SUBLANES = 8
VMEM_LIMIT = 56 * 1024 * 1024
KEY_CHUNK = 256
HALO = 16
TM_PROJ = 512
TM_MIX = 256
TM_ROUTE = 512
TM_MOE = 256
TM_COMB = 256
IDXT_ROWS = IDX_HEADS * IDX_DIM + 16
WT_ROWS = ATTN_DIM + IDXT_ROWS
INT_MIN = -(2 ** 31)
NEG_BIG = -1e30


def _cparams(sem):
    return pltpu.CompilerParams(dimension_semantics=sem, vmem_limit_bytes=VMEM_LIMIT)


def _log2(n):
    k = n.bit_length() - 1
    assert 1 << k == n
    return k


def _dot(a, b):
    return jnp.dot(a, b, preferred_element_type=F32)


def _dot_nt(a, b):
    return lax.dot_general(a, b, (((1,), (1,)), ((), ())), preferred_element_type=F32)


def _layernorm_rows(z, g, b):
    mu = jnp.mean(z, axis=-1, keepdims=True)
    d = z - mu
    var = jnp.mean(d * d, axis=-1, keepdims=True)
    return d * lax.rsqrt(var + LN_EPS) * g + b


def _inproj_kernel(x_ref, wn_ref, wt_ref, kg_ref, kb_ref,
                   a_ref, q_ref, k_ref, ki_ref, vt_ref, it_ref):
    x = x_ref[...]
    a_ref[...] = _dot(x, wn_ref[:, 0:OFF_Q])
    q_ref[...] = (_dot(x, wn_ref[:, OFF_Q:OFF_K]) * (HEAD_DIM ** -0.5)).astype(BF16)
    k_ref[...] = _dot(x, wn_ref[:, OFF_K:OFF_V]).astype(BF16)
    ki = _dot(x, wn_ref[:, OFF_V:OFF_V + LANES])[:, 0:IDX_DIM]
    ki_ref[...] = _layernorm_rows(ki, kg_ref[...], kb_ref[...])
    t = _dot_nt(wt_ref[...], x)
    for c in range(TM_PROJ // KEY_CHUNK):
        vt_ref[c] = t[0:ATTN_DIM, c * KEY_CHUNK:(c + 1) * KEY_CHUNK].astype(BF16)
    it_ref[...] = t[ATTN_DIM:WT_ROWS, :]


def _inproj(xb, wn, wt, kg, kb):
    T = xb.shape[0]
    nt = T // TM_PROJ
    cpt = TM_PROJ // KEY_CHUNK
    full = lambda i: (0, 0)
    return pl.pallas_call(
        _inproj_kernel,
        grid=(nt,),
        in_specs=[
            pl.BlockSpec((TM_PROJ, D_MODEL), lambda i: (i, 0)),
            pl.BlockSpec(wn.shape, full),
            pl.BlockSpec(wt.shape, full),
            pl.BlockSpec((1, IDX_DIM), full),
            pl.BlockSpec((1, IDX_DIM), full),
        ],
        out_specs=[
            pl.BlockSpec((TM_PROJ, OFF_Q), lambda i: (i, 0)),
            pl.BlockSpec((TM_PROJ, ATTN_DIM), lambda i: (i, 0)),
            pl.BlockSpec((TM_PROJ, ATTN_DIM), lambda i: (i, 0)),
            pl.BlockSpec((TM_PROJ, IDX_DIM), lambda i: (i, 0)),
            pl.BlockSpec((cpt, ATTN_DIM, KEY_CHUNK), lambda i: (i, 0, 0)),
            pl.BlockSpec((IDXT_ROWS, TM_PROJ), lambda i: (0, i)),
        ],
        out_shape=[
            jax.ShapeDtypeStruct((T, OFF_Q), F32),
            jax.ShapeDtypeStruct((T, ATTN_DIM), BF16),
            jax.ShapeDtypeStruct((T, ATTN_DIM), BF16),
            jax.ShapeDtypeStruct((T, IDX_DIM), F32),
            jax.ShapeDtypeStruct((T // KEY_CHUNK, ATTN_DIM, KEY_CHUNK), BF16),
            jax.ShapeDtypeStruct((IDXT_ROWS, T), F32),
        ],
        compiler_params=_cparams(("arbitrary",)),
        name="inproj",
    )(xb, wn, wt, kg, kb)


def _attn_kernel(n_sel, it_ref, ki_ref, q_ref, k_ref, vt_ref, o_ref,
                 keys_ref, rhs_ref, acc_ref, m_ref, l_ref):
    i = pl.program_id(1)
    nch = lax.shift_right_logical(i * Q_BLOCK + Q_BLOCK + KEY_CHUNK - 1, _log2(KEY_CHUNK))
    q0 = i * Q_BLOCK
    KC = KEY_CHUNK
    row = lax.broadcasted_iota(I32, (KC, Q_BLOCK), 0)
    lane = lax.broadcasted_iota(I32, (KC, Q_BLOCK), 1)
    qpos = q0 + lane

    qcat = jnp.concatenate(
        [it_ref[h * IDX_DIM:(h + 1) * IDX_DIM, :] for h in range(IDX_HEADS)], axis=1).astype(BF16)
    w_all = it_ref[IDX_HEADS * IDX_DIM:IDX_HEADS * IDX_DIM + IDX_HEADS, :] * (
        (IDX_HEADS ** -0.5) * (IDX_DIM ** -0.5))

    def score_body(c, carry):
        base = pl.multiple_of(c * KC, KC)
        kic = ki_ref[pl.ds(base, KC), :].astype(BF16)
        s = _dot(kic, qcat)
        sc = jnp.zeros((KC, Q_BLOCK), F32)
        for h in range(IDX_HEADS):
            sc = sc + jnp.maximum(s[:, h * Q_BLOCK:(h + 1) * Q_BLOCK], 0.0) * w_all[h:h + 1, :]
        bits = lax.bitcast_convert_type(sc, I32)
        key = bits ^ ((bits >> 31) & jnp.int32(0x7FFFFFFF))
        key = jnp.where(bits == jnp.int32(INT_MIN), jnp.int32(0), key)
        key = jnp.where(base + row <= qpos, key, jnp.int32(INT_MIN))
        keys_ref[pl.ds(base, KC), :] = key
        return carry

    lax.fori_loop(0, nch, score_body, 0)

    def count_ge(cand):
        def body(c, acc):
            base = pl.multiple_of(c * KC, KC)
            blk = keys_ref[pl.ds(base, KC), :]
            hit = jnp.where(blk >= cand, 1.0, 0.0).astype(F32)
            return acc + hit.reshape(KC // SUBLANES, SUBLANES, Q_BLOCK).sum(axis=0)
        acc = lax.fori_loop(0, nch, body, jnp.zeros((SUBLANES, Q_BLOCK), F32))
        return jnp.sum(acc, axis=0, keepdims=True)

    want = jnp.float32(n_sel)
    zero_row = jnp.zeros((1, Q_BLOCK), I32)
    lo = jnp.where(count_ge(zero_row) >= want, zero_row, jnp.int32(INT_MIN))

    def bit_body(it, lo):
        cand = lo + lax.shift_left(jnp.int32(1), jnp.int32(30) - it)
        return jnp.where(count_ge(cand) >= want, cand, lo)

    lo = lax.fori_loop(0, 31, bit_body, lo)
    tau = jnp.maximum(lo, jnp.int32(INT_MIN + 1))

    q = q_ref[...]
    lane_q = lax.broadcasted_iota(I32, (Q_BLOCK, 2 * HEAD_DIM), 1)
    for p in range(ATTN_HEADS // 2):
        qp = q[:, p * 2 * HEAD_DIM:(p + 1) * 2 * HEAD_DIM]
        rhs_ref[p, 0:Q_BLOCK, :] = jnp.where(lane_q < HEAD_DIM, qp, jnp.zeros_like(qp))
        rhs_ref[p, Q_BLOCK:2 * Q_BLOCK, :] = jnp.where(lane_q >= HEAD_DIM, qp, jnp.zeros_like(qp))
    acc_ref[...] = jnp.zeros_like(acc_ref)
    m_ref[...] = jnp.full_like(m_ref, NEG_BIG)
    l_ref[...] = jnp.zeros_like(l_ref)

    def attn_body(c, carry):
        base = pl.multiple_of(c * KC, KC)
        sel = keys_ref[pl.ds(base, KC), :] >= tau
        tk = (base + row - (q0 + Q_BLOCK - 1)).astype(F32)
        kc = k_ref[pl.ds(base, KC), :]
        for p in range(ATTN_HEADS // 2):
            l2 = _dot_nt(kc[:, p * 2 * HEAD_DIM:(p + 1) * 2 * HEAD_DIM], rhs_ref[p])
            for hh in range(2):
                h = 2 * p + hh
                slope = 2.0 ** (-8.0 * (h + 1) / ATTN_HEADS)
                lg = l2[:, hh * Q_BLOCK:(hh + 1) * Q_BLOCK] + slope * tk
                lg = jnp.where(sel, lg, NEG_BIG)
                m_old = m_ref[h:h + 1, :]
                m_new = jnp.maximum(m_old, jnp.max(lg, axis=0, keepdims=True))
                alpha = jnp.exp(m_old - m_new)
                pr = jnp.exp(lg - m_new)
                l_ref[h:h + 1, :] = l_ref[h:h + 1, :] * alpha + jnp.sum(pr, axis=0, keepdims=True)
                pv = _dot(vt_ref[c, h * HEAD_DIM:(h + 1) * HEAD_DIM, :], pr.astype(BF16))
                acc_ref[h * HEAD_DIM:(h + 1) * HEAD_DIM, :] = (
                    acc_ref[h * HEAD_DIM:(h + 1) * HEAD_DIM, :] * alpha + pv)
                m_ref[h:h + 1, :] = m_new
        return carry

    lax.fori_loop(0, nch, attn_body, 0)

    outs = []
    for h in range(ATTN_HEADS):
        outs.append(acc_ref[h * HEAD_DIM:(h + 1) * HEAD_DIM, :] / l_ref[h:h + 1, :])
    o_ref[...] = jnp.concatenate(outs, axis=0).T


def _attn(it, ki, q, k, vt, B, S):
    T = B * S
    nb = S // Q_BLOCK
    n_sel = min(INDEX_TOPK_MAX, S // 4)
    return pl.pallas_call(
        functools.partial(_attn_kernel, n_sel),
        grid=(B, nb),
        in_specs=[
            pl.BlockSpec((IDXT_ROWS, Q_BLOCK), lambda b, i: (0, b * nb + i)),
            pl.BlockSpec((S, IDX_DIM), lambda b, i: (b, 0)),
            pl.BlockSpec((Q_BLOCK, ATTN_DIM), lambda b, i: (b * nb + i, 0)),
            pl.BlockSpec((S, ATTN_DIM), lambda b, i: (b, 0)),
            pl.BlockSpec((S // KEY_CHUNK, ATTN_DIM, KEY_CHUNK), lambda b, i: (b, 0, 0)),
        ],
        out_specs=pl.BlockSpec((Q_BLOCK, ATTN_DIM), lambda b, i: (b * nb + i, 0)),
        out_shape=jax.ShapeDtypeStruct((T, ATTN_DIM), F32),
        scratch_shapes=[
            pltpu.VMEM((S, Q_BLOCK), I32),
            pltpu.VMEM((ATTN_HEADS // 2, 2 * Q_BLOCK, 2 * HEAD_DIM), BF16),
            pltpu.VMEM((ATTN_DIM, Q_BLOCK), F32),
            pltpu.VMEM((ATTN_HEADS, Q_BLOCK), F32),
            pltpu.VMEM((ATTN_HEADS, Q_BLOCK), F32),
        ],
        compiler_params=_cparams(("arbitrary", "arbitrary")),
        name="attn",
    )(it, ki, q, k, vt)


def _mixout_kernel(tiles_per_seq, a_ref, halo_ref, y_ref, x_ref, cw_ref, wpool_ref, ps_ref,
                   wout_ref, g_ref, b_ref, o_ref, ext_ref):
    i = pl.program_id(0)
    tm = TM_MIX
    first = lax.rem(i, tiles_per_seq) == 0
    halo = jnp.where(first, 0.0, halo_ref[...])
    a = a_ref[...]
    h_c, gb_c, gc_c, p_c = (a[:, j * CONV_DIM:(j + 1) * CONV_DIM] for j in range(4))

    t0 = 2 * HALO
    n_ext = t0 + tm
    ext_ref[0:HALO, :] = jnp.zeros((HALO, CONV_DIM), F32)

    ext_ref[HALO:t0, :] = halo[:, 2 * CONV_DIM:3 * CONV_DIM] * halo[:, 0:CONV_DIM]
    u = gc_c * h_c
    ext_ref[t0:n_ext, :] = u
    cw = cw_ref[...]
    conv = cw[2:3, :] * u
    conv = conv + cw[1:2, :] * ext_ref[t0 - 1:n_ext - 1, :]
    conv = conv + cw[0:1, :] * ext_ref[t0 - 2:n_ext - 2, :]
    y_conv = gb_c * conv

    ext_ref[HALO:t0, :] = halo[:, OFF_P:OFF_P + POOL_DIM]
    ext_ref[t0:n_ext, :] = p_c
    sums = {}
    step = 1
    while step < POOL_WINDOWS[-1]:
        cur = ext_ref[HALO:n_ext, :] + ext_ref[HALO - step:n_ext - step, :]
        ext_ref[HALO:n_ext, :] = cur
        step *= 2
        sums[step] = ext_ref[t0:n_ext, :]
    tpos = (lax.rem(i, tiles_per_seq) * tm + lax.broadcasted_iota(I32, (tm, POOL_DIM), 0) + 1).astype(F32)
    grp = lax.shift_right_logical(lax.broadcasted_iota(I32, (tm, POOL_DIM), 1), _log2(POOL_GROUP))
    mean = jnp.zeros((tm, POOL_DIM), F32)
    for gi, w in enumerate(POOL_WINDOWS):
        mean = jnp.where(grp == gi, sums[w] / jnp.minimum(tpos, float(w)), mean)
    mixed = mean - p_c
    y_pool = _dot(mixed.astype(BF16), wpool_ref[...]) * ps_ref[...]

    mix = _dot(y_conv.astype(BF16), wout_ref[0:CONV_DIM, :])
    mix = mix + _dot(y_pool.astype(BF16), wout_ref[CONV_DIM:CONV_DIM + POOL_DIM, :])
    mix = mix + _dot(y_ref[...].astype(BF16), wout_ref[CONV_DIM + POOL_DIM:D_MODEL, :])
    z = DN_ALPHA * x_ref[...] + mix
    o_ref[...] = _layernorm_rows(z, g_ref[...], b_ref[...])


def _mixout(a, y_attn, x, cw, wpool_bd, ps, wout, g, b, S):
    T = a.shape[0]
    nt = T // TM_MIX
    tps = S // TM_MIX
    hb = TM_MIX // HALO
    full = lambda i: (0, 0)
    return pl.pallas_call(
        functools.partial(_mixout_kernel, tps),
        grid=(nt,),
        in_specs=[
            pl.BlockSpec((TM_MIX, OFF_Q), lambda i: (i, 0)),
            pl.BlockSpec((HALO, OFF_Q), lambda i: (jnp.maximum(i * hb - 1, 0), 0)),
            pl.BlockSpec((TM_MIX, ATTN_DIM), lambda i: (i, 0)),
            pl.BlockSpec((TM_MIX, D_MODEL), lambda i: (i, 0)),
            pl.BlockSpec((CONV_WIDTH, CONV_DIM), full),
            pl.BlockSpec((POOL_DIM, POOL_DIM), full),
            pl.BlockSpec((1, POOL_DIM), full),
            pl.BlockSpec((D_MODEL, D_MODEL), full),
            pl.BlockSpec((1, D_MODEL), full),
            pl.BlockSpec((1, D_MODEL), full),
        ],
        out_specs=pl.BlockSpec((TM_MIX, D_MODEL), lambda i: (i, 0)),
        out_shape=jax.ShapeDtypeStruct((T, D_MODEL), F32),
        scratch_shapes=[pltpu.VMEM((2 * HALO + TM_MIX, CONV_DIM), F32)],
        compiler_params=_cparams(("arbitrary",)),
        name="mixout",
    )(a, a, y_attn, x, cw, wpool_bd, ps, wout, g, b)


def _router_kernel(x_ref, rw_ref, rb_ref, idx_ref, gate_ref, cnt_ref):
    i = pl.program_id(0)
    tm = TM_ROUTE
    logits = lax.dot_general(rw_ref[...], x_ref[...], (((1,), (1,)), ((), ())),
                             precision=lax.Precision.HIGHEST,
                             preferred_element_type=F32) + rb_ref[...]
    erow = lax.broadcasted_iota(I32, (N_EXPERTS, tm), 0).astype(F32)
    work = logits
    vals, idxs = [], []
    multi = jnp.zeros((N_EXPERTS, tm), F32)
    for _ in range(TOP_K):
        mx = jnp.max(work, axis=0, keepdims=True)
        pick = jnp.min(jnp.where(work == mx, erow, float(N_EXPERTS)), axis=0, keepdims=True)
        hit = erow == pick
        work = jnp.where(hit, -jnp.inf, work)
        multi = multi + hit.astype(F32)
        vals.append(mx)
        idxs.append(pick)
    es = [jnp.exp(v - vals[0]) for v in vals]
    den = es[0] + es[1] + es[2] + es[3]
    idx_ref[...] = jnp.concatenate(idxs, axis=0).astype(I32)
    gate_ref[...] = jnp.concatenate([e / den for e in es] + [jnp.zeros((SUBLANES - TOP_K, tm), F32)], axis=0)
    part = multi[:, 0:LANES]
    for j in range(1, tm // LANES):
        part = part + multi[:, j * LANES:(j + 1) * LANES]

    @pl.when(i == 0)
    def _():
        cnt_ref[...] = jnp.zeros_like(cnt_ref)

    cnt_ref[...] += part


def _router(x1, rwT, rb):
    T = x1.shape[0]
    nt = T // TM_ROUTE
    full = lambda i: (0, 0)
    return pl.pallas_call(
        _router_kernel,
        grid=(nt,),
        in_specs=[
            pl.BlockSpec((TM_ROUTE, D_MODEL), lambda i: (i, 0)),
            pl.BlockSpec((N_EXPERTS, D_MODEL), full),
            pl.BlockSpec((N_EXPERTS, 1), full),
        ],
        out_specs=[
            pl.BlockSpec((TOP_K, TM_ROUTE), lambda i: (0, i)),
            pl.BlockSpec((SUBLANES, TM_ROUTE), lambda i: (0, i)),
            pl.BlockSpec((N_EXPERTS, LANES), full),
        ],
        out_shape=[
            jax.ShapeDtypeStruct((TOP_K, T), I32),
            jax.ShapeDtypeStruct((SUBLANES, T), F32),
            jax.ShapeDtypeStruct((N_EXPERTS, LANES), F32),
        ],
        compiler_params=_cparams(("arbitrary",)),
        name="router",
    )(x1, rwT, rb)


def _ranks_kernel(n_tiles_pad, idx_ref, cnt_ref, dest_ref, meta_ref, tri_ref, start_ref, carry_ref):
    i = pl.program_id(0)
    tm = TM_ROUTE
    erow = lax.broadcasted_iota(I32, (N_EXPERTS, LANES), 0)
    elane = lax.broadcasted_iota(I32, (N_EXPERTS, LANES), 1)

    @pl.when(i == 0)
    def _():
        cnt = jnp.sum(cnt_ref[...], axis=1, keepdims=True)
        cnt_i = jnp.broadcast_to(cnt, (N_EXPERTS, LANES)).astype(I32)
        padded = lax.shift_left(lax.shift_right_logical(cnt_i + (TM_MOE - 1), _log2(TM_MOE)), _log2(TM_MOE))
        r = lax.broadcasted_iota(I32, (N_EXPERTS, N_EXPERTS), 0)
        c = lax.broadcasted_iota(I32, (N_EXPERTS, N_EXPERTS), 1)
        low = (c <= r).astype(F32)
        pad_end = lax.dot_general(low, padded.astype(F32), (((1,), (0,)), ((), ())),
                                  precision=lax.Precision.HIGHEST, preferred_element_type=F32)
        pad_start = pad_end - padded.astype(F32)
        start_ref[...] = pad_start
        carry_ref[...] = jnp.zeros_like(carry_ref)
        a = lax.broadcasted_iota(I32, (tm, tm), 0)
        bcol = lax.broadcasted_iota(I32, (tm, tm), 1)
        tri_ref[...] = (a < bcol).astype(BF16)
        ntp = n_tiles_pad
        tstart = (lax.broadcasted_iota(I32, (N_EXPERTS, ntp), 1) * TM_MOE).astype(F32)
        pe = jnp.concatenate([pad_end] * (ntp // LANES), axis=1)
        texp = jnp.sum((pe <= tstart).astype(F32), axis=0, keepdims=True)
        texp = jnp.minimum(texp, float(N_EXPERTS - 1)).astype(I32)
        total = jnp.max(pad_end, axis=0, keepdims=True)
        n_used = lax.shift_right_logical(total.astype(I32), _log2(TM_MOE))
        zstart = jnp.sum(jnp.where(erow == elane, pad_start + cnt_i.astype(F32), 0.0),
                         axis=0, keepdims=True).astype(I32)
        zlen = jnp.sum(jnp.where(erow == elane, (padded - cnt_i).astype(F32), 0.0),
                       axis=0, keepdims=True).astype(I32)
        lanes_pad = jnp.zeros((1, ntp - LANES), I32)
        meta_ref[...] = jnp.concatenate(
            [texp,
             jnp.concatenate([n_used, lanes_pad], axis=1),
             jnp.concatenate([zstart, lanes_pad], axis=1),
             jnp.concatenate([zlen, lanes_pad], axis=1),
             jnp.zeros((SUBLANES - 4, ntp), I32)], axis=0)

    idx = idx_ref[...]
    erow_t = lax.broadcasted_iota(I32, (N_EXPERTS, tm), 0)
    hits = [erow_t == idx[k:k + 1, :] for k in range(TOP_K)]
    multi = hits[0].astype(F32)
    for k in range(1, TOP_K):
        multi = multi + hits[k].astype(F32)
    prefix = _dot(multi.astype(BF16), tri_ref[...])
    base = jnp.concatenate([carry_ref[...] + start_ref[...]] * (tm // LANES), axis=1)
    tot = prefix + base
    dest_ref[...] = jnp.concatenate(
        [jnp.sum(jnp.where(hits[k], tot, 0.0), axis=0, keepdims=True) for k in range(TOP_K)],
        axis=0).astype(I32)
    carry_ref[...] += jnp.broadcast_to(jnp.sum(multi, axis=1, keepdims=True), (N_EXPERTS, LANES))


def _ranks(idxT, cnt, n_tiles_pad):
    T = idxT.shape[1]
    nt = T // TM_ROUTE
    full = lambda i: (0, 0)
    return pl.pallas_call(
        functools.partial(_ranks_kernel, n_tiles_pad),
        grid=(nt,),
        in_specs=[
            pl.BlockSpec((TOP_K, TM_ROUTE), lambda i: (0, i)),
            pl.BlockSpec((N_EXPERTS, LANES), full),
        ],
        out_specs=[
            pl.BlockSpec((TOP_K, TM_ROUTE), lambda i: (0, i)),
            pl.BlockSpec((SUBLANES, n_tiles_pad), full),
        ],
        out_shape=[
            jax.ShapeDtypeStruct((TOP_K, T), I32),
            jax.ShapeDtypeStruct((SUBLANES, n_tiles_pad), I32),
        ],
        scratch_shapes=[
            pltpu.VMEM((TM_ROUTE, TM_ROUTE), BF16),
            pltpu.VMEM((N_EXPERTS, LANES), F32),
            pltpu.VMEM((N_EXPERTS, LANES), F32),
        ],
        compiler_params=_cparams(("arbitrary",)),
        name="ranks",
    )(idxT, cnt)


RT = D_MODEL // LANES


def _row_copy(src_ref, s, dst_ref, d, sem):
    return pltpu.make_async_copy(src_ref.at[pl.ds(pl.multiple_of(s * RT, RT), RT)],
                                 dst_ref.at[pl.ds(pl.multiple_of(d * RT, RT), RT)], sem)


def _to_row_tiled(dst_ref, val, rows):
    for s in range(RT):
        dst_ref[pl.ds(s, rows, stride=RT), :] = val[:, s * LANES:(s + 1) * LANES]


def _from_row_tiled(src_ref, rows):
    return [src_ref[pl.ds(s, rows, stride=RT), :] for s in range(RT)]


def _scatter_kernel(meta_ref, x_ref, dest_ref, xs_ref, stage_ref, zero_ref, sem):
    i = pl.program_id(0)
    tm = TM_ROUTE

    @pl.when(i == 0)
    def _():
        zero_ref[...] = jnp.zeros_like(zero_ref)

        def zcopy(row, nrows):
            off = pl.multiple_of(row * RT, RT)
            return pltpu.make_async_copy(zero_ref.at[pl.ds(0, nrows * RT)],
                                         xs_ref.at[pl.ds(off, nrows * RT)], sem)

        def pad_fill(wait):
            def body(e, c):
                row = meta_ref[2, e]
                plen = meta_ref[3, e]
                for bit in reversed(range(_log2(TM_MOE))):
                    size = 1 << bit
                    has = (plen & size) != 0

                    @pl.when(has)
                    def _():
                        cp = zcopy(row, size)
                        cp.wait() if wait else cp.start()

                    row = row + jnp.where(has, size, 0)
                return c
            lax.fori_loop(0, N_EXPERTS, body, 0)

        def tail_fill(wait):
            def body(j, c):
                cp = zcopy(j * TM_MOE, TM_MOE)
                cp.wait() if wait else cp.start()
                return c
            lax.fori_loop(meta_ref[1, 0], xs_ref.shape[0] // (TM_MOE * RT), body, 0)

        pad_fill(False)
        tail_fill(False)
        pad_fill(True)
        tail_fill(True)

    _to_row_tiled(stage_ref, x_ref[...], tm)

    def start(t, c):
        for k in range(TOP_K):
            _row_copy(stage_ref, t, xs_ref, dest_ref[k, t], sem).start()
        return c

    def wait(t, c):
        for k in range(TOP_K):
            _row_copy(stage_ref, t, xs_ref, dest_ref[k, t], sem).wait()
        return c

    lax.fori_loop(0, tm, start, 0)
    lax.fori_loop(0, tm, wait, 0)


def _scatter(meta, x1, destT, n_rows_alloc):
    T = x1.shape[0]
    nt = T // TM_ROUTE
    return pl.pallas_call(
        _scatter_kernel,
        grid_spec=pltpu.PrefetchScalarGridSpec(
            num_scalar_prefetch=1,
            grid=(nt,),
            in_specs=[
                pl.BlockSpec((TM_ROUTE, D_MODEL), lambda i, m: (i, 0)),
                pl.BlockSpec((TOP_K, TM_ROUTE), lambda i, m: (0, i), memory_space=pltpu.SMEM),
            ],
            out_specs=pl.BlockSpec(memory_space=pl.ANY),
            scratch_shapes=[
                pltpu.VMEM((TM_ROUTE * RT, LANES), F32),
                pltpu.VMEM((TM_MOE * RT, LANES), F32),
                pltpu.SemaphoreType.DMA(()),
            ],
        ),
        out_shape=jax.ShapeDtypeStruct((n_rows_alloc * RT, LANES), F32),
        compiler_params=_cparams(("arbitrary",)),
        name="scatter",
    )(meta, x1, destT)


def _gmm_kernel(meta_ref, xs_ref, wgu_ref, bgu_ref, wd_ref, bd_ref, ys_ref, wgu_b, wd_b, lhs_ref, act_ref):
    j = pl.program_id(0)
    n_used = meta_ref[1, 0]
    e_now = meta_ref[0, j]
    e_prev = meta_ref[0, jnp.maximum(j - 1, 0)]
    used = j < n_used

    @pl.when(used & ((j == 0) | (e_now != e_prev)))
    def _():
        wgu_b[...] = wgu_ref[0].astype(BF16)
        wd_b[...] = wd_ref[0].astype(BF16)

    @pl.when(used)
    def _():
        for s, piece in enumerate(_from_row_tiled(xs_ref, TM_MOE)):
            lhs_ref[:, s * LANES:(s + 1) * LANES] = piece.astype(BF16)
        x = lhs_ref[...]
        nc = 256
        for c in range(D_FF // nc):
            gate = _dot(x, wgu_b[:, c * nc:(c + 1) * nc]) + bgu_ref[0, :, c * nc:(c + 1) * nc]
            up = _dot(x, wgu_b[:, D_FF + c * nc:D_FF + (c + 1) * nc]) + bgu_ref[0, :, D_FF + c * nc:D_FF + (c + 1) * nc]
            gate = jnp.minimum(gate, SWIGLU_LIMIT)
            up = jnp.clip(up, -SWIGLU_LIMIT, SWIGLU_LIMIT)
            act = gate * jax.nn.sigmoid(SWIGLU_ALPHA * gate) * (up + 1.0)
            act_ref[:, c * nc:(c + 1) * nc] = act.astype(BF16)
        _to_row_tiled(ys_ref, _dot(act_ref[...], wd_b[...]) + bd_ref[0], TM_MOE)

    @pl.when(jnp.logical_not(used))
    def _():
        ys_ref[...] = jnp.zeros_like(ys_ref)


def _gmm(meta, xs, w_gu, b_gu, w_down, b_down, n_tiles):
    last = lambda m: jnp.maximum(m[1, 0] - 1, 0)
    return pl.pallas_call(
        _gmm_kernel,
        grid_spec=pltpu.PrefetchScalarGridSpec(
            num_scalar_prefetch=1,
            grid=(n_tiles,),
            in_specs=[
                pl.BlockSpec((TM_MOE * RT, LANES), lambda j, m: (jnp.minimum(j, last(m)), 0)),
                pl.BlockSpec((1, D_MODEL, 2 * D_FF), lambda j, m: (m[0, j], 0, 0)),
                pl.BlockSpec((1, 1, 2 * D_FF), lambda j, m: (m[0, j], 0, 0)),
                pl.BlockSpec((1, D_FF, D_MODEL), lambda j, m: (m[0, j], 0, 0)),
                pl.BlockSpec((1, 1, D_MODEL), lambda j, m: (m[0, j], 0, 0)),
            ],
            out_specs=pl.BlockSpec((TM_MOE * RT, LANES), lambda j, m: (j, 0)),
            scratch_shapes=[
                pltpu.VMEM((D_MODEL, 2 * D_FF), BF16),
                pltpu.VMEM((D_FF, D_MODEL), BF16),
                pltpu.VMEM((TM_MOE, D_MODEL), BF16),
                pltpu.VMEM((TM_MOE, D_FF), BF16),
            ],
        ),
        out_shape=jax.ShapeDtypeStruct((n_tiles * TM_MOE * RT, LANES), F32),
        compiler_params=_cparams(("arbitrary",)),
        name="gmm",
    )(meta, xs, w_gu, b_gu, w_down, b_down)


def _combine_kernel(x_ref, dest_ref, gate_ref, ys_ref, g_ref, b_ref, o_ref, ob_ref, buf_ref, sem):
    tm = TM_COMB

    def start(t, c):
        for k in range(TOP_K):
            _row_copy(ys_ref, dest_ref[k, t], buf_ref.at[k], t, sem).start()
        return c

    def wait(t, c):
        for k in range(TOP_K):
            _row_copy(ys_ref, dest_ref[k, t], buf_ref.at[k], t, sem).wait()
        return c

    lax.fori_loop(0, tm, start, 0)
    lax.fori_loop(0, tm, wait, 0)
    gates = jnp.concatenate(
        [gate_ref[...], jnp.zeros((LANES - SUBLANES, tm), F32)], axis=0).T
    pieces = None
    for k in range(TOP_K):
        gk = gates[:, k:k + 1]
        rows = [p * gk for p in _from_row_tiled(buf_ref.at[k], tm)]
        pieces = rows if pieces is None else [a + r for a, r in zip(pieces, rows)]
    z = DN_ALPHA * x_ref[...] + jnp.concatenate(pieces, axis=1)
    out = _layernorm_rows(z, g_ref[...], b_ref[...])
    o_ref[...] = out
    ob_ref[...] = out.astype(BF16)


def _combine(x1, destT, gateT, ys, g, b):
    T = x1.shape[0]
    nt = T // TM_COMB
    full = lambda i: (0, 0)
    return pl.pallas_call(
        _combine_kernel,
        grid=(nt,),
        in_specs=[
            pl.BlockSpec((TM_COMB, D_MODEL), lambda i: (i, 0)),
            pl.BlockSpec((TOP_K, TM_COMB), lambda i: (0, i), memory_space=pltpu.SMEM),
            pl.BlockSpec((SUBLANES, TM_COMB), lambda i: (0, i)),
            pl.BlockSpec(memory_space=pl.ANY),
            pl.BlockSpec((1, D_MODEL), full),
            pl.BlockSpec((1, D_MODEL), full),
        ],
        out_specs=[
            pl.BlockSpec((TM_COMB, D_MODEL), lambda i: (i, 0)),
            pl.BlockSpec((TM_COMB, D_MODEL), lambda i: (i, 0)),
        ],
        out_shape=[
            jax.ShapeDtypeStruct((T, D_MODEL), F32),
            jax.ShapeDtypeStruct((T, D_MODEL), BF16),
        ],
        scratch_shapes=[
            pltpu.VMEM((TOP_K, TM_COMB * RT, LANES), F32),
            pltpu.SemaphoreType.DMA(()),
        ],
        compiler_params=_cparams(("arbitrary",)),
        name="combine",
    )(x1, destT, gateT, ys, g, b)


def _block_diag(w):
    g, c, _ = w.shape
    out = jnp.zeros((g * c, g * c), w.dtype)
    for i in range(g):
        out = out.at[i * c:(i + 1) * c, i * c:(i + 1) * c].set(w[i])
    return out


def kernel(x, w_in, conv_w, w_pool, pool_scale, idx_kn_g, idx_kn_b, w_out, ln1_g, ln1_b, router_w,
           router_b, w_gu, b_gu, w_down, b_down, ln2_g, ln2_b):
    B, S, D = x.shape
    T = B * S
    depth = w_in.shape[0]
    n_pairs = T * TOP_K
    n_tiles = (n_pairs + N_EXPERTS * (TM_MOE - 1)) // TM_MOE + 1
    n_tiles_pad = ((n_tiles + LANES - 1) // LANES) * LANES
    n_rows_alloc = (n_tiles + 1) * TM_MOE

    xf = x.reshape(T, D)
    xb = xf.astype(BF16)
    for l in range(depth):
        wl = w_in[l]
        wn = jnp.concatenate(
            [wl[:, 0:OFF_V], wl[:, OFF_KI:OFF_WI], jnp.zeros((D, LANES - IDX_DIM), F32)], axis=1).astype(BF16)
        wt = jnp.concatenate(
            [wl[:, OFF_V:OFF_QI], wl[:, OFF_QI:OFF_KI], wl[:, OFF_WI:D_IN],
             jnp.zeros((D, IDXT_ROWS - IDX_HEADS * IDX_DIM - IDX_HEADS), F32)], axis=1).T.astype(BF16)
        a, q, k, ki, vt, it = _inproj(xb, wn, wt, idx_kn_g[l].reshape(1, -1), idx_kn_b[l].reshape(1, -1))
        y_attn = _attn(it, ki, q, k, vt, B, S)
        x1 = _mixout(a, y_attn, xf, conv_w[l], _block_diag(w_pool[l]).astype(BF16),
                     pool_scale[l].reshape(1, -1), w_out[l].astype(BF16),
                     ln1_g[l].reshape(1, -1), ln1_b[l].reshape(1, -1), S)
        idxT, gateT, cnt = _router(x1, router_w[l].T, router_b[l].reshape(-1, 1))
        destT, meta = _ranks(idxT, cnt, n_tiles_pad)
        xs = _scatter(meta, x1, destT, n_rows_alloc)
        ys = _gmm(meta, xs, w_gu[l], b_gu[l].reshape(N_EXPERTS, 1, -1), w_down[l],
                  b_down[l].reshape(N_EXPERTS, 1, -1), n_tiles)
        xf, xb = _combine(x1, destT, gateT, ys, ln2_g[l].reshape(1, -1), ln2_b[l].reshape(1, -1))
    return xf.reshape(B, S, D)
```

```python
import functools

import jax
import jax.numpy as jnp
import numpy as np
from jax import lax
from jax.experimental import pallas as pl
from jax.experimental.pallas import tpu as pltpu

F32 = jnp.float32
BF16 = jnp.bfloat16
I32 = jnp.int32

D_MODEL = 1024
CONV_DIM = 256
CONV_WIDTH = 3
POOL_WINDOWS = (2, 4, 8, 16)
POOL_GROUP = 64
POOL_DIM = 256
ATTN_HEADS = 8
HEAD_DIM = 64
ATTN_DIM = 512
IDX_HEADS = 8
IDX_DIM = 32
INDEX_TOPK_MAX = 256
Q_BLOCK = 128
N_EXPERTS = 32
TOP_K = 4
D_FF = 1024
SWIGLU_ALPHA = 1.702
SWIGLU_LIMIT = 7.0
LN_EPS = 1e-5
DEPTH = 2
DN_ALPHA = (2.0 * DEPTH) ** 0.25

OFF_P = 3 * CONV_DIM
OFF_Q = OFF_P + POOL_DIM
OFF_K = OFF_Q + ATTN_DIM
OFF_V = OFF_K + ATTN_DIM
OFF_QI = OFF_V + ATTN_DIM
OFF_KI = OFF_QI + IDX_HEADS * IDX_DIM
OFF_WI = OFF_KI + IDX_DIM
D_IN = OFF_WI + IDX_HEADS

LANES = 128
SUBLANES = 8
VMEM_LIMIT = 56 * 1024 * 1024
KEY_CHUNK = 512
HALO = 16
TM_PROJ = 512
TM_MIX = 256
TM_ROUTE = 512
TM_MOE = 256
TM_COMB = 256
IDXT_ROWS = IDX_HEADS * IDX_DIM + 16
WT_ROWS = ATTN_DIM + IDXT_ROWS
INT_MIN = -(2 ** 31)
NEG_BIG = -1e30


def _cparams(sem, flags=None):
    return pltpu.CompilerParams(dimension_semantics=sem, vmem_limit_bytes=VMEM_LIMIT, flags=flags)


def _log2(n):
    k = n.bit_length() - 1
    assert 1 << k == n
    return k


def _dot(a, b):
    return jnp.dot(a, b, preferred_element_type=F32)


def _dot_nt(a, b):
    return lax.dot_general(a, b, (((1,), (1,)), ((), ())), preferred_element_type=F32)


def _tree(x, op):
    parts = [x[j * SUBLANES:(j + 1) * SUBLANES, :] for j in range(x.shape[0] // SUBLANES)]
    while len(parts) > 1:
        nxt = [op(parts[j], parts[j + 1]) for j in range(0, len(parts) - 1, 2)]
        if len(parts) % 2:
            nxt.append(parts[-1])
        parts = nxt
    return parts[0]


def _layernorm_rows(z, g, b):
    mu = jnp.mean(z, axis=-1, keepdims=True)
    d = z - mu
    var = jnp.mean(d * d, axis=-1, keepdims=True)
    return d * lax.rsqrt(var + LN_EPS) * g + b


def _inproj_kernel(x_ref, wn_ref, wt_ref, kg_ref, kb_ref,
                   a_ref, q_ref, k_ref, ki_ref, vt_ref, it_ref):
    x = x_ref[...]
    a_ref[...] = _dot(x, wn_ref[:, 0:OFF_Q])
    q_ref[...] = (_dot(x, wn_ref[:, OFF_Q:OFF_K]) * (HEAD_DIM ** -0.5)).astype(BF16)
    k_ref[...] = _dot(x, wn_ref[:, OFF_K:OFF_V]).astype(BF16)
    ki = _dot(x, wn_ref[:, OFF_V:OFF_V + LANES])[:, 0:IDX_DIM]
    ki_ref[...] = _layernorm_rows(ki, kg_ref[...], kb_ref[...])
    t = _dot_nt(wt_ref[...], x)
    for c in range(TM_PROJ // KEY_CHUNK):
        vt_ref[c] = t[0:ATTN_DIM, c * KEY_CHUNK:(c + 1) * KEY_CHUNK].astype(BF16)
    it_ref[...] = t[ATTN_DIM:WT_ROWS, :]


def _inproj(xb, wn, wt, kg, kb):
    T = xb.shape[0]
    nt = T // TM_PROJ
    cpt = TM_PROJ // KEY_CHUNK
    full = lambda i: (0, 0)
    return pl.pallas_call(
        _inproj_kernel,
        grid=(nt,),
        in_specs=[
            pl.BlockSpec((TM_PROJ, D_MODEL), lambda i: (i, 0)),
            pl.BlockSpec(wn.shape, full),
            pl.BlockSpec(wt.shape, full),
            pl.BlockSpec((1, IDX_DIM), full),
            pl.BlockSpec((1, IDX_DIM), full),
        ],
        out_specs=[
            pl.BlockSpec((TM_PROJ, OFF_Q), lambda i: (i, 0)),
            pl.BlockSpec((TM_PROJ, ATTN_DIM), lambda i: (i, 0)),
            pl.BlockSpec((TM_PROJ, ATTN_DIM), lambda i: (i, 0)),
            pl.BlockSpec((TM_PROJ, IDX_DIM), lambda i: (i, 0)),
            pl.BlockSpec((cpt, ATTN_DIM, KEY_CHUNK), lambda i: (i, 0, 0)),
            pl.BlockSpec((IDXT_ROWS, TM_PROJ), lambda i: (0, i)),
        ],
        out_shape=[
            jax.ShapeDtypeStruct((T, OFF_Q), F32),
            jax.ShapeDtypeStruct((T, ATTN_DIM), BF16),
            jax.ShapeDtypeStruct((T, ATTN_DIM), BF16),
            jax.ShapeDtypeStruct((T, IDX_DIM), F32),
            jax.ShapeDtypeStruct((T // KEY_CHUNK, ATTN_DIM, KEY_CHUNK), BF16),
            jax.ShapeDtypeStruct((IDXT_ROWS, T), F32),
        ],
        compiler_params=_cparams(("arbitrary",)),
        name="inproj",
    )(xb, wn, wt, kg, kb)


def _attn_kernel(n_sel, it_ref, ki_ref, q_ref, k_ref, vt_ref, o_ref,
                 keys_ref, rhs_ref, acc_ref, lg_ref):
    i = pl.program_id(1)
    nch = lax.shift_right_logical(i * Q_BLOCK + Q_BLOCK + KEY_CHUNK - 1, _log2(KEY_CHUNK))
    q0 = i * Q_BLOCK
    KC = KEY_CHUNK
    row = lax.broadcasted_iota(I32, (KC, Q_BLOCK), 0)
    lane = lax.broadcasted_iota(I32, (KC, Q_BLOCK), 1)
    qpos = q0 + lane

    qcat = jnp.concatenate(
        [it_ref[h * IDX_DIM:(h + 1) * IDX_DIM, :] for h in range(IDX_HEADS)], axis=1).astype(BF16)
    w_all = it_ref[IDX_HEADS * IDX_DIM:IDX_HEADS * IDX_DIM + IDX_HEADS, :] * (
        (IDX_HEADS ** -0.5) * (IDX_DIM ** -0.5))

    def score_body(c, carry):
        base = pl.multiple_of(c * KC, KC)
        kic = ki_ref[pl.ds(base, KC), :].astype(BF16)
        s = _dot(kic, qcat)
        sc = jnp.zeros((KC, Q_BLOCK), F32)
        for h in range(IDX_HEADS):
            sc = sc + jnp.maximum(s[:, h * Q_BLOCK:(h + 1) * Q_BLOCK], 0.0) * w_all[h:h + 1, :]
        bits = lax.bitcast_convert_type(sc, I32)
        key = bits ^ ((bits >> 31) & jnp.int32(0x7FFFFFFF))
        key = jnp.where(bits == jnp.int32(INT_MIN), jnp.int32(0), key)
        key = jnp.where(base + row <= qpos, key, jnp.int32(INT_MIN))
        keys_ref[pl.ds(base, KC), :] = key
        return carry

    lax.fori_loop(0, nch, score_body, 0)

    def count_ge(cand):
        def body(c, acc):
            base = pl.multiple_of(c * KC, KC)
            blk = keys_ref[pl.ds(base, KC), :]
            hit = jnp.where(blk >= cand, 1.0, 0.0).astype(F32)
            return acc + _tree(hit, jnp.add)
        acc = lax.fori_loop(0, nch, body, jnp.zeros((SUBLANES, Q_BLOCK), F32))
        return jnp.sum(acc, axis=0, keepdims=True)

    want = jnp.float32(n_sel)
    zero_row = jnp.zeros((1, Q_BLOCK), I32)
    lo = jnp.where(count_ge(zero_row) >= want, zero_row, jnp.int32(INT_MIN))

    def bit_body(it, lo):
        cand = lo + lax.shift_left(jnp.int32(1), jnp.int32(30) - it)
        return jnp.where(count_ge(cand) >= want, cand, lo)

    lo = lax.fori_loop(0, 31, bit_body, lo)
    tau = jnp.maximum(lo, jnp.int32(INT_MIN + 1))

    q = q_ref[...]
    lane_q = lax.broadcasted_iota(I32, (Q_BLOCK, 2 * HEAD_DIM), 1)
    for p in range(ATTN_HEADS // 2):
        qp = q[:, p * 2 * HEAD_DIM:(p + 1) * 2 * HEAD_DIM]
        rhs_ref[p, 0:Q_BLOCK, :] = jnp.where(lane_q < HEAD_DIM, qp, jnp.zeros_like(qp))
        rhs_ref[p, Q_BLOCK:2 * Q_BLOCK, :] = jnp.where(lane_q >= HEAD_DIM, qp, jnp.zeros_like(qp))
    acc_ref[...] = jnp.zeros_like(acc_ref)

    def stage_a_prep(c):
        base = pl.multiple_of(c * KC, KC)
        tk = (base + row - (q0 + Q_BLOCK - 1)).astype(F32)
        tkm = jnp.where(keys_ref[pl.ds(base, KC), :] >= tau, tk, NEG_BIG)
        return tkm, k_ref[pl.ds(base, KC), :]

    def stage_a_pair(c, p, tkm, kc):
        l2 = _dot_nt(kc[:, p * 2 * HEAD_DIM:(p + 1) * 2 * HEAD_DIM], rhs_ref[p])
        mcs = []
        for hh in range(2):
            h = 2 * p + hh
            slope = 2.0 ** (-8.0 * (h + 1) / ATTN_HEADS)
            lg = l2[:, hh * Q_BLOCK:(hh + 1) * Q_BLOCK] + slope * tkm
            lg_ref[h] = lg
            mcs.append(jnp.max(_tree(lg, jnp.maximum), axis=0, keepdims=True))
        return mcs

    def stage_b_pair(c, p, m_new, alpha):
        sums = []
        for hh in range(2):
            h = 2 * p + hh
            pr = jnp.exp(lg_ref[h] - m_new[h:h + 1, :])
            sums.append(jnp.sum(_tree(pr, jnp.add), axis=0, keepdims=True))
            pv = _dot(vt_ref[c, h * HEAD_DIM:(h + 1) * HEAD_DIM, :], pr.astype(BF16))
            acc_ref[h * HEAD_DIM:(h + 1) * HEAD_DIM, :] = (
                acc_ref[h * HEAD_DIM:(h + 1) * HEAD_DIM, :] * alpha[h:h + 1, :] + pv)
        return sums

    def stage_b_all(c, m_old, l_old, mc):
        m_new = jnp.maximum(m_old, mc)
        alpha = jnp.exp(m_old - m_new)
        return m_new, alpha

    tkm0, kc0 = stage_a_prep(0)
    mc0 = jnp.concatenate(sum([stage_a_pair(0, p, tkm0, kc0) for p in range(ATTN_HEADS // 2)], []), axis=0)

    def attn_body(c, carry):
        m_old, l_old, mc_prev = carry
        m_new, alpha = stage_b_all(c - 1, m_old, l_old, mc_prev)
        tkm, kc = stage_a_prep(c)
        mcs, sums = [], []
        for p in range(ATTN_HEADS // 2):
            sums += stage_b_pair(c - 1, p, m_new, alpha)
            mcs += stage_a_pair(c, p, tkm, kc)
        return m_new, l_old * alpha + jnp.concatenate(sums, axis=0), jnp.concatenate(mcs, axis=0)

    m0 = jnp.full((ATTN_HEADS, Q_BLOCK), NEG_BIG, F32)
    l0 = jnp.zeros((ATTN_HEADS, Q_BLOCK), F32)
    m_old, l_old, mc_prev = lax.fori_loop(1, nch, attn_body, (m0, l0, mc0))
    m_new, alpha = stage_b_all(nch - 1, m_old, l_old, mc_prev)
    sums = sum([stage_b_pair(nch - 1, p, m_new, alpha) for p in range(ATTN_HEADS // 2)], [])
    l_fin = l_old * alpha + jnp.concatenate(sums, axis=0)

    outs = []
    for h in range(ATTN_HEADS):
        outs.append(acc_ref[h * HEAD_DIM:(h + 1) * HEAD_DIM, :] / l_fin[h:h + 1, :])
    o_ref[...] = jnp.concatenate(outs, axis=0).T


def _attn(it, ki, q, k, vt, B, S):
    T = B * S
    nb = S // Q_BLOCK
    n_sel = min(INDEX_TOPK_MAX, S // 4)
    return pl.pallas_call(
        functools.partial(_attn_kernel, n_sel),
        grid=(B, nb),
        in_specs=[
            pl.BlockSpec((IDXT_ROWS, Q_BLOCK), lambda b, i: (0, b * nb + i)),
            pl.BlockSpec((S, IDX_DIM), lambda b, i: (b, 0)),
            pl.BlockSpec((Q_BLOCK, ATTN_DIM), lambda b, i: (b * nb + i, 0)),
            pl.BlockSpec((S, ATTN_DIM), lambda b, i: (b, 0)),
            pl.BlockSpec((S // KEY_CHUNK, ATTN_DIM, KEY_CHUNK), lambda b, i: (b, 0, 0)),
        ],
        out_specs=pl.BlockSpec((Q_BLOCK, ATTN_DIM), lambda b, i: (b * nb + i, 0)),
        out_shape=jax.ShapeDtypeStruct((T, ATTN_DIM), F32),
        scratch_shapes=[
            pltpu.VMEM((S, Q_BLOCK), I32),
            pltpu.VMEM((ATTN_HEADS // 2, 2 * Q_BLOCK, 2 * HEAD_DIM), BF16),
            pltpu.VMEM((ATTN_DIM, Q_BLOCK), F32),
            pltpu.VMEM((ATTN_HEADS, KEY_CHUNK, Q_BLOCK), F32),
        ],
        compiler_params=_cparams(("arbitrary", "arbitrary")),
        name="attn",
    )(it, ki, q, k, vt)


def _mixout_kernel(tiles_per_seq, a_ref, halo_ref, y_ref, x_ref, cw_ref, wpool_ref, ps_ref,
                   wout_ref, g_ref, b_ref, o_ref, ext_ref):
    i = pl.program_id(0)
    tm = TM_MIX
    first = lax.rem(i, tiles_per_seq) == 0
    halo = jnp.where(first, 0.0, halo_ref[...])
    a = a_ref[...]
    h_c, gb_c, gc_c, p_c = (a[:, j * CONV_DIM:(j + 1) * CONV_DIM] for j in range(4))

    t0 = 2 * HALO
    n_ext = t0 + tm
    ext_ref[0:HALO, :] = jnp.zeros((HALO, CONV_DIM), F32)

    ext_ref[HALO:t0, :] = halo[:, 2 * CONV_DIM:3 * CONV_DIM] * halo[:, 0:CONV_DIM]
    u = gc_c * h_c
    ext_ref[t0:n_ext, :] = u
    cw = cw_ref[...]
    conv = cw[2:3, :] * u
    conv = conv + cw[1:2, :] * ext_ref[t0 - 1:n_ext - 1, :]
    conv = conv + cw[0:1, :] * ext_ref[t0 - 2:n_ext - 2, :]
    y_conv = gb_c * conv

    ext_ref[HALO:t0, :] = halo[:, OFF_P:OFF_P + POOL_DIM]
    ext_ref[t0:n_ext, :] = p_c
    sums = {}
    step = 1
    while step < POOL_WINDOWS[-1]:
        cur = ext_ref[HALO:n_ext, :] + ext_ref[HALO - step:n_ext - step, :]
        ext_ref[HALO:n_ext, :] = cur
        step *= 2
        sums[step] = ext_ref[t0:n_ext, :]
    tpos = (lax.rem(i, tiles_per_seq) * tm + lax.broadcasted_iota(I32, (tm, POOL_DIM), 0) + 1).astype(F32)
    grp = lax.shift_right_logical(lax.broadcasted_iota(I32, (tm, POOL_DIM), 1), _log2(POOL_GROUP))
    mean = jnp.zeros((tm, POOL_DIM), F32)
    for gi, w in enumerate(POOL_WINDOWS):
        mean = jnp.where(grp == gi, sums[w] / jnp.minimum(tpos, float(w)), mean)
    mixed = mean - p_c
    y_pool = _dot(mixed.astype(BF16), wpool_ref[...]) * ps_ref[...]

    mix = _dot(y_conv.astype(BF16), wout_ref[0:CONV_DIM, :])
    mix = mix + _dot(y_pool.astype(BF16), wout_ref[CONV_DIM:CONV_DIM + POOL_DIM, :])
    mix = mix + _dot(y_ref[...].astype(BF16), wout_ref[CONV_DIM + POOL_DIM:D_MODEL, :])
    z = DN_ALPHA * x_ref[...] + mix
    o_ref[...] = _layernorm_rows(z, g_ref[...], b_ref[...])


def _mixout(a, y_attn, x, cw, wpool_bd, ps, wout, g, b, S):
    T = a.shape[0]
    nt = T // TM_MIX
    tps = S // TM_MIX
    hb = TM_MIX // HALO
    full = lambda i: (0, 0)
    return pl.pallas_call(
        functools.partial(_mixout_kernel, tps),
        grid=(nt,),
        in_specs=[
            pl.BlockSpec((TM_MIX, OFF_Q), lambda i: (i, 0)),
            pl.BlockSpec((HALO, OFF_Q), lambda i: (jnp.maximum(i * hb - 1, 0), 0)),
            pl.BlockSpec((TM_MIX, ATTN_DIM), lambda i: (i, 0)),
            pl.BlockSpec((TM_MIX, D_MODEL), lambda i: (i, 0)),
            pl.BlockSpec((CONV_WIDTH, CONV_DIM), full),
            pl.BlockSpec((POOL_DIM, POOL_DIM), full),
            pl.BlockSpec((1, POOL_DIM), full),
            pl.BlockSpec((D_MODEL, D_MODEL), full),
            pl.BlockSpec((1, D_MODEL), full),
            pl.BlockSpec((1, D_MODEL), full),
        ],
        out_specs=pl.BlockSpec((TM_MIX, D_MODEL), lambda i: (i, 0)),
        out_shape=jax.ShapeDtypeStruct((T, D_MODEL), F32),
        scratch_shapes=[pltpu.VMEM((2 * HALO + TM_MIX, CONV_DIM), F32)],
        compiler_params=_cparams(("arbitrary",)),
        name="mixout",
    )(a, a, y_attn, x, cw, wpool_bd, ps, wout, g, b)


def _router_kernel(x_ref, rw_ref, rb_ref, idx_ref, gate_ref, cnt_ref):
    i = pl.program_id(0)
    tm = TM_ROUTE
    logits = lax.dot_general(rw_ref[...], x_ref[...], (((1,), (1,)), ((), ())),
                             precision=lax.Precision.HIGHEST,
                             preferred_element_type=F32) + rb_ref[...]
    erow = lax.broadcasted_iota(I32, (N_EXPERTS, tm), 0).astype(F32)
    work = logits
    vals, idxs = [], []
    multi = jnp.zeros((N_EXPERTS, tm), F32)
    for _ in range(TOP_K):
        mx = jnp.max(work, axis=0, keepdims=True)
        pick = jnp.min(jnp.where(work == mx, erow, float(N_EXPERTS)), axis=0, keepdims=True)
        hit = erow == pick
        work = jnp.where(hit, -jnp.inf, work)
        multi = multi + hit.astype(F32)
        vals.append(mx)
        idxs.append(pick)
    es = [jnp.exp(v - vals[0]) for v in vals]
    den = es[0] + es[1] + es[2] + es[3]
    idx_ref[...] = jnp.concatenate(idxs, axis=0).astype(I32)
    gate_ref[...] = jnp.concatenate([e / den for e in es] + [jnp.zeros((SUBLANES - TOP_K, tm), F32)], axis=0)
    part = multi[:, 0:LANES]
    for j in range(1, tm // LANES):
        part = part + multi[:, j * LANES:(j + 1) * LANES]

    @pl.when(i == 0)
    def _():
        cnt_ref[...] = jnp.zeros_like(cnt_ref)

    cnt_ref[...] += part


def _router(x1, rwT, rb):
    T = x1.shape[0]
    nt = T // TM_ROUTE
    full = lambda i: (0, 0)
    return pl.pallas_call(
        _router_kernel,
        grid=(nt,),
        in_specs=[
            pl.BlockSpec((TM_ROUTE, D_MODEL), lambda i: (i, 0)),
            pl.BlockSpec((N_EXPERTS, D_MODEL), full),
            pl.BlockSpec((N_EXPERTS, 1), full),
        ],
        out_specs=[
            pl.BlockSpec((TOP_K, TM_ROUTE), lambda i: (0, i)),
            pl.BlockSpec((SUBLANES, TM_ROUTE), lambda i: (0, i)),
            pl.BlockSpec((N_EXPERTS, LANES), full),
        ],
        out_shape=[
            jax.ShapeDtypeStruct((TOP_K, T), I32),
            jax.ShapeDtypeStruct((SUBLANES, T), F32),
            jax.ShapeDtypeStruct((N_EXPERTS, LANES), F32),
        ],
        compiler_params=_cparams(("arbitrary",)),
        name="router",
    )(x1, rwT, rb)


def _ranks_kernel(n_tiles_pad, idx_ref, cnt_ref, dest_ref, meta_ref, tri_ref, start_ref, carry_ref):
    i = pl.program_id(0)
    tm = TM_ROUTE
    erow = lax.broadcasted_iota(I32, (N_EXPERTS, LANES), 0)
    elane = lax.broadcasted_iota(I32, (N_EXPERTS, LANES), 1)

    @pl.when(i == 0)
    def _():
        cnt = jnp.sum(cnt_ref[...], axis=1, keepdims=True)
        cnt_i = jnp.broadcast_to(cnt, (N_EXPERTS, LANES)).astype(I32)
        padded = lax.shift_left(lax.shift_right_logical(cnt_i + (TM_MOE - 1), _log2(TM_MOE)), _log2(TM_MOE))
        r = lax.broadcasted_iota(I32, (N_EXPERTS, N_EXPERTS), 0)
        c = lax.broadcasted_iota(I32, (N_EXPERTS, N_EXPERTS), 1)
        low = (c <= r).astype(F32)
        pad_end = lax.dot_general(low, padded.astype(F32), (((1,), (0,)), ((), ())),
                                  precision=lax.Precision.HIGHEST, preferred_element_type=F32)
        pad_start = pad_end - padded.astype(F32)
        start_ref[...] = pad_start
        carry_ref[...] = jnp.zeros_like(carry_ref)
        a = lax.broadcasted_iota(I32, (tm, tm), 0)
        bcol = lax.broadcasted_iota(I32, (tm, tm), 1)
        tri_ref[...] = (a < bcol).astype(BF16)
        ntp = n_tiles_pad
        tstart = (lax.broadcasted_iota(I32, (N_EXPERTS, ntp), 1) * TM_MOE).astype(F32)
        pe = jnp.concatenate([pad_end] * (ntp // LANES), axis=1)
        texp = jnp.sum((pe <= tstart).astype(F32), axis=0, keepdims=True)
        texp = jnp.minimum(texp, float(N_EXPERTS - 1)).astype(I32)
        total = jnp.max(pad_end, axis=0, keepdims=True)
        n_used = lax.shift_right_logical(total.astype(I32), _log2(TM_MOE))
        zstart = jnp.sum(jnp.where(erow == elane, pad_start + cnt_i.astype(F32), 0.0),
                         axis=0, keepdims=True).astype(I32)
        zlen = jnp.sum(jnp.where(erow == elane, (padded - cnt_i).astype(F32), 0.0),
                       axis=0, keepdims=True).astype(I32)
        lanes_pad = jnp.zeros((1, ntp - LANES), I32)
        meta_ref[...] = jnp.concatenate(
            [texp,
             jnp.concatenate([n_used, lanes_pad], axis=1),
             jnp.concatenate([zstart, lanes_pad], axis=1),
             jnp.concatenate([zlen, lanes_pad], axis=1),
             jnp.zeros((SUBLANES - 4, ntp), I32)], axis=0)

    idx = idx_ref[...]
    erow_t = lax.broadcasted_iota(I32, (N_EXPERTS, tm), 0)
    hits = [erow_t == idx[k:k + 1, :] for k in range(TOP_K)]
    multi = hits[0].astype(F32)
    for k in range(1, TOP_K):
        multi = multi + hits[k].astype(F32)
    prefix = _dot(multi.astype(BF16), tri_ref[...])
    base = jnp.concatenate([carry_ref[...] + start_ref[...]] * (tm // LANES), axis=1)
    tot = prefix + base
    dest_ref[...] = jnp.concatenate(
        [jnp.sum(jnp.where(hits[k], tot, 0.0), axis=0, keepdims=True) for k in range(TOP_K)],
        axis=0).astype(I32)
    carry_ref[...] += jnp.broadcast_to(jnp.sum(multi, axis=1, keepdims=True), (N_EXPERTS, LANES))


def _ranks(idxT, cnt, n_tiles_pad):
    T = idxT.shape[1]
    nt = T // TM_ROUTE
    full = lambda i: (0, 0)
    return pl.pallas_call(
        functools.partial(_ranks_kernel, n_tiles_pad),
        grid=(nt,),
        in_specs=[
            pl.BlockSpec((TOP_K, TM_ROUTE), lambda i: (0, i)),
            pl.BlockSpec((N_EXPERTS, LANES), full),
        ],
        out_specs=[
            pl.BlockSpec((TOP_K, TM_ROUTE), lambda i: (0, i)),
            pl.BlockSpec((SUBLANES, n_tiles_pad), full),
        ],
        out_shape=[
            jax.ShapeDtypeStruct((TOP_K, T), I32),
            jax.ShapeDtypeStruct((SUBLANES, n_tiles_pad), I32),
        ],
        scratch_shapes=[
            pltpu.VMEM((TM_ROUTE, TM_ROUTE), BF16),
            pltpu.VMEM((N_EXPERTS, LANES), F32),
            pltpu.VMEM((N_EXPERTS, LANES), F32),
        ],
        compiler_params=_cparams(("arbitrary",)),
        name="ranks",
    )(idxT, cnt)


RT = D_MODEL // LANES
DMA_UNROLL = 8


def _row_copy(src_ref, s, dst_ref, d, sem):
    return pltpu.make_async_copy(src_ref.at[pl.ds(pl.multiple_of(s * RT, RT), RT)],
                                 dst_ref.at[pl.ds(pl.multiple_of(d * RT, RT), RT)], sem)


def _to_row_tiled(dst_ref, val, rows):
    for s in range(RT):
        dst_ref[pl.ds(s, rows, stride=RT), :] = val[:, s * LANES:(s + 1) * LANES]


def _from_row_tiled(src_ref, rows):
    return [src_ref[pl.ds(s, rows, stride=RT), :] for s in range(RT)]


def _scatter_kernel(meta_ref, x_ref, dest_ref, xs_ref, stage_ref, zero_ref, sem):
    i = pl.program_id(0)
    tm = TM_ROUTE

    @pl.when(i == 0)
    def _():
        zero_ref[...] = jnp.zeros_like(zero_ref)

        def zcopy(row, nrows):
            off = pl.multiple_of(row * RT, RT)
            return pltpu.make_async_copy(zero_ref.at[pl.ds(0, nrows * RT)],
                                         xs_ref.at[pl.ds(off, nrows * RT)], sem)

        def pad_fill(wait):
            def body(e, c):
                row = meta_ref[2, e]
                plen = meta_ref[3, e]
                for bit in reversed(range(_log2(TM_MOE))):
                    size = 1 << bit
                    has = (plen & size) != 0

                    @pl.when(has)
                    def _():
                        cp = zcopy(row, size)
                        cp.wait() if wait else cp.start()

                    row = row + jnp.where(has, size, 0)
                return c
            lax.fori_loop(0, N_EXPERTS, body, 0)

        def tail_fill(wait):
            def body(j, c):
                cp = zcopy(j * TM_MOE, TM_MOE)
                cp.wait() if wait else cp.start()
                return c
            lax.fori_loop(meta_ref[1, 0], xs_ref.shape[0] // (TM_MOE * RT), body, 0)

        pad_fill(False)
        tail_fill(False)
        pad_fill(True)
        tail_fill(True)

    _to_row_tiled(stage_ref, x_ref[...], tm)

    def start(t, c):
        for k in range(TOP_K):
            _row_copy(stage_ref, t, xs_ref, dest_ref[k, t], sem).start(priority=k % 2)
        return c

    lax.fori_loop(0, tm, start, 0, unroll=DMA_UNROLL)
    for k in range(TOP_K):
        pltpu.make_async_copy(stage_ref, xs_ref.at[pl.ds(0, tm * RT)], sem).wait()


def _scatter(meta, x1, destT, n_rows_alloc):
    T = x1.shape[0]
    nt = T // TM_ROUTE
    return pl.pallas_call(
        _scatter_kernel,
        grid_spec=pltpu.PrefetchScalarGridSpec(
            num_scalar_prefetch=1,
            grid=(nt,),
            in_specs=[
                pl.BlockSpec((TM_ROUTE, D_MODEL), lambda i, m: (i, 0)),
                pl.BlockSpec((TOP_K, TM_ROUTE), lambda i, m: (0, i), memory_space=pltpu.SMEM),
            ],
            out_specs=pl.BlockSpec(memory_space=pl.ANY),
            scratch_shapes=[
                pltpu.VMEM((TM_ROUTE * RT, LANES), F32),
                pltpu.VMEM((TM_MOE * RT, LANES), F32),
                pltpu.SemaphoreType.DMA(()),
            ],
        ),
        out_shape=jax.ShapeDtypeStruct((n_rows_alloc * RT, LANES), F32),
        compiler_params=_cparams(("arbitrary",)),
        name="scatter",
    )(meta, x1, destT)


def _gmm_kernel(meta_ref, xs_ref, wgu_ref, bgu_ref, wd_ref, bd_ref, ys_ref, wgu_b, wd_b, lhs_ref, act_ref):
    j = pl.program_id(0)
    n_used = meta_ref[1, 0]
    e_now = meta_ref[0, j]
    e_prev = meta_ref[0, jnp.maximum(j - 1, 0)]
    used = j < n_used

    @pl.when(used & ((j == 0) | (e_now != e_prev)))
    def _():
        wgu_b[...] = wgu_ref[0, 0].astype(BF16)
        wd_b[...] = wd_ref[0, 0].astype(BF16)

    @pl.when(used)
    def _():
        for s, piece in enumerate(_from_row_tiled(xs_ref, TM_MOE)):
            lhs_ref[:, s * LANES:(s + 1) * LANES] = piece.astype(BF16)
        x = lhs_ref[...]
        nc = 256
        for c in range(D_FF // nc):
            gate = _dot(x, wgu_b[:, c * nc:(c + 1) * nc]) + bgu_ref[0, 0, :, c * nc:(c + 1) * nc]
            up = (_dot(x, wgu_b[:, D_FF + c * nc:D_FF + (c + 1) * nc])
                  + bgu_ref[0, 0, :, D_FF + c * nc:D_FF + (c + 1) * nc])
            gate = jnp.minimum(gate, SWIGLU_LIMIT)
            up = jnp.clip(up, -SWIGLU_LIMIT, SWIGLU_LIMIT)
            act = gate * jax.nn.sigmoid(SWIGLU_ALPHA * gate) * (up + 1.0)
            act_ref[:, c * nc:(c + 1) * nc] = act.astype(BF16)
        _to_row_tiled(ys_ref, _dot(act_ref[...], wd_b[...]) + bd_ref[0, 0], TM_MOE)

    @pl.when(jnp.logical_not(used))
    def _():
        ys_ref[...] = jnp.zeros_like(ys_ref)


def _gmm(meta, xs, w_gu, b_gu, w_down, b_down, n_tiles, layer):
    last = lambda m: jnp.maximum(m[1, 0] - 1, 0)
    return pl.pallas_call(
        _gmm_kernel,
        grid_spec=pltpu.PrefetchScalarGridSpec(
            num_scalar_prefetch=1,
            grid=(n_tiles,),
            in_specs=[
                pl.BlockSpec((TM_MOE * RT, LANES), lambda j, m: (jnp.minimum(j, last(m)), 0)),
                pl.BlockSpec((1, 1, D_MODEL, 2 * D_FF), lambda j, m: (layer, m[0, j], 0, 0)),
                pl.BlockSpec((1, 1, 1, 2 * D_FF), lambda j, m: (layer, m[0, j], 0, 0)),
                pl.BlockSpec((1, 1, D_FF, D_MODEL), lambda j, m: (layer, m[0, j], 0, 0)),
                pl.BlockSpec((1, 1, 1, D_MODEL), lambda j, m: (layer, m[0, j], 0, 0)),
            ],
            out_specs=pl.BlockSpec((TM_MOE * RT, LANES), lambda j, m: (j, 0)),
            scratch_shapes=[
                pltpu.VMEM((D_MODEL, 2 * D_FF), BF16),
                pltpu.VMEM((D_FF, D_MODEL), BF16),
                pltpu.VMEM((TM_MOE, D_MODEL), BF16),
                pltpu.VMEM((TM_MOE, D_FF), BF16),
            ],
        ),
        out_shape=jax.ShapeDtypeStruct((n_tiles * TM_MOE * RT, LANES), F32),
        compiler_params=_cparams(("arbitrary",)),
        name="gmm",
    )(meta, xs, w_gu, b_gu, w_down, b_down)


def _combine_kernel(x_ref, dest_ref, gate_ref, ys_ref, g_ref, b_ref, o_ref, ob_ref, buf_ref, sem):
    tm = TM_COMB

    def start(t, c):
        for k in range(TOP_K):
            _row_copy(ys_ref, dest_ref[k, t], buf_ref.at[k], t, sem).start(priority=k % 2)
        return c

    lax.fori_loop(0, tm, start, 0, unroll=DMA_UNROLL)
    for k in range(TOP_K):
        pltpu.make_async_copy(ys_ref.at[pl.ds(0, tm * RT)], buf_ref.at[k], sem).wait()
    gates = jnp.concatenate(
        [gate_ref[...], jnp.zeros((LANES - SUBLANES, tm), F32)], axis=0).T
    pieces = None
    for k in range(TOP_K):
        gk = gates[:, k:k + 1]
        rows = [p * gk for p in _from_row_tiled(buf_ref.at[k], tm)]
        pieces = rows if pieces is None else [a + r for a, r in zip(pieces, rows)]
    z = DN_ALPHA * x_ref[...] + jnp.concatenate(pieces, axis=1)
    out = _layernorm_rows(z, g_ref[...], b_ref[...])
    o_ref[...] = out
    ob_ref[...] = out.astype(BF16)


def _combine(x1, destT, gateT, ys, g, b):
    T = x1.shape[0]
    nt = T // TM_COMB
    full = lambda i: (0, 0)
    return pl.pallas_call(
        _combine_kernel,
        grid=(nt,),
        in_specs=[
            pl.BlockSpec((TM_COMB, D_MODEL), lambda i: (i, 0)),
            pl.BlockSpec((TOP_K, TM_COMB), lambda i: (0, i), memory_space=pltpu.SMEM),
            pl.BlockSpec((SUBLANES, TM_COMB), lambda i: (0, i)),
            pl.BlockSpec(memory_space=pl.ANY),
            pl.BlockSpec((1, D_MODEL), full),
            pl.BlockSpec((1, D_MODEL), full),
        ],
        out_specs=[
            pl.BlockSpec((TM_COMB, D_MODEL), lambda i: (i, 0)),
            pl.BlockSpec((TM_COMB, D_MODEL), lambda i: (i, 0)),
        ],
        out_shape=[
            jax.ShapeDtypeStruct((T, D_MODEL), F32),
            jax.ShapeDtypeStruct((T, D_MODEL), BF16),
        ],
        scratch_shapes=[
            pltpu.VMEM((TOP_K, TM_COMB * RT, LANES), F32),
            pltpu.SemaphoreType.DMA(()),
        ],
        compiler_params=_cparams(("arbitrary",)),
        name="combine",
    )(x1, destT, gateT, ys, g, b)


def _block_diag(w):
    g, c, _ = w.shape
    out = jnp.zeros((g * c, g * c), w.dtype)
    for i in range(g):
        out = out.at[i * c:(i + 1) * c, i * c:(i + 1) * c].set(w[i])
    return out


def kernel(x, w_in, conv_w, w_pool, pool_scale, idx_kn_g, idx_kn_b, w_out, ln1_g, ln1_b, router_w,
           router_b, w_gu, b_gu, w_down, b_down, ln2_g, ln2_b):
    B, S, D = x.shape
    T = B * S
    depth = w_in.shape[0]
    n_pairs = T * TOP_K
    n_tiles = (n_pairs + N_EXPERTS * (TM_MOE - 1)) // TM_MOE + 1
    n_tiles_pad = ((n_tiles + LANES - 1) // LANES) * LANES
    n_rows_alloc = (n_tiles + 1) * TM_MOE

    xf = x.reshape(T, D)
    xb = xf.astype(BF16)
    for l in range(depth):
        wl = w_in[l]
        wn = jnp.concatenate(
            [wl[:, 0:OFF_V], wl[:, OFF_KI:OFF_WI], jnp.zeros((D, LANES - IDX_DIM), F32)], axis=1).astype(BF16)
        wt = jnp.concatenate(
            [wl[:, OFF_V:OFF_QI], wl[:, OFF_QI:OFF_KI], wl[:, OFF_WI:D_IN],
             jnp.zeros((D, IDXT_ROWS - IDX_HEADS * IDX_DIM - IDX_HEADS), F32)], axis=1).T.astype(BF16)
        a, q, k, ki, vt, it = _inproj(xb, wn, wt, idx_kn_g[l].reshape(1, -1), idx_kn_b[l].reshape(1, -1))
        y_attn = _attn(it, ki, q, k, vt, B, S)
        x1 = _mixout(a, y_attn, xf, conv_w[l], _block_diag(w_pool[l]).astype(BF16),
                     pool_scale[l].reshape(1, -1), w_out[l].astype(BF16),
                     ln1_g[l].reshape(1, -1), ln1_b[l].reshape(1, -1), S)
        idxT, gateT, cnt = _router(x1, router_w[l].T, router_b[l].reshape(-1, 1))
        destT, meta = _ranks(idxT, cnt, n_tiles_pad)
        xs = _scatter(meta, x1, destT, n_rows_alloc)
        ys = _gmm(meta, xs, w_gu, b_gu.reshape(depth, N_EXPERTS, 1, -1), w_down,
                  b_down.reshape(depth, N_EXPERTS, 1, -1), n_tiles, l)
        xf, xb = _combine(x1, destT, gateT, ys, ln2_g[l].reshape(1, -1), ln2_b[l].reshape(1, -1))
    return xf.reshape(B, S, D)
```

```python
import functools

import jax
import jax.numpy as jnp
import numpy as np
from jax import lax
from jax.experimental import pallas as pl
from jax.experimental.pallas import tpu as pltpu

F32 = jnp.float32
BF16 = jnp.bfloat16
I32 = jnp.int32

D_MODEL = 1024
CONV_DIM = 256
CONV_WIDTH = 3
POOL_WINDOWS = (2, 4, 8, 16)
POOL_GROUP = 64
POOL_DIM = 256
ATTN_HEADS = 8
HEAD_DIM = 64
ATTN_DIM = 512
IDX_HEADS = 8
IDX_DIM = 32
INDEX_TOPK_MAX = 256
Q_BLOCK = 128
N_EXPERTS = 32
TOP_K = 4
D_FF = 1024
SWIGLU_ALPHA = 1.702
SWIGLU_LIMIT = 7.0
LN_EPS = 1e-5
DEPTH = 2
DN_ALPHA = (2.0 * DEPTH) ** 0.25

OFF_P = 3 * CONV_DIM
OFF_Q = OFF_P + POOL_DIM
OFF_K = OFF_Q + ATTN_DIM
OFF_V = OFF_K + ATTN_DIM
OFF_QI = OFF_V + ATTN_DIM
OFF_KI = OFF_QI + IDX_HEADS * IDX_DIM
OFF_WI = OFF_KI + IDX_DIM
D_IN = OFF_WI + IDX_HEADS

LANES = 128
SUBLANES = 8
VMEM_LIMIT = 56 * 1024 * 1024
KEY_CHUNK = 512
HALO = 16
TM_PROJ = 512
TM_MIX = 256
TM_ROUTE = 512
TM_MOE = 256
TM_COMB = 256
IDXT_ROWS = IDX_HEADS * IDX_DIM + 16
WT_ROWS = ATTN_DIM + IDXT_ROWS
INT_MIN = -(2 ** 31)
NEG_BIG = -1e30


def _cparams(sem, flags=None):
    return pltpu.CompilerParams(dimension_semantics=sem, vmem_limit_bytes=VMEM_LIMIT, flags=flags)


def _log2(n):
    k = n.bit_length() - 1
    assert 1 << k == n
    return k


def _dot(a, b):
    return jnp.dot(a, b, preferred_element_type=F32)


def _dot_nt(a, b):
    return lax.dot_general(a, b, (((1,), (1,)), ((), ())), preferred_element_type=F32)


def _tree(x, op):
    parts = [x[j * SUBLANES:(j + 1) * SUBLANES, :] for j in range(x.shape[0] // SUBLANES)]
    while len(parts) > 1:
        nxt = [op(parts[j], parts[j + 1]) for j in range(0, len(parts) - 1, 2)]
        if len(parts) % 2:
            nxt.append(parts[-1])
        parts = nxt
    return parts[0]


def _layernorm_rows(z, g, b):
    mu = jnp.mean(z, axis=-1, keepdims=True)
    d = z - mu
    var = jnp.mean(d * d, axis=-1, keepdims=True)
    return d * lax.rsqrt(var + LN_EPS) * g + b


def _inproj_kernel(x_ref, wn_ref, wt_ref, kg_ref, kb_ref,
                   a_ref, q_ref, k_ref, ki_ref, vt_ref, it_ref):
    x = x_ref[...]
    a_ref[...] = _dot(x, wn_ref[:, 0:OFF_Q])
    q_ref[...] = (_dot(x, wn_ref[:, OFF_Q:OFF_K]) * (HEAD_DIM ** -0.5)).astype(BF16)
    k_ref[...] = _dot(x, wn_ref[:, OFF_K:OFF_V]).astype(BF16)
    ki = _dot(x, wn_ref[:, OFF_V:OFF_V + LANES])[:, 0:IDX_DIM]
    ki_ref[...] = _layernorm_rows(ki, kg_ref[...], kb_ref[...])
    t = _dot_nt(wt_ref[...], x)
    for c in range(TM_PROJ // KEY_CHUNK):
        vt_ref[c] = t[0:ATTN_DIM, c * KEY_CHUNK:(c + 1) * KEY_CHUNK].astype(BF16)
    it_ref[...] = t[ATTN_DIM:WT_ROWS, :]


def _inproj(xb, wn, wt, kg, kb):
    T = xb.shape[0]
    nt = T // TM_PROJ
    cpt = TM_PROJ // KEY_CHUNK
    full = lambda i: (0, 0)
    return pl.pallas_call(
        _inproj_kernel,
        grid=(nt,),
        in_specs=[
            pl.BlockSpec((TM_PROJ, D_MODEL), lambda i: (i, 0)),
            pl.BlockSpec(wn.shape, full),
            pl.BlockSpec(wt.shape, full),
            pl.BlockSpec((1, IDX_DIM), full),
            pl.BlockSpec((1, IDX_DIM), full),
        ],
        out_specs=[
            pl.BlockSpec((TM_PROJ, OFF_Q), lambda i: (i, 0)),
            pl.BlockSpec((TM_PROJ, ATTN_DIM), lambda i: (i, 0)),
            pl.BlockSpec((TM_PROJ, ATTN_DIM), lambda i: (i, 0)),
            pl.BlockSpec((TM_PROJ, IDX_DIM), lambda i: (i, 0)),
            pl.BlockSpec((cpt, ATTN_DIM, KEY_CHUNK), lambda i: (i, 0, 0)),
            pl.BlockSpec((IDXT_ROWS, TM_PROJ), lambda i: (0, i)),
        ],
        out_shape=[
            jax.ShapeDtypeStruct((T, OFF_Q), F32),
            jax.ShapeDtypeStruct((T, ATTN_DIM), BF16),
            jax.ShapeDtypeStruct((T, ATTN_DIM), BF16),
            jax.ShapeDtypeStruct((T, IDX_DIM), F32),
            jax.ShapeDtypeStruct((T // KEY_CHUNK, ATTN_DIM, KEY_CHUNK), BF16),
            jax.ShapeDtypeStruct((IDXT_ROWS, T), F32),
        ],
        compiler_params=_cparams(("arbitrary",)),
        name="inproj",
    )(xb, wn, wt, kg, kb)


def _count_ge_packed(d_ref, nch, cand):
    rows = 2 * SUBLANES
    n_acc = 4
    assert d_ref.shape[0] // rows <= 256

    def body(c, accs):
        base = pl.multiple_of(c * KEY_CHUNK, KEY_CHUNK)
        blk = d_ref[pl.ds(base, KEY_CHUNK), :]
        hit = jnp.where(blk >= cand, jnp.ones((), BF16), jnp.zeros((), BF16))
        accs = list(accs)
        for j in range(KEY_CHUNK // rows):
            accs[j % n_acc] = accs[j % n_acc] + hit[j * rows:(j + 1) * rows, :]
        return tuple(accs)

    zero = jnp.zeros((rows, d_ref.shape[1]), BF16)
    accs = lax.fori_loop(0, nch, body, (zero,) * n_acc)
    total = (accs[0] + accs[1]) + (accs[2] + accs[3])
    return jnp.sum(total.astype(F32), axis=0, keepdims=True)


def _digit_search(count_ge, steps, lo, want):
    c_hi = jnp.zeros_like(want)
    for make in steps:
        cand, cand_cmp = make(lo)
        cnt = count_ge(cand_cmp)
        ok = cnt >= want
        lo = jnp.where(ok, cand, lo)
        c_hi = jnp.where(ok, c_hi, cnt)
    return lo, c_hi


DIGIT_BITS = 8


def _top_digit(key, valid):
    return jnp.where(valid, ((key >> (32 - DIGIT_BITS)) + (1 << (DIGIT_BITS - 1))).astype(F32), -1.0).astype(BF16)


def _topk_threshold(keys_ref, dg_ref, nch, n_sel, n_causal):
    lanes = keys_ref.shape[1]
    want = jnp.full((1, lanes), float(n_sel), F32)

    def digit_step(bit):
        def make(lo):
            cand = lo + float(1 << bit)
            return cand, cand.astype(BF16)
        return make

    prefix = None
    for shift in reversed(range(0, 32, DIGIT_BITS)):
        if prefix is not None:
            def prep(c, carry, shift=shift, prefix=prefix):
                base = pl.multiple_of(c * KEY_CHUNK, KEY_CHUNK)
                key = keys_ref[pl.ds(base, KEY_CHUNK), :]
                digit = ((key >> shift) & jnp.int32((1 << DIGIT_BITS) - 1)).astype(F32)
                dg_ref[pl.ds(base, KEY_CHUNK), :] = jnp.where(
                    (key >> (shift + DIGIT_BITS)) == prefix, digit, -1.0).astype(BF16)
                return carry

            lax.fori_loop(0, nch, prep, 0)
        dig, above = _digit_search(functools.partial(_count_ge_packed, dg_ref, nch),
                                   [digit_step(b) for b in reversed(range(DIGIT_BITS))],
                                   jnp.zeros((1, lanes), F32), want)
        want = want - above
        if prefix is None:
            prefix = dig.astype(I32) - (1 << (DIGIT_BITS - 1))
        else:
            prefix = lax.shift_left(prefix, DIGIT_BITS) | dig.astype(I32)

    return jnp.where(n_causal <= n_sel, jnp.int32(INT_MIN + 1), prefix)


def _attn_kernel(n_sel, it_ref, ki_ref, q_ref, k_ref, vt_ref, o_ref,
                 keys_ref, dg_ref, rhs_ref, acc_ref, lg_ref):
    i = pl.program_id(1)
    nch = lax.shift_right_logical(i * Q_BLOCK + Q_BLOCK + KEY_CHUNK - 1, _log2(KEY_CHUNK))
    q0 = i * Q_BLOCK
    KC = KEY_CHUNK
    row = lax.broadcasted_iota(I32, (KC, Q_BLOCK), 0)
    lane = lax.broadcasted_iota(I32, (KC, Q_BLOCK), 1)
    qpos = q0 + lane

    qcat = jnp.concatenate(
        [it_ref[h * IDX_DIM:(h + 1) * IDX_DIM, :] for h in range(IDX_HEADS)], axis=1).astype(BF16)
    w_all = it_ref[IDX_HEADS * IDX_DIM:IDX_HEADS * IDX_DIM + IDX_HEADS, :] * (
        (IDX_HEADS ** -0.5) * (IDX_DIM ** -0.5))

    def score_body(c, carry):
        base = pl.multiple_of(c * KC, KC)
        kic = ki_ref[pl.ds(base, KC), :].astype(BF16)
        s = _dot(kic, qcat)
        sc = jnp.zeros((KC, Q_BLOCK), F32)
        for h in range(IDX_HEADS):
            sc = sc + jnp.maximum(s[:, h * Q_BLOCK:(h + 1) * Q_BLOCK], 0.0) * w_all[h:h + 1, :]
        bits = lax.bitcast_convert_type(sc, I32)
        bits = jnp.where(bits == jnp.int32(INT_MIN), jnp.int32(0), bits)
        causal = base + row <= qpos
        key = bits ^ ((bits >> 31) & jnp.int32(0x7FFFFFFF))
        keys_ref[pl.ds(base, KC), :] = jnp.where(causal, key, jnp.int32(INT_MIN))
        dg_ref[pl.ds(base, KC), :] = _top_digit(key, causal)
        return carry

    lax.fori_loop(0, nch, score_body, 0)

    n_causal = q0 + lax.broadcasted_iota(I32, (1, Q_BLOCK), 1) + 1
    tau = _topk_threshold(keys_ref, dg_ref, nch, n_sel, n_causal)

    q = q_ref[...]
    lane_q = lax.broadcasted_iota(I32, (Q_BLOCK, 2 * HEAD_DIM), 1)
    for p in range(ATTN_HEADS // 2):
        qp = q[:, p * 2 * HEAD_DIM:(p + 1) * 2 * HEAD_DIM]
        rhs_ref[p, 0:Q_BLOCK, :] = jnp.where(lane_q < HEAD_DIM, qp, jnp.zeros_like(qp))
        rhs_ref[p, Q_BLOCK:2 * Q_BLOCK, :] = jnp.where(lane_q >= HEAD_DIM, qp, jnp.zeros_like(qp))
    acc_ref[...] = jnp.zeros_like(acc_ref)

    def stage_a_prep(c):
        base = pl.multiple_of(c * KC, KC)
        tk = (base + row - (q0 + Q_BLOCK - 1)).astype(F32)
        tkm = jnp.where(keys_ref[pl.ds(base, KC), :] >= tau, tk, NEG_BIG)
        return tkm, k_ref[pl.ds(base, KC), :]

    def stage_a_pair(c, p, tkm, kc):
        l2 = _dot_nt(kc[:, p * 2 * HEAD_DIM:(p + 1) * 2 * HEAD_DIM], rhs_ref[p])
        mcs = []
        for hh in range(2):
            h = 2 * p + hh
            slope = 2.0 ** (-8.0 * (h + 1) / ATTN_HEADS)
            lg = l2[:, hh * Q_BLOCK:(hh + 1) * Q_BLOCK] + slope * tkm
            lg_ref[h] = lg
            mcs.append(jnp.max(_tree(lg, jnp.maximum), axis=0, keepdims=True))
        return mcs

    def stage_b_pair(c, p, m_new, alpha):
        sums = []
        for hh in range(2):
            h = 2 * p + hh
            pr = jnp.exp(lg_ref[h] - m_new[h:h + 1, :])
            sums.append(jnp.sum(_tree(pr, jnp.add), axis=0, keepdims=True))
            pv = _dot(vt_ref[c, h * HEAD_DIM:(h + 1) * HEAD_DIM, :], pr.astype(BF16))
            acc_ref[h * HEAD_DIM:(h + 1) * HEAD_DIM, :] = (
                acc_ref[h * HEAD_DIM:(h + 1) * HEAD_DIM, :] * alpha[h:h + 1, :] + pv)
        return sums

    def stage_b_all(c, m_old, l_old, mc):
        m_new = jnp.maximum(m_old, mc)
        alpha = jnp.exp(m_old - m_new)
        return m_new, alpha

    tkm0, kc0 = stage_a_prep(0)
    mc0 = jnp.concatenate(sum([stage_a_pair(0, p, tkm0, kc0) for p in range(ATTN_HEADS // 2)], []), axis=0)

    def attn_body(c, carry):
        m_old, l_old, mc_prev = carry
        m_new, alpha = stage_b_all(c - 1, m_old, l_old, mc_prev)
        tkm, kc = stage_a_prep(c)
        mcs, sums = [], []
        for p in range(ATTN_HEADS // 2):
            sums += stage_b_pair(c - 1, p, m_new, alpha)
            mcs += stage_a_pair(c, p, tkm, kc)
        return m_new, l_old * alpha + jnp.concatenate(sums, axis=0), jnp.concatenate(mcs, axis=0)

    m0 = jnp.full((ATTN_HEADS, Q_BLOCK), NEG_BIG, F32)
    l0 = jnp.zeros((ATTN_HEADS, Q_BLOCK), F32)
    m_old, l_old, mc_prev = lax.fori_loop(1, nch, attn_body, (m0, l0, mc0))
    m_new, alpha = stage_b_all(nch - 1, m_old, l_old, mc_prev)
    sums = sum([stage_b_pair(nch - 1, p, m_new, alpha) for p in range(ATTN_HEADS // 2)], [])
    l_fin = l_old * alpha + jnp.concatenate(sums, axis=0)

    outs = []
    for h in range(ATTN_HEADS):
        outs.append(acc_ref[h * HEAD_DIM:(h + 1) * HEAD_DIM, :] / l_fin[h:h + 1, :])
    o_ref[...] = jnp.concatenate(outs, axis=0).T


def _attn(it, ki, q, k, vt, B, S):
    T = B * S
    nb = S // Q_BLOCK
    n_sel = min(INDEX_TOPK_MAX, S // 4)
    return pl.pallas_call(
        functools.partial(_attn_kernel, n_sel),
        grid=(B, nb),
        in_specs=[
            pl.BlockSpec((IDXT_ROWS, Q_BLOCK), lambda b, i: (0, b * nb + i)),
            pl.BlockSpec((S, IDX_DIM), lambda b, i: (b, 0)),
            pl.BlockSpec((Q_BLOCK, ATTN_DIM), lambda b, i: (b * nb + i, 0)),
            pl.BlockSpec((S, ATTN_DIM), lambda b, i: (b, 0)),
            pl.BlockSpec((S // KEY_CHUNK, ATTN_DIM, KEY_CHUNK), lambda b, i: (b, 0, 0)),
        ],
        out_specs=pl.BlockSpec((Q_BLOCK, ATTN_DIM), lambda b, i: (b * nb + i, 0)),
        out_shape=jax.ShapeDtypeStruct((T, ATTN_DIM), F32),
        scratch_shapes=[
            pltpu.VMEM((S, Q_BLOCK), I32),
            pltpu.VMEM((S, Q_BLOCK), BF16),
            pltpu.VMEM((ATTN_HEADS // 2, 2 * Q_BLOCK, 2 * HEAD_DIM), BF16),
            pltpu.VMEM((ATTN_DIM, Q_BLOCK), F32),
            pltpu.VMEM((ATTN_HEADS, KEY_CHUNK, Q_BLOCK), F32),
        ],
        compiler_params=_cparams(("arbitrary", "arbitrary")),
        name="attn",
    )(it, ki, q, k, vt)


def _mixout_kernel(tiles_per_seq, a_ref, halo_ref, y_ref, x_ref, cw_ref, wpool_ref, ps_ref,
                   wout_ref, g_ref, b_ref, o_ref, ext_ref):
    i = pl.program_id(0)
    tm = TM_MIX
    first = lax.rem(i, tiles_per_seq) == 0
    halo = jnp.where(first, 0.0, halo_ref[...])
    a = a_ref[...]
    h_c, gb_c, gc_c, p_c = (a[:, j * CONV_DIM:(j + 1) * CONV_DIM] for j in range(4))

    t0 = 2 * HALO
    n_ext = t0 + tm
    ext_ref[0:HALO, :] = jnp.zeros((HALO, CONV_DIM), F32)

    ext_ref[HALO:t0, :] = halo[:, 2 * CONV_DIM:3 * CONV_DIM] * halo[:, 0:CONV_DIM]
    u = gc_c * h_c
    ext_ref[t0:n_ext, :] = u
    cw = cw_ref[...]
    conv = cw[2:3, :] * u
    conv = conv + cw[1:2, :] * ext_ref[t0 - 1:n_ext - 1, :]
    conv = conv + cw[0:1, :] * ext_ref[t0 - 2:n_ext - 2, :]
    y_conv = gb_c * conv

    ext_ref[HALO:t0, :] = halo[:, OFF_P:OFF_P + POOL_DIM]
    ext_ref[t0:n_ext, :] = p_c
    sums = {}
    step = 1
    while step < POOL_WINDOWS[-1]:
        cur = ext_ref[HALO:n_ext, :] + ext_ref[HALO - step:n_ext - step, :]
        ext_ref[HALO:n_ext, :] = cur
        step *= 2
        sums[step] = ext_ref[t0:n_ext, :]
    tpos = (lax.rem(i, tiles_per_seq) * tm + lax.broadcasted_iota(I32, (tm, POOL_DIM), 0) + 1).astype(F32)
    grp = lax.shift_right_logical(lax.broadcasted_iota(I32, (tm, POOL_DIM), 1), _log2(POOL_GROUP))
    mean = jnp.zeros((tm, POOL_DIM), F32)
    for gi, w in enumerate(POOL_WINDOWS):
        mean = jnp.where(grp == gi, sums[w] / jnp.minimum(tpos, float(w)), mean)
    mixed = mean - p_c
    y_pool = _dot(mixed.astype(BF16), wpool_ref[...]) * ps_ref[...]

    mix = _dot(y_conv.astype(BF16), wout_ref[0:CONV_DIM, :])
    mix = mix + _dot(y_pool.astype(BF16), wout_ref[CONV_DIM:CONV_DIM + POOL_DIM, :])
    mix = mix + _dot(y_ref[...].astype(BF16), wout_ref[CONV_DIM + POOL_DIM:D_MODEL, :])
    z = DN_ALPHA * x_ref[...] + mix
    o_ref[...] = _layernorm_rows(z, g_ref[...], b_ref[...])


def _mixout(a, y_attn, x, cw, wpool_bd, ps, wout, g, b, S):
    T = a.shape[0]
    nt = T // TM_MIX
    tps = S // TM_MIX
    hb = TM_MIX // HALO
    full = lambda i: (0, 0)
    return pl.pallas_call(
        functools.partial(_mixout_kernel, tps),
        grid=(nt,),
        in_specs=[
            pl.BlockSpec((TM_MIX, OFF_Q), lambda i: (i, 0)),
            pl.BlockSpec((HALO, OFF_Q), lambda i: (jnp.maximum(i * hb - 1, 0), 0)),
            pl.BlockSpec((TM_MIX, ATTN_DIM), lambda i: (i, 0)),
            pl.BlockSpec((TM_MIX, D_MODEL), lambda i: (i, 0)),
            pl.BlockSpec((CONV_WIDTH, CONV_DIM), full),
            pl.BlockSpec((POOL_DIM, POOL_DIM), full),
            pl.BlockSpec((1, POOL_DIM), full),
            pl.BlockSpec((D_MODEL, D_MODEL), full),
            pl.BlockSpec((1, D_MODEL), full),
            pl.BlockSpec((1, D_MODEL), full),
        ],
        out_specs=pl.BlockSpec((TM_MIX, D_MODEL), lambda i: (i, 0)),
        out_shape=jax.ShapeDtypeStruct((T, D_MODEL), F32),
        scratch_shapes=[pltpu.VMEM((2 * HALO + TM_MIX, CONV_DIM), F32)],
        compiler_params=_cparams(("arbitrary",)),
        name="mixout",
    )(a, a, y_attn, x, cw, wpool_bd, ps, wout, g, b)


def _router_kernel(x_ref, rw_ref, rb_ref, idx_ref, gate_ref, cnt_ref):
    i = pl.program_id(0)
    tm = TM_ROUTE
    logits = lax.dot_general(rw_ref[...], x_ref[...], (((1,), (1,)), ((), ())),
                             precision=lax.Precision.HIGHEST,
                             preferred_element_type=F32) + rb_ref[...]
    erow = lax.broadcasted_iota(I32, (N_EXPERTS, tm), 0).astype(F32)
    work = logits
    vals, idxs = [], []
    multi = jnp.zeros((N_EXPERTS, tm), F32)
    for _ in range(TOP_K):
        mx = jnp.max(work, axis=0, keepdims=True)
        pick = jnp.min(jnp.where(work == mx, erow, float(N_EXPERTS)), axis=0, keepdims=True)
        hit = erow == pick
        work = jnp.where(hit, -jnp.inf, work)
        multi = multi + hit.astype(F32)
        vals.append(mx)
        idxs.append(pick)
    es = [jnp.exp(v - vals[0]) for v in vals]
    den = es[0] + es[1] + es[2] + es[3]
    idx_ref[...] = jnp.concatenate(idxs, axis=0).astype(I32)
    gate_ref[...] = jnp.concatenate([e / den for e in es] + [jnp.zeros((SUBLANES - TOP_K, tm), F32)], axis=0)
    part = multi[:, 0:LANES]
    for j in range(1, tm // LANES):
        part = part + multi[:, j * LANES:(j + 1) * LANES]

    @pl.when(i == 0)
    def _():
        cnt_ref[...] = jnp.zeros_like(cnt_ref)

    cnt_ref[...] += part


def _router(x1, rwT, rb):
    T = x1.shape[0]
    nt = T // TM_ROUTE
    full = lambda i: (0, 0)
    return pl.pallas_call(
        _router_kernel,
        grid=(nt,),
        in_specs=[
            pl.BlockSpec((TM_ROUTE, D_MODEL), lambda i: (i, 0)),
            pl.BlockSpec((N_EXPERTS, D_MODEL), full),
            pl.BlockSpec((N_EXPERTS, 1), full),
        ],
        out_specs=[
            pl.BlockSpec((TOP_K, TM_ROUTE), lambda i: (0, i)),
            pl.BlockSpec((SUBLANES, TM_ROUTE), lambda i: (0, i)),
            pl.BlockSpec((N_EXPERTS, LANES), full),
        ],
        out_shape=[
            jax.ShapeDtypeStruct((TOP_K, T), I32),
            jax.ShapeDtypeStruct((SUBLANES, T), F32),
            jax.ShapeDtypeStruct((N_EXPERTS, LANES), F32),
        ],
        compiler_params=_cparams(("arbitrary",)),
        name="router",
    )(x1, rwT, rb)


def _ranks_kernel(n_tiles_pad, idx_ref, cnt_ref, dest_ref, meta_ref, tri_ref, start_ref, carry_ref):
    i = pl.program_id(0)
    tm = TM_ROUTE
    erow = lax.broadcasted_iota(I32, (N_EXPERTS, LANES), 0)
    elane = lax.broadcasted_iota(I32, (N_EXPERTS, LANES), 1)

    @pl.when(i == 0)
    def _():
        cnt = jnp.sum(cnt_ref[...], axis=1, keepdims=True)
        cnt_i = jnp.broadcast_to(cnt, (N_EXPERTS, LANES)).astype(I32)
        padded = lax.shift_left(lax.shift_right_logical(cnt_i + (TM_MOE - 1), _log2(TM_MOE)), _log2(TM_MOE))
        r = lax.broadcasted_iota(I32, (N_EXPERTS, N_EXPERTS), 0)
        c = lax.broadcasted_iota(I32, (N_EXPERTS, N_EXPERTS), 1)
        low = (c <= r).astype(F32)
        pad_end = lax.dot_general(low, padded.astype(F32), (((1,), (0,)), ((), ())),
                                  precision=lax.Precision.HIGHEST, preferred_element_type=F32)
        pad_start = pad_end - padded.astype(F32)
        start_ref[...] = pad_start
        carry_ref[...] = jnp.zeros_like(carry_ref)
        a = lax.broadcasted_iota(I32, (tm, tm), 0)
        bcol = lax.broadcasted_iota(I32, (tm, tm), 1)
        tri_ref[...] = (a < bcol).astype(BF16)
        ntp = n_tiles_pad
        tstart = (lax.broadcasted_iota(I32, (N_EXPERTS, ntp), 1) * TM_MOE).astype(F32)
        pe = jnp.concatenate([pad_end] * (ntp // LANES), axis=1)
        texp = jnp.sum((pe <= tstart).astype(F32), axis=0, keepdims=True)
        texp = jnp.minimum(texp, float(N_EXPERTS - 1)).astype(I32)
        total = jnp.max(pad_end, axis=0, keepdims=True)
        n_used = lax.shift_right_logical(total.astype(I32), _log2(TM_MOE))
        zstart = jnp.sum(jnp.where(erow == elane, pad_start + cnt_i.astype(F32), 0.0),
                         axis=0, keepdims=True).astype(I32)
        zlen = jnp.sum(jnp.where(erow == elane, (padded - cnt_i).astype(F32), 0.0),
                       axis=0, keepdims=True).astype(I32)
        lanes_pad = jnp.zeros((1, ntp - LANES), I32)
        meta_ref[...] = jnp.concatenate(
            [texp,
             jnp.concatenate([n_used, lanes_pad], axis=1),
             jnp.concatenate([zstart, lanes_pad], axis=1),
             jnp.concatenate([zlen, lanes_pad], axis=1),
             jnp.zeros((SUBLANES - 4, ntp), I32)], axis=0)

    idx = idx_ref[...]
    erow_t = lax.broadcasted_iota(I32, (N_EXPERTS, tm), 0)
    hits = [erow_t == idx[k:k + 1, :] for k in range(TOP_K)]
    multi = hits[0].astype(F32)
    for k in range(1, TOP_K):
        multi = multi + hits[k].astype(F32)
    prefix = _dot(multi.astype(BF16), tri_ref[...])
    base = jnp.concatenate([carry_ref[...] + start_ref[...]] * (tm // LANES), axis=1)
    tot = prefix + base
    dest_ref[...] = jnp.concatenate(
        [jnp.sum(jnp.where(hits[k], tot, 0.0), axis=0, keepdims=True) for k in range(TOP_K)],
        axis=0).astype(I32)
    carry_ref[...] += jnp.broadcast_to(jnp.sum(multi, axis=1, keepdims=True), (N_EXPERTS, LANES))


def _ranks(idxT, cnt, n_tiles_pad):
    T = idxT.shape[1]
    nt = T // TM_ROUTE
    full = lambda i: (0, 0)
    return pl.pallas_call(
        functools.partial(_ranks_kernel, n_tiles_pad),
        grid=(nt,),
        in_specs=[
            pl.BlockSpec((TOP_K, TM_ROUTE), lambda i: (0, i)),
            pl.BlockSpec((N_EXPERTS, LANES), full),
        ],
        out_specs=[
            pl.BlockSpec((TOP_K, TM_ROUTE), lambda i: (0, i)),
            pl.BlockSpec((SUBLANES, n_tiles_pad), full),
        ],
        out_shape=[
            jax.ShapeDtypeStruct((TOP_K, T), I32),
            jax.ShapeDtypeStruct((SUBLANES, n_tiles_pad), I32),
        ],
        scratch_shapes=[
            pltpu.VMEM((TM_ROUTE, TM_ROUTE), BF16),
            pltpu.VMEM((N_EXPERTS, LANES), F32),
            pltpu.VMEM((N_EXPERTS, LANES), F32),
        ],
        compiler_params=_cparams(("arbitrary",)),
        name="ranks",
    )(idxT, cnt)


RT = D_MODEL // LANES
DMA_UNROLL = 8


def _row_copy(src_ref, s, dst_ref, d, sem):
    return pltpu.make_async_copy(src_ref.at[pl.ds(pl.multiple_of(s * RT, RT), RT)],
                                 dst_ref.at[pl.ds(pl.multiple_of(d * RT, RT), RT)], sem)


def _to_row_tiled(dst_ref, val, rows):
    for s in range(RT):
        dst_ref[pl.ds(s, rows, stride=RT), :] = val[:, s * LANES:(s + 1) * LANES]


def _from_row_tiled(src_ref, rows):
    return [src_ref[pl.ds(s, rows, stride=RT), :] for s in range(RT)]


def _scatter_kernel(meta_ref, x_ref, dest_ref, xs_ref, stage_ref, zero_ref, sem):
    i = pl.program_id(0)
    tm = TM_ROUTE

    @pl.when(i == 0)
    def _():
        zero_ref[...] = jnp.zeros_like(zero_ref)

        def zcopy(row, nrows):
            off = pl.multiple_of(row * RT, RT)
            return pltpu.make_async_copy(zero_ref.at[pl.ds(0, nrows * RT)],
                                         xs_ref.at[pl.ds(off, nrows * RT)], sem)

        def pad_fill(wait):
            def body(e, c):
                row = meta_ref[2, e]
                plen = meta_ref[3, e]
                for bit in reversed(range(_log2(TM_MOE))):
                    size = 1 << bit
                    has = (plen & size) != 0

                    @pl.when(has)
                    def _():
                        cp = zcopy(row, size)
                        cp.wait() if wait else cp.start()

                    row = row + jnp.where(has, size, 0)
                return c
            lax.fori_loop(0, N_EXPERTS, body, 0)

        def tail_fill(wait):
            def body(j, c):
                cp = zcopy(j * TM_MOE, TM_MOE)
                cp.wait() if wait else cp.start()
                return c
            lax.fori_loop(meta_ref[1, 0], xs_ref.shape[0] // (TM_MOE * RT), body, 0)

        pad_fill(False)
        tail_fill(False)
        pad_fill(True)
        tail_fill(True)

    _to_row_tiled(stage_ref, x_ref[...], tm)

    def start(t, c):
        for k in range(TOP_K):
            _row_copy(stage_ref, t, xs_ref, dest_ref[k, t], sem).start(priority=k % 2)
        return c

    lax.fori_loop(0, tm, start, 0, unroll=DMA_UNROLL)
    for k in range(TOP_K):
        pltpu.make_async_copy(stage_ref, xs_ref.at[pl.ds(0, tm * RT)], sem).wait()


def _scatter(meta, x1, destT, n_rows_alloc):
    T = x1.shape[0]
    nt = T // TM_ROUTE
    return pl.pallas_call(
        _scatter_kernel,
        grid_spec=pltpu.PrefetchScalarGridSpec(
            num_scalar_prefetch=1,
            grid=(nt,),
            in_specs=[
                pl.BlockSpec((TM_ROUTE, D_MODEL), lambda i, m: (i, 0)),
                pl.BlockSpec((TOP_K, TM_ROUTE), lambda i, m: (0, i), memory_space=pltpu.SMEM),
            ],
            out_specs=pl.BlockSpec(memory_space=pl.ANY),
            scratch_shapes=[
                pltpu.VMEM((TM_ROUTE * RT, LANES), F32),
                pltpu.VMEM((TM_MOE * RT, LANES), F32),
                pltpu.SemaphoreType.DMA(()),
            ],
        ),
        out_shape=jax.ShapeDtypeStruct((n_rows_alloc * RT, LANES), F32),
        compiler_params=_cparams(("arbitrary",)),
        name="scatter",
    )(meta, x1, destT)


def _gmm_kernel(meta_ref, xs_ref, wgu_ref, bgu_ref, wd_ref, bd_ref, ys_ref, wgu_b, wd_b, lhs_ref, act_ref):
    j = pl.program_id(0)
    n_used = meta_ref[1, 0]
    e_now = meta_ref[0, j]
    e_prev = meta_ref[0, jnp.maximum(j - 1, 0)]
    used = j < n_used

    @pl.when(used & ((j == 0) | (e_now != e_prev)))
    def _():
        wgu_b[...] = wgu_ref[0, 0].astype(BF16)
        wd_b[...] = wd_ref[0, 0].astype(BF16)

    @pl.when(used)
    def _():
        for s, piece in enumerate(_from_row_tiled(xs_ref, TM_MOE)):
            lhs_ref[:, s * LANES:(s + 1) * LANES] = piece.astype(BF16)
        x = lhs_ref[...]
        nc = 256
        for c in range(D_FF // nc):
            gate = _dot(x, wgu_b[:, c * nc:(c + 1) * nc]) + bgu_ref[0, 0, :, c * nc:(c + 1) * nc]
            up = (_dot(x, wgu_b[:, D_FF + c * nc:D_FF + (c + 1) * nc])
                  + bgu_ref[0, 0, :, D_FF + c * nc:D_FF + (c + 1) * nc])
            gate = jnp.minimum(gate, SWIGLU_LIMIT)
            up = jnp.clip(up, -SWIGLU_LIMIT, SWIGLU_LIMIT)
            act = gate * jax.nn.sigmoid(SWIGLU_ALPHA * gate) * (up + 1.0)
            act_ref[:, c * nc:(c + 1) * nc] = act.astype(BF16)
        _to_row_tiled(ys_ref, _dot(act_ref[...], wd_b[...]) + bd_ref[0, 0], TM_MOE)

    @pl.when(jnp.logical_not(used))
    def _():
        ys_ref[...] = jnp.zeros_like(ys_ref)


def _gmm(meta, xs, w_gu, b_gu, w_down, b_down, n_tiles, layer):
    last = lambda m: jnp.maximum(m[1, 0] - 1, 0)
    return pl.pallas_call(
        _gmm_kernel,
        grid_spec=pltpu.PrefetchScalarGridSpec(
            num_scalar_prefetch=1,
            grid=(n_tiles,),
            in_specs=[
                pl.BlockSpec((TM_MOE * RT, LANES), lambda j, m: (jnp.minimum(j, last(m)), 0)),
                pl.BlockSpec((1, 1, D_MODEL, 2 * D_FF), lambda j, m: (layer, m[0, j], 0, 0)),
                pl.BlockSpec((1, 1, 1, 2 * D_FF), lambda j, m: (layer, m[0, j], 0, 0)),
                pl.BlockSpec((1, 1, D_FF, D_MODEL), lambda j, m: (layer, m[0, j], 0, 0)),
                pl.BlockSpec((1, 1, 1, D_MODEL), lambda j, m: (layer, m[0, j], 0, 0)),
            ],
            out_specs=pl.BlockSpec((TM_MOE * RT, LANES), lambda j, m: (j, 0)),
            scratch_shapes=[
                pltpu.VMEM((D_MODEL, 2 * D_FF), BF16),
                pltpu.VMEM((D_FF, D_MODEL), BF16),
                pltpu.VMEM((TM_MOE, D_MODEL), BF16),
                pltpu.VMEM((TM_MOE, D_FF), BF16),
            ],
        ),
        out_shape=jax.ShapeDtypeStruct((n_tiles * TM_MOE * RT, LANES), F32),
        compiler_params=_cparams(("arbitrary",)),
        name="gmm",
    )(meta, xs, w_gu, b_gu, w_down, b_down)


def _combine_kernel(x_ref, dest_ref, gate_ref, ys_ref, g_ref, b_ref, o_ref, ob_ref, buf_ref, sem):
    tm = TM_COMB

    def start(t, c):
        for k in range(TOP_K):
            _row_copy(ys_ref, dest_ref[k, t], buf_ref.at[k], t, sem).start(priority=k % 2)
        return c

    lax.fori_loop(0, tm, start, 0, unroll=DMA_UNROLL)
    for k in range(TOP_K):
        pltpu.make_async_copy(ys_ref.at[pl.ds(0, tm * RT)], buf_ref.at[k], sem).wait()
    gates = jnp.concatenate(
        [gate_ref[...], jnp.zeros((LANES - SUBLANES, tm), F32)], axis=0).T
    pieces = None
    for k in range(TOP_K):
        gk = gates[:, k:k + 1]
        rows = [p * gk for p in _from_row_tiled(buf_ref.at[k], tm)]
        pieces = rows if pieces is None else [a + r for a, r in zip(pieces, rows)]
    z = DN_ALPHA * x_ref[...] + jnp.concatenate(pieces, axis=1)
    out = _layernorm_rows(z, g_ref[...], b_ref[...])
    o_ref[...] = out
    ob_ref[...] = out.astype(BF16)


def _combine(x1, destT, gateT, ys, g, b):
    T = x1.shape[0]
    nt = T // TM_COMB
    full = lambda i: (0, 0)
    return pl.pallas_call(
        _combine_kernel,
        grid=(nt,),
        in_specs=[
            pl.BlockSpec((TM_COMB, D_MODEL), lambda i: (i, 0)),
            pl.BlockSpec((TOP_K, TM_COMB), lambda i: (0, i), memory_space=pltpu.SMEM),
            pl.BlockSpec((SUBLANES, TM_COMB), lambda i: (0, i)),
            pl.BlockSpec(memory_space=pl.ANY),
            pl.BlockSpec((1, D_MODEL), full),
            pl.BlockSpec((1, D_MODEL), full),
        ],
        out_specs=[
            pl.BlockSpec((TM_COMB, D_MODEL), lambda i: (i, 0)),
            pl.BlockSpec((TM_COMB, D_MODEL), lambda i: (i, 0)),
        ],
        out_shape=[
            jax.ShapeDtypeStruct((T, D_MODEL), F32),
            jax.ShapeDtypeStruct((T, D_MODEL), BF16),
        ],
        scratch_shapes=[
            pltpu.VMEM((TOP_K, TM_COMB * RT, LANES), F32),
            pltpu.SemaphoreType.DMA(()),
        ],
        compiler_params=_cparams(("arbitrary",)),
        name="combine",
    )(x1, destT, gateT, ys, g, b)


def _block_diag(w):
    g, c, _ = w.shape
    out = jnp.zeros((g * c, g * c), w.dtype)
    for i in range(g):
        out = out.at[i * c:(i + 1) * c, i * c:(i + 1) * c].set(w[i])
    return out


def kernel(x, w_in, conv_w, w_pool, pool_scale, idx_kn_g, idx_kn_b, w_out, ln1_g, ln1_b, router_w,
           router_b, w_gu, b_gu, w_down, b_down, ln2_g, ln2_b):
    B, S, D = x.shape
    T = B * S
    depth = w_in.shape[0]
    n_pairs = T * TOP_K
    n_tiles = (n_pairs + N_EXPERTS * (TM_MOE - 1)) // TM_MOE + 1
    n_tiles_pad = ((n_tiles + LANES - 1) // LANES) * LANES
    n_rows_alloc = (n_tiles + 1) * TM_MOE

    xf = x.reshape(T, D)
    xb = xf.astype(BF16)
    for l in range(depth):
        wl = w_in[l]
        wn = jnp.concatenate(
            [wl[:, 0:OFF_V], wl[:, OFF_KI:OFF_WI], jnp.zeros((D, LANES - IDX_DIM), F32)], axis=1).astype(BF16)
        wt = jnp.concatenate(
            [wl[:, OFF_V:OFF_QI], wl[:, OFF_QI:OFF_KI], wl[:, OFF_WI:D_IN],
             jnp.zeros((D, IDXT_ROWS - IDX_HEADS * IDX_DIM - IDX_HEADS), F32)], axis=1).T.astype(BF16)
        a, q, k, ki, vt, it = _inproj(xb, wn, wt, idx_kn_g[l].reshape(1, -1), idx_kn_b[l].reshape(1, -1))
        y_attn = _attn(it, ki, q, k, vt, B, S)
        x1 = _mixout(a, y_attn, xf, conv_w[l], _block_diag(w_pool[l]).astype(BF16),
                     pool_scale[l].reshape(1, -1), w_out[l].astype(BF16),
                     ln1_g[l].reshape(1, -1), ln1_b[l].reshape(1, -1), S)
        idxT, gateT, cnt = _router(x1, router_w[l].T, router_b[l].reshape(-1, 1))
        destT, meta = _ranks(idxT, cnt, n_tiles_pad)
        xs = _scatter(meta, x1, destT, n_rows_alloc)
        ys = _gmm(meta, xs, w_gu, b_gu.reshape(depth, N_EXPERTS, 1, -1), w_down,
                  b_down.reshape(depth, N_EXPERTS, 1, -1), n_tiles, l)
        xf, xb = _combine(x1, destT, gateT, ys, ln2_g[l].reshape(1, -1), ln2_b[l].reshape(1, -1))
    return xf.reshape(B, S, D)
```

```python
import functools

import jax
import jax.numpy as jnp
import numpy as np
from jax import lax
from jax.experimental import pallas as pl
from jax.experimental.pallas import tpu as pltpu

F32 = jnp.float32
BF16 = jnp.bfloat16
I32 = jnp.int32

D_MODEL = 1024
CONV_DIM = 256
CONV_WIDTH = 3
POOL_WINDOWS = (2, 4, 8, 16)
POOL_GROUP = 64
POOL_DIM = 256
ATTN_HEADS = 8
HEAD_DIM = 64
ATTN_DIM = 512
IDX_HEADS = 8
IDX_DIM = 32
INDEX_TOPK_MAX = 256
Q_BLOCK = 128
N_EXPERTS = 32
TOP_K = 4
D_FF = 1024
SWIGLU_ALPHA = 1.702
SWIGLU_LIMIT = 7.0
LN_EPS = 1e-5
DEPTH = 2
DN_ALPHA = (2.0 * DEPTH) ** 0.25

OFF_P = 3 * CONV_DIM
OFF_Q = OFF_P + POOL_DIM
OFF_K = OFF_Q + ATTN_DIM
OFF_V = OFF_K + ATTN_DIM
OFF_QI = OFF_V + ATTN_DIM
OFF_KI = OFF_QI + IDX_HEADS * IDX_DIM
OFF_WI = OFF_KI + IDX_DIM
D_IN = OFF_WI + IDX_HEADS

LANES = 128
SUBLANES = 8
VMEM_LIMIT = 56 * 1024 * 1024
KEY_CHUNK = 512
HALO = 16
TM_PROJ = 512
TM_MIX = 256
TM_ROUTE = 512
TM_MOE = 512
TM_COMB = 256
IDXT_ROWS = IDX_HEADS * IDX_DIM + 16
WT_ROWS = ATTN_DIM + IDXT_ROWS
INT_MIN = -(2 ** 31)
NEG_BIG = -1e30


def _cparams(sem, flags=None):
    return pltpu.CompilerParams(dimension_semantics=sem, vmem_limit_bytes=VMEM_LIMIT, flags=flags)


def _log2(n):
    k = n.bit_length() - 1
    assert 1 << k == n
    return k


def _dot(a, b):
    return jnp.dot(a, b, preferred_element_type=F32)


def _dot_nt(a, b):
    return lax.dot_general(a, b, (((1,), (1,)), ((), ())), preferred_element_type=F32)


def _tree(x, op):
    parts = [x[j * SUBLANES:(j + 1) * SUBLANES, :] for j in range(x.shape[0] // SUBLANES)]
    while len(parts) > 1:
        nxt = [op(parts[j], parts[j + 1]) for j in range(0, len(parts) - 1, 2)]
        if len(parts) % 2:
            nxt.append(parts[-1])
        parts = nxt
    return parts[0]


def _layernorm_rows(z, g, b):
    mu = jnp.mean(z, axis=-1, keepdims=True)
    d = z - mu
    var = jnp.mean(d * d, axis=-1, keepdims=True)
    return d * lax.rsqrt(var + LN_EPS) * g + b


def _inproj_kernel(x_ref, wn_ref, wt_ref, kg_ref, kb_ref,
                   a_ref, q_ref, k_ref, ki_ref, vt_ref, it_ref):
    x = x_ref[...]
    a_ref[...] = _dot(x, wn_ref[:, 0:OFF_Q])
    q_ref[...] = (_dot(x, wn_ref[:, OFF_Q:OFF_K]) * (HEAD_DIM ** -0.5)).astype(BF16)
    k_ref[...] = _dot(x, wn_ref[:, OFF_K:OFF_V]).astype(BF16)
    ki = _dot(x, wn_ref[:, OFF_V:OFF_V + LANES])[:, 0:IDX_DIM]
    ki_ref[...] = _layernorm_rows(ki, kg_ref[...], kb_ref[...])
    t = _dot_nt(wt_ref[...], x)
    for c in range(TM_PROJ // KEY_CHUNK):
        vt_ref[c] = t[0:ATTN_DIM, c * KEY_CHUNK:(c + 1) * KEY_CHUNK].astype(BF16)
    it_ref[...] = t[ATTN_DIM:WT_ROWS, :]


def _inproj(xb, wn, wt, kg, kb):
    T = xb.shape[0]
    nt = T // TM_PROJ
    cpt = TM_PROJ // KEY_CHUNK
    full = lambda i: (0, 0)
    return pl.pallas_call(
        _inproj_kernel,
        grid=(nt,),
        in_specs=[
            pl.BlockSpec((TM_PROJ, D_MODEL), lambda i: (i, 0)),
            pl.BlockSpec(wn.shape, full),
            pl.BlockSpec(wt.shape, full),
            pl.BlockSpec((1, IDX_DIM), full),
            pl.BlockSpec((1, IDX_DIM), full),
        ],
        out_specs=[
            pl.BlockSpec((TM_PROJ, OFF_Q), lambda i: (i, 0)),
            pl.BlockSpec((TM_PROJ, ATTN_DIM), lambda i: (i, 0)),
            pl.BlockSpec((TM_PROJ, ATTN_DIM), lambda i: (i, 0)),
            pl.BlockSpec((TM_PROJ, IDX_DIM), lambda i: (i, 0)),
            pl.BlockSpec((cpt, ATTN_DIM, KEY_CHUNK), lambda i: (i, 0, 0)),
            pl.BlockSpec((IDXT_ROWS, TM_PROJ), lambda i: (0, i)),
        ],
        out_shape=[
            jax.ShapeDtypeStruct((T, OFF_Q), F32),
            jax.ShapeDtypeStruct((T, ATTN_DIM), BF16),
            jax.ShapeDtypeStruct((T, ATTN_DIM), BF16),
            jax.ShapeDtypeStruct((T, IDX_DIM), F32),
            jax.ShapeDtypeStruct((T // KEY_CHUNK, ATTN_DIM, KEY_CHUNK), BF16),
            jax.ShapeDtypeStruct((IDXT_ROWS, T), F32),
        ],
        compiler_params=_cparams(("arbitrary",)),
        name="inproj",
    )(xb, wn, wt, kg, kb)


def _attn_kernel(n_sel, it_ref, ki_ref, q_ref, k_ref, vt_ref, o_ref,
                 keys_ref, rhs_ref, acc_ref, lg_ref):
    i = pl.program_id(1)
    nch = lax.shift_right_logical(i * Q_BLOCK + Q_BLOCK + KEY_CHUNK - 1, _log2(KEY_CHUNK))
    q0 = i * Q_BLOCK
    KC = KEY_CHUNK
    row = lax.broadcasted_iota(I32, (KC, Q_BLOCK), 0)
    lane = lax.broadcasted_iota(I32, (KC, Q_BLOCK), 1)
    qpos = q0 + lane

    qcat = jnp.concatenate(
        [it_ref[h * IDX_DIM:(h + 1) * IDX_DIM, :] for h in range(IDX_HEADS)], axis=1).astype(BF16)
    w_all = it_ref[IDX_HEADS * IDX_DIM:IDX_HEADS * IDX_DIM + IDX_HEADS, :] * (
        (IDX_HEADS ** -0.5) * (IDX_DIM ** -0.5))

    def score_body(c, carry):
        base = pl.multiple_of(c * KC, KC)
        kic = ki_ref[pl.ds(base, KC), :].astype(BF16)
        s = _dot(kic, qcat)
        sc = jnp.zeros((KC, Q_BLOCK), F32)
        for h in range(IDX_HEADS):
            sc = sc + jnp.maximum(s[:, h * Q_BLOCK:(h + 1) * Q_BLOCK], 0.0) * w_all[h:h + 1, :]
        bits = lax.bitcast_convert_type(sc, I32)
        key = bits ^ ((bits >> 31) & jnp.int32(0x7FFFFFFF))
        key = jnp.where(bits == jnp.int32(INT_MIN), jnp.int32(0), key)
        key = jnp.where(base + row <= qpos, key, jnp.int32(INT_MIN))
        keys_ref[pl.ds(base, KC), :] = key
        return carry

    lax.fori_loop(0, nch, score_body, 0)

    def count_ge(cand):
        def body(c, acc):
            base = pl.multiple_of(c * KC, KC)
            blk = keys_ref[pl.ds(base, KC), :]
            hit = jnp.where(blk >= cand, 1.0, 0.0).astype(F32)
            return acc + _tree(hit, jnp.add)
        acc = lax.fori_loop(0, nch, body, jnp.zeros((SUBLANES, Q_BLOCK), F32))
        return jnp.sum(acc, axis=0, keepdims=True)

    want = jnp.float32(n_sel)
    zero_row = jnp.zeros((1, Q_BLOCK), I32)
    lo = jnp.where(count_ge(zero_row) >= want, zero_row, jnp.int32(INT_MIN))

    def bit_body(it, lo):
        cand = lo + lax.shift_left(jnp.int32(1), jnp.int32(30) - it)
        return jnp.where(count_ge(cand) >= want, cand, lo)

    lo = lax.fori_loop(0, 31, bit_body, lo)
    tau = jnp.maximum(lo, jnp.int32(INT_MIN + 1))

    q = q_ref[...]
    lane_q = lax.broadcasted_iota(I32, (Q_BLOCK, 2 * HEAD_DIM), 1)
    for p in range(ATTN_HEADS // 2):
        qp = q[:, p * 2 * HEAD_DIM:(p + 1) * 2 * HEAD_DIM]
        rhs_ref[p, 0:Q_BLOCK, :] = jnp.where(lane_q < HEAD_DIM, qp, jnp.zeros_like(qp))
        rhs_ref[p, Q_BLOCK:2 * Q_BLOCK, :] = jnp.where(lane_q >= HEAD_DIM, qp, jnp.zeros_like(qp))
    acc_ref[...] = jnp.zeros_like(acc_ref)

    def stage_a_prep(c):
        base = pl.multiple_of(c * KC, KC)
        tk = (base + row - (q0 + Q_BLOCK - 1)).astype(F32)
        tkm = jnp.where(keys_ref[pl.ds(base, KC), :] >= tau, tk, NEG_BIG)
        return tkm, k_ref[pl.ds(base, KC), :]

    def stage_a_pair(c, p, tkm, kc):
        l2 = _dot_nt(kc[:, p * 2 * HEAD_DIM:(p + 1) * 2 * HEAD_DIM], rhs_ref[p])
        mcs = []
        for hh in range(2):
            h = 2 * p + hh
            slope = 2.0 ** (-8.0 * (h + 1) / ATTN_HEADS)
            lg = l2[:, hh * Q_BLOCK:(hh + 1) * Q_BLOCK] + slope * tkm
            lg_ref[h] = lg
            mcs.append(jnp.max(_tree(lg, jnp.maximum), axis=0, keepdims=True))
        return mcs

    def stage_b_pair(c, p, m_new, alpha):
        sums = []
        for hh in range(2):
            h = 2 * p + hh
            pr = jnp.exp(lg_ref[h] - m_new[h:h + 1, :])
            sums.append(jnp.sum(_tree(pr, jnp.add), axis=0, keepdims=True))
            pv = _dot(vt_ref[c, h * HEAD_DIM:(h + 1) * HEAD_DIM, :], pr.astype(BF16))
            acc_ref[h * HEAD_DIM:(h + 1) * HEAD_DIM, :] = (
                acc_ref[h * HEAD_DIM:(h + 1) * HEAD_DIM, :] * alpha[h:h + 1, :] + pv)
        return sums

    def stage_b_all(c, m_old, l_old, mc):
        m_new = jnp.maximum(m_old, mc)
        alpha = jnp.exp(m_old - m_new)
        return m_new, alpha

    tkm0, kc0 = stage_a_prep(0)
    mc0 = jnp.concatenate(sum([stage_a_pair(0, p, tkm0, kc0) for p in range(ATTN_HEADS // 2)], []), axis=0)

    def attn_body(c, carry):
        m_old, l_old, mc_prev = carry
        m_new, alpha = stage_b_all(c - 1, m_old, l_old, mc_prev)
        tkm, kc = stage_a_prep(c)
        mcs, sums = [], []
        for p in range(ATTN_HEADS // 2):
            sums += stage_b_pair(c - 1, p, m_new, alpha)
            mcs += stage_a_pair(c, p, tkm, kc)
        return m_new, l_old * alpha + jnp.concatenate(sums, axis=0), jnp.concatenate(mcs, axis=0)

    m0 = jnp.full((ATTN_HEADS, Q_BLOCK), NEG_BIG, F32)
    l0 = jnp.zeros((ATTN_HEADS, Q_BLOCK), F32)
    m_old, l_old, mc_prev = lax.fori_loop(1, nch, attn_body, (m0, l0, mc0))
    m_new, alpha = stage_b_all(nch - 1, m_old, l_old, mc_prev)
    sums = sum([stage_b_pair(nch - 1, p, m_new, alpha) for p in range(ATTN_HEADS // 2)], [])
    l_fin = l_old * alpha + jnp.concatenate(sums, axis=0)

    outs = []
    for h in range(ATTN_HEADS):
        outs.append(acc_ref[h * HEAD_DIM:(h + 1) * HEAD_DIM, :] / l_fin[h:h + 1, :])
    o_ref[...] = jnp.concatenate(outs, axis=0).T


def _attn(it, ki, q, k, vt, B, S):
    T = B * S
    nb = S // Q_BLOCK
    n_sel = min(INDEX_TOPK_MAX, S // 4)
    return pl.pallas_call(
        functools.partial(_attn_kernel, n_sel),
        grid=(B, nb),
        in_specs=[
            pl.BlockSpec((IDXT_ROWS, Q_BLOCK), lambda b, i: (0, b * nb + i)),
            pl.BlockSpec((S, IDX_DIM), lambda b, i: (b, 0)),
            pl.BlockSpec((Q_BLOCK, ATTN_DIM), lambda b, i: (b * nb + i, 0)),
            pl.BlockSpec((S, ATTN_DIM), lambda b, i: (b, 0)),
            pl.BlockSpec((S // KEY_CHUNK, ATTN_DIM, KEY_CHUNK), lambda b, i: (b, 0, 0)),
        ],
        out_specs=pl.BlockSpec((Q_BLOCK, ATTN_DIM), lambda b, i: (b * nb + i, 0)),
        out_shape=jax.ShapeDtypeStruct((T, ATTN_DIM), F32),
        scratch_shapes=[
            pltpu.VMEM((S, Q_BLOCK), I32),
            pltpu.VMEM((ATTN_HEADS // 2, 2 * Q_BLOCK, 2 * HEAD_DIM), BF16),
            pltpu.VMEM((ATTN_DIM, Q_BLOCK), F32),
            pltpu.VMEM((ATTN_HEADS, KEY_CHUNK, Q_BLOCK), F32),
        ],
        compiler_params=_cparams(("arbitrary", "arbitrary")),
        name="attn",
    )(it, ki, q, k, vt)


def _mixout_kernel(tiles_per_seq, a_ref, halo_ref, y_ref, x_ref, cw_ref, wpool_ref, ps_ref,
                   wout_ref, g_ref, b_ref, o_ref, ext_ref):
    i = pl.program_id(0)
    tm = TM_MIX
    first = lax.rem(i, tiles_per_seq) == 0
    halo = jnp.where(first, 0.0, halo_ref[...])
    a = a_ref[...]
    h_c, gb_c, gc_c, p_c = (a[:, j * CONV_DIM:(j + 1) * CONV_DIM] for j in range(4))

    t0 = 2 * HALO
    n_ext = t0 + tm
    ext_ref[0:HALO, :] = jnp.zeros((HALO, CONV_DIM), F32)

    ext_ref[HALO:t0, :] = halo[:, 2 * CONV_DIM:3 * CONV_DIM] * halo[:, 0:CONV_DIM]
    u = gc_c * h_c
    ext_ref[t0:n_ext, :] = u
    cw = cw_ref[...]
    conv = cw[2:3, :] * u
    conv = conv + cw[1:2, :] * ext_ref[t0 - 1:n_ext - 1, :]
    conv = conv + cw[0:1, :] * ext_ref[t0 - 2:n_ext - 2, :]
    y_conv = gb_c * conv

    ext_ref[HALO:t0, :] = halo[:, OFF_P:OFF_P + POOL_DIM]
    ext_ref[t0:n_ext, :] = p_c
    sums = {}
    step = 1
    while step < POOL_WINDOWS[-1]:
        cur = ext_ref[HALO:n_ext, :] + ext_ref[HALO - step:n_ext - step, :]
        ext_ref[HALO:n_ext, :] = cur
        step *= 2
        sums[step] = ext_ref[t0:n_ext, :]
    tpos = (lax.rem(i, tiles_per_seq) * tm + lax.broadcasted_iota(I32, (tm, POOL_DIM), 0) + 1).astype(F32)
    grp = lax.shift_right_logical(lax.broadcasted_iota(I32, (tm, POOL_DIM), 1), _log2(POOL_GROUP))
    mean = jnp.zeros((tm, POOL_DIM), F32)
    for gi, w in enumerate(POOL_WINDOWS):
        mean = jnp.where(grp == gi, sums[w] / jnp.minimum(tpos, float(w)), mean)
    mixed = mean - p_c
    y_pool = _dot(mixed.astype(BF16), wpool_ref[...]) * ps_ref[...]

    mix = _dot(y_conv.astype(BF16), wout_ref[0:CONV_DIM, :])
    mix = mix + _dot(y_pool.astype(BF16), wout_ref[CONV_DIM:CONV_DIM + POOL_DIM, :])
    mix = mix + _dot(y_ref[...].astype(BF16), wout_ref[CONV_DIM + POOL_DIM:D_MODEL, :])
    z = DN_ALPHA * x_ref[...] + mix
    o_ref[...] = _layernorm_rows(z, g_ref[...], b_ref[...])


def _mixout(a, y_attn, x, cw, wpool_bd, ps, wout, g, b, S):
    T = a.shape[0]
    nt = T // TM_MIX
    tps = S // TM_MIX
    hb = TM_MIX // HALO
    full = lambda i: (0, 0)
    return pl.pallas_call(
        functools.partial(_mixout_kernel, tps),
        grid=(nt,),
        in_specs=[
            pl.BlockSpec((TM_MIX, OFF_Q), lambda i: (i, 0)),
            pl.BlockSpec((HALO, OFF_Q), lambda i: (jnp.maximum(i * hb - 1, 0), 0)),
            pl.BlockSpec((TM_MIX, ATTN_DIM), lambda i: (i, 0)),
            pl.BlockSpec((TM_MIX, D_MODEL), lambda i: (i, 0)),
            pl.BlockSpec((CONV_WIDTH, CONV_DIM), full),
            pl.BlockSpec((POOL_DIM, POOL_DIM), full),
            pl.BlockSpec((1, POOL_DIM), full),
            pl.BlockSpec((D_MODEL, D_MODEL), full),
            pl.BlockSpec((1, D_MODEL), full),
            pl.BlockSpec((1, D_MODEL), full),
        ],
        out_specs=pl.BlockSpec((TM_MIX, D_MODEL), lambda i: (i, 0)),
        out_shape=jax.ShapeDtypeStruct((T, D_MODEL), F32),
        scratch_shapes=[pltpu.VMEM((2 * HALO + TM_MIX, CONV_DIM), F32)],
        compiler_params=_cparams(("arbitrary",)),
        name="mixout",
    )(a, a, y_attn, x, cw, wpool_bd, ps, wout, g, b)


def _router_kernel(x_ref, rw_ref, rb_ref, idx_ref, gate_ref, cnt_ref):
    i = pl.program_id(0)
    tm = TM_ROUTE
    logits = lax.dot_general(rw_ref[...], x_ref[...], (((1,), (1,)), ((), ())),
                             precision=lax.Precision.HIGHEST,
                             preferred_element_type=F32) + rb_ref[...]
    erow = lax.broadcasted_iota(I32, (N_EXPERTS, tm), 0).astype(F32)
    work = logits
    vals, idxs = [], []
    multi = jnp.zeros((N_EXPERTS, tm), F32)
    for _ in range(TOP_K):
        mx = jnp.max(work, axis=0, keepdims=True)
        pick = jnp.min(jnp.where(work == mx, erow, float(N_EXPERTS)), axis=0, keepdims=True)
        hit = erow == pick
        work = jnp.where(hit, -jnp.inf, work)
        multi = multi + hit.astype(F32)
        vals.append(mx)
        idxs.append(pick)
    es = [jnp.exp(v - vals[0]) for v in vals]
    den = es[0] + es[1] + es[2] + es[3]
    idx_ref[...] = jnp.concatenate(idxs, axis=0).astype(I32)
    gate_ref[...] = jnp.concatenate([e / den for e in es] + [jnp.zeros((SUBLANES - TOP_K, tm), F32)], axis=0)
    part = multi[:, 0:LANES]
    for j in range(1, tm // LANES):
        part = part + multi[:, j * LANES:(j + 1) * LANES]

    @pl.when(i == 0)
    def _():
        cnt_ref[...] = jnp.zeros_like(cnt_ref)

    cnt_ref[...] += part


def _router(x1, rwT, rb):
    T = x1.shape[0]
    nt = T // TM_ROUTE
    full = lambda i: (0, 0)
    return pl.pallas_call(
        _router_kernel,
        grid=(nt,),
        in_specs=[
            pl.BlockSpec((TM_ROUTE, D_MODEL), lambda i: (i, 0)),
            pl.BlockSpec((N_EXPERTS, D_MODEL), full),
            pl.BlockSpec((N_EXPERTS, 1), full),
        ],
        out_specs=[
            pl.BlockSpec((TOP_K, TM_ROUTE), lambda i: (0, i)),
            pl.BlockSpec((SUBLANES, TM_ROUTE), lambda i: (0, i)),
            pl.BlockSpec((N_EXPERTS, LANES), full),
        ],
        out_shape=[
            jax.ShapeDtypeStruct((TOP_K, T), I32),
            jax.ShapeDtypeStruct((SUBLANES, T), F32),
            jax.ShapeDtypeStruct((N_EXPERTS, LANES), F32),
        ],
        compiler_params=_cparams(("arbitrary",)),
        name="router",
    )(x1, rwT, rb)


def _ranks_kernel(n_tiles_pad, idx_ref, cnt_ref, dest_ref, meta_ref, tri_ref, start_ref, carry_ref):
    i = pl.program_id(0)
    tm = TM_ROUTE
    erow = lax.broadcasted_iota(I32, (N_EXPERTS, LANES), 0)
    elane = lax.broadcasted_iota(I32, (N_EXPERTS, LANES), 1)

    @pl.when(i == 0)
    def _():
        cnt = jnp.sum(cnt_ref[...], axis=1, keepdims=True)
        cnt_i = jnp.broadcast_to(cnt, (N_EXPERTS, LANES)).astype(I32)
        padded = lax.shift_left(lax.shift_right_logical(cnt_i + (TM_MOE - 1), _log2(TM_MOE)), _log2(TM_MOE))
        r = lax.broadcasted_iota(I32, (N_EXPERTS, N_EXPERTS), 0)
        c = lax.broadcasted_iota(I32, (N_EXPERTS, N_EXPERTS), 1)
        low = (c <= r).astype(F32)
        pad_end = lax.dot_general(low, padded.astype(F32), (((1,), (0,)), ((), ())),
                                  precision=lax.Precision.HIGHEST, preferred_element_type=F32)
        pad_start = pad_end - padded.astype(F32)
        start_ref[...] = pad_start
        carry_ref[...] = jnp.zeros_like(carry_ref)
        a = lax.broadcasted_iota(I32, (tm, tm), 0)
        bcol = lax.broadcasted_iota(I32, (tm, tm), 1)
        tri_ref[...] = (a < bcol).astype(BF16)
        ntp = n_tiles_pad
        tstart = (lax.broadcasted_iota(I32, (N_EXPERTS, ntp), 1) * TM_MOE).astype(F32)
        pe = jnp.concatenate([pad_end] * (ntp // LANES), axis=1)
        texp = jnp.sum((pe <= tstart).astype(F32), axis=0, keepdims=True)
        texp = jnp.minimum(texp, float(N_EXPERTS - 1)).astype(I32)
        total = jnp.max(pad_end, axis=0, keepdims=True)
        n_used = lax.shift_right_logical(total.astype(I32), _log2(TM_MOE))
        zstart = jnp.sum(jnp.where(erow == elane, pad_start + cnt_i.astype(F32), 0.0),
                         axis=0, keepdims=True).astype(I32)
        zlen = jnp.sum(jnp.where(erow == elane, (padded - cnt_i).astype(F32), 0.0),
                       axis=0, keepdims=True).astype(I32)
        lanes_pad = jnp.zeros((1, ntp - LANES), I32)
        meta_ref[...] = jnp.concatenate(
            [texp,
             jnp.concatenate([n_used, lanes_pad], axis=1),
             jnp.concatenate([zstart, lanes_pad], axis=1),
             jnp.concatenate([zlen, lanes_pad], axis=1),
             jnp.zeros((SUBLANES - 4, ntp), I32)], axis=0)

    idx = idx_ref[...]
    erow_t = lax.broadcasted_iota(I32, (N_EXPERTS, tm), 0)
    hits = [erow_t == idx[k:k + 1, :] for k in range(TOP_K)]
    multi = hits[0].astype(F32)
    for k in range(1, TOP_K):
        multi = multi + hits[k].astype(F32)
    prefix = _dot(multi.astype(BF16), tri_ref[...])
    base = jnp.concatenate([carry_ref[...] + start_ref[...]] * (tm // LANES), axis=1)
    tot = prefix + base
    dest_ref[...] = jnp.concatenate(
        [jnp.sum(jnp.where(hits[k], tot, 0.0), axis=0, keepdims=True) for k in range(TOP_K)],
        axis=0).astype(I32)
    carry_ref[...] += jnp.broadcast_to(jnp.sum(multi, axis=1, keepdims=True), (N_EXPERTS, LANES))


def _ranks(idxT, cnt, n_tiles_pad):
    T = idxT.shape[1]
    nt = T // TM_ROUTE
    full = lambda i: (0, 0)
    return pl.pallas_call(
        functools.partial(_ranks_kernel, n_tiles_pad),
        grid=(nt,),
        in_specs=[
            pl.BlockSpec((TOP_K, TM_ROUTE), lambda i: (0, i)),
            pl.BlockSpec((N_EXPERTS, LANES), full),
        ],
        out_specs=[
            pl.BlockSpec((TOP_K, TM_ROUTE), lambda i: (0, i)),
            pl.BlockSpec((SUBLANES, n_tiles_pad), full),
        ],
        out_shape=[
            jax.ShapeDtypeStruct((TOP_K, T), I32),
            jax.ShapeDtypeStruct((SUBLANES, n_tiles_pad), I32),
        ],
        scratch_shapes=[
            pltpu.VMEM((TM_ROUTE, TM_ROUTE), BF16),
            pltpu.VMEM((N_EXPERTS, LANES), F32),
            pltpu.VMEM((N_EXPERTS, LANES), F32),
        ],
        compiler_params=_cparams(("arbitrary",)),
        name="ranks",
    )(idxT, cnt)


RT = D_MODEL // LANES
DMA_UNROLL = 8


def _row_copy(src_ref, s, dst_ref, d, sem):
    return pltpu.make_async_copy(src_ref.at[pl.ds(pl.multiple_of(s * RT, RT), RT)],
                                 dst_ref.at[pl.ds(pl.multiple_of(d * RT, RT), RT)], sem)


def _to_row_tiled(dst_ref, val, rows):
    for s in range(RT):
        dst_ref[pl.ds(s, rows, stride=RT), :] = val[:, s * LANES:(s + 1) * LANES]


def _from_row_tiled(src_ref, rows):
    return [src_ref[pl.ds(s, rows, stride=RT), :] for s in range(RT)]


def _scatter_kernel(meta_ref, x_ref, dest_ref, xs_ref, stage_ref, zero_ref, sem):
    i = pl.program_id(0)
    tm = TM_ROUTE

    @pl.when(i == 0)
    def _():
        zero_ref[...] = jnp.zeros_like(zero_ref)

        def zcopy(row, nrows):
            off = pl.multiple_of(row * RT, RT)
            return pltpu.make_async_copy(zero_ref.at[pl.ds(0, nrows * RT)],
                                         xs_ref.at[pl.ds(off, nrows * RT)], sem)

        def pad_fill(wait):
            def body(e, c):
                row = meta_ref[2, e]
                plen = meta_ref[3, e]
                for bit in reversed(range(_log2(TM_MOE))):
                    size = 1 << bit
                    has = (plen & size) != 0

                    @pl.when(has)
                    def _():
                        cp = zcopy(row, size)
                        cp.wait() if wait else cp.start()

                    row = row + jnp.where(has, size, 0)
                return c
            lax.fori_loop(0, N_EXPERTS, body, 0)

        def tail_fill(wait):
            def body(j, c):
                cp = zcopy(j * TM_MOE, TM_MOE)
                cp.wait() if wait else cp.start()
                return c
            lax.fori_loop(meta_ref[1, 0], xs_ref.shape[0] // (TM_MOE * RT), body, 0)

        pad_fill(False)
        tail_fill(False)
        pad_fill(True)
        tail_fill(True)

    _to_row_tiled(stage_ref, x_ref[...], tm)

    def start(t, c):
        for k in range(TOP_K):
            _row_copy(stage_ref, t, xs_ref, dest_ref[k, t], sem).start(priority=k % 2)
        return c

    lax.fori_loop(0, tm, start, 0, unroll=DMA_UNROLL)
    for k in range(TOP_K):
        pltpu.make_async_copy(stage_ref, xs_ref.at[pl.ds(0, tm * RT)], sem).wait()


def _scatter(meta, x1, destT, n_rows_alloc):
    T = x1.shape[0]
    nt = T // TM_ROUTE
    return pl.pallas_call(
        _scatter_kernel,
        grid_spec=pltpu.PrefetchScalarGridSpec(
            num_scalar_prefetch=1,
            grid=(nt,),
            in_specs=[
                pl.BlockSpec((TM_ROUTE, D_MODEL), lambda i, m: (i, 0)),
                pl.BlockSpec((TOP_K, TM_ROUTE), lambda i, m: (0, i), memory_space=pltpu.SMEM),
            ],
            out_specs=pl.BlockSpec(memory_space=pl.ANY),
            scratch_shapes=[
                pltpu.VMEM((TM_ROUTE * RT, LANES), F32),
                pltpu.VMEM((TM_MOE * RT, LANES), F32),
                pltpu.SemaphoreType.DMA(()),
            ],
        ),
        out_shape=jax.ShapeDtypeStruct((n_rows_alloc * RT, LANES), F32),
        compiler_params=_cparams(("arbitrary",)),
        name="scatter",
    )(meta, x1, destT)


def _gmm_kernel(meta_ref, xs_ref, wgu_ref, bgu_ref, wd_ref, bd_ref, ys_ref, wgu_b, wd_b, lhs_ref, act_ref):
    j = pl.program_id(0)
    n_used = meta_ref[1, 0]
    e_now = meta_ref[0, j]
    e_prev = meta_ref[0, jnp.maximum(j - 1, 0)]
    used = j < n_used

    @pl.when(used & ((j == 0) | (e_now != e_prev)))
    def _():
        wgu_b[...] = wgu_ref[0, 0].astype(BF16)
        wd_b[...] = wd_ref[0, 0].astype(BF16)

    @pl.when(used)
    def _():
        for s, piece in enumerate(_from_row_tiled(xs_ref, TM_MOE)):
            lhs_ref[:, s * LANES:(s + 1) * LANES] = piece.astype(BF16)
        x = lhs_ref[...]
        nc = 256
        for c in range(D_FF // nc):
            gate = _dot(x, wgu_b[:, c * nc:(c + 1) * nc]) + bgu_ref[0, 0, :, c * nc:(c + 1) * nc]
            up = (_dot(x, wgu_b[:, D_FF + c * nc:D_FF + (c + 1) * nc])
                  + bgu_ref[0, 0, :, D_FF + c * nc:D_FF + (c + 1) * nc])
            gate = jnp.minimum(gate, SWIGLU_LIMIT)
            up = jnp.clip(up, -SWIGLU_LIMIT, SWIGLU_LIMIT)
            act = gate * jax.nn.sigmoid(SWIGLU_ALPHA * gate) * (up + 1.0)
            act_ref[:, c * nc:(c + 1) * nc] = act.astype(BF16)
        _to_row_tiled(ys_ref, _dot(act_ref[...], wd_b[...]) + bd_ref[0, 0], TM_MOE)

    @pl.when(jnp.logical_not(used))
    def _():
        ys_ref[...] = jnp.zeros_like(ys_ref)


def _gmm(meta, xs, w_gu, b_gu, w_down, b_down, n_tiles, layer):
    last = lambda m: jnp.maximum(m[1, 0] - 1, 0)
    return pl.pallas_call(
        _gmm_kernel,
        grid_spec=pltpu.PrefetchScalarGridSpec(
            num_scalar_prefetch=1,
            grid=(n_tiles,),
            in_specs=[
                pl.BlockSpec((TM_MOE * RT, LANES), lambda j, m: (jnp.minimum(j, last(m)), 0)),
                pl.BlockSpec((1, 1, D_MODEL, 2 * D_FF), lambda j, m: (layer, m[0, j], 0, 0)),
                pl.BlockSpec((1, 1, 1, 2 * D_FF), lambda j, m: (layer, m[0, j], 0, 0)),
                pl.BlockSpec((1, 1, D_FF, D_MODEL), lambda j, m: (layer, m[0, j], 0, 0)),
                pl.BlockSpec((1, 1, 1, D_MODEL), lambda j, m: (layer, m[0, j], 0, 0)),
            ],
            out_specs=pl.BlockSpec((TM_MOE * RT, LANES), lambda j, m: (j, 0)),
            scratch_shapes=[
                pltpu.VMEM((D_MODEL, 2 * D_FF), BF16),
                pltpu.VMEM((D_FF, D_MODEL), BF16),
                pltpu.VMEM((TM_MOE, D_MODEL), BF16),
                pltpu.VMEM((TM_MOE, D_FF), BF16),
            ],
        ),
        out_shape=jax.ShapeDtypeStruct((n_tiles * TM_MOE * RT, LANES), F32),
        compiler_params=_cparams(("arbitrary",)),
        name="gmm",
    )(meta, xs, w_gu, b_gu, w_down, b_down)


def _combine_kernel(x_ref, dest_ref, gate_ref, ys_ref, g_ref, b_ref, o_ref, ob_ref, buf_ref, sem):
    tm = TM_COMB

    def start(t, c):
        for k in range(TOP_K):
            _row_copy(ys_ref, dest_ref[k, t], buf_ref.at[k], t, sem).start(priority=k % 2)
        return c

    lax.fori_loop(0, tm, start, 0, unroll=DMA_UNROLL)
    for k in range(TOP_K):
        pltpu.make_async_copy(ys_ref.at[pl.ds(0, tm * RT)], buf_ref.at[k], sem).wait()
    gates = jnp.concatenate(
        [gate_ref[...], jnp.zeros((LANES - SUBLANES, tm), F32)], axis=0).T
    pieces = None
    for k in range(TOP_K):
        gk = gates[:, k:k + 1]
        rows = [p * gk for p in _from_row_tiled(buf_ref.at[k], tm)]
        pieces = rows if pieces is None else [a + r for a, r in zip(pieces, rows)]
    z = DN_ALPHA * x_ref[...] + jnp.concatenate(pieces, axis=1)
    out = _layernorm_rows(z, g_ref[...], b_ref[...])
    o_ref[...] = out
    ob_ref[...] = out.astype(BF16)


def _combine(x1, destT, gateT, ys, g, b):
    T = x1.shape[0]
    nt = T // TM_COMB
    full = lambda i: (0, 0)
    return pl.pallas_call(
        _combine_kernel,
        grid=(nt,),
        in_specs=[
            pl.BlockSpec((TM_COMB, D_MODEL), lambda i: (i, 0)),
            pl.BlockSpec((TOP_K, TM_COMB), lambda i: (0, i), memory_space=pltpu.SMEM),
            pl.BlockSpec((SUBLANES, TM_COMB), lambda i: (0, i)),
            pl.BlockSpec(memory_space=pl.ANY),
            pl.BlockSpec((1, D_MODEL), full),
            pl.BlockSpec((1, D_MODEL), full),
        ],
        out_specs=[
            pl.BlockSpec((TM_COMB, D_MODEL), lambda i: (i, 0)),
            pl.BlockSpec((TM_COMB, D_MODEL), lambda i: (i, 0)),
        ],
        out_shape=[
            jax.ShapeDtypeStruct((T, D_MODEL), F32),
            jax.ShapeDtypeStruct((T, D_MODEL), BF16),
        ],
        scratch_shapes=[
            pltpu.VMEM((TOP_K, TM_COMB * RT, LANES), F32),
            pltpu.SemaphoreType.DMA(()),
        ],
        compiler_params=_cparams(("arbitrary",)),
        name="combine",
    )(x1, destT, gateT, ys, g, b)


def _block_diag(w):
    g, c, _ = w.shape
    out = jnp.zeros((g * c, g * c), w.dtype)
    for i in range(g):
        out = out.at[i * c:(i + 1) * c, i * c:(i + 1) * c].set(w[i])
    return out


def kernel(x, w_in, conv_w, w_pool, pool_scale, idx_kn_g, idx_kn_b, w_out, ln1_g, ln1_b, router_w,
           router_b, w_gu, b_gu, w_down, b_down, ln2_g, ln2_b):
    B, S, D = x.shape
    T = B * S
    depth = w_in.shape[0]
    n_pairs = T * TOP_K
    n_tiles = (n_pairs + N_EXPERTS * (TM_MOE - 1)) // TM_MOE + 1
    n_tiles_pad = ((n_tiles + LANES - 1) // LANES) * LANES
    n_rows_alloc = (n_tiles + 1) * TM_MOE

    xf = x.reshape(T, D)
    xb = xf.astype(BF16)
    for l in range(depth):
        wl = w_in[l]
        wn = jnp.concatenate(
            [wl[:, 0:OFF_V], wl[:, OFF_KI:OFF_WI], jnp.zeros((D, LANES - IDX_DIM), F32)], axis=1).astype(BF16)
        wt = jnp.concatenate(
            [wl[:, OFF_V:OFF_QI], wl[:, OFF_QI:OFF_KI], wl[:, OFF_WI:D_IN],
             jnp.zeros((D, IDXT_ROWS - IDX_HEADS * IDX_DIM - IDX_HEADS), F32)], axis=1).T.astype(BF16)
        a, q, k, ki, vt, it = _inproj(xb, wn, wt, idx_kn_g[l].reshape(1, -1), idx_kn_b[l].reshape(1, -1))
        y_attn = _attn(it, ki, q, k, vt, B, S)
        x1 = _mixout(a, y_attn, xf, conv_w[l], _block_diag(w_pool[l]).astype(BF16),
                     pool_scale[l].reshape(1, -1), w_out[l].astype(BF16),
                     ln1_g[l].reshape(1, -1), ln1_b[l].reshape(1, -1), S)
        idxT, gateT, cnt = _router(x1, router_w[l].T, router_b[l].reshape(-1, 1))
        destT, meta = _ranks(idxT, cnt, n_tiles_pad)
        xs = _scatter(meta, x1, destT, n_rows_alloc)
        ys = _gmm(meta, xs, w_gu, b_gu.reshape(depth, N_EXPERTS, 1, -1), w_down,
                  b_down.reshape(depth, N_EXPERTS, 1, -1), n_tiles, l)
        xf, xb = _combine(x1, destT, gateT, ys, ln2_g[l].reshape(1, -1), ln2_b[l].reshape(1, -1))
    return xf.reshape(B, S, D)
```

```python
import functools

import jax
import jax.numpy as jnp
import numpy as np
from jax import lax
from jax.experimental import pallas as pl
from jax.experimental.pallas import tpu as pltpu

F32 = jnp.float32
BF16 = jnp.bfloat16
I32 = jnp.int32

D_MODEL = 1024
CONV_DIM = 256
CONV_WIDTH = 3
POOL_WINDOWS = (2, 4, 8, 16)
POOL_GROUP = 64
POOL_DIM = 256
ATTN_HEADS = 8
HEAD_DIM = 64
ATTN_DIM = 512
IDX_HEADS = 8
IDX_DIM = 32
INDEX_TOPK_MAX = 256
Q_BLOCK = 128
N_EXPERTS = 32
TOP_K = 4
D_FF = 1024
SWIGLU_ALPHA = 1.702
SWIGLU_LIMIT = 7.0
LN_EPS = 1e-5
DEPTH = 2
DN_ALPHA = (2.0 * DEPTH) ** 0.25

OFF_P = 3 * CONV_DIM
OFF_Q = OFF_P + POOL_DIM
OFF_K = OFF_Q + ATTN_DIM
OFF_V = OFF_K + ATTN_DIM
OFF_QI = OFF_V + ATTN_DIM
OFF_KI = OFF_QI + IDX_HEADS * IDX_DIM
OFF_WI = OFF_KI + IDX_DIM
D_IN = OFF_WI + IDX_HEADS

LANES = 128
SUBLANES = 8
VMEM_LIMIT = 56 * 1024 * 1024
KEY_CHUNK = 512
HALO = 16
TM_PROJ = 512
TM_MIX = 256
TM_ROUTE = 512
TM_MOE = 512
TM_COMB = 256
IDXT_ROWS = IDX_HEADS * IDX_DIM + 16
WT_ROWS = ATTN_DIM + IDXT_ROWS
INT_MIN = -(2 ** 31)
NEG_BIG = -1e30


def _cparams(sem, flags=None):
    return pltpu.CompilerParams(dimension_semantics=sem, vmem_limit_bytes=VMEM_LIMIT, flags=flags)


def _log2(n):
    k = n.bit_length() - 1
    assert 1 << k == n
    return k


def _dot(a, b):
    return jnp.dot(a, b, preferred_element_type=F32)


def _dot_nt(a, b):
    return lax.dot_general(a, b, (((1,), (1,)), ((), ())), preferred_element_type=F32)


def _tree(x, op):
    parts = [x[j * SUBLANES:(j + 1) * SUBLANES, :] for j in range(x.shape[0] // SUBLANES)]
    while len(parts) > 1:
        nxt = [op(parts[j], parts[j + 1]) for j in range(0, len(parts) - 1, 2)]
        if len(parts) % 2:
            nxt.append(parts[-1])
        parts = nxt
    return parts[0]


def _layernorm_rows(z, g, b):
    mu = jnp.mean(z, axis=-1, keepdims=True)
    d = z - mu
    var = jnp.mean(d * d, axis=-1, keepdims=True)
    return d * lax.rsqrt(var + LN_EPS) * g + b


def _inproj_kernel(x_ref, wn_ref, wt_ref, kg_ref, kb_ref,
                   a_ref, q_ref, k_ref, ki_ref, vt_ref, it_ref):
    x = x_ref[...]
    a_ref[...] = _dot(x, wn_ref[:, 0:OFF_Q])
    q_ref[...] = (_dot(x, wn_ref[:, OFF_Q:OFF_K]) * (HEAD_DIM ** -0.5)).astype(BF16)
    k_ref[...] = _dot(x, wn_ref[:, OFF_K:OFF_V]).astype(BF16)
    ki = _dot(x, wn_ref[:, OFF_V:OFF_V + LANES])[:, 0:IDX_DIM]
    ki_ref[...] = _layernorm_rows(ki, kg_ref[...], kb_ref[...])
    t = _dot_nt(wt_ref[...], x)
    for c in range(TM_PROJ // KEY_CHUNK):
        vt_ref[c] = t[0:ATTN_DIM, c * KEY_CHUNK:(c + 1) * KEY_CHUNK].astype(BF16)
    it_ref[...] = t[ATTN_DIM:WT_ROWS, :]


def _inproj(xb, wn, wt, kg, kb):
    T = xb.shape[0]
    nt = T // TM_PROJ
    cpt = TM_PROJ // KEY_CHUNK
    full = lambda i: (0, 0)
    return pl.pallas_call(
        _inproj_kernel,
        grid=(nt,),
        in_specs=[
            pl.BlockSpec((TM_PROJ, D_MODEL), lambda i: (i, 0)),
            pl.BlockSpec(wn.shape, full),
            pl.BlockSpec(wt.shape, full),
            pl.BlockSpec((1, IDX_DIM), full),
            pl.BlockSpec((1, IDX_DIM), full),
        ],
        out_specs=[
            pl.BlockSpec((TM_PROJ, OFF_Q), lambda i: (i, 0)),
            pl.BlockSpec((TM_PROJ, ATTN_DIM), lambda i: (i, 0)),
            pl.BlockSpec((TM_PROJ, ATTN_DIM), lambda i: (i, 0)),
            pl.BlockSpec((TM_PROJ, IDX_DIM), lambda i: (i, 0)),
            pl.BlockSpec((cpt, ATTN_DIM, KEY_CHUNK), lambda i: (i, 0, 0)),
            pl.BlockSpec((IDXT_ROWS, TM_PROJ), lambda i: (0, i)),
        ],
        out_shape=[
            jax.ShapeDtypeStruct((T, OFF_Q), F32),
            jax.ShapeDtypeStruct((T, ATTN_DIM), BF16),
            jax.ShapeDtypeStruct((T, ATTN_DIM), BF16),
            jax.ShapeDtypeStruct((T, IDX_DIM), F32),
            jax.ShapeDtypeStruct((T // KEY_CHUNK, ATTN_DIM, KEY_CHUNK), BF16),
            jax.ShapeDtypeStruct((IDXT_ROWS, T), F32),
        ],
        compiler_params=_cparams(("arbitrary",)),
        name="inproj",
    )(xb, wn, wt, kg, kb)


def _attn_kernel(n_sel, it_ref, ki_ref, q_ref, k_ref, vt_ref, o_ref,
                 keys_ref, rhs_ref, acc_ref, lg_ref):
    i = pl.program_id(1)
    nch = lax.shift_right_logical(i * Q_BLOCK + Q_BLOCK + KEY_CHUNK - 1, _log2(KEY_CHUNK))
    q0 = i * Q_BLOCK
    KC = KEY_CHUNK
    row = lax.broadcasted_iota(I32, (KC, Q_BLOCK), 0)
    lane = lax.broadcasted_iota(I32, (KC, Q_BLOCK), 1)
    qpos = q0 + lane

    qcat = jnp.concatenate(
        [it_ref[h * IDX_DIM:(h + 1) * IDX_DIM, :] for h in range(IDX_HEADS)], axis=1).astype(BF16)
    w_all = it_ref[IDX_HEADS * IDX_DIM:IDX_HEADS * IDX_DIM + IDX_HEADS, :] * (
        (IDX_HEADS ** -0.5) * (IDX_DIM ** -0.5))

    def score_body(c, carry):
        base = pl.multiple_of(c * KC, KC)
        kic = ki_ref[pl.ds(base, KC), :].astype(BF16)
        s = _dot(kic, qcat)
        sc = jnp.zeros((KC, Q_BLOCK), F32)
        for h in range(IDX_HEADS):
            sc = sc + jnp.maximum(s[:, h * Q_BLOCK:(h + 1) * Q_BLOCK], 0.0) * w_all[h:h + 1, :]
        bits = lax.bitcast_convert_type(sc, I32)
        key = bits ^ ((bits >> 31) & jnp.int32(0x7FFFFFFF))
        key = jnp.where(bits == jnp.int32(INT_MIN), jnp.int32(0), key)
        key = jnp.where(base + row <= qpos, key, jnp.int32(INT_MIN))
        keys_ref[pl.ds(base, KC), :] = key
        return carry

    lax.fori_loop(0, nch, score_body, 0)

    def count_ge(cand):
        def body(c, acc):
            base = pl.multiple_of(c * KC, KC)
            blk = keys_ref[pl.ds(base, KC), :]
            hit = jnp.where(blk >= cand, 1.0, 0.0).astype(F32)
            return acc + _tree(hit, jnp.add)
        acc = lax.fori_loop(0, nch, body, jnp.zeros((SUBLANES, Q_BLOCK), F32))
        return jnp.sum(acc, axis=0, keepdims=True)

    want = jnp.float32(n_sel)
    zero_row = jnp.zeros((1, Q_BLOCK), I32)
    cnt0 = count_ge(zero_row)
    lo = jnp.where(cnt0 >= want, zero_row, jnp.int32(INT_MIN))

    def bit_body(it, carry):
        lo, cnt_lo = carry
        cand = lo + lax.shift_left(jnp.int32(1), jnp.int32(30) - it)
        cnt = count_ge(cand)
        ok = cnt >= want
        return jnp.where(ok, cand, lo), jnp.where(ok, cnt, cnt_lo)

    lo, cnt_lo = lax.fori_loop(0, 31, bit_body, (lo, cnt0))
    tau = jnp.maximum(lo, jnp.int32(INT_MIN + 1))

    n_causal = q0 + lax.broadcasted_iota(I32, (1, Q_BLOCK), 1) + 1
    excess = jnp.where(n_causal > n_sel, cnt_lo - want, 0.0)

    @pl.when(jnp.max(excess) > 0.0)
    def _():
        def count_tied_before(limit):
            def body(c, acc):
                base = pl.multiple_of(c * KC, KC)
                blk = keys_ref[pl.ds(base, KC), :]
                hit = jnp.where((blk == tau) & (base + row < limit), 1.0, 0.0).astype(F32)
                return acc + _tree(hit, jnp.add)
            acc = lax.fori_loop(0, nch, body, jnp.zeros((SUBLANES, Q_BLOCK), F32))
            return jnp.sum(acc, axis=0, keepdims=True)

        n_keys = keys_ref.shape[0]
        keep = count_tied_before(jnp.full((1, Q_BLOCK), n_keys, I32)) - excess

        def pos_body(it, y):
            cand = y + lax.shift_left(jnp.int32(1), jnp.int32(_log2(n_keys) - 1) - it)
            return jnp.where(count_tied_before(cand) < keep, cand, y)

        last = lax.fori_loop(0, _log2(n_keys), pos_body, jnp.zeros((1, Q_BLOCK), I32))

        def retire(c, carry):
            base = pl.multiple_of(c * KC, KC)
            blk = keys_ref[pl.ds(base, KC), :]
            keys_ref[pl.ds(base, KC), :] = jnp.where(
                (blk == tau) & (base + row > last), jnp.int32(INT_MIN), blk)
            return carry

        lax.fori_loop(0, nch, retire, 0)

    q = q_ref[...]
    lane_q = lax.broadcasted_iota(I32, (Q_BLOCK, 2 * HEAD_DIM), 1)
    for p in range(ATTN_HEADS // 2):
        qp = q[:, p * 2 * HEAD_DIM:(p + 1) * 2 * HEAD_DIM]
        rhs_ref[p, 0:Q_BLOCK, :] = jnp.where(lane_q < HEAD_DIM, qp, jnp.zeros_like(qp))
        rhs_ref[p, Q_BLOCK:2 * Q_BLOCK, :] = jnp.where(lane_q >= HEAD_DIM, qp, jnp.zeros_like(qp))
    acc_ref[...] = jnp.zeros_like(acc_ref)

    def stage_a_prep(c):
        base = pl.multiple_of(c * KC, KC)
        tk = (base + row - (q0 + Q_BLOCK - 1)).astype(F32)
        tkm = jnp.where(keys_ref[pl.ds(base, KC), :] >= tau, tk, NEG_BIG)
        return tkm, k_ref[pl.ds(base, KC), :]

    def stage_a_pair(c, p, tkm, kc):
        l2 = _dot_nt(kc[:, p * 2 * HEAD_DIM:(p + 1) * 2 * HEAD_DIM], rhs_ref[p])
        mcs = []
        for hh in range(2):
            h = 2 * p + hh
            slope = 2.0 ** (-8.0 * (h + 1) / ATTN_HEADS)
            lg = l2[:, hh * Q_BLOCK:(hh + 1) * Q_BLOCK] + slope * tkm
            lg_ref[h] = lg
            mcs.append(jnp.max(_tree(lg, jnp.maximum), axis=0, keepdims=True))
        return mcs

    def stage_b_pair(c, p, m_new, alpha):
        sums = []
        for hh in range(2):
            h = 2 * p + hh
            pr = jnp.exp(lg_ref[h] - m_new[h:h + 1, :])
            sums.append(jnp.sum(_tree(pr, jnp.add), axis=0, keepdims=True))
            pv = _dot(vt_ref[c, h * HEAD_DIM:(h + 1) * HEAD_DIM, :], pr.astype(BF16))
            acc_ref[h * HEAD_DIM:(h + 1) * HEAD_DIM, :] = (
                acc_ref[h * HEAD_DIM:(h + 1) * HEAD_DIM, :] * alpha[h:h + 1, :] + pv)
        return sums

    def stage_b_all(c, m_old, l_old, mc):
        m_new = jnp.maximum(m_old, mc)
        alpha = jnp.exp(m_old - m_new)
        return m_new, alpha

    tkm0, kc0 = stage_a_prep(0)
    mc0 = jnp.concatenate(sum([stage_a_pair(0, p, tkm0, kc0) for p in range(ATTN_HEADS // 2)], []), axis=0)

    def attn_body(c, carry):
        m_old, l_old, mc_prev = carry
        m_new, alpha = stage_b_all(c - 1, m_old, l_old, mc_prev)
        tkm, kc = stage_a_prep(c)
        mcs, sums = [], []
        for p in range(ATTN_HEADS // 2):
            sums += stage_b_pair(c - 1, p, m_new, alpha)
            mcs += stage_a_pair(c, p, tkm, kc)
        return m_new, l_old * alpha + jnp.concatenate(sums, axis=0), jnp.concatenate(mcs, axis=0)

    m0 = jnp.full((ATTN_HEADS, Q_BLOCK), NEG_BIG, F32)
    l0 = jnp.zeros((ATTN_HEADS, Q_BLOCK), F32)
    m_old, l_old, mc_prev = lax.fori_loop(1, nch, attn_body, (m0, l0, mc0))
    m_new, alpha = stage_b_all(nch - 1, m_old, l_old, mc_prev)
    sums = sum([stage_b_pair(nch - 1, p, m_new, alpha) for p in range(ATTN_HEADS // 2)], [])
    l_fin = l_old * alpha + jnp.concatenate(sums, axis=0)

    outs = []
    for h in range(ATTN_HEADS):
        outs.append(acc_ref[h * HEAD_DIM:(h + 1) * HEAD_DIM, :] / l_fin[h:h + 1, :])
    o_ref[...] = jnp.concatenate(outs, axis=0).T


def _attn(it, ki, q, k, vt, B, S):
    T = B * S
    nb = S // Q_BLOCK
    n_sel = min(INDEX_TOPK_MAX, S // 4)
    return pl.pallas_call(
        functools.partial(_attn_kernel, n_sel),
        grid=(B, nb),
        in_specs=[
            pl.BlockSpec((IDXT_ROWS, Q_BLOCK), lambda b, i: (0, b * nb + i)),
            pl.BlockSpec((S, IDX_DIM), lambda b, i: (b, 0)),
            pl.BlockSpec((Q_BLOCK, ATTN_DIM), lambda b, i: (b * nb + i, 0)),
            pl.BlockSpec((S, ATTN_DIM), lambda b, i: (b, 0)),
            pl.BlockSpec((S // KEY_CHUNK, ATTN_DIM, KEY_CHUNK), lambda b, i: (b, 0, 0)),
        ],
        out_specs=pl.BlockSpec((Q_BLOCK, ATTN_DIM), lambda b, i: (b * nb + i, 0)),
        out_shape=jax.ShapeDtypeStruct((T, ATTN_DIM), F32),
        scratch_shapes=[
            pltpu.VMEM((S, Q_BLOCK), I32),
            pltpu.VMEM((ATTN_HEADS // 2, 2 * Q_BLOCK, 2 * HEAD_DIM), BF16),
            pltpu.VMEM((ATTN_DIM, Q_BLOCK), F32),
            pltpu.VMEM((ATTN_HEADS, KEY_CHUNK, Q_BLOCK), F32),
        ],
        compiler_params=_cparams(("arbitrary", "arbitrary")),
        name="attn",
    )(it, ki, q, k, vt)


def _mixout_kernel(tiles_per_seq, a_ref, halo_ref, y_ref, x_ref, cw_ref, wpool_ref, ps_ref,
                   wout_ref, g_ref, b_ref, o_ref, ext_ref):
    i = pl.program_id(0)
    tm = TM_MIX
    first = lax.rem(i, tiles_per_seq) == 0
    halo = jnp.where(first, 0.0, halo_ref[...])
    a = a_ref[...]
    h_c, gb_c, gc_c, p_c = (a[:, j * CONV_DIM:(j + 1) * CONV_DIM] for j in range(4))

    t0 = 2 * HALO
    n_ext = t0 + tm
    ext_ref[0:HALO, :] = jnp.zeros((HALO, CONV_DIM), F32)

    ext_ref[HALO:t0, :] = halo[:, 2 * CONV_DIM:3 * CONV_DIM] * halo[:, 0:CONV_DIM]
    u = gc_c * h_c
    ext_ref[t0:n_ext, :] = u
    cw = cw_ref[...]
    conv = cw[2:3, :] * u
    conv = conv + cw[1:2, :] * ext_ref[t0 - 1:n_ext - 1, :]
    conv = conv + cw[0:1, :] * ext_ref[t0 - 2:n_ext - 2, :]
    y_conv = gb_c * conv

    ext_ref[HALO:t0, :] = halo[:, OFF_P:OFF_P + POOL_DIM]
    ext_ref[t0:n_ext, :] = p_c
    sums = {}
    step = 1
    while step < POOL_WINDOWS[-1]:
        cur = ext_ref[HALO:n_ext, :] + ext_ref[HALO - step:n_ext - step, :]
        ext_ref[HALO:n_ext, :] = cur
        step *= 2
        sums[step] = ext_ref[t0:n_ext, :]
    tpos = (lax.rem(i, tiles_per_seq) * tm + lax.broadcasted_iota(I32, (tm, POOL_DIM), 0) + 1).astype(F32)
    grp = lax.shift_right_logical(lax.broadcasted_iota(I32, (tm, POOL_DIM), 1), _log2(POOL_GROUP))
    mean = jnp.zeros((tm, POOL_DIM), F32)
    for gi, w in enumerate(POOL_WINDOWS):
        mean = jnp.where(grp == gi, sums[w] / jnp.minimum(tpos, float(w)), mean)
    mixed = mean - p_c
    y_pool = _dot(mixed.astype(BF16), wpool_ref[...]) * ps_ref[...]

    mix = _dot(y_conv.astype(BF16), wout_ref[0:CONV_DIM, :])
    mix = mix + _dot(y_pool.astype(BF16), wout_ref[CONV_DIM:CONV_DIM + POOL_DIM, :])
    mix = mix + _dot(y_ref[...].astype(BF16), wout_ref[CONV_DIM + POOL_DIM:D_MODEL, :])
    z = DN_ALPHA * x_ref[...] + mix
    o_ref[...] = _layernorm_rows(z, g_ref[...], b_ref[...])


def _mixout(a, y_attn, x, cw, wpool_bd, ps, wout, g, b, S):
    T = a.shape[0]
    nt = T // TM_MIX
    tps = S // TM_MIX
    hb = TM_MIX // HALO
    full = lambda i: (0, 0)
    return pl.pallas_call(
        functools.partial(_mixout_kernel, tps),
        grid=(nt,),
        in_specs=[
            pl.BlockSpec((TM_MIX, OFF_Q), lambda i: (i, 0)),
            pl.BlockSpec((HALO, OFF_Q), lambda i: (jnp.maximum(i * hb - 1, 0), 0)),
            pl.BlockSpec((TM_MIX, ATTN_DIM), lambda i: (i, 0)),
            pl.BlockSpec((TM_MIX, D_MODEL), lambda i: (i, 0)),
            pl.BlockSpec((CONV_WIDTH, CONV_DIM), full),
            pl.BlockSpec((POOL_DIM, POOL_DIM), full),
            pl.BlockSpec((1, POOL_DIM), full),
            pl.BlockSpec((D_MODEL, D_MODEL), full),
            pl.BlockSpec((1, D_MODEL), full),
            pl.BlockSpec((1, D_MODEL), full),
        ],
        out_specs=pl.BlockSpec((TM_MIX, D_MODEL), lambda i: (i, 0)),
        out_shape=jax.ShapeDtypeStruct((T, D_MODEL), F32),
        scratch_shapes=[pltpu.VMEM((2 * HALO + TM_MIX, CONV_DIM), F32)],
        compiler_params=_cparams(("arbitrary",)),
        name="mixout",
    )(a, a, y_attn, x, cw, wpool_bd, ps, wout, g, b)


def _router_kernel(x_ref, rw_ref, rb_ref, idx_ref, gate_ref, cnt_ref):
    i = pl.program_id(0)
    tm = TM_ROUTE
    logits = lax.dot_general(rw_ref[...], x_ref[...], (((1,), (1,)), ((), ())),
                             precision=lax.Precision.HIGHEST,
                             preferred_element_type=F32) + rb_ref[...]
    erow = lax.broadcasted_iota(I32, (N_EXPERTS, tm), 0).astype(F32)
    work = logits
    vals, idxs = [], []
    multi = jnp.zeros((N_EXPERTS, tm), F32)
    for _ in range(TOP_K):
        mx = jnp.max(work, axis=0, keepdims=True)
        pick = jnp.min(jnp.where(work == mx, erow, float(N_EXPERTS)), axis=0, keepdims=True)
        hit = erow == pick
        work = jnp.where(hit, -jnp.inf, work)
        multi = multi + hit.astype(F32)
        vals.append(mx)
        idxs.append(pick)
    es = [jnp.exp(v - vals[0]) for v in vals]
    den = es[0] + es[1] + es[2] + es[3]
    idx_ref[...] = jnp.concatenate(idxs, axis=0).astype(I32)
    gate_ref[...] = jnp.concatenate([e / den for e in es] + [jnp.zeros((SUBLANES - TOP_K, tm), F32)], axis=0)
    part = multi[:, 0:LANES]
    for j in range(1, tm // LANES):
        part = part + multi[:, j * LANES:(j + 1) * LANES]

    @pl.when(i == 0)
    def _():
        cnt_ref[...] = jnp.zeros_like(cnt_ref)

    cnt_ref[...] += part


def _router(x1, rwT, rb):
    T = x1.shape[0]
    nt = T // TM_ROUTE
    full = lambda i: (0, 0)
    return pl.pallas_call(
        _router_kernel,
        grid=(nt,),
        in_specs=[
            pl.BlockSpec((TM_ROUTE, D_MODEL), lambda i: (i, 0)),
            pl.BlockSpec((N_EXPERTS, D_MODEL), full),
            pl.BlockSpec((N_EXPERTS, 1), full),
        ],
        out_specs=[
            pl.BlockSpec((TOP_K, TM_ROUTE), lambda i: (0, i)),
            pl.BlockSpec((SUBLANES, TM_ROUTE), lambda i: (0, i)),
            pl.BlockSpec((N_EXPERTS, LANES), full),
        ],
        out_shape=[
            jax.ShapeDtypeStruct((TOP_K, T), I32),
            jax.ShapeDtypeStruct((SUBLANES, T), F32),
            jax.ShapeDtypeStruct((N_EXPERTS, LANES), F32),
        ],
        compiler_params=_cparams(("arbitrary",)),
        name="router",
    )(x1, rwT, rb)


def _ranks_kernel(n_tiles_pad, idx_ref, cnt_ref, dest_ref, meta_ref, tri_ref, start_ref, carry_ref):
    i = pl.program_id(0)
    tm = TM_ROUTE
    erow = lax.broadcasted_iota(I32, (N_EXPERTS, LANES), 0)
    elane = lax.broadcasted_iota(I32, (N_EXPERTS, LANES), 1)

    @pl.when(i == 0)
    def _():
        cnt = jnp.sum(cnt_ref[...], axis=1, keepdims=True)
        cnt_i = jnp.broadcast_to(cnt, (N_EXPERTS, LANES)).astype(I32)
        padded = lax.shift_left(lax.shift_right_logical(cnt_i + (TM_MOE - 1), _log2(TM_MOE)), _log2(TM_MOE))
        r = lax.broadcasted_iota(I32, (N_EXPERTS, N_EXPERTS), 0)
        c = lax.broadcasted_iota(I32, (N_EXPERTS, N_EXPERTS), 1)
        low = (c <= r).astype(F32)
        pad_end = lax.dot_general(low, padded.astype(F32), (((1,), (0,)), ((), ())),
                                  precision=lax.Precision.HIGHEST, preferred_element_type=F32)
        pad_start = pad_end - padded.astype(F32)
        start_ref[...] = pad_start
        carry_ref[...] = jnp.zeros_like(carry_ref)
        a = lax.broadcasted_iota(I32, (tm, tm), 0)
        bcol = lax.broadcasted_iota(I32, (tm, tm), 1)
        tri_ref[...] = (a < bcol).astype(BF16)
        ntp = n_tiles_pad
        tstart = (lax.broadcasted_iota(I32, (N_EXPERTS, ntp), 1) * TM_MOE).astype(F32)
        pe = jnp.concatenate([pad_end] * (ntp // LANES), axis=1)
        texp = jnp.sum((pe <= tstart).astype(F32), axis=0, keepdims=True)
        texp = jnp.minimum(texp, float(N_EXPERTS - 1)).astype(I32)
        total = jnp.max(pad_end, axis=0, keepdims=True)
        n_used = lax.shift_right_logical(total.astype(I32), _log2(TM_MOE))
        zstart = jnp.sum(jnp.where(erow == elane, pad_start + cnt_i.astype(F32), 0.0),
                         axis=0, keepdims=True).astype(I32)
        zlen = jnp.sum(jnp.where(erow == elane, (padded - cnt_i).astype(F32), 0.0),
                       axis=0, keepdims=True).astype(I32)
        lanes_pad = jnp.zeros((1, ntp - LANES), I32)
        meta_ref[...] = jnp.concatenate(
            [texp,
             jnp.concatenate([n_used, lanes_pad], axis=1),
             jnp.concatenate([zstart, lanes_pad], axis=1),
             jnp.concatenate([zlen, lanes_pad], axis=1),
             jnp.zeros((SUBLANES - 4, ntp), I32)], axis=0)

    idx = idx_ref[...]
    erow_t = lax.broadcasted_iota(I32, (N_EXPERTS, tm), 0)
    hits = [erow_t == idx[k:k + 1, :] for k in range(TOP_K)]
    multi = hits[0].astype(F32)
    for k in range(1, TOP_K):
        multi = multi + hits[k].astype(F32)
    prefix = _dot(multi.astype(BF16), tri_ref[...])
    base = jnp.concatenate([carry_ref[...] + start_ref[...]] * (tm // LANES), axis=1)
    tot = prefix + base
    dest_ref[...] = jnp.concatenate(
        [jnp.sum(jnp.where(hits[k], tot, 0.0), axis=0, keepdims=True) for k in range(TOP_K)],
        axis=0).astype(I32)
    carry_ref[...] += jnp.broadcast_to(jnp.sum(multi, axis=1, keepdims=True), (N_EXPERTS, LANES))


def _ranks(idxT, cnt, n_tiles_pad):
    T = idxT.shape[1]
    nt = T // TM_ROUTE
    full = lambda i: (0, 0)
    return pl.pallas_call(
        functools.partial(_ranks_kernel, n_tiles_pad),
        grid=(nt,),
        in_specs=[
            pl.BlockSpec((TOP_K, TM_ROUTE), lambda i: (0, i)),
            pl.BlockSpec((N_EXPERTS, LANES), full),
        ],
        out_specs=[
            pl.BlockSpec((TOP_K, TM_ROUTE), lambda i: (0, i)),
            pl.BlockSpec((SUBLANES, n_tiles_pad), full),
        ],
        out_shape=[
            jax.ShapeDtypeStruct((TOP_K, T), I32),
            jax.ShapeDtypeStruct((SUBLANES, n_tiles_pad), I32),
        ],
        scratch_shapes=[
            pltpu.VMEM((TM_ROUTE, TM_ROUTE), BF16),
            pltpu.VMEM((N_EXPERTS, LANES), F32),
            pltpu.VMEM((N_EXPERTS, LANES), F32),
        ],
        compiler_params=_cparams(("arbitrary",)),
        name="ranks",
    )(idxT, cnt)


RT = D_MODEL // LANES
DMA_UNROLL = 8


def _row_copy(src_ref, s, dst_ref, d, sem):
    return pltpu.make_async_copy(src_ref.at[pl.ds(pl.multiple_of(s * RT, RT), RT)],
                                 dst_ref.at[pl.ds(pl.multiple_of(d * RT, RT), RT)], sem)


def _to_row_tiled(dst_ref, val, rows):
    for s in range(RT):
        dst_ref[pl.ds(s, rows, stride=RT), :] = val[:, s * LANES:(s + 1) * LANES]


def _from_row_tiled(src_ref, rows):
    return [src_ref[pl.ds(s, rows, stride=RT), :] for s in range(RT)]


def _scatter_kernel(meta_ref, x_ref, dest_ref, xs_ref, stage_ref, zero_ref, sem):
    i = pl.program_id(0)
    tm = TM_ROUTE

    @pl.when(i == 0)
    def _():
        zero_ref[...] = jnp.zeros_like(zero_ref)

        def zcopy(row, nrows):
            off = pl.multiple_of(row * RT, RT)
            return pltpu.make_async_copy(zero_ref.at[pl.ds(0, nrows * RT)],
                                         xs_ref.at[pl.ds(off, nrows * RT)], sem)

        def pad_fill(wait):
            def body(e, c):
                row = meta_ref[2, e]
                plen = meta_ref[3, e]
                for bit in reversed(range(_log2(TM_MOE))):
                    size = 1 << bit
                    has = (plen & size) != 0

                    @pl.when(has)
                    def _():
                        cp = zcopy(row, size)
                        cp.wait() if wait else cp.start()

                    row = row + jnp.where(has, size, 0)
                return c
            lax.fori_loop(0, N_EXPERTS, body, 0)

        def tail_fill(wait):
            def body(j, c):
                cp = zcopy(j * TM_MOE, TM_MOE)
                cp.wait() if wait else cp.start()
                return c
            lax.fori_loop(meta_ref[1, 0], xs_ref.shape[0] // (TM_MOE * RT), body, 0)

        pad_fill(False)
        tail_fill(False)
        pad_fill(True)
        tail_fill(True)

    _to_row_tiled(stage_ref, x_ref[...], tm)

    def start(t, c):
        for k in range(TOP_K):
            _row_copy(stage_ref, t, xs_ref, dest_ref[k, t], sem).start(priority=k % 2)
        return c

    lax.fori_loop(0, tm, start, 0, unroll=DMA_UNROLL)
    for k in range(TOP_K):
        pltpu.make_async_copy(stage_ref, xs_ref.at[pl.ds(0, tm * RT)], sem).wait()


def _scatter(meta, x1, destT, n_rows_alloc):
    T = x1.shape[0]
    nt = T // TM_ROUTE
    return pl.pallas_call(
        _scatter_kernel,
        grid_spec=pltpu.PrefetchScalarGridSpec(
            num_scalar_prefetch=1,
            grid=(nt,),
            in_specs=[
                pl.BlockSpec((TM_ROUTE, D_MODEL), lambda i, m: (i, 0)),
                pl.BlockSpec((TOP_K, TM_ROUTE), lambda i, m: (0, i), memory_space=pltpu.SMEM),
            ],
            out_specs=pl.BlockSpec(memory_space=pl.ANY),
            scratch_shapes=[
                pltpu.VMEM((TM_ROUTE * RT, LANES), F32),
                pltpu.VMEM((TM_MOE * RT, LANES), F32),
                pltpu.SemaphoreType.DMA(()),
            ],
        ),
        out_shape=jax.ShapeDtypeStruct((n_rows_alloc * RT, LANES), F32),
        compiler_params=_cparams(("arbitrary",)),
        name="scatter",
    )(meta, x1, destT)


def _gmm_kernel(meta_ref, xs_ref, wgu_ref, bgu_ref, wd_ref, bd_ref, ys_ref, wgu_b, wd_b, lhs_ref, act_ref):
    j = pl.program_id(0)
    n_used = meta_ref[1, 0]
    e_now = meta_ref[0, j]
    e_prev = meta_ref[0, jnp.maximum(j - 1, 0)]
    used = j < n_used

    @pl.when(used & ((j == 0) | (e_now != e_prev)))
    def _():
        wgu_b[...] = wgu_ref[0, 0].astype(BF16)
        wd_b[...] = wd_ref[0, 0].astype(BF16)

    @pl.when(used)
    def _():
        for s, piece in enumerate(_from_row_tiled(xs_ref, TM_MOE)):
            lhs_ref[:, s * LANES:(s + 1) * LANES] = piece.astype(BF16)
        x = lhs_ref[...]
        nc = 256
        for c in range(D_FF // nc):
            gate = _dot(x, wgu_b[:, c * nc:(c + 1) * nc]) + bgu_ref[0, 0, :, c * nc:(c + 1) * nc]
            up = (_dot(x, wgu_b[:, D_FF + c * nc:D_FF + (c + 1) * nc])
                  + bgu_ref[0, 0, :, D_FF + c * nc:D_FF + (c + 1) * nc])
            gate = jnp.minimum(gate, SWIGLU_LIMIT)
            up = jnp.clip(up, -SWIGLU_LIMIT, SWIGLU_LIMIT)
            act = gate * jax.nn.sigmoid(SWIGLU_ALPHA * gate) * (up + 1.0)
            act_ref[:, c * nc:(c + 1) * nc] = act.astype(BF16)
        _to_row_tiled(ys_ref, _dot(act_ref[...], wd_b[...]) + bd_ref[0, 0], TM_MOE)

    @pl.when(jnp.logical_not(used))
    def _():
        ys_ref[...] = jnp.zeros_like(ys_ref)


def _gmm(meta, xs, w_gu, b_gu, w_down, b_down, n_tiles, layer):
    last = lambda m: jnp.maximum(m[1, 0] - 1, 0)
    return pl.pallas_call(
        _gmm_kernel,
        grid_spec=pltpu.PrefetchScalarGridSpec(
            num_scalar_prefetch=1,
            grid=(n_tiles,),
            in_specs=[
                pl.BlockSpec((TM_MOE * RT, LANES), lambda j, m: (jnp.minimum(j, last(m)), 0)),
                pl.BlockSpec((1, 1, D_MODEL, 2 * D_FF), lambda j, m: (layer, m[0, j], 0, 0)),
                pl.BlockSpec((1, 1, 1, 2 * D_FF), lambda j, m: (layer, m[0, j], 0, 0)),
                pl.BlockSpec((1, 1, D_FF, D_MODEL), lambda j, m: (layer, m[0, j], 0, 0)),
                pl.BlockSpec((1, 1, 1, D_MODEL), lambda j, m: (layer, m[0, j], 0, 0)),
            ],
            out_specs=pl.BlockSpec((TM_MOE * RT, LANES), lambda j, m: (j, 0)),
            scratch_shapes=[
                pltpu.VMEM((D_MODEL, 2 * D_FF), BF16),
                pltpu.VMEM((D_FF, D_MODEL), BF16),
                pltpu.VMEM((TM_MOE, D_MODEL), BF16),
                pltpu.VMEM((TM_MOE, D_FF), BF16),
            ],
        ),
        out_shape=jax.ShapeDtypeStruct((n_tiles * TM_MOE * RT, LANES), F32),
        compiler_params=_cparams(("arbitrary",)),
        name="gmm",
    )(meta, xs, w_gu, b_gu, w_down, b_down)


def _combine_kernel(x_ref, dest_ref, gate_ref, ys_ref, g_ref, b_ref, o_ref, ob_ref, buf_ref, sem):
    tm = TM_COMB

    def start(t, c):
        for k in range(TOP_K):
            _row_copy(ys_ref, dest_ref[k, t], buf_ref.at[k], t, sem).start(priority=k % 2)
        return c

    lax.fori_loop(0, tm, start, 0, unroll=DMA_UNROLL)
    for k in range(TOP_K):
        pltpu.make_async_copy(ys_ref.at[pl.ds(0, tm * RT)], buf_ref.at[k], sem).wait()
    gates = jnp.concatenate(
        [gate_ref[...], jnp.zeros((LANES - SUBLANES, tm), F32)], axis=0).T
    pieces = None
    for k in range(TOP_K):
        gk = gates[:, k:k + 1]
        rows = [p * gk for p in _from_row_tiled(buf_ref.at[k], tm)]
        pieces = rows if pieces is None else [a + r for a, r in zip(pieces, rows)]
    z = DN_ALPHA * x_ref[...] + jnp.concatenate(pieces, axis=1)
    out = _layernorm_rows(z, g_ref[...], b_ref[...])
    o_ref[...] = out
    ob_ref[...] = out.astype(BF16)


def _combine(x1, destT, gateT, ys, g, b):
    T = x1.shape[0]
    nt = T // TM_COMB
    full = lambda i: (0, 0)
    return pl.pallas_call(
        _combine_kernel,
        grid=(nt,),
        in_specs=[
            pl.BlockSpec((TM_COMB, D_MODEL), lambda i: (i, 0)),
            pl.BlockSpec((TOP_K, TM_COMB), lambda i: (0, i), memory_space=pltpu.SMEM),
            pl.BlockSpec((SUBLANES, TM_COMB), lambda i: (0, i)),
            pl.BlockSpec(memory_space=pl.ANY),
            pl.BlockSpec((1, D_MODEL), full),
            pl.BlockSpec((1, D_MODEL), full),
        ],
        out_specs=[
            pl.BlockSpec((TM_COMB, D_MODEL), lambda i: (i, 0)),
            pl.BlockSpec((TM_COMB, D_MODEL), lambda i: (i, 0)),
        ],
        out_shape=[
            jax.ShapeDtypeStruct((T, D_MODEL), F32),
            jax.ShapeDtypeStruct((T, D_MODEL), BF16),
        ],
        scratch_shapes=[
            pltpu.VMEM((TOP_K, TM_COMB * RT, LANES), F32),
            pltpu.SemaphoreType.DMA(()),
        ],
        compiler_params=_cparams(("arbitrary",)),
        name="combine",
    )(x1, destT, gateT, ys, g, b)


def _block_diag(w):
    g, c, _ = w.shape
    out = jnp.zeros((g * c, g * c), w.dtype)
    for i in range(g):
        out = out.at[i * c:(i + 1) * c, i * c:(i + 1) * c].set(w[i])
    return out


def kernel(x, w_in, conv_w, w_pool, pool_scale, idx_kn_g, idx_kn_b, w_out, ln1_g, ln1_b, router_w,
           router_b, w_gu, b_gu, w_down, b_down, ln2_g, ln2_b):
    B, S, D = x.shape
    T = B * S
    depth = w_in.shape[0]
    n_pairs = T * TOP_K
    n_tiles = (n_pairs + N_EXPERTS * (TM_MOE - 1)) // TM_MOE + 1
    n_tiles_pad = ((n_tiles + LANES - 1) // LANES) * LANES
    n_rows_alloc = (n_tiles + 1) * TM_MOE

    xf = x.reshape(T, D)
    xb = xf.astype(BF16)
    for l in range(depth):
        wl = w_in[l]
        wn = jnp.concatenate(
            [wl[:, 0:OFF_V], wl[:, OFF_KI:OFF_WI], jnp.zeros((D, LANES - IDX_DIM), F32)], axis=1).astype(BF16)
        wt = jnp.concatenate(
            [wl[:, OFF_V:OFF_QI], wl[:, OFF_QI:OFF_KI], wl[:, OFF_WI:D_IN],
             jnp.zeros((D, IDXT_ROWS - IDX_HEADS * IDX_DIM - IDX_HEADS), F32)], axis=1).T.astype(BF16)
        a, q, k, ki, vt, it = _inproj(xb, wn, wt, idx_kn_g[l].reshape(1, -1), idx_kn_b[l].reshape(1, -1))
        y_attn = _attn(it, ki, q, k, vt, B, S)
        x1 = _mixout(a, y_attn, xf, conv_w[l], _block_diag(w_pool[l]).astype(BF16),
                     pool_scale[l].reshape(1, -1), w_out[l].astype(BF16),
                     ln1_g[l].reshape(1, -1), ln1_b[l].reshape(1, -1), S)
        idxT, gateT, cnt = _router(x1, router_w[l].T, router_b[l].reshape(-1, 1))
        destT, meta = _ranks(idxT, cnt, n_tiles_pad)
        xs = _scatter(meta, x1, destT, n_rows_alloc)
        ys = _gmm(meta, xs, w_gu, b_gu.reshape(depth, N_EXPERTS, 1, -1), w_down,
                  b_down.reshape(depth, N_EXPERTS, 1, -1), n_tiles, l)
        xf, xb = _combine(x1, destT, gateT, ys, ln2_g[l].reshape(1, -1), ln2_b[l].reshape(1, -1))
    return xf.reshape(B, S, D)
```

```python
import functools

import jax
import jax.numpy as jnp
import numpy as np
from jax import lax
from jax.experimental import pallas as pl
from jax.experimental.pallas import tpu as pltpu

F32 = jnp.float32
BF16 = jnp.bfloat16
I32 = jnp.int32

D_MODEL = 1024
CONV_DIM = 256
CONV_WIDTH = 3
POOL_WINDOWS = (2, 4, 8, 16)
POOL_GROUP = 64
POOL_DIM = 256
ATTN_HEADS = 8
HEAD_DIM = 64
ATTN_DIM = 512
IDX_HEADS = 8
IDX_DIM = 32
INDEX_TOPK_MAX = 256
Q_BLOCK = 128
N_EXPERTS = 32
TOP_K = 4
D_FF = 1024
SWIGLU_ALPHA = 1.702
SWIGLU_LIMIT = 7.0
LN_EPS = 1e-5
DEPTH = 2
DN_ALPHA = (2.0 * DEPTH) ** 0.25

OFF_P = 3 * CONV_DIM
OFF_Q = OFF_P + POOL_DIM
OFF_K = OFF_Q + ATTN_DIM
OFF_V = OFF_K + ATTN_DIM
OFF_QI = OFF_V + ATTN_DIM
OFF_KI = OFF_QI + IDX_HEADS * IDX_DIM
OFF_WI = OFF_KI + IDX_DIM
D_IN = OFF_WI + IDX_HEADS

LANES = 128
SUBLANES = 8
VMEM_LIMIT = 56 * 1024 * 1024
KEY_CHUNK = 512
TIE_ROUNDS = 4
HALO = 16
TM_PROJ = 512
TM_MIX = 256
TM_ROUTE = 512
TM_MOE = 512
TM_COMB = 256
IDXT_ROWS = IDX_HEADS * IDX_DIM + 16
WT_ROWS = ATTN_DIM + IDXT_ROWS
INT_MIN = -(2 ** 31)
NEG_BIG = -1e30


def _cparams(sem, flags=None):
    return pltpu.CompilerParams(dimension_semantics=sem, vmem_limit_bytes=VMEM_LIMIT, flags=flags)


def _log2(n):
    k = n.bit_length() - 1
    assert 1 << k == n
    return k


def _dot(a, b):
    return jnp.dot(a, b, preferred_element_type=F32)


def _dot_nt(a, b):
    return lax.dot_general(a, b, (((1,), (1,)), ((), ())), preferred_element_type=F32)


def _tree(x, op):
    parts = [x[j * SUBLANES:(j + 1) * SUBLANES, :] for j in range(x.shape[0] // SUBLANES)]
    while len(parts) > 1:
        nxt = [op(parts[j], parts[j + 1]) for j in range(0, len(parts) - 1, 2)]
        if len(parts) % 2:
            nxt.append(parts[-1])
        parts = nxt
    return parts[0]


def _layernorm_rows(z, g, b):
    mu = jnp.mean(z, axis=-1, keepdims=True)
    d = z - mu
    var = jnp.mean(d * d, axis=-1, keepdims=True)
    return d * lax.rsqrt(var + LN_EPS) * g + b


def _inproj_kernel(x_ref, wn_ref, wt_ref, kg_ref, kb_ref,
                   a_ref, q_ref, k_ref, ki_ref, vt_ref, it_ref):
    x = x_ref[...]
    a_ref[...] = _dot(x, wn_ref[:, 0:OFF_Q])
    q_ref[...] = (_dot(x, wn_ref[:, OFF_Q:OFF_K]) * (HEAD_DIM ** -0.5)).astype(BF16)
    k_ref[...] = _dot(x, wn_ref[:, OFF_K:OFF_V]).astype(BF16)
    ki = _dot(x, wn_ref[:, OFF_V:OFF_V + LANES])[:, 0:IDX_DIM]
    ki_ref[...] = _layernorm_rows(ki, kg_ref[...], kb_ref[...])
    t = _dot_nt(wt_ref[...], x)
    for c in range(TM_PROJ // KEY_CHUNK):
        vt_ref[c] = t[0:ATTN_DIM, c * KEY_CHUNK:(c + 1) * KEY_CHUNK].astype(BF16)
    it_ref[...] = t[ATTN_DIM:WT_ROWS, :]


def _inproj(xb, wn, wt, kg, kb):
    T = xb.shape[0]
    nt = T // TM_PROJ
    cpt = TM_PROJ // KEY_CHUNK
    full = lambda i: (0, 0)
    return pl.pallas_call(
        _inproj_kernel,
        grid=(nt,),
        in_specs=[
            pl.BlockSpec((TM_PROJ, D_MODEL), lambda i: (i, 0)),
            pl.BlockSpec(wn.shape, full),
            pl.BlockSpec(wt.shape, full),
            pl.BlockSpec((1, IDX_DIM), full),
            pl.BlockSpec((1, IDX_DIM), full),
        ],
        out_specs=[
            pl.BlockSpec((TM_PROJ, OFF_Q), lambda i: (i, 0)),
            pl.BlockSpec((TM_PROJ, ATTN_DIM), lambda i: (i, 0)),
            pl.BlockSpec((TM_PROJ, ATTN_DIM), lambda i: (i, 0)),
            pl.BlockSpec((TM_PROJ, IDX_DIM), lambda i: (i, 0)),
            pl.BlockSpec((cpt, ATTN_DIM, KEY_CHUNK), lambda i: (i, 0, 0)),
            pl.BlockSpec((IDXT_ROWS, TM_PROJ), lambda i: (0, i)),
        ],
        out_shape=[
            jax.ShapeDtypeStruct((T, OFF_Q), F32),
            jax.ShapeDtypeStruct((T, ATTN_DIM), BF16),
            jax.ShapeDtypeStruct((T, ATTN_DIM), BF16),
            jax.ShapeDtypeStruct((T, IDX_DIM), F32),
            jax.ShapeDtypeStruct((T // KEY_CHUNK, ATTN_DIM, KEY_CHUNK), BF16),
            jax.ShapeDtypeStruct((IDXT_ROWS, T), F32),
        ],
        compiler_params=_cparams(("arbitrary",)),
        name="inproj",
    )(xb, wn, wt, kg, kb)


def _attn_kernel(n_sel, it_ref, ki_ref, q_ref, k_ref, vt_ref, o_ref,
                 keys_ref, rhs_ref, acc_ref, lg_ref):
    i = pl.program_id(1)
    nch = lax.shift_right_logical(i * Q_BLOCK + Q_BLOCK + KEY_CHUNK - 1, _log2(KEY_CHUNK))
    q0 = i * Q_BLOCK
    KC = KEY_CHUNK
    row = lax.broadcasted_iota(I32, (KC, Q_BLOCK), 0)
    lane = lax.broadcasted_iota(I32, (KC, Q_BLOCK), 1)
    qpos = q0 + lane

    qcat = jnp.concatenate(
        [it_ref[h * IDX_DIM:(h + 1) * IDX_DIM, :] for h in range(IDX_HEADS)], axis=1).astype(BF16)
    w_all = it_ref[IDX_HEADS * IDX_DIM:IDX_HEADS * IDX_DIM + IDX_HEADS, :] * (
        (IDX_HEADS ** -0.5) * (IDX_DIM ** -0.5))

    def score_body(c, carry):
        base = pl.multiple_of(c * KC, KC)
        kic = ki_ref[pl.ds(base, KC), :].astype(BF16)
        s = _dot(kic, qcat)
        sc = jnp.zeros((KC, Q_BLOCK), F32)
        for h in range(IDX_HEADS):
            sc = sc + jnp.maximum(s[:, h * Q_BLOCK:(h + 1) * Q_BLOCK], 0.0) * w_all[h:h + 1, :]
        bits = lax.bitcast_convert_type(sc, I32)
        key = bits ^ ((bits >> 31) & jnp.int32(0x7FFFFFFF))
        key = jnp.where(bits == jnp.int32(INT_MIN), jnp.int32(0), key)
        key = jnp.where(base + row <= qpos, key, jnp.int32(INT_MIN))
        keys_ref[pl.ds(base, KC), :] = key
        return carry

    lax.fori_loop(0, nch, score_body, 0)

    def count_ge(cand):
        def body(c, acc):
            base = pl.multiple_of(c * KC, KC)
            blk = keys_ref[pl.ds(base, KC), :]
            hit = jnp.where(blk >= cand, 1.0, 0.0).astype(F32)
            return acc + _tree(hit, jnp.add)
        acc = lax.fori_loop(0, nch, body, jnp.zeros((SUBLANES, Q_BLOCK), F32))
        return jnp.sum(acc, axis=0, keepdims=True)

    want = jnp.float32(n_sel)
    zero_row = jnp.zeros((1, Q_BLOCK), I32)
    cnt0 = count_ge(zero_row)
    lo = jnp.where(cnt0 >= want, zero_row, jnp.int32(INT_MIN))

    def bit_body(it, carry):
        lo, cnt_lo = carry
        cand = lo + lax.shift_left(jnp.int32(1), jnp.int32(30) - it)
        cnt = count_ge(cand)
        ok = cnt >= want
        return jnp.where(ok, cand, lo), jnp.where(ok, cnt, cnt_lo)

    lo, cnt_lo = lax.fori_loop(0, 31, bit_body, (lo, cnt0))
    tau = jnp.maximum(lo, jnp.int32(INT_MIN + 1))

    n_causal = q0 + lax.broadcasted_iota(I32, (1, Q_BLOCK), 1) + 1
    excess = jnp.where(n_causal > n_sel, cnt_lo - want, 0.0)
    max_excess = jnp.max(excess)

    @pl.when((max_excess > 0.0) & (max_excess <= float(TIE_ROUNDS)))
    def _():
        def one_round(left):
            def find(c, acc):
                base = pl.multiple_of(c * KC, KC)
                blk = keys_ref[pl.ds(base, KC), :]
                return jnp.maximum(acc, _tree(jnp.where(blk == tau, (base + row).astype(F32), -1.0),
                                              jnp.maximum))
            top = lax.fori_loop(0, nch, find, jnp.full((SUBLANES, Q_BLOCK), -1.0, F32))
            top = jnp.max(top, axis=0, keepdims=True).astype(I32)
            hit_lane = left > 0.0

            def retire(c, carry):
                base = pl.multiple_of(c * KC, KC)
                blk = keys_ref[pl.ds(base, KC), :]
                keys_ref[pl.ds(base, KC), :] = jnp.where(
                    hit_lane & (blk == tau) & (base + row == top), jnp.int32(INT_MIN), blk)
                return carry

            lax.fori_loop(0, nch, retire, 0)
            return left - 1.0

        lax.while_loop(lambda left: jnp.max(left) > 0.0, one_round, excess)

    @pl.when(max_excess > float(TIE_ROUNDS))
    def _():
        def count_tied_before(limit):
            def body(c, acc):
                base = pl.multiple_of(c * KC, KC)
                blk = keys_ref[pl.ds(base, KC), :]
                hit = jnp.where((blk == tau) & (base + row < limit), 1.0, 0.0).astype(F32)
                return acc + _tree(hit, jnp.add)
            acc = lax.fori_loop(0, nch, body, jnp.zeros((SUBLANES, Q_BLOCK), F32))
            return jnp.sum(acc, axis=0, keepdims=True)

        n_keys = keys_ref.shape[0]
        keep = count_tied_before(jnp.full((1, Q_BLOCK), n_keys, I32)) - excess

        def pos_body(it, y):
            cand = y + lax.shift_left(jnp.int32(1), jnp.int32(_log2(n_keys) - 1) - it)
            return jnp.where(count_tied_before(cand) < keep, cand, y)

        last = lax.fori_loop(0, _log2(n_keys), pos_body, jnp.zeros((1, Q_BLOCK), I32))

        def retire(c, carry):
            base = pl.multiple_of(c * KC, KC)
            blk = keys_ref[pl.ds(base, KC), :]
            keys_ref[pl.ds(base, KC), :] = jnp.where(
                (blk == tau) & (base + row > last), jnp.int32(INT_MIN), blk)
            return carry

        lax.fori_loop(0, nch, retire, 0)

    q = q_ref[...]
    lane_q = lax.broadcasted_iota(I32, (Q_BLOCK, 2 * HEAD_DIM), 1)
    for p in range(ATTN_HEADS // 2):
        qp = q[:, p * 2 * HEAD_DIM:(p + 1) * 2 * HEAD_DIM]
        rhs_ref[p, 0:Q_BLOCK, :] = jnp.where(lane_q < HEAD_DIM, qp, jnp.zeros_like(qp))
        rhs_ref[p, Q_BLOCK:2 * Q_BLOCK, :] = jnp.where(lane_q >= HEAD_DIM, qp, jnp.zeros_like(qp))
    acc_ref[...] = jnp.zeros_like(acc_ref)

    def stage_a_prep(c):
        base = pl.multiple_of(c * KC, KC)
        tk = (base + row - (q0 + Q_BLOCK - 1)).astype(F32)
        tkm = jnp.where(keys_ref[pl.ds(base, KC), :] >= tau, tk, NEG_BIG)
        return tkm, k_ref[pl.ds(base, KC), :]

    def stage_a_pair(c, p, tkm, kc):
        l2 = _dot_nt(kc[:, p * 2 * HEAD_DIM:(p + 1) * 2 * HEAD_DIM], rhs_ref[p])
        mcs = []
        for hh in range(2):
            h = 2 * p + hh
            slope = 2.0 ** (-8.0 * (h + 1) / ATTN_HEADS)
            lg = l2[:, hh * Q_BLOCK:(hh + 1) * Q_BLOCK] + slope * tkm
            lg_ref[h] = lg
            mcs.append(jnp.max(_tree(lg, jnp.maximum), axis=0, keepdims=True))
        return mcs

    def stage_b_pair(c, p, m_new, alpha):
        sums = []
        for hh in range(2):
            h = 2 * p + hh
            pr = jnp.exp(lg_ref[h] - m_new[h:h + 1, :])
            sums.append(jnp.sum(_tree(pr, jnp.add), axis=0, keepdims=True))
            pv = _dot(vt_ref[c, h * HEAD_DIM:(h + 1) * HEAD_DIM, :], pr.astype(BF16))
            acc_ref[h * HEAD_DIM:(h + 1) * HEAD_DIM, :] = (
                acc_ref[h * HEAD_DIM:(h + 1) * HEAD_DIM, :] * alpha[h:h + 1, :] + pv)
        return sums

    def stage_b_all(c, m_old, l_old, mc):
        m_new = jnp.maximum(m_old, mc)
        alpha = jnp.exp(m_old - m_new)
        return m_new, alpha

    tkm0, kc0 = stage_a_prep(0)
    mc0 = jnp.concatenate(sum([stage_a_pair(0, p, tkm0, kc0) for p in range(ATTN_HEADS // 2)], []), axis=0)

    def attn_body(c, carry):
        m_old, l_old, mc_prev = carry
        m_new, alpha = stage_b_all(c - 1, m_old, l_old, mc_prev)
        tkm, kc = stage_a_prep(c)
        mcs, sums = [], []
        for p in range(ATTN_HEADS // 2):
            sums += stage_b_pair(c - 1, p, m_new, alpha)
            mcs += stage_a_pair(c, p, tkm, kc)
        return m_new, l_old * alpha + jnp.concatenate(sums, axis=0), jnp.concatenate(mcs, axis=0)

    m0 = jnp.full((ATTN_HEADS, Q_BLOCK), NEG_BIG, F32)
    l0 = jnp.zeros((ATTN_HEADS, Q_BLOCK), F32)
    m_old, l_old, mc_prev = lax.fori_loop(1, nch, attn_body, (m0, l0, mc0))
    m_new, alpha = stage_b_all(nch - 1, m_old, l_old, mc_prev)
    sums = sum([stage_b_pair(nch - 1, p, m_new, alpha) for p in range(ATTN_HEADS // 2)], [])
    l_fin = l_old * alpha + jnp.concatenate(sums, axis=0)

    outs = []
    for h in range(ATTN_HEADS):
        outs.append(acc_ref[h * HEAD_DIM:(h + 1) * HEAD_DIM, :] / l_fin[h:h + 1, :])
    o_ref[...] = jnp.concatenate(outs, axis=0).T


def _attn(it, ki, q, k, vt, B, S):
    T = B * S
    nb = S // Q_BLOCK
    n_sel = min(INDEX_TOPK_MAX, S // 4)
    return pl.pallas_call(
        functools.partial(_attn_kernel, n_sel),
        grid=(B, nb),
        in_specs=[
            pl.BlockSpec((IDXT_ROWS, Q_BLOCK), lambda b, i: (0, b * nb + i)),
            pl.BlockSpec((S, IDX_DIM), lambda b, i: (b, 0)),
            pl.BlockSpec((Q_BLOCK, ATTN_DIM), lambda b, i: (b * nb + i, 0)),
            pl.BlockSpec((S, ATTN_DIM), lambda b, i: (b, 0)),
            pl.BlockSpec((S // KEY_CHUNK, ATTN_DIM, KEY_CHUNK), lambda b, i: (b, 0, 0)),
        ],
        out_specs=pl.BlockSpec((Q_BLOCK, ATTN_DIM), lambda b, i: (b * nb + i, 0)),
        out_shape=jax.ShapeDtypeStruct((T, ATTN_DIM), F32),
        scratch_shapes=[
            pltpu.VMEM((S, Q_BLOCK), I32),
            pltpu.VMEM((ATTN_HEADS // 2, 2 * Q_BLOCK, 2 * HEAD_DIM), BF16),
            pltpu.VMEM((ATTN_DIM, Q_BLOCK), F32),
            pltpu.VMEM((ATTN_HEADS, KEY_CHUNK, Q_BLOCK), F32),
        ],
        compiler_params=_cparams(("arbitrary", "arbitrary")),
        name="attn",
    )(it, ki, q, k, vt)


def _mixout_kernel(tiles_per_seq, a_ref, halo_ref, y_ref, x_ref, cw_ref, wpool_ref, ps_ref,
                   wout_ref, g_ref, b_ref, o_ref, ext_ref):
    i = pl.program_id(0)
    tm = TM_MIX
    first = lax.rem(i, tiles_per_seq) == 0
    halo = jnp.where(first, 0.0, halo_ref[...])
    a = a_ref[...]
    h_c, gb_c, gc_c, p_c = (a[:, j * CONV_DIM:(j + 1) * CONV_DIM] for j in range(4))

    t0 = 2 * HALO
    n_ext = t0 + tm
    ext_ref[0:HALO, :] = jnp.zeros((HALO, CONV_DIM), F32)

    ext_ref[HALO:t0, :] = halo[:, 2 * CONV_DIM:3 * CONV_DIM] * halo[:, 0:CONV_DIM]
    u = gc_c * h_c
    ext_ref[t0:n_ext, :] = u
    cw = cw_ref[...]
    conv = cw[2:3, :] * u
    conv = conv + cw[1:2, :] * ext_ref[t0 - 1:n_ext - 1, :]
    conv = conv + cw[0:1, :] * ext_ref[t0 - 2:n_ext - 2, :]
    y_conv = gb_c * conv

    ext_ref[HALO:t0, :] = halo[:, OFF_P:OFF_P + POOL_DIM]
    ext_ref[t0:n_ext, :] = p_c
    sums = {}
    step = 1
    while step < POOL_WINDOWS[-1]:
        cur = ext_ref[HALO:n_ext, :] + ext_ref[HALO - step:n_ext - step, :]
        ext_ref[HALO:n_ext, :] = cur
        step *= 2
        sums[step] = ext_ref[t0:n_ext, :]
    tpos = (lax.rem(i, tiles_per_seq) * tm + lax.broadcasted_iota(I32, (tm, POOL_DIM), 0) + 1).astype(F32)
    grp = lax.shift_right_logical(lax.broadcasted_iota(I32, (tm, POOL_DIM), 1), _log2(POOL_GROUP))
    mean = jnp.zeros((tm, POOL_DIM), F32)
    for gi, w in enumerate(POOL_WINDOWS):
        mean = jnp.where(grp == gi, sums[w] / jnp.minimum(tpos, float(w)), mean)
    mixed = mean - p_c
    y_pool = _dot(mixed.astype(BF16), wpool_ref[...]) * ps_ref[...]

    mix = _dot(y_conv.astype(BF16), wout_ref[0:CONV_DIM, :])
    mix = mix + _dot(y_pool.astype(BF16), wout_ref[CONV_DIM:CONV_DIM + POOL_DIM, :])
    mix = mix + _dot(y_ref[...].astype(BF16), wout_ref[CONV_DIM + POOL_DIM:D_MODEL, :])
    z = DN_ALPHA * x_ref[...] + mix
    o_ref[...] = _layernorm_rows(z, g_ref[...], b_ref[...])


def _mixout(a, y_attn, x, cw, wpool_bd, ps, wout, g, b, S):
    T = a.shape[0]
    nt = T // TM_MIX
    tps = S // TM_MIX
    hb = TM_MIX // HALO
    full = lambda i: (0, 0)
    return pl.pallas_call(
        functools.partial(_mixout_kernel, tps),
        grid=(nt,),
        in_specs=[
            pl.BlockSpec((TM_MIX, OFF_Q), lambda i: (i, 0)),
            pl.BlockSpec((HALO, OFF_Q), lambda i: (jnp.maximum(i * hb - 1, 0), 0)),
            pl.BlockSpec((TM_MIX, ATTN_DIM), lambda i: (i, 0)),
            pl.BlockSpec((TM_MIX, D_MODEL), lambda i: (i, 0)),
            pl.BlockSpec((CONV_WIDTH, CONV_DIM), full),
            pl.BlockSpec((POOL_DIM, POOL_DIM), full),
            pl.BlockSpec((1, POOL_DIM), full),
            pl.BlockSpec((D_MODEL, D_MODEL), full),
            pl.BlockSpec((1, D_MODEL), full),
            pl.BlockSpec((1, D_MODEL), full),
        ],
        out_specs=pl.BlockSpec((TM_MIX, D_MODEL), lambda i: (i, 0)),
        out_shape=jax.ShapeDtypeStruct((T, D_MODEL), F32),
        scratch_shapes=[pltpu.VMEM((2 * HALO + TM_MIX, CONV_DIM), F32)],
        compiler_params=_cparams(("arbitrary",)),
        name="mixout",
    )(a, a, y_attn, x, cw, wpool_bd, ps, wout, g, b)


def _router_kernel(x_ref, rw_ref, rb_ref, idx_ref, gate_ref, cnt_ref):
    i = pl.program_id(0)
    tm = TM_ROUTE
    logits = lax.dot_general(rw_ref[...], x_ref[...], (((1,), (1,)), ((), ())),
                             precision=lax.Precision.HIGHEST,
                             preferred_element_type=F32) + rb_ref[...]
    erow = lax.broadcasted_iota(I32, (N_EXPERTS, tm), 0).astype(F32)
    work = logits
    vals, idxs = [], []
    multi = jnp.zeros((N_EXPERTS, tm), F32)
    for _ in range(TOP_K):
        mx = jnp.max(work, axis=0, keepdims=True)
        pick = jnp.min(jnp.where(work == mx, erow, float(N_EXPERTS)), axis=0, keepdims=True)
        hit = erow == pick
        work = jnp.where(hit, -jnp.inf, work)
        multi = multi + hit.astype(F32)
        vals.append(mx)
        idxs.append(pick)
    es = [jnp.exp(v - vals[0]) for v in vals]
    den = es[0] + es[1] + es[2] + es[3]
    idx_ref[...] = jnp.concatenate(idxs, axis=0).astype(I32)
    gate_ref[...] = jnp.concatenate([e / den for e in es] + [jnp.zeros((SUBLANES - TOP_K, tm), F32)], axis=0)
    part = multi[:, 0:LANES]
    for j in range(1, tm // LANES):
        part = part + multi[:, j * LANES:(j + 1) * LANES]

    @pl.when(i == 0)
    def _():
        cnt_ref[...] = jnp.zeros_like(cnt_ref)

    cnt_ref[...] += part


def _router(x1, rwT, rb):
    T = x1.shape[0]
    nt = T // TM_ROUTE
    full = lambda i: (0, 0)
    return pl.pallas_call(
        _router_kernel,
        grid=(nt,),
        in_specs=[
            pl.BlockSpec((TM_ROUTE, D_MODEL), lambda i: (i, 0)),
            pl.BlockSpec((N_EXPERTS, D_MODEL), full),
            pl.BlockSpec((N_EXPERTS, 1), full),
        ],
        out_specs=[
            pl.BlockSpec((TOP_K, TM_ROUTE), lambda i: (0, i)),
            pl.BlockSpec((SUBLANES, TM_ROUTE), lambda i: (0, i)),
            pl.BlockSpec((N_EXPERTS, LANES), full),
        ],
        out_shape=[
            jax.ShapeDtypeStruct((TOP_K, T), I32),
            jax.ShapeDtypeStruct((SUBLANES, T), F32),
            jax.ShapeDtypeStruct((N_EXPERTS, LANES), F32),
        ],
        compiler_params=_cparams(("arbitrary",)),
        name="router",
    )(x1, rwT, rb)


def _ranks_kernel(n_tiles_pad, idx_ref, cnt_ref, dest_ref, meta_ref, tri_ref, start_ref, carry_ref):
    i = pl.program_id(0)
    tm = TM_ROUTE
    erow = lax.broadcasted_iota(I32, (N_EXPERTS, LANES), 0)
    elane = lax.broadcasted_iota(I32, (N_EXPERTS, LANES), 1)

    @pl.when(i == 0)
    def _():
        cnt = jnp.sum(cnt_ref[...], axis=1, keepdims=True)
        cnt_i = jnp.broadcast_to(cnt, (N_EXPERTS, LANES)).astype(I32)
        padded = lax.shift_left(lax.shift_right_logical(cnt_i + (TM_MOE - 1), _log2(TM_MOE)), _log2(TM_MOE))
        r = lax.broadcasted_iota(I32, (N_EXPERTS, N_EXPERTS), 0)
        c = lax.broadcasted_iota(I32, (N_EXPERTS, N_EXPERTS), 1)
        low = (c <= r).astype(F32)
        pad_end = lax.dot_general(low, padded.astype(F32), (((1,), (0,)), ((), ())),
                                  precision=lax.Precision.HIGHEST, preferred_element_type=F32)
        pad_start = pad_end - padded.astype(F32)
        start_ref[...] = pad_start
        carry_ref[...] = jnp.zeros_like(carry_ref)
        a = lax.broadcasted_iota(I32, (tm, tm), 0)
        bcol = lax.broadcasted_iota(I32, (tm, tm), 1)
        tri_ref[...] = (a < bcol).astype(BF16)
        ntp = n_tiles_pad
        tstart = (lax.broadcasted_iota(I32, (N_EXPERTS, ntp), 1) * TM_MOE).astype(F32)
        pe = jnp.concatenate([pad_end] * (ntp // LANES), axis=1)
        texp = jnp.sum((pe <= tstart).astype(F32), axis=0, keepdims=True)
        texp = jnp.minimum(texp, float(N_EXPERTS - 1)).astype(I32)
        total = jnp.max(pad_end, axis=0, keepdims=True)
        n_used = lax.shift_right_logical(total.astype(I32), _log2(TM_MOE))
        zstart = jnp.sum(jnp.where(erow == elane, pad_start + cnt_i.astype(F32), 0.0),
                         axis=0, keepdims=True).astype(I32)
        zlen = jnp.sum(jnp.where(erow == elane, (padded - cnt_i).astype(F32), 0.0),
                       axis=0, keepdims=True).astype(I32)
        lanes_pad = jnp.zeros((1, ntp - LANES), I32)
        meta_ref[...] = jnp.concatenate(
            [texp,
             jnp.concatenate([n_used, lanes_pad], axis=1),
             jnp.concatenate([zstart, lanes_pad], axis=1),
             jnp.concatenate([zlen, lanes_pad], axis=1),
             jnp.zeros((SUBLANES - 4, ntp), I32)], axis=0)

    idx = idx_ref[...]
    erow_t = lax.broadcasted_iota(I32, (N_EXPERTS, tm), 0)
    hits = [erow_t == idx[k:k + 1, :] for k in range(TOP_K)]
    multi = hits[0].astype(F32)
    for k in range(1, TOP_K):
        multi = multi + hits[k].astype(F32)
    prefix = _dot(multi.astype(BF16), tri_ref[...])
    base = jnp.concatenate([carry_ref[...] + start_ref[...]] * (tm // LANES), axis=1)
    tot = prefix + base
    dest_ref[...] = jnp.concatenate(
        [jnp.sum(jnp.where(hits[k], tot, 0.0), axis=0, keepdims=True) for k in range(TOP_K)],
        axis=0).astype(I32)
    carry_ref[...] += jnp.broadcast_to(jnp.sum(multi, axis=1, keepdims=True), (N_EXPERTS, LANES))


def _ranks(idxT, cnt, n_tiles_pad):
    T = idxT.shape[1]
    nt = T // TM_ROUTE
    full = lambda i: (0, 0)
    return pl.pallas_call(
        functools.partial(_ranks_kernel, n_tiles_pad),
        grid=(nt,),
        in_specs=[
            pl.BlockSpec((TOP_K, TM_ROUTE), lambda i: (0, i)),
            pl.BlockSpec((N_EXPERTS, LANES), full),
        ],
        out_specs=[
            pl.BlockSpec((TOP_K, TM_ROUTE), lambda i: (0, i)),
            pl.BlockSpec((SUBLANES, n_tiles_pad), full),
        ],
        out_shape=[
            jax.ShapeDtypeStruct((TOP_K, T), I32),
            jax.ShapeDtypeStruct((SUBLANES, n_tiles_pad), I32),
        ],
        scratch_shapes=[
            pltpu.VMEM((TM_ROUTE, TM_ROUTE), BF16),
            pltpu.VMEM((N_EXPERTS, LANES), F32),
            pltpu.VMEM((N_EXPERTS, LANES), F32),
        ],
        compiler_params=_cparams(("arbitrary",)),
        name="ranks",
    )(idxT, cnt)


RT = D_MODEL // LANES
DMA_UNROLL = 8


def _row_copy(src_ref, s, dst_ref, d, sem):
    return pltpu.make_async_copy(src_ref.at[pl.ds(pl.multiple_of(s * RT, RT), RT)],
                                 dst_ref.at[pl.ds(pl.multiple_of(d * RT, RT), RT)], sem)


def _to_row_tiled(dst_ref, val, rows):
    for s in range(RT):
        dst_ref[pl.ds(s, rows, stride=RT), :] = val[:, s * LANES:(s + 1) * LANES]


def _from_row_tiled(src_ref, rows):
    return [src_ref[pl.ds(s, rows, stride=RT), :] for s in range(RT)]


def _scatter_kernel(meta_ref, x_ref, dest_ref, xs_ref, stage_ref, zero_ref, sem):
    i = pl.program_id(0)
    tm = TM_ROUTE

    @pl.when(i == 0)
    def _():
        zero_ref[...] = jnp.zeros_like(zero_ref)

        def zcopy(row, nrows):
            off = pl.multiple_of(row * RT, RT)
            return pltpu.make_async_copy(zero_ref.at[pl.ds(0, nrows * RT)],
                                         xs_ref.at[pl.ds(off, nrows * RT)], sem)

        def pad_fill(wait):
            def body(e, c):
                row = meta_ref[2, e]
                plen = meta_ref[3, e]
                for bit in reversed(range(_log2(TM_MOE))):
                    size = 1 << bit
                    has = (plen & size) != 0

                    @pl.when(has)
                    def _():
                        cp = zcopy(row, size)
                        cp.wait() if wait else cp.start()

                    row = row + jnp.where(has, size, 0)
                return c
            lax.fori_loop(0, N_EXPERTS, body, 0)

        def tail_fill(wait):
            def body(j, c):
                cp = zcopy(j * TM_MOE, TM_MOE)
                cp.wait() if wait else cp.start()
                return c
            lax.fori_loop(meta_ref[1, 0], xs_ref.shape[0] // (TM_MOE * RT), body, 0)

        pad_fill(False)
        tail_fill(False)
        pad_fill(True)
        tail_fill(True)

    _to_row_tiled(stage_ref, x_ref[...], tm)

    def start(t, c):
        for k in range(TOP_K):
            _row_copy(stage_ref, t, xs_ref, dest_ref[k, t], sem).start(priority=k % 2)
        return c

    lax.fori_loop(0, tm, start, 0, unroll=DMA_UNROLL)
    for k in range(TOP_K):
        pltpu.make_async_copy(stage_ref, xs_ref.at[pl.ds(0, tm * RT)], sem).wait()


def _scatter(meta, x1, destT, n_rows_alloc):
    T = x1.shape[0]
    nt = T // TM_ROUTE
    return pl.pallas_call(
        _scatter_kernel,
        grid_spec=pltpu.PrefetchScalarGridSpec(
            num_scalar_prefetch=1,
            grid=(nt,),
            in_specs=[
                pl.BlockSpec((TM_ROUTE, D_MODEL), lambda i, m: (i, 0)),
                pl.BlockSpec((TOP_K, TM_ROUTE), lambda i, m: (0, i), memory_space=pltpu.SMEM),
            ],
            out_specs=pl.BlockSpec(memory_space=pl.ANY),
            scratch_shapes=[
                pltpu.VMEM((TM_ROUTE * RT, LANES), F32),
                pltpu.VMEM((TM_MOE * RT, LANES), F32),
                pltpu.SemaphoreType.DMA(()),
            ],
        ),
        out_shape=jax.ShapeDtypeStruct((n_rows_alloc * RT, LANES), F32),
        compiler_params=_cparams(("arbitrary",)),
        name="scatter",
    )(meta, x1, destT)


def _gmm_kernel(meta_ref, xs_ref, wgu_ref, bgu_ref, wd_ref, bd_ref, ys_ref, wgu_b, wd_b, lhs_ref, act_ref):
    j = pl.program_id(0)
    n_used = meta_ref[1, 0]
    e_now = meta_ref[0, j]
    e_prev = meta_ref[0, jnp.maximum(j - 1, 0)]
    used = j < n_used

    @pl.when(used & ((j == 0) | (e_now != e_prev)))
    def _():
        wgu_b[...] = wgu_ref[0, 0].astype(BF16)
        wd_b[...] = wd_ref[0, 0].astype(BF16)

    @pl.when(used)
    def _():
        for s, piece in enumerate(_from_row_tiled(xs_ref, TM_MOE)):
            lhs_ref[:, s * LANES:(s + 1) * LANES] = piece.astype(BF16)
        x = lhs_ref[...]
        nc = 256
        for c in range(D_FF // nc):
            gate = _dot(x, wgu_b[:, c * nc:(c + 1) * nc]) + bgu_ref[0, 0, :, c * nc:(c + 1) * nc]
            up = (_dot(x, wgu_b[:, D_FF + c * nc:D_FF + (c + 1) * nc])
                  + bgu_ref[0, 0, :, D_FF + c * nc:D_FF + (c + 1) * nc])
            gate = jnp.minimum(gate, SWIGLU_LIMIT)
            up = jnp.clip(up, -SWIGLU_LIMIT, SWIGLU_LIMIT)
            act = gate * jax.nn.sigmoid(SWIGLU_ALPHA * gate) * (up + 1.0)
            act_ref[:, c * nc:(c + 1) * nc] = act.astype(BF16)
        _to_row_tiled(ys_ref, _dot(act_ref[...], wd_b[...]) + bd_ref[0, 0], TM_MOE)

    @pl.when(jnp.logical_not(used))
    def _():
        ys_ref[...] = jnp.zeros_like(ys_ref)


def _gmm(meta, xs, w_gu, b_gu, w_down, b_down, n_tiles, layer):
    last = lambda m: jnp.maximum(m[1, 0] - 1, 0)
    return pl.pallas_call(
        _gmm_kernel,
        grid_spec=pltpu.PrefetchScalarGridSpec(
            num_scalar_prefetch=1,
            grid=(n_tiles,),
            in_specs=[
                pl.BlockSpec((TM_MOE * RT, LANES), lambda j, m: (jnp.minimum(j, last(m)), 0)),
                pl.BlockSpec((1, 1, D_MODEL, 2 * D_FF), lambda j, m: (layer, m[0, j], 0, 0)),
                pl.BlockSpec((1, 1, 1, 2 * D_FF), lambda j, m: (layer, m[0, j], 0, 0)),
                pl.BlockSpec((1, 1, D_FF, D_MODEL), lambda j, m: (layer, m[0, j], 0, 0)),
                pl.BlockSpec((1, 1, 1, D_MODEL), lambda j, m: (layer, m[0, j], 0, 0)),
            ],
            out_specs=pl.BlockSpec((TM_MOE * RT, LANES), lambda j, m: (j, 0)),
            scratch_shapes=[
                pltpu.VMEM((D_MODEL, 2 * D_FF), BF16),
                pltpu.VMEM((D_FF, D_MODEL), BF16),
                pltpu.VMEM((TM_MOE, D_MODEL), BF16),
                pltpu.VMEM((TM_MOE, D_FF), BF16),
            ],
        ),
        out_shape=jax.ShapeDtypeStruct((n_tiles * TM_MOE * RT, LANES), F32),
        compiler_params=_cparams(("arbitrary",)),
        name="gmm",
    )(meta, xs, w_gu, b_gu, w_down, b_down)


def _combine_kernel(x_ref, dest_ref, gate_ref, ys_ref, g_ref, b_ref, o_ref, ob_ref, buf_ref, sem):
    tm = TM_COMB

    def start(t, c):
        for k in range(TOP_K):
            _row_copy(ys_ref, dest_ref[k, t], buf_ref.at[k], t, sem).start(priority=k % 2)
        return c

    lax.fori_loop(0, tm, start, 0, unroll=DMA_UNROLL)
    for k in range(TOP_K):
        pltpu.make_async_copy(ys_ref.at[pl.ds(0, tm * RT)], buf_ref.at[k], sem).wait()
    gates = jnp.concatenate(
        [gate_ref[...], jnp.zeros((LANES - SUBLANES, tm), F32)], axis=0).T
    pieces = None
    for k in range(TOP_K):
        gk = gates[:, k:k + 1]
        rows = [p * gk for p in _from_row_tiled(buf_ref.at[k], tm)]
        pieces = rows if pieces is None else [a + r for a, r in zip(pieces, rows)]
    z = DN_ALPHA * x_ref[...] + jnp.concatenate(pieces, axis=1)
    out = _layernorm_rows(z, g_ref[...], b_ref[...])
    o_ref[...] = out
    ob_ref[...] = out.astype(BF16)


def _combine(x1, destT, gateT, ys, g, b):
    T = x1.shape[0]
    nt = T // TM_COMB
    full = lambda i: (0, 0)
    return pl.pallas_call(
        _combine_kernel,
        grid=(nt,),
        in_specs=[
            pl.BlockSpec((TM_COMB, D_MODEL), lambda i: (i, 0)),
            pl.BlockSpec((TOP_K, TM_COMB), lambda i: (0, i), memory_space=pltpu.SMEM),
            pl.BlockSpec((SUBLANES, TM_COMB), lambda i: (0, i)),
            pl.BlockSpec(memory_space=pl.ANY),
            pl.BlockSpec((1, D_MODEL), full),
            pl.BlockSpec((1, D_MODEL), full),
        ],
        out_specs=[
            pl.BlockSpec((TM_COMB, D_MODEL), lambda i: (i, 0)),
            pl.BlockSpec((TM_COMB, D_MODEL), lambda i: (i, 0)),
        ],
        out_shape=[
            jax.ShapeDtypeStruct((T, D_MODEL), F32),
            jax.ShapeDtypeStruct((T, D_MODEL), BF16),
        ],
        scratch_shapes=[
            pltpu.VMEM((TOP_K, TM_COMB * RT, LANES), F32),
            pltpu.SemaphoreType.DMA(()),
        ],
        compiler_params=_cparams(("arbitrary",)),
        name="combine",
    )(x1, destT, gateT, ys, g, b)


def _block_diag(w):
    g, c, _ = w.shape
    out = jnp.zeros((g * c, g * c), w.dtype)
    for i in range(g):
        out = out.at[i * c:(i + 1) * c, i * c:(i + 1) * c].set(w[i])
    return out


def kernel(x, w_in, conv_w, w_pool, pool_scale, idx_kn_g, idx_kn_b, w_out, ln1_g, ln1_b, router_w,
           router_b, w_gu, b_gu, w_down, b_down, ln2_g, ln2_b):
    B, S, D = x.shape
    T = B * S
    depth = w_in.shape[0]
    n_pairs = T * TOP_K
    n_tiles = (n_pairs + N_EXPERTS * (TM_MOE - 1)) // TM_MOE + 1
    n_tiles_pad = ((n_tiles + LANES - 1) // LANES) * LANES
    n_rows_alloc = (n_tiles + 1) * TM_MOE

    xf = x.reshape(T, D)
    xb = xf.astype(BF16)
    for l in range(depth):
        wl = w_in[l]
        wn = jnp.concatenate(
            [wl[:, 0:OFF_V], wl[:, OFF_KI:OFF_WI], jnp.zeros((D, LANES - IDX_DIM), F32)], axis=1).astype(BF16)
        wt = jnp.concatenate(
            [wl[:, OFF_V:OFF_QI], wl[:, OFF_QI:OFF_KI], wl[:, OFF_WI:D_IN],
             jnp.zeros((D, IDXT_ROWS - IDX_HEADS * IDX_DIM - IDX_HEADS), F32)], axis=1).T.astype(BF16)
        a, q, k, ki, vt, it = _inproj(xb, wn, wt, idx_kn_g[l].reshape(1, -1), idx_kn_b[l].reshape(1, -1))
        y_attn = _attn(it, ki, q, k, vt, B, S)
        x1 = _mixout(a, y_attn, xf, conv_w[l], _block_diag(w_pool[l]).astype(BF16),
                     pool_scale[l].reshape(1, -1), w_out[l].astype(BF16),
                     ln1_g[l].reshape(1, -1), ln1_b[l].reshape(1, -1), S)
        idxT, gateT, cnt = _router(x1, router_w[l].T, router_b[l].reshape(-1, 1))
        destT, meta = _ranks(idxT, cnt, n_tiles_pad)
        xs = _scatter(meta, x1, destT, n_rows_alloc)
        ys = _gmm(meta, xs, w_gu, b_gu.reshape(depth, N_EXPERTS, 1, -1), w_down,
                  b_down.reshape(depth, N_EXPERTS, 1, -1), n_tiles, l)
        xf, xb = _combine(x1, destT, gateT, ys, ln2_g[l].reshape(1, -1), ln2_b[l].reshape(1, -1))
    return xf.reshape(B, S, D)
```

```python
import functools

import jax
import jax.numpy as jnp
import numpy as np
from jax import lax
from jax.experimental import pallas as pl
from jax.experimental.pallas import tpu as pltpu

F32 = jnp.float32
BF16 = jnp.bfloat16
I32 = jnp.int32

D_MODEL = 1024
CONV_DIM = 256
CONV_WIDTH = 3
POOL_WINDOWS = (2, 4, 8, 16)
POOL_GROUP = 64
POOL_DIM = 256
ATTN_HEADS = 8
HEAD_DIM = 64
ATTN_DIM = 512
IDX_HEADS = 8
IDX_DIM = 32
INDEX_TOPK_MAX = 256
Q_BLOCK = 128
N_EXPERTS = 32
TOP_K = 4
D_FF = 1024
SWIGLU_ALPHA = 1.702
SWIGLU_LIMIT = 7.0
LN_EPS = 1e-5
DEPTH = 2
DN_ALPHA = (2.0 * DEPTH) ** 0.25

OFF_P = 3 * CONV_DIM
OFF_Q = OFF_P + POOL_DIM
OFF_K = OFF_Q + ATTN_DIM
OFF_V = OFF_K + ATTN_DIM
OFF_QI = OFF_V + ATTN_DIM
OFF_KI = OFF_QI + IDX_HEADS * IDX_DIM
OFF_WI = OFF_KI + IDX_DIM
D_IN = OFF_WI + IDX_HEADS

LANES = 128
SUBLANES = 8
VMEM_LIMIT = 56 * 1024 * 1024
KEY_CHUNK = 512
HALO = 16
TM_PROJ = 512
TM_MIX = 256
TM_ROUTE = 512
TM_MOE = 512
TM_COMB = 256
IDXT_ROWS = IDX_HEADS * IDX_DIM + 16
WT_ROWS = ATTN_DIM + IDXT_ROWS
INT_MIN = -(2 ** 31)
NEG_BIG = -1e30


def _cparams(sem, flags=None):
    return pltpu.CompilerParams(dimension_semantics=sem, vmem_limit_bytes=VMEM_LIMIT, flags=flags)


def _log2(n):
    k = n.bit_length() - 1
    assert 1 << k == n
    return k


def _dot(a, b):
    return jnp.dot(a, b, preferred_element_type=F32)


def _dot_nt(a, b):
    return lax.dot_general(a, b, (((1,), (1,)), ((), ())), preferred_element_type=F32)


def _tree(x, op):
    parts = [x[j * SUBLANES:(j + 1) * SUBLANES, :] for j in range(x.shape[0] // SUBLANES)]
    while len(parts) > 1:
        nxt = [op(parts[j], parts[j + 1]) for j in range(0, len(parts) - 1, 2)]
        if len(parts) % 2:
            nxt.append(parts[-1])
        parts = nxt
    return parts[0]


def _layernorm_rows(z, g, b):
    mu = jnp.mean(z, axis=-1, keepdims=True)
    d = z - mu
    var = jnp.mean(d * d, axis=-1, keepdims=True)
    return d * lax.rsqrt(var + LN_EPS) * g + b


def _inproj_kernel(x_ref, wn_ref, wt_ref, kg_ref, kb_ref,
                   a_ref, q_ref, k_ref, ki_ref, vt_ref, it_ref):
    x = x_ref[...]
    a_ref[...] = _dot(x, wn_ref[:, 0:OFF_Q])
    q_ref[...] = (_dot(x, wn_ref[:, OFF_Q:OFF_K]) * (HEAD_DIM ** -0.5)).astype(BF16)
    k_ref[...] = _dot(x, wn_ref[:, OFF_K:OFF_V]).astype(BF16)
    ki = _dot(x, wn_ref[:, OFF_V:OFF_V + LANES])[:, 0:IDX_DIM]
    ki_ref[...] = _layernorm_rows(ki, kg_ref[...], kb_ref[...])
    t = _dot_nt(wt_ref[...], x)
    for c in range(TM_PROJ // KEY_CHUNK):
        vt_ref[c] = t[0:ATTN_DIM, c * KEY_CHUNK:(c + 1) * KEY_CHUNK].astype(BF16)
    it_ref[...] = t[ATTN_DIM:WT_ROWS, :]


def _inproj(xb, wn, wt, kg, kb):
    T = xb.shape[0]
    nt = T // TM_PROJ
    cpt = TM_PROJ // KEY_CHUNK
    full = lambda i: (0, 0)
    return pl.pallas_call(
        _inproj_kernel,
        grid=(nt,),
        in_specs=[
            pl.BlockSpec((TM_PROJ, D_MODEL), lambda i: (i, 0)),
            pl.BlockSpec(wn.shape, full),
            pl.BlockSpec(wt.shape, full),
            pl.BlockSpec((1, IDX_DIM), full),
            pl.BlockSpec((1, IDX_DIM), full),
        ],
        out_specs=[
            pl.BlockSpec((TM_PROJ, OFF_Q), lambda i: (i, 0)),
            pl.BlockSpec((TM_PROJ, ATTN_DIM), lambda i: (i, 0)),
            pl.BlockSpec((TM_PROJ, ATTN_DIM), lambda i: (i, 0)),
            pl.BlockSpec((TM_PROJ, IDX_DIM), lambda i: (i, 0)),
            pl.BlockSpec((cpt, ATTN_DIM, KEY_CHUNK), lambda i: (i, 0, 0)),
            pl.BlockSpec((IDXT_ROWS, TM_PROJ), lambda i: (0, i)),
        ],
        out_shape=[
            jax.ShapeDtypeStruct((T, OFF_Q), F32),
            jax.ShapeDtypeStruct((T, ATTN_DIM), BF16),
            jax.ShapeDtypeStruct((T, ATTN_DIM), BF16),
            jax.ShapeDtypeStruct((T, IDX_DIM), F32),
            jax.ShapeDtypeStruct((T // KEY_CHUNK, ATTN_DIM, KEY_CHUNK), BF16),
            jax.ShapeDtypeStruct((IDXT_ROWS, T), F32),
        ],
        compiler_params=_cparams(("arbitrary",)),
        name="inproj",
    )(xb, wn, wt, kg, kb)


def _attn_kernel(n_sel, it_ref, ki_ref, q_ref, k_ref, vt_ref, o_ref,
                 keys_ref, rhs_ref, acc_ref, lg_ref):
    i = pl.program_id(1)
    nch = lax.shift_right_logical(i * Q_BLOCK + Q_BLOCK + KEY_CHUNK - 1, _log2(KEY_CHUNK))
    q0 = i * Q_BLOCK
    KC = KEY_CHUNK
    row = lax.broadcasted_iota(I32, (KC, Q_BLOCK), 0)
    lane = lax.broadcasted_iota(I32, (KC, Q_BLOCK), 1)
    qpos = q0 + lane

    qcat = jnp.concatenate(
        [it_ref[h * IDX_DIM:(h + 1) * IDX_DIM, :] for h in range(IDX_HEADS)], axis=1).astype(BF16)
    w_all = it_ref[IDX_HEADS * IDX_DIM:IDX_HEADS * IDX_DIM + IDX_HEADS, :] * (
        (IDX_HEADS ** -0.5) * (IDX_DIM ** -0.5))

    def score_body(c, carry):
        base = pl.multiple_of(c * KC, KC)
        kic = ki_ref[pl.ds(base, KC), :].astype(BF16)
        s = _dot(kic, qcat)
        sc = jnp.zeros((KC, Q_BLOCK), F32)
        for h in range(IDX_HEADS):
            sc = sc + jnp.maximum(s[:, h * Q_BLOCK:(h + 1) * Q_BLOCK], 0.0) * w_all[h:h + 1, :]
        bits = lax.bitcast_convert_type(sc, I32)
        key = bits ^ ((bits >> 31) & jnp.int32(0x7FFFFFFF))
        key = jnp.where(bits == jnp.int32(INT_MIN), jnp.int32(0), key)
        key = jnp.where(base + row <= qpos, key, jnp.int32(INT_MIN))
        keys_ref[pl.ds(base, KC), :] = key
        return carry

    lax.fori_loop(0, nch, score_body, 0)

    def count_ge(cand):
        def body(c, acc):
            base = pl.multiple_of(c * KC, KC)
            blk = keys_ref[pl.ds(base, KC), :]
            hit = jnp.where(blk >= cand, 1.0, 0.0).astype(F32)
            return acc + _tree(hit, jnp.add)
        acc = lax.fori_loop(0, nch, body, jnp.zeros((SUBLANES, Q_BLOCK), F32))
        return jnp.sum(acc, axis=0, keepdims=True)

    want = jnp.float32(n_sel)
    zero_row = jnp.zeros((1, Q_BLOCK), I32)
    cnt0 = count_ge(zero_row)
    lo = jnp.where(cnt0 >= want, zero_row, jnp.int32(INT_MIN))

    def bit_body(it, carry):
        lo, cnt_lo = carry
        cand = lo + lax.shift_left(jnp.int32(1), jnp.int32(30) - it)
        cnt = count_ge(cand)
        ok = cnt >= want
        return jnp.where(ok, cand, lo), jnp.where(ok, cnt, cnt_lo)

    lo, cnt_lo = lax.fori_loop(0, 31, bit_body, (lo, cnt0))
    tau = jnp.maximum(lo, jnp.int32(INT_MIN + 1))

    n_causal = q0 + lax.broadcasted_iota(I32, (1, Q_BLOCK), 1) + 1
    excess = jnp.where(n_causal > n_sel, cnt_lo - want, 0.0)

    @pl.when(jnp.max(excess) > 0.0)
    def _():
        keep = want - count_ge(tau + 1)
        r = lax.broadcasted_iota(I32, (KC, KC), 0)
        cc = lax.broadcasted_iota(I32, (KC, KC), 1)
        tri = (cc <= r).astype(BF16)

        def retire(c, before):
            base = pl.multiple_of(c * KC, KC)
            blk = keys_ref[pl.ds(base, KC), :]
            tied = blk == tau
            upto = _dot(tri, jnp.where(tied, 1.0, 0.0).astype(BF16)) + before
            keys_ref[pl.ds(base, KC), :] = jnp.where(tied & (upto > keep), jnp.int32(INT_MIN), blk)
            return upto[KC - 1:KC, :]

        lax.fori_loop(0, nch, retire, jnp.zeros((1, Q_BLOCK), F32))

    q = q_ref[...]
    lane_q = lax.broadcasted_iota(I32, (Q_BLOCK, 2 * HEAD_DIM), 1)
    for p in range(ATTN_HEADS // 2):
        qp = q[:, p * 2 * HEAD_DIM:(p + 1) * 2 * HEAD_DIM]
        rhs_ref[p, 0:Q_BLOCK, :] = jnp.where(lane_q < HEAD_DIM, qp, jnp.zeros_like(qp))
        rhs_ref[p, Q_BLOCK:2 * Q_BLOCK, :] = jnp.where(lane_q >= HEAD_DIM, qp, jnp.zeros_like(qp))
    acc_ref[...] = jnp.zeros_like(acc_ref)

    def stage_a_prep(c):
        base = pl.multiple_of(c * KC, KC)
        tk = (base + row - (q0 + Q_BLOCK - 1)).astype(F32)
        tkm = jnp.where(keys_ref[pl.ds(base, KC), :] >= tau, tk, NEG_BIG)
        return tkm, k_ref[pl.ds(base, KC), :]

    def stage_a_pair(c, p, tkm, kc):
        l2 = _dot_nt(kc[:, p * 2 * HEAD_DIM:(p + 1) * 2 * HEAD_DIM], rhs_ref[p])
        mcs = []
        for hh in range(2):
            h = 2 * p + hh
            slope = 2.0 ** (-8.0 * (h + 1) / ATTN_HEADS)
            lg = l2[:, hh * Q_BLOCK:(hh + 1) * Q_BLOCK] + slope * tkm
            lg_ref[h] = lg
            mcs.append(jnp.max(_tree(lg, jnp.maximum), axis=0, keepdims=True))
        return mcs

    def stage_b_pair(c, p, m_new, alpha):
        sums = []
        for hh in range(2):
            h = 2 * p + hh
            pr = jnp.exp(lg_ref[h] - m_new[h:h + 1, :])
            sums.append(jnp.sum(_tree(pr, jnp.add), axis=0, keepdims=True))
            pv = _dot(vt_ref[c, h * HEAD_DIM:(h + 1) * HEAD_DIM, :], pr.astype(BF16))
            acc_ref[h * HEAD_DIM:(h + 1) * HEAD_DIM, :] = (
                acc_ref[h * HEAD_DIM:(h + 1) * HEAD_DIM, :] * alpha[h:h + 1, :] + pv)
        return sums

    def stage_b_all(c, m_old, l_old, mc):
        m_new = jnp.maximum(m_old, mc)
        alpha = jnp.exp(m_old - m_new)
        return m_new, alpha

    tkm0, kc0 = stage_a_prep(0)
    mc0 = jnp.concatenate(sum([stage_a_pair(0, p, tkm0, kc0) for p in range(ATTN_HEADS // 2)], []), axis=0)

    def attn_body(c, carry):
        m_old, l_old, mc_prev = carry
        m_new, alpha = stage_b_all(c - 1, m_old, l_old, mc_prev)
        tkm, kc = stage_a_prep(c)
        mcs, sums = [], []
        for p in range(ATTN_HEADS // 2):
            sums += stage_b_pair(c - 1, p, m_new, alpha)
            mcs += stage_a_pair(c, p, tkm, kc)
        return m_new, l_old * alpha + jnp.concatenate(sums, axis=0), jnp.concatenate(mcs, axis=0)

    m0 = jnp.full((ATTN_HEADS, Q_BLOCK), NEG_BIG, F32)
    l0 = jnp.zeros((ATTN_HEADS, Q_BLOCK), F32)
    m_old, l_old, mc_prev = lax.fori_loop(1, nch, attn_body, (m0, l0, mc0))
    m_new, alpha = stage_b_all(nch - 1, m_old, l_old, mc_prev)
    sums = sum([stage_b_pair(nch - 1, p, m_new, alpha) for p in range(ATTN_HEADS // 2)], [])
    l_fin = l_old * alpha + jnp.concatenate(sums, axis=0)

    outs = []
    for h in range(ATTN_HEADS):
        outs.append(acc_ref[h * HEAD_DIM:(h + 1) * HEAD_DIM, :] / l_fin[h:h + 1, :])
    o_ref[...] = jnp.concatenate(outs, axis=0).T


def _attn(it, ki, q, k, vt, B, S):
    T = B * S
    nb = S // Q_BLOCK
    n_sel = min(INDEX_TOPK_MAX, S // 4)
    return pl.pallas_call(
        functools.partial(_attn_kernel, n_sel),
        grid=(B, nb),
        in_specs=[
            pl.BlockSpec((IDXT_ROWS, Q_BLOCK), lambda b, i: (0, b * nb + i)),
            pl.BlockSpec((S, IDX_DIM), lambda b, i: (b, 0)),
            pl.BlockSpec((Q_BLOCK, ATTN_DIM), lambda b, i: (b * nb + i, 0)),
            pl.BlockSpec((S, ATTN_DIM), lambda b, i: (b, 0)),
            pl.BlockSpec((S // KEY_CHUNK, ATTN_DIM, KEY_CHUNK), lambda b, i: (b, 0, 0)),
        ],
        out_specs=pl.BlockSpec((Q_BLOCK, ATTN_DIM), lambda b, i: (b * nb + i, 0)),
        out_shape=jax.ShapeDtypeStruct((T, ATTN_DIM), F32),
        scratch_shapes=[
            pltpu.VMEM((S, Q_BLOCK), I32),
            pltpu.VMEM((ATTN_HEADS // 2, 2 * Q_BLOCK, 2 * HEAD_DIM), BF16),
            pltpu.VMEM((ATTN_DIM, Q_BLOCK), F32),
            pltpu.VMEM((ATTN_HEADS, KEY_CHUNK, Q_BLOCK), F32),
        ],
        compiler_params=_cparams(("arbitrary", "arbitrary")),
        name="attn",
    )(it, ki, q, k, vt)


def _mixout_kernel(tiles_per_seq, a_ref, halo_ref, y_ref, x_ref, cw_ref, wpool_ref, ps_ref,
                   wout_ref, g_ref, b_ref, o_ref, ext_ref):
    i = pl.program_id(0)
    tm = TM_MIX
    first = lax.rem(i, tiles_per_seq) == 0
    halo = jnp.where(first, 0.0, halo_ref[...])
    a = a_ref[...]
    h_c, gb_c, gc_c, p_c = (a[:, j * CONV_DIM:(j + 1) * CONV_DIM] for j in range(4))

    t0 = 2 * HALO
    n_ext = t0 + tm
    ext_ref[0:HALO, :] = jnp.zeros((HALO, CONV_DIM), F32)

    ext_ref[HALO:t0, :] = halo[:, 2 * CONV_DIM:3 * CONV_DIM] * halo[:, 0:CONV_DIM]
    u = gc_c * h_c
    ext_ref[t0:n_ext, :] = u
    cw = cw_ref[...]
    conv = cw[2:3, :] * u
    conv = conv + cw[1:2, :] * ext_ref[t0 - 1:n_ext - 1, :]
    conv = conv + cw[0:1, :] * ext_ref[t0 - 2:n_ext - 2, :]
    y_conv = gb_c * conv

    ext_ref[HALO:t0, :] = halo[:, OFF_P:OFF_P + POOL_DIM]
    ext_ref[t0:n_ext, :] = p_c
    sums = {}
    step = 1
    while step < POOL_WINDOWS[-1]:
        cur = ext_ref[HALO:n_ext, :] + ext_ref[HALO - step:n_ext - step, :]
        ext_ref[HALO:n_ext, :] = cur
        step *= 2
        sums[step] = ext_ref[t0:n_ext, :]
    tpos = (lax.rem(i, tiles_per_seq) * tm + lax.broadcasted_iota(I32, (tm, POOL_DIM), 0) + 1).astype(F32)
    grp = lax.shift_right_logical(lax.broadcasted_iota(I32, (tm, POOL_DIM), 1), _log2(POOL_GROUP))
    mean = jnp.zeros((tm, POOL_DIM), F32)
    for gi, w in enumerate(POOL_WINDOWS):
        mean = jnp.where(grp == gi, sums[w] / jnp.minimum(tpos, float(w)), mean)
    mixed = mean - p_c
    y_pool = _dot(mixed.astype(BF16), wpool_ref[...]) * ps_ref[...]

    mix = _dot(y_conv.astype(BF16), wout_ref[0:CONV_DIM, :])
    mix = mix + _dot(y_pool.astype(BF16), wout_ref[CONV_DIM:CONV_DIM + POOL_DIM, :])
    mix = mix + _dot(y_ref[...].astype(BF16), wout_ref[CONV_DIM + POOL_DIM:D_MODEL, :])
    z = DN_ALPHA * x_ref[...] + mix
    o_ref[...] = _layernorm_rows(z, g_ref[...], b_ref[...])


def _mixout(a, y_attn, x, cw, wpool_bd, ps, wout, g, b, S):
    T = a.shape[0]
    nt = T // TM_MIX
    tps = S // TM_MIX
    hb = TM_MIX // HALO
    full = lambda i: (0, 0)
    return pl.pallas_call(
        functools.partial(_mixout_kernel, tps),
        grid=(nt,),
        in_specs=[
            pl.BlockSpec((TM_MIX, OFF_Q), lambda i: (i, 0)),
            pl.BlockSpec((HALO, OFF_Q), lambda i: (jnp.maximum(i * hb - 1, 0), 0)),
            pl.BlockSpec((TM_MIX, ATTN_DIM), lambda i: (i, 0)),
            pl.BlockSpec((TM_MIX, D_MODEL), lambda i: (i, 0)),
            pl.BlockSpec((CONV_WIDTH, CONV_DIM), full),
            pl.BlockSpec((POOL_DIM, POOL_DIM), full),
            pl.BlockSpec((1, POOL_DIM), full),
            pl.BlockSpec((D_MODEL, D_MODEL), full),
            pl.BlockSpec((1, D_MODEL), full),
            pl.BlockSpec((1, D_MODEL), full),
        ],
        out_specs=pl.BlockSpec((TM_MIX, D_MODEL), lambda i: (i, 0)),
        out_shape=jax.ShapeDtypeStruct((T, D_MODEL), F32),
        scratch_shapes=[pltpu.VMEM((2 * HALO + TM_MIX, CONV_DIM), F32)],
        compiler_params=_cparams(("arbitrary",)),
        name="mixout",
    )(a, a, y_attn, x, cw, wpool_bd, ps, wout, g, b)


def _router_kernel(x_ref, rw_ref, rb_ref, idx_ref, gate_ref, cnt_ref):
    i = pl.program_id(0)
    tm = TM_ROUTE
    logits = lax.dot_general(rw_ref[...], x_ref[...], (((1,), (1,)), ((), ())),
                             precision=lax.Precision.HIGHEST,
                             preferred_element_type=F32) + rb_ref[...]
    erow = lax.broadcasted_iota(I32, (N_EXPERTS, tm), 0).astype(F32)
    work = logits
    vals, idxs = [], []
    multi = jnp.zeros((N_EXPERTS, tm), F32)
    for _ in range(TOP_K):
        mx = jnp.max(work, axis=0, keepdims=True)
        pick = jnp.min(jnp.where(work == mx, erow, float(N_EXPERTS)), axis=0, keepdims=True)
        hit = erow == pick
        work = jnp.where(hit, -jnp.inf, work)
        multi = multi + hit.astype(F32)
        vals.append(mx)
        idxs.append(pick)
    es = [jnp.exp(v - vals[0]) for v in vals]
    den = es[0] + es[1] + es[2] + es[3]
    idx_ref[...] = jnp.concatenate(idxs, axis=0).astype(I32)
    gate_ref[...] = jnp.concatenate([e / den for e in es] + [jnp.zeros((SUBLANES - TOP_K, tm), F32)], axis=0)
    part = multi[:, 0:LANES]
    for j in range(1, tm // LANES):
        part = part + multi[:, j * LANES:(j + 1) * LANES]

    @pl.when(i == 0)
    def _():
        cnt_ref[...] = jnp.zeros_like(cnt_ref)

    cnt_ref[...] += part


def _router(x1, rwT, rb):
    T = x1.shape[0]
    nt = T // TM_ROUTE
    full = lambda i: (0, 0)
    return pl.pallas_call(
        _router_kernel,
        grid=(nt,),
        in_specs=[
            pl.BlockSpec((TM_ROUTE, D_MODEL), lambda i: (i, 0)),
            pl.BlockSpec((N_EXPERTS, D_MODEL), full),
            pl.BlockSpec((N_EXPERTS, 1), full),
        ],
        out_specs=[
            pl.BlockSpec((TOP_K, TM_ROUTE), lambda i: (0, i)),
            pl.BlockSpec((SUBLANES, TM_ROUTE), lambda i: (0, i)),
            pl.BlockSpec((N_EXPERTS, LANES), full),
        ],
        out_shape=[
            jax.ShapeDtypeStruct((TOP_K, T), I32),
            jax.ShapeDtypeStruct((SUBLANES, T), F32),
            jax.ShapeDtypeStruct((N_EXPERTS, LANES), F32),
        ],
        compiler_params=_cparams(("arbitrary",)),
        name="router",
    )(x1, rwT, rb)


def _ranks_kernel(n_tiles_pad, idx_ref, cnt_ref, dest_ref, meta_ref, tri_ref, start_ref, carry_ref):
    i = pl.program_id(0)
    tm = TM_ROUTE
    erow = lax.broadcasted_iota(I32, (N_EXPERTS, LANES), 0)
    elane = lax.broadcasted_iota(I32, (N_EXPERTS, LANES), 1)

    @pl.when(i == 0)
    def _():
        cnt = jnp.sum(cnt_ref[...], axis=1, keepdims=True)
        cnt_i = jnp.broadcast_to(cnt, (N_EXPERTS, LANES)).astype(I32)
        padded = lax.shift_left(lax.shift_right_logical(cnt_i + (TM_MOE - 1), _log2(TM_MOE)), _log2(TM_MOE))
        r = lax.broadcasted_iota(I32, (N_EXPERTS, N_EXPERTS), 0)
        c = lax.broadcasted_iota(I32, (N_EXPERTS, N_EXPERTS), 1)
        low = (c <= r).astype(F32)
        pad_end = lax.dot_general(low, padded.astype(F32), (((1,), (0,)), ((), ())),
                                  precision=lax.Precision.HIGHEST, preferred_element_type=F32)
        pad_start = pad_end - padded.astype(F32)
        start_ref[...] = pad_start
        carry_ref[...] = jnp.zeros_like(carry_ref)
        a = lax.broadcasted_iota(I32, (tm, tm), 0)
        bcol = lax.broadcasted_iota(I32, (tm, tm), 1)
        tri_ref[...] = (a < bcol).astype(BF16)
        ntp = n_tiles_pad
        tstart = (lax.broadcasted_iota(I32, (N_EXPERTS, ntp), 1) * TM_MOE).astype(F32)
        pe = jnp.concatenate([pad_end] * (ntp // LANES), axis=1)
        texp = jnp.sum((pe <= tstart).astype(F32), axis=0, keepdims=True)
        texp = jnp.minimum(texp, float(N_EXPERTS - 1)).astype(I32)
        total = jnp.max(pad_end, axis=0, keepdims=True)
        n_used = lax.shift_right_logical(total.astype(I32), _log2(TM_MOE))
        zstart = jnp.sum(jnp.where(erow == elane, pad_start + cnt_i.astype(F32), 0.0),
                         axis=0, keepdims=True).astype(I32)
        zlen = jnp.sum(jnp.where(erow == elane, (padded - cnt_i).astype(F32), 0.0),
                       axis=0, keepdims=True).astype(I32)
        lanes_pad = jnp.zeros((1, ntp - LANES), I32)
        meta_ref[...] = jnp.concatenate(
            [texp,
             jnp.concatenate([n_used, lanes_pad], axis=1),
             jnp.concatenate([zstart, lanes_pad], axis=1),
             jnp.concatenate([zlen, lanes_pad], axis=1),
             jnp.zeros((SUBLANES - 4, ntp), I32)], axis=0)

    idx = idx_ref[...]
    erow_t = lax.broadcasted_iota(I32, (N_EXPERTS, tm), 0)
    hits = [erow_t == idx[k:k + 1, :] for k in range(TOP_K)]
    multi = hits[0].astype(F32)
    for k in range(1, TOP_K):
        multi = multi + hits[k].astype(F32)
    prefix = _dot(multi.astype(BF16), tri_ref[...])
    base = jnp.concatenate([carry_ref[...] + start_ref[...]] * (tm // LANES), axis=1)
    tot = prefix + base
    dest_ref[...] = jnp.concatenate(
        [jnp.sum(jnp.where(hits[k], tot, 0.0), axis=0, keepdims=True) for k in range(TOP_K)],
        axis=0).astype(I32)
    carry_ref[...] += jnp.broadcast_to(jnp.sum(multi, axis=1, keepdims=True), (N_EXPERTS, LANES))


def _ranks(idxT, cnt, n_tiles_pad):
    T = idxT.shape[1]
    nt = T // TM_ROUTE
    full = lambda i: (0, 0)
    return pl.pallas_call(
        functools.partial(_ranks_kernel, n_tiles_pad),
        grid=(nt,),
        in_specs=[
            pl.BlockSpec((TOP_K, TM_ROUTE), lambda i: (0, i)),
            pl.BlockSpec((N_EXPERTS, LANES), full),
        ],
        out_specs=[
            pl.BlockSpec((TOP_K, TM_ROUTE), lambda i: (0, i)),
            pl.BlockSpec((SUBLANES, n_tiles_pad), full),
        ],
        out_shape=[
            jax.ShapeDtypeStruct((TOP_K, T), I32),
            jax.ShapeDtypeStruct((SUBLANES, n_tiles_pad), I32),
        ],
        scratch_shapes=[
            pltpu.VMEM((TM_ROUTE, TM_ROUTE), BF16),
            pltpu.VMEM((N_EXPERTS, LANES), F32),
            pltpu.VMEM((N_EXPERTS, LANES), F32),
        ],
        compiler_params=_cparams(("arbitrary",)),
        name="ranks",
    )(idxT, cnt)


RT = D_MODEL // LANES
DMA_UNROLL = 8


def _row_copy(src_ref, s, dst_ref, d, sem):
    return pltpu.make_async_copy(src_ref.at[pl.ds(pl.multiple_of(s * RT, RT), RT)],
                                 dst_ref.at[pl.ds(pl.multiple_of(d * RT, RT), RT)], sem)


def _to_row_tiled(dst_ref, val, rows):
    for s in range(RT):
        dst_ref[pl.ds(s, rows, stride=RT), :] = val[:, s * LANES:(s + 1) * LANES]


def _from_row_tiled(src_ref, rows):
    return [src_ref[pl.ds(s, rows, stride=RT), :] for s in range(RT)]


def _scatter_kernel(meta_ref, x_ref, dest_ref, xs_ref, stage_ref, zero_ref, sem):
    i = pl.program_id(0)
    tm = TM_ROUTE

    @pl.when(i == 0)
    def _():
        zero_ref[...] = jnp.zeros_like(zero_ref)

        def zcopy(row, nrows):
            off = pl.multiple_of(row * RT, RT)
            return pltpu.make_async_copy(zero_ref.at[pl.ds(0, nrows * RT)],
                                         xs_ref.at[pl.ds(off, nrows * RT)], sem)

        def pad_fill(wait):
            def body(e, c):
                row = meta_ref[2, e]
                plen = meta_ref[3, e]
                for bit in reversed(range(_log2(TM_MOE))):
                    size = 1 << bit
                    has = (plen & size) != 0

                    @pl.when(has)
                    def _():
                        cp = zcopy(row, size)
                        cp.wait() if wait else cp.start()

                    row = row + jnp.where(has, size, 0)
                return c
            lax.fori_loop(0, N_EXPERTS, body, 0)

        def tail_fill(wait):
            def body(j, c):
                cp = zcopy(j * TM_MOE, TM_MOE)
                cp.wait() if wait else cp.start()
                return c
            lax.fori_loop(meta_ref[1, 0], xs_ref.shape[0] // (TM_MOE * RT), body, 0)

        pad_fill(False)
        tail_fill(False)
        pad_fill(True)
        tail_fill(True)

    _to_row_tiled(stage_ref, x_ref[...], tm)

    def start(t, c):
        for k in range(TOP_K):
            _row_copy(stage_ref, t, xs_ref, dest_ref[k, t], sem).start(priority=k % 2)
        return c

    lax.fori_loop(0, tm, start, 0, unroll=DMA_UNROLL)
    for k in range(TOP_K):
        pltpu.make_async_copy(stage_ref, xs_ref.at[pl.ds(0, tm * RT)], sem).wait()


def _scatter(meta, x1, destT, n_rows_alloc):
    T = x1.shape[0]
    nt = T // TM_ROUTE
    return pl.pallas_call(
        _scatter_kernel,
        grid_spec=pltpu.PrefetchScalarGridSpec(
            num_scalar_prefetch=1,
            grid=(nt,),
            in_specs=[
                pl.BlockSpec((TM_ROUTE, D_MODEL), lambda i, m: (i, 0)),
                pl.BlockSpec((TOP_K, TM_ROUTE), lambda i, m: (0, i), memory_space=pltpu.SMEM),
            ],
            out_specs=pl.BlockSpec(memory_space=pl.ANY),
            scratch_shapes=[
                pltpu.VMEM((TM_ROUTE * RT, LANES), F32),
                pltpu.VMEM((TM_MOE * RT, LANES), F32),
                pltpu.SemaphoreType.DMA(()),
            ],
        ),
        out_shape=jax.ShapeDtypeStruct((n_rows_alloc * RT, LANES), F32),
        compiler_params=_cparams(("arbitrary",)),
        name="scatter",
    )(meta, x1, destT)


def _gmm_kernel(meta_ref, xs_ref, wgu_ref, bgu_ref, wd_ref, bd_ref, ys_ref, wgu_b, wd_b, lhs_ref, act_ref):
    j = pl.program_id(0)
    n_used = meta_ref[1, 0]
    e_now = meta_ref[0, j]
    e_prev = meta_ref[0, jnp.maximum(j - 1, 0)]
    used = j < n_used

    @pl.when(used & ((j == 0) | (e_now != e_prev)))
    def _():
        wgu_b[...] = wgu_ref[0, 0].astype(BF16)
        wd_b[...] = wd_ref[0, 0].astype(BF16)

    @pl.when(used)
    def _():
        for s, piece in enumerate(_from_row_tiled(xs_ref, TM_MOE)):
            lhs_ref[:, s * LANES:(s + 1) * LANES] = piece.astype(BF16)
        x = lhs_ref[...]
        nc = 256
        for c in range(D_FF // nc):
            gate = _dot(x, wgu_b[:, c * nc:(c + 1) * nc]) + bgu_ref[0, 0, :, c * nc:(c + 1) * nc]
            up = (_dot(x, wgu_b[:, D_FF + c * nc:D_FF + (c + 1) * nc])
                  + bgu_ref[0, 0, :, D_FF + c * nc:D_FF + (c + 1) * nc])
            gate = jnp.minimum(gate, SWIGLU_LIMIT)
            up = jnp.clip(up, -SWIGLU_LIMIT, SWIGLU_LIMIT)
            act = gate * jax.nn.sigmoid(SWIGLU_ALPHA * gate) * (up + 1.0)
            act_ref[:, c * nc:(c + 1) * nc] = act.astype(BF16)
        _to_row_tiled(ys_ref, _dot(act_ref[...], wd_b[...]) + bd_ref[0, 0], TM_MOE)

    @pl.when(jnp.logical_not(used))
    def _():
        ys_ref[...] = jnp.zeros_like(ys_ref)


def _gmm(meta, xs, w_gu, b_gu, w_down, b_down, n_tiles, layer):
    last = lambda m: jnp.maximum(m[1, 0] - 1, 0)
    return pl.pallas_call(
        _gmm_kernel,
        grid_spec=pltpu.PrefetchScalarGridSpec(
            num_scalar_prefetch=1,
            grid=(n_tiles,),
            in_specs=[
                pl.BlockSpec((TM_MOE * RT, LANES), lambda j, m: (jnp.minimum(j, last(m)), 0)),
                pl.BlockSpec((1, 1, D_MODEL, 2 * D_FF), lambda j, m: (layer, m[0, j], 0, 0)),
                pl.BlockSpec((1, 1, 1, 2 * D_FF), lambda j, m: (layer, m[0, j], 0, 0)),
                pl.BlockSpec((1, 1, D_FF, D_MODEL), lambda j, m: (layer, m[0, j], 0, 0)),
                pl.BlockSpec((1, 1, 1, D_MODEL), lambda j, m: (layer, m[0, j], 0, 0)),
            ],
            out_specs=pl.BlockSpec((TM_MOE * RT, LANES), lambda j, m: (j, 0)),
            scratch_shapes=[
                pltpu.VMEM((D_MODEL, 2 * D_FF), BF16),
                pltpu.VMEM((D_FF, D_MODEL), BF16),
                pltpu.VMEM((TM_MOE, D_MODEL), BF16),
                pltpu.VMEM((TM_MOE, D_FF), BF16),
            ],
        ),
        out_shape=jax.ShapeDtypeStruct((n_tiles * TM_MOE * RT, LANES), F32),
        compiler_params=_cparams(("arbitrary",)),
        name="gmm",
    )(meta, xs, w_gu, b_gu, w_down, b_down)


def _combine_kernel(x_ref, dest_ref, gate_ref, ys_ref, g_ref, b_ref, o_ref, ob_ref, buf_ref, sem):
    tm = TM_COMB

    def start(t, c):
        for k in range(TOP_K):
            _row_copy(ys_ref, dest_ref[k, t], buf_ref.at[k], t, sem).start(priority=k % 2)
        return c

    lax.fori_loop(0, tm, start, 0, unroll=DMA_UNROLL)
    for k in range(TOP_K):
        pltpu.make_async_copy(ys_ref.at[pl.ds(0, tm * RT)], buf_ref.at[k], sem).wait()
    gates = jnp.concatenate(
        [gate_ref[...], jnp.zeros((LANES - SUBLANES, tm), F32)], axis=0).T
    pieces = None
    for k in range(TOP_K):
        gk = gates[:, k:k + 1]
        rows = [p * gk for p in _from_row_tiled(buf_ref.at[k], tm)]
        pieces = rows if pieces is None else [a + r for a, r in zip(pieces, rows)]
    z = DN_ALPHA * x_ref[...] + jnp.concatenate(pieces, axis=1)
    out = _layernorm_rows(z, g_ref[...], b_ref[...])
    o_ref[...] = out
    ob_ref[...] = out.astype(BF16)


def _combine(x1, destT, gateT, ys, g, b):
    T = x1.shape[0]
    nt = T // TM_COMB
    full = lambda i: (0, 0)
    return pl.pallas_call(
        _combine_kernel,
        grid=(nt,),
        in_specs=[
            pl.BlockSpec((TM_COMB, D_MODEL), lambda i: (i, 0)),
            pl.BlockSpec((TOP_K, TM_COMB), lambda i: (0, i), memory_space=pltpu.SMEM),
            pl.BlockSpec((SUBLANES, TM_COMB), lambda i: (0, i)),
            pl.BlockSpec(memory_space=pl.ANY),
            pl.BlockSpec((1, D_MODEL), full),
            pl.BlockSpec((1, D_MODEL), full),
        ],
        out_specs=[
            pl.BlockSpec((TM_COMB, D_MODEL), lambda i: (i, 0)),
            pl.BlockSpec((TM_COMB, D_MODEL), lambda i: (i, 0)),
        ],
        out_shape=[
            jax.ShapeDtypeStruct((T, D_MODEL), F32),
            jax.ShapeDtypeStruct((T, D_MODEL), BF16),
        ],
        scratch_shapes=[
            pltpu.VMEM((TOP_K, TM_COMB * RT, LANES), F32),
            pltpu.SemaphoreType.DMA(()),
        ],
        compiler_params=_cparams(("arbitrary",)),
        name="combine",
    )(x1, destT, gateT, ys, g, b)


def _block_diag(w):
    g, c, _ = w.shape
    out = jnp.zeros((g * c, g * c), w.dtype)
    for i in range(g):
        out = out.at[i * c:(i + 1) * c, i * c:(i + 1) * c].set(w[i])
    return out


def kernel(x, w_in, conv_w, w_pool, pool_scale, idx_kn_g, idx_kn_b, w_out, ln1_g, ln1_b, router_w,
           router_b, w_gu, b_gu, w_down, b_down, ln2_g, ln2_b):
    B, S, D = x.shape
    T = B * S
    depth = w_in.shape[0]
    n_pairs = T * TOP_K
    n_tiles = (n_pairs + N_EXPERTS * (TM_MOE - 1)) // TM_MOE + 1
    n_tiles_pad = ((n_tiles + LANES - 1) // LANES) * LANES
    n_rows_alloc = (n_tiles + 1) * TM_MOE

    xf = x.reshape(T, D)
    xb = xf.astype(BF16)
    for l in range(depth):
        wl = w_in[l]
        wn = jnp.concatenate(
            [wl[:, 0:OFF_V], wl[:, OFF_KI:OFF_WI], jnp.zeros((D, LANES - IDX_DIM), F32)], axis=1).astype(BF16)
        wt = jnp.concatenate(
            [wl[:, OFF_V:OFF_QI], wl[:, OFF_QI:OFF_KI], wl[:, OFF_WI:D_IN],
             jnp.zeros((D, IDXT_ROWS - IDX_HEADS * IDX_DIM - IDX_HEADS), F32)], axis=1).T.astype(BF16)
        a, q, k, ki, vt, it = _inproj(xb, wn, wt, idx_kn_g[l].reshape(1, -1), idx_kn_b[l].reshape(1, -1))
        y_attn = _attn(it, ki, q, k, vt, B, S)
        x1 = _mixout(a, y_attn, xf, conv_w[l], _block_diag(w_pool[l]).astype(BF16),
                     pool_scale[l].reshape(1, -1), w_out[l].astype(BF16),
                     ln1_g[l].reshape(1, -1), ln1_b[l].reshape(1, -1), S)
        idxT, gateT, cnt = _router(x1, router_w[l].T, router_b[l].reshape(-1, 1))
        destT, meta = _ranks(idxT, cnt, n_tiles_pad)
        xs = _scatter(meta, x1, destT, n_rows_alloc)
        ys = _gmm(meta, xs, w_gu, b_gu.reshape(depth, N_EXPERTS, 1, -1), w_down,
                  b_down.reshape(depth, N_EXPERTS, 1, -1), n_tiles, l)
        xf, xb = _combine(x1, destT, gateT, ys, ln2_g[l].reshape(1, -1), ln2_b[l].reshape(1, -1))
    return xf.reshape(B, S, D)
```

```python
import functools

import jax
import jax.numpy as jnp
import numpy as np
from jax import lax
from jax.experimental import pallas as pl
from jax.experimental.pallas import tpu as pltpu

F32 = jnp.float32
BF16 = jnp.bfloat16
I32 = jnp.int32

D_MODEL = 1024
CONV_DIM = 256
CONV_WIDTH = 3
POOL_WINDOWS = (2, 4, 8, 16)
POOL_GROUP = 64
POOL_DIM = 256
ATTN_HEADS = 8
HEAD_DIM = 64
ATTN_DIM = 512
IDX_HEADS = 8
IDX_DIM = 32
INDEX_TOPK_MAX = 256
Q_BLOCK = 128
N_EXPERTS = 32
TOP_K = 4
D_FF = 1024
SWIGLU_ALPHA = 1.702
SWIGLU_LIMIT = 7.0
LN_EPS = 1e-5
DEPTH = 2
DN_ALPHA = (2.0 * DEPTH) ** 0.25

OFF_P = 3 * CONV_DIM
OFF_Q = OFF_P + POOL_DIM
OFF_K = OFF_Q + ATTN_DIM
OFF_V = OFF_K + ATTN_DIM
OFF_QI = OFF_V + ATTN_DIM
OFF_KI = OFF_QI + IDX_HEADS * IDX_DIM
OFF_WI = OFF_KI + IDX_DIM
D_IN = OFF_WI + IDX_HEADS

LANES = 128
SUBLANES = 8
VMEM_LIMIT = 56 * 1024 * 1024
KEY_CHUNK = 512
HALO = 16
TM_PROJ = 512
TM_MIX = 256
TM_ROUTE = 512
TM_MOE = 512
TM_COMB = 256
IDXT_ROWS = IDX_HEADS * IDX_DIM + 16
WT_ROWS = ATTN_DIM + IDXT_ROWS
INT_MIN = -(2 ** 31)
NEG_BIG = -1e30


def _cparams(sem, flags=None):
    return pltpu.CompilerParams(dimension_semantics=sem, vmem_limit_bytes=VMEM_LIMIT, flags=flags)


def _log2(n):
    k = n.bit_length() - 1
    assert 1 << k == n
    return k


def _dot(a, b):
    return jnp.dot(a, b, preferred_element_type=F32)


def _dot_nt(a, b):
    return lax.dot_general(a, b, (((1,), (1,)), ((), ())), preferred_element_type=F32)


def _tree(x, op):
    parts = [x[j * SUBLANES:(j + 1) * SUBLANES, :] for j in range(x.shape[0] // SUBLANES)]
    while len(parts) > 1:
        nxt = [op(parts[j], parts[j + 1]) for j in range(0, len(parts) - 1, 2)]
        if len(parts) % 2:
            nxt.append(parts[-1])
        parts = nxt
    return parts[0]


def _layernorm_rows(z, g, b):
    mu = jnp.mean(z, axis=-1, keepdims=True)
    d = z - mu
    var = jnp.mean(d * d, axis=-1, keepdims=True)
    return d * lax.rsqrt(var + LN_EPS) * g + b


def _inproj_kernel(x_ref, wn_ref, wt_ref, kg_ref, kb_ref,
                   a_ref, q_ref, k_ref, ki_ref, vt_ref, it_ref):
    x = x_ref[...]
    a_ref[...] = _dot(x, wn_ref[:, 0:OFF_Q])
    q_ref[...] = (_dot(x, wn_ref[:, OFF_Q:OFF_K]) * (HEAD_DIM ** -0.5)).astype(BF16)
    k_ref[...] = _dot(x, wn_ref[:, OFF_K:OFF_V]).astype(BF16)
    ki = _dot(x, wn_ref[:, OFF_V:OFF_V + LANES])[:, 0:IDX_DIM]
    ki_ref[...] = _layernorm_rows(ki, kg_ref[...], kb_ref[...])
    t = _dot_nt(wt_ref[...], x)
    for c in range(TM_PROJ // KEY_CHUNK):
        vt_ref[c] = t[0:ATTN_DIM, c * KEY_CHUNK:(c + 1) * KEY_CHUNK].astype(BF16)
    it_ref[...] = t[ATTN_DIM:WT_ROWS, :]


def _inproj(xb, wn, wt, kg, kb):
    T = xb.shape[0]
    nt = T // TM_PROJ
    cpt = TM_PROJ // KEY_CHUNK
    full = lambda i: (0, 0)
    return pl.pallas_call(
        _inproj_kernel,
        grid=(nt,),
        in_specs=[
            pl.BlockSpec((TM_PROJ, D_MODEL), lambda i: (i, 0)),
            pl.BlockSpec(wn.shape, full),
            pl.BlockSpec(wt.shape, full),
            pl.BlockSpec((1, IDX_DIM), full),
            pl.BlockSpec((1, IDX_DIM), full),
        ],
        out_specs=[
            pl.BlockSpec((TM_PROJ, OFF_Q), lambda i: (i, 0)),
            pl.BlockSpec((TM_PROJ, ATTN_DIM), lambda i: (i, 0)),
            pl.BlockSpec((TM_PROJ, ATTN_DIM), lambda i: (i, 0)),
            pl.BlockSpec((TM_PROJ, IDX_DIM), lambda i: (i, 0)),
            pl.BlockSpec((cpt, ATTN_DIM, KEY_CHUNK), lambda i: (i, 0, 0)),
            pl.BlockSpec((IDXT_ROWS, TM_PROJ), lambda i: (0, i)),
        ],
        out_shape=[
            jax.ShapeDtypeStruct((T, OFF_Q), F32),
            jax.ShapeDtypeStruct((T, ATTN_DIM), BF16),
            jax.ShapeDtypeStruct((T, ATTN_DIM), BF16),
            jax.ShapeDtypeStruct((T, IDX_DIM), F32),
            jax.ShapeDtypeStruct((T // KEY_CHUNK, ATTN_DIM, KEY_CHUNK), BF16),
            jax.ShapeDtypeStruct((IDXT_ROWS, T), F32),
        ],
        compiler_params=_cparams(("arbitrary",)),
        name="inproj",
    )(xb, wn, wt, kg, kb)


FIELD_BITS = 8
DIGIT_BITS = FIELD_BITS - 1
FIELDS = 32 // FIELD_BITS
DIGIT_SHIFTS = (25, 18, 11, 4, 0)
FIELD_ONES = 0x01010101
FIELD_GUARDS = 0x80808080 - (1 << 32)


def _pack_fields(f):
    q = f.shape[0] // FIELDS
    w = f[0:q]
    for j in range(1, FIELDS):
        w = w | lax.shift_left(f[j * q:(j + 1) * q], FIELD_BITS * j)
    return w | jnp.int32(FIELD_GUARDS)


def _count_fields_ge(words_ref, nch, cand):
    wq = KEY_CHUNK // FIELDS
    assert words_ref.shape[0] // SUBLANES < (1 << FIELD_BITS)
    cvec = cand * jnp.int32(FIELD_ONES)

    def body(c, acc):
        base = pl.multiple_of(c * wq, wq)
        w = words_ref[pl.ds(base, wq), :]
        hit = lax.shift_right_logical(w - cvec, DIGIT_BITS) & jnp.int32(FIELD_ONES)
        return acc + _tree(hit, jnp.add)

    acc = lax.fori_loop(0, nch, body, jnp.zeros((SUBLANES, words_ref.shape[1]), I32))
    tot = acc & 255
    for j in range(1, FIELDS):
        tot = tot + (lax.shift_right_logical(acc, FIELD_BITS * j) & 255)
    return jnp.sum(tot.astype(F32), axis=0, keepdims=True)


def _topk_threshold(keys_ref, words_ref, nch, n_sel, n_causal):
    lanes = keys_ref.shape[1]
    wq = KEY_CHUNK // FIELDS
    want = jnp.full((1, lanes), float(n_sel), F32)
    bucket = n_causal.astype(F32)
    prefix = None
    for idx, shift in enumerate(DIGIT_SHIFTS):
        width = (DIGIT_SHIFTS[idx - 1] if idx else 32) - shift
        if idx:
            def prep(c, carry, shift=shift, width=width, prefix=prefix):
                key = keys_ref[pl.ds(pl.multiple_of(c * KEY_CHUNK, KEY_CHUNK), KEY_CHUNK), :]
                digit = (key >> shift) & jnp.int32((1 << width) - 1)
                fields = jnp.where((key >> (shift + width)) == prefix, digit, 0)
                words_ref[pl.ds(pl.multiple_of(c * wq, wq), wq), :] = _pack_fields(fields)
                return carry

            lax.fori_loop(0, nch, prep, 0)
        dig = jnp.zeros((1, lanes), I32)
        at_dig = bucket
        above = jnp.zeros((1, lanes), F32)
        for bit in reversed(range(width)):
            cand = dig + (1 << bit)
            cnt = _count_fields_ge(words_ref, nch, cand)
            ok = cnt >= want
            dig = jnp.where(ok, cand, dig)
            at_dig = jnp.where(ok, cnt, at_dig)
            above = jnp.where(ok, above, cnt)
        want = want - above
        bucket = at_dig - above
        if idx:
            prefix = lax.shift_left(prefix, width) | dig
        else:
            prefix = dig - (1 << (DIGIT_BITS - 1))
    all_taken = n_causal <= n_sel
    tau = jnp.where(all_taken, jnp.int32(INT_MIN + 1), prefix)
    return tau, want, jnp.where(all_taken, 0.0, bucket - want)


def _attn_kernel(n_sel, it_ref, ki_ref, q_ref, k_ref, vt_ref, o_ref,
                 keys_ref, words_ref, rhs_ref, acc_ref, lg_ref):
    i = pl.program_id(1)
    nch = lax.shift_right_logical(i * Q_BLOCK + Q_BLOCK + KEY_CHUNK - 1, _log2(KEY_CHUNK))
    q0 = i * Q_BLOCK
    KC = KEY_CHUNK
    row = lax.broadcasted_iota(I32, (KC, Q_BLOCK), 0)
    lane = lax.broadcasted_iota(I32, (KC, Q_BLOCK), 1)
    qpos = q0 + lane

    qcat = jnp.concatenate(
        [it_ref[h * IDX_DIM:(h + 1) * IDX_DIM, :] for h in range(IDX_HEADS)], axis=1).astype(BF16)
    w_all = it_ref[IDX_HEADS * IDX_DIM:IDX_HEADS * IDX_DIM + IDX_HEADS, :] * (
        (IDX_HEADS ** -0.5) * (IDX_DIM ** -0.5))

    def score_body(c, carry):
        base = pl.multiple_of(c * KC, KC)
        kic = ki_ref[pl.ds(base, KC), :].astype(BF16)
        s = _dot(kic, qcat)
        sc = jnp.zeros((KC, Q_BLOCK), F32)
        for h in range(IDX_HEADS):
            sc = sc + jnp.maximum(s[:, h * Q_BLOCK:(h + 1) * Q_BLOCK], 0.0) * w_all[h:h + 1, :]
        bits = lax.bitcast_convert_type(sc, I32)
        key = bits ^ ((bits >> 31) & jnp.int32(0x7FFFFFFF))
        key = jnp.where(bits == jnp.int32(INT_MIN), jnp.int32(0), key)
        causal = base + row <= qpos
        keys_ref[pl.ds(base, KC), :] = jnp.where(causal, key, jnp.int32(INT_MIN))
        top = jnp.where(causal, (key >> DIGIT_SHIFTS[0]) + (1 << (DIGIT_BITS - 1)), 0)
        words_ref[pl.ds(pl.multiple_of(c * (KC // FIELDS), KC // FIELDS), KC // FIELDS), :] = _pack_fields(top)
        return carry

    lax.fori_loop(0, nch, score_body, 0)

    n_causal = q0 + lax.broadcasted_iota(I32, (1, Q_BLOCK), 1) + 1
    tau, keep, excess = _topk_threshold(keys_ref, words_ref, nch, n_sel, n_causal)

    @pl.when(jnp.max(excess) > 0.0)
    def _():
        r = lax.broadcasted_iota(I32, (KC, KC), 0)
        cc = lax.broadcasted_iota(I32, (KC, KC), 1)
        tri = (cc <= r).astype(BF16)

        def retire(c, before):
            base = pl.multiple_of(c * KC, KC)
            blk = keys_ref[pl.ds(base, KC), :]
            tied = blk == tau
            upto = _dot(tri, jnp.where(tied, 1.0, 0.0).astype(BF16)) + before
            keys_ref[pl.ds(base, KC), :] = jnp.where(tied & (upto > keep), jnp.int32(INT_MIN), blk)
            return upto[KC - 1:KC, :]

        lax.fori_loop(0, nch, retire, jnp.zeros((1, Q_BLOCK), F32))

    q = q_ref[...]
    lane_q = lax.broadcasted_iota(I32, (Q_BLOCK, 2 * HEAD_DIM), 1)
    for p in range(ATTN_HEADS // 2):
        qp = q[:, p * 2 * HEAD_DIM:(p + 1) * 2 * HEAD_DIM]
        rhs_ref[p, 0:Q_BLOCK, :] = jnp.where(lane_q < HEAD_DIM, qp, jnp.zeros_like(qp))
        rhs_ref[p, Q_BLOCK:2 * Q_BLOCK, :] = jnp.where(lane_q >= HEAD_DIM, qp, jnp.zeros_like(qp))
    acc_ref[...] = jnp.zeros_like(acc_ref)

    def stage_a_prep(c):
        base = pl.multiple_of(c * KC, KC)
        tk = (base + row - (q0 + Q_BLOCK - 1)).astype(F32)
        tkm = jnp.where(keys_ref[pl.ds(base, KC), :] >= tau, tk, NEG_BIG)
        return tkm, k_ref[pl.ds(base, KC), :]

    def stage_a_pair(c, p, tkm, kc):
        l2 = _dot_nt(kc[:, p * 2 * HEAD_DIM:(p + 1) * 2 * HEAD_DIM], rhs_ref[p])
        mcs = []
        for hh in range(2):
            h = 2 * p + hh
            slope = 2.0 ** (-8.0 * (h + 1) / ATTN_HEADS)
            lg = l2[:, hh * Q_BLOCK:(hh + 1) * Q_BLOCK] + slope * tkm
            lg_ref[h] = lg
            mcs.append(jnp.max(_tree(lg, jnp.maximum), axis=0, keepdims=True))
        return mcs

    def stage_b_pair(c, p, m_new, alpha):
        sums = []
        for hh in range(2):
            h = 2 * p + hh
            pr = jnp.exp(lg_ref[h] - m_new[h:h + 1, :])
            sums.append(jnp.sum(_tree(pr, jnp.add), axis=0, keepdims=True))
            pv = _dot(vt_ref[c, h * HEAD_DIM:(h + 1) * HEAD_DIM, :], pr.astype(BF16))
            acc_ref[h * HEAD_DIM:(h + 1) * HEAD_DIM, :] = (
                acc_ref[h * HEAD_DIM:(h + 1) * HEAD_DIM, :] * alpha[h:h + 1, :] + pv)
        return sums

    def stage_b_all(c, m_old, l_old, mc):
        m_new = jnp.maximum(m_old, mc)
        alpha = jnp.exp(m_old - m_new)
        return m_new, alpha

    tkm0, kc0 = stage_a_prep(0)
    mc0 = jnp.concatenate(sum([stage_a_pair(0, p, tkm0, kc0) for p in range(ATTN_HEADS // 2)], []), axis=0)

    def attn_body(c, carry):
        m_old, l_old, mc_prev = carry
        m_new, alpha = stage_b_all(c - 1, m_old, l_old, mc_prev)
        tkm, kc = stage_a_prep(c)
        mcs, sums = [], []
        for p in range(ATTN_HEADS // 2):
            sums += stage_b_pair(c - 1, p, m_new, alpha)
            mcs += stage_a_pair(c, p, tkm, kc)
        return m_new, l_old * alpha + jnp.concatenate(sums, axis=0), jnp.concatenate(mcs, axis=0)

    m0 = jnp.full((ATTN_HEADS, Q_BLOCK), NEG_BIG, F32)
    l0 = jnp.zeros((ATTN_HEADS, Q_BLOCK), F32)
    m_old, l_old, mc_prev = lax.fori_loop(1, nch, attn_body, (m0, l0, mc0))
    m_new, alpha = stage_b_all(nch - 1, m_old, l_old, mc_prev)
    sums = sum([stage_b_pair(nch - 1, p, m_new, alpha) for p in range(ATTN_HEADS // 2)], [])
    l_fin = l_old * alpha + jnp.concatenate(sums, axis=0)

    outs = []
    for h in range(ATTN_HEADS):
        outs.append(acc_ref[h * HEAD_DIM:(h + 1) * HEAD_DIM, :] / l_fin[h:h + 1, :])
    o_ref[...] = jnp.concatenate(outs, axis=0).T


def _attn(it, ki, q, k, vt, B, S):
    T = B * S
    nb = S // Q_BLOCK
    n_sel = min(INDEX_TOPK_MAX, S // 4)
    return pl.pallas_call(
        functools.partial(_attn_kernel, n_sel),
        grid=(B, nb),
        in_specs=[
            pl.BlockSpec((IDXT_ROWS, Q_BLOCK), lambda b, i: (0, b * nb + i)),
            pl.BlockSpec((S, IDX_DIM), lambda b, i: (b, 0)),
            pl.BlockSpec((Q_BLOCK, ATTN_DIM), lambda b, i: (b * nb + i, 0)),
            pl.BlockSpec((S, ATTN_DIM), lambda b, i: (b, 0)),
            pl.BlockSpec((S // KEY_CHUNK, ATTN_DIM, KEY_CHUNK), lambda b, i: (b, 0, 0)),
        ],
        out_specs=pl.BlockSpec((Q_BLOCK, ATTN_DIM), lambda b, i: (b * nb + i, 0)),
        out_shape=jax.ShapeDtypeStruct((T, ATTN_DIM), F32),
        scratch_shapes=[
            pltpu.VMEM((S, Q_BLOCK), I32),
            pltpu.VMEM((S // FIELDS, Q_BLOCK), I32),
            pltpu.VMEM((ATTN_HEADS // 2, 2 * Q_BLOCK, 2 * HEAD_DIM), BF16),
            pltpu.VMEM((ATTN_DIM, Q_BLOCK), F32),
            pltpu.VMEM((ATTN_HEADS, KEY_CHUNK, Q_BLOCK), F32),
        ],
        compiler_params=_cparams(("arbitrary", "arbitrary")),
        name="attn",
    )(it, ki, q, k, vt)


def _mixout_kernel(tiles_per_seq, a_ref, halo_ref, y_ref, x_ref, cw_ref, wpool_ref, ps_ref,
                   wout_ref, g_ref, b_ref, o_ref, ext_ref):
    i = pl.program_id(0)
    tm = TM_MIX
    first = lax.rem(i, tiles_per_seq) == 0
    halo = jnp.where(first, 0.0, halo_ref[...])
    a = a_ref[...]
    h_c, gb_c, gc_c, p_c = (a[:, j * CONV_DIM:(j + 1) * CONV_DIM] for j in range(4))

    t0 = 2 * HALO
    n_ext = t0 + tm
    ext_ref[0:HALO, :] = jnp.zeros((HALO, CONV_DIM), F32)

    ext_ref[HALO:t0, :] = halo[:, 2 * CONV_DIM:3 * CONV_DIM] * halo[:, 0:CONV_DIM]
    u = gc_c * h_c
    ext_ref[t0:n_ext, :] = u
    cw = cw_ref[...]
    conv = cw[2:3, :] * u
    conv = conv + cw[1:2, :] * ext_ref[t0 - 1:n_ext - 1, :]
    conv = conv + cw[0:1, :] * ext_ref[t0 - 2:n_ext - 2, :]
    y_conv = gb_c * conv

    ext_ref[HALO:t0, :] = halo[:, OFF_P:OFF_P + POOL_DIM]
    ext_ref[t0:n_ext, :] = p_c
    sums = {}
    step = 1
    while step < POOL_WINDOWS[-1]:
        cur = ext_ref[HALO:n_ext, :] + ext_ref[HALO - step:n_ext - step, :]
        ext_ref[HALO:n_ext, :] = cur
        step *= 2
        sums[step] = ext_ref[t0:n_ext, :]
    tpos = (lax.rem(i, tiles_per_seq) * tm + lax.broadcasted_iota(I32, (tm, POOL_DIM), 0) + 1).astype(F32)
    grp = lax.shift_right_logical(lax.broadcasted_iota(I32, (tm, POOL_DIM), 1), _log2(POOL_GROUP))
    mean = jnp.zeros((tm, POOL_DIM), F32)
    for gi, w in enumerate(POOL_WINDOWS):
        mean = jnp.where(grp == gi, sums[w] / jnp.minimum(tpos, float(w)), mean)
    mixed = mean - p_c
    y_pool = _dot(mixed.astype(BF16), wpool_ref[...]) * ps_ref[...]

    mix = _dot(y_conv.astype(BF16), wout_ref[0:CONV_DIM, :])
    mix = mix + _dot(y_pool.astype(BF16), wout_ref[CONV_DIM:CONV_DIM + POOL_DIM, :])
    mix = mix + _dot(y_ref[...].astype(BF16), wout_ref[CONV_DIM + POOL_DIM:D_MODEL, :])
    z = DN_ALPHA * x_ref[...] + mix
    o_ref[...] = _layernorm_rows(z, g_ref[...], b_ref[...])


def _mixout(a, y_attn, x, cw, wpool_bd, ps, wout, g, b, S):
    T = a.shape[0]
    nt = T // TM_MIX
    tps = S // TM_MIX
    hb = TM_MIX // HALO
    full = lambda i: (0, 0)
    return pl.pallas_call(
        functools.partial(_mixout_kernel, tps),
        grid=(nt,),
        in_specs=[
            pl.BlockSpec((TM_MIX, OFF_Q), lambda i: (i, 0)),
            pl.BlockSpec((HALO, OFF_Q), lambda i: (jnp.maximum(i * hb - 1, 0), 0)),
            pl.BlockSpec((TM_MIX, ATTN_DIM), lambda i: (i, 0)),
            pl.BlockSpec((TM_MIX, D_MODEL), lambda i: (i, 0)),
            pl.BlockSpec((CONV_WIDTH, CONV_DIM), full),
            pl.BlockSpec((POOL_DIM, POOL_DIM), full),
            pl.BlockSpec((1, POOL_DIM), full),
            pl.BlockSpec((D_MODEL, D_MODEL), full),
            pl.BlockSpec((1, D_MODEL), full),
            pl.BlockSpec((1, D_MODEL), full),
        ],
        out_specs=pl.BlockSpec((TM_MIX, D_MODEL), lambda i: (i, 0)),
        out_shape=jax.ShapeDtypeStruct((T, D_MODEL), F32),
        scratch_shapes=[pltpu.VMEM((2 * HALO + TM_MIX, CONV_DIM), F32)],
        compiler_params=_cparams(("arbitrary",)),
        name="mixout",
    )(a, a, y_attn, x, cw, wpool_bd, ps, wout, g, b)


def _router_kernel(x_ref, rw_ref, rb_ref, idx_ref, gate_ref, cnt_ref):
    i = pl.program_id(0)
    tm = TM_ROUTE
    logits = lax.dot_general(rw_ref[...], x_ref[...], (((1,), (1,)), ((), ())),
                             precision=lax.Precision.HIGHEST,
                             preferred_element_type=F32) + rb_ref[...]
    erow = lax.broadcasted_iota(I32, (N_EXPERTS, tm), 0).astype(F32)
    work = logits
    vals, idxs = [], []
    multi = jnp.zeros((N_EXPERTS, tm), F32)
    for _ in range(TOP_K):
        mx = jnp.max(work, axis=0, keepdims=True)
        pick = jnp.min(jnp.where(work == mx, erow, float(N_EXPERTS)), axis=0, keepdims=True)
        hit = erow == pick
        work = jnp.where(hit, -jnp.inf, work)
        multi = multi + hit.astype(F32)
        vals.append(mx)
        idxs.append(pick)
    es = [jnp.exp(v - vals[0]) for v in vals]
    den = es[0] + es[1] + es[2] + es[3]
    idx_ref[...] = jnp.concatenate(idxs, axis=0).astype(I32)
    gate_ref[...] = jnp.concatenate([e / den for e in es] + [jnp.zeros((SUBLANES - TOP_K, tm), F32)], axis=0)
    part = multi[:, 0:LANES]
    for j in range(1, tm // LANES):
        part = part + multi[:, j * LANES:(j + 1) * LANES]

    @pl.when(i == 0)
    def _():
        cnt_ref[...] = jnp.zeros_like(cnt_ref)

    cnt_ref[...] += part


def _router(x1, rwT, rb):
    T = x1.shape[0]
    nt = T // TM_ROUTE
    full = lambda i: (0, 0)
    return pl.pallas_call(
        _router_kernel,
        grid=(nt,),
        in_specs=[
            pl.BlockSpec((TM_ROUTE, D_MODEL), lambda i: (i, 0)),
            pl.BlockSpec((N_EXPERTS, D_MODEL), full),
            pl.BlockSpec((N_EXPERTS, 1), full),
        ],
        out_specs=[
            pl.BlockSpec((TOP_K, TM_ROUTE), lambda i: (0, i)),
            pl.BlockSpec((SUBLANES, TM_ROUTE), lambda i: (0, i)),
            pl.BlockSpec((N_EXPERTS, LANES), full),
        ],
        out_shape=[
            jax.ShapeDtypeStruct((TOP_K, T), I32),
            jax.ShapeDtypeStruct((SUBLANES, T), F32),
            jax.ShapeDtypeStruct((N_EXPERTS, LANES), F32),
        ],
        compiler_params=_cparams(("arbitrary",)),
        name="router",
    )(x1, rwT, rb)


def _ranks_kernel(n_tiles_pad, idx_ref, cnt_ref, dest_ref, meta_ref, tri_ref, start_ref, carry_ref):
    i = pl.program_id(0)
    tm = TM_ROUTE
    erow = lax.broadcasted_iota(I32, (N_EXPERTS, LANES), 0)
    elane = lax.broadcasted_iota(I32, (N_EXPERTS, LANES), 1)

    @pl.when(i == 0)
    def _():
        cnt = jnp.sum(cnt_ref[...], axis=1, keepdims=True)
        cnt_i = jnp.broadcast_to(cnt, (N_EXPERTS, LANES)).astype(I32)
        padded = lax.shift_left(lax.shift_right_logical(cnt_i + (TM_MOE - 1), _log2(TM_MOE)), _log2(TM_MOE))
        r = lax.broadcasted_iota(I32, (N_EXPERTS, N_EXPERTS), 0)
        c = lax.broadcasted_iota(I32, (N_EXPERTS, N_EXPERTS), 1)
        low = (c <= r).astype(F32)
        pad_end = lax.dot_general(low, padded.astype(F32), (((1,), (0,)), ((), ())),
                                  precision=lax.Precision.HIGHEST, preferred_element_type=F32)
        pad_start = pad_end - padded.astype(F32)
        start_ref[...] = pad_start
        carry_ref[...] = jnp.zeros_like(carry_ref)
        a = lax.broadcasted_iota(I32, (tm, tm), 0)
        bcol = lax.broadcasted_iota(I32, (tm, tm), 1)
        tri_ref[...] = (a < bcol).astype(BF16)
        ntp = n_tiles_pad
        tstart = (lax.broadcasted_iota(I32, (N_EXPERTS, ntp), 1) * TM_MOE).astype(F32)
        pe = jnp.concatenate([pad_end] * (ntp // LANES), axis=1)
        texp = jnp.sum((pe <= tstart).astype(F32), axis=0, keepdims=True)
        texp = jnp.minimum(texp, float(N_EXPERTS - 1)).astype(I32)
        total = jnp.max(pad_end, axis=0, keepdims=True)
        n_used = lax.shift_right_logical(total.astype(I32), _log2(TM_MOE))
        zstart = jnp.sum(jnp.where(erow == elane, pad_start + cnt_i.astype(F32), 0.0),
                         axis=0, keepdims=True).astype(I32)
        zlen = jnp.sum(jnp.where(erow == elane, (padded - cnt_i).astype(F32), 0.0),
                       axis=0, keepdims=True).astype(I32)
        lanes_pad = jnp.zeros((1, ntp - LANES), I32)
        meta_ref[...] = jnp.concatenate(
            [texp,
             jnp.concatenate([n_used, lanes_pad], axis=1),
             jnp.concatenate([zstart, lanes_pad], axis=1),
             jnp.concatenate([zlen, lanes_pad], axis=1),
             jnp.zeros((SUBLANES - 4, ntp), I32)], axis=0)

    idx = idx_ref[...]
    erow_t = lax.broadcasted_iota(I32, (N_EXPERTS, tm), 0)
    hits = [erow_t == idx[k:k + 1, :] for k in range(TOP_K)]
    multi = hits[0].astype(F32)
    for k in range(1, TOP_K):
        multi = multi + hits[k].astype(F32)
    prefix = _dot(multi.astype(BF16), tri_ref[...])
    base = jnp.concatenate([carry_ref[...] + start_ref[...]] * (tm // LANES), axis=1)
    tot = prefix + base
    dest_ref[...] = jnp.concatenate(
        [jnp.sum(jnp.where(hits[k], tot, 0.0), axis=0, keepdims=True) for k in range(TOP_K)],
        axis=0).astype(I32)
    carry_ref[...] += jnp.broadcast_to(jnp.sum(multi, axis=1, keepdims=True), (N_EXPERTS, LANES))


def _ranks(idxT, cnt, n_tiles_pad):
    T = idxT.shape[1]
    nt = T // TM_ROUTE
    full = lambda i: (0, 0)
    return pl.pallas_call(
        functools.partial(_ranks_kernel, n_tiles_pad),
        grid=(nt,),
        in_specs=[
            pl.BlockSpec((TOP_K, TM_ROUTE), lambda i: (0, i)),
            pl.BlockSpec((N_EXPERTS, LANES), full),
        ],
        out_specs=[
            pl.BlockSpec((TOP_K, TM_ROUTE), lambda i: (0, i)),
            pl.BlockSpec((SUBLANES, n_tiles_pad), full),
        ],
        out_shape=[
            jax.ShapeDtypeStruct((TOP_K, T), I32),
            jax.ShapeDtypeStruct((SUBLANES, n_tiles_pad), I32),
        ],
        scratch_shapes=[
            pltpu.VMEM((TM_ROUTE, TM_ROUTE), BF16),
            pltpu.VMEM((N_EXPERTS, LANES), F32),
            pltpu.VMEM((N_EXPERTS, LANES), F32),
        ],
        compiler_params=_cparams(("arbitrary",)),
        name="ranks",
    )(idxT, cnt)


RT = D_MODEL // LANES
DMA_UNROLL = 8


def _row_copy(src_ref, s, dst_ref, d, sem):
    return pltpu.make_async_copy(src_ref.at[pl.ds(pl.multiple_of(s * RT, RT), RT)],
                                 dst_ref.at[pl.ds(pl.multiple_of(d * RT, RT), RT)], sem)


def _to_row_tiled(dst_ref, val, rows):
    for s in range(RT):
        dst_ref[pl.ds(s, rows, stride=RT), :] = val[:, s * LANES:(s + 1) * LANES]


def _from_row_tiled(src_ref, rows):
    return [src_ref[pl.ds(s, rows, stride=RT), :] for s in range(RT)]


def _scatter_kernel(meta_ref, x_ref, dest_ref, xs_ref, stage_ref, zero_ref, sem):
    i = pl.program_id(0)
    tm = TM_ROUTE

    @pl.when(i == 0)
    def _():
        zero_ref[...] = jnp.zeros_like(zero_ref)

        def zcopy(row, nrows):
            off = pl.multiple_of(row * RT, RT)
            return pltpu.make_async_copy(zero_ref.at[pl.ds(0, nrows * RT)],
                                         xs_ref.at[pl.ds(off, nrows * RT)], sem)

        def pad_fill(wait):
            def body(e, c):
                row = meta_ref[2, e]
                plen = meta_ref[3, e]
                for bit in reversed(range(_log2(TM_MOE))):
                    size = 1 << bit
                    has = (plen & size) != 0

                    @pl.when(has)
                    def _():
                        cp = zcopy(row, size)
                        cp.wait() if wait else cp.start()

                    row = row + jnp.where(has, size, 0)
                return c
            lax.fori_loop(0, N_EXPERTS, body, 0)

        def tail_fill(wait):
            def body(j, c):
                cp = zcopy(j * TM_MOE, TM_MOE)
                cp.wait() if wait else cp.start()
                return c
            lax.fori_loop(meta_ref[1, 0], xs_ref.shape[0] // (TM_MOE * RT), body, 0)

        pad_fill(False)
        tail_fill(False)
        pad_fill(True)
        tail_fill(True)

    _to_row_tiled(stage_ref, x_ref[...], tm)

    def start(t, c):
        for k in range(TOP_K):
            _row_copy(stage_ref, t, xs_ref, dest_ref[k, t], sem).start(priority=k % 2)
        return c

    lax.fori_loop(0, tm, start, 0, unroll=DMA_UNROLL)
    for k in range(TOP_K):
        pltpu.make_async_copy(stage_ref, xs_ref.at[pl.ds(0, tm * RT)], sem).wait()


def _scatter(meta, x1, destT, n_rows_alloc):
    T = x1.shape[0]
    nt = T // TM_ROUTE
    return pl.pallas_call(
        _scatter_kernel,
        grid_spec=pltpu.PrefetchScalarGridSpec(
            num_scalar_prefetch=1,
            grid=(nt,),
            in_specs=[
                pl.BlockSpec((TM_ROUTE, D_MODEL), lambda i, m: (i, 0)),
                pl.BlockSpec((TOP_K, TM_ROUTE), lambda i, m: (0, i), memory_space=pltpu.SMEM),
            ],
            out_specs=pl.BlockSpec(memory_space=pl.ANY),
            scratch_shapes=[
                pltpu.VMEM((TM_ROUTE * RT, LANES), F32),
                pltpu.VMEM((TM_MOE * RT, LANES), F32),
                pltpu.SemaphoreType.DMA(()),
            ],
        ),
        out_shape=jax.ShapeDtypeStruct((n_rows_alloc * RT, LANES), F32),
        compiler_params=_cparams(("arbitrary",)),
        name="scatter",
    )(meta, x1, destT)


def _gmm_kernel(meta_ref, xs_ref, wgu_ref, bgu_ref, wd_ref, bd_ref, ys_ref, wgu_b, wd_b, lhs_ref, act_ref):
    j = pl.program_id(0)
    n_used = meta_ref[1, 0]
    e_now = meta_ref[0, j]
    e_prev = meta_ref[0, jnp.maximum(j - 1, 0)]
    used = j < n_used

    @pl.when(used & ((j == 0) | (e_now != e_prev)))
    def _():
        wgu_b[...] = wgu_ref[0, 0].astype(BF16)
        wd_b[...] = wd_ref[0, 0].astype(BF16)

    @pl.when(used)
    def _():
        for s, piece in enumerate(_from_row_tiled(xs_ref, TM_MOE)):
            lhs_ref[:, s * LANES:(s + 1) * LANES] = piece.astype(BF16)
        x = lhs_ref[...]
        nc = 256
        for c in range(D_FF // nc):
            gate = _dot(x, wgu_b[:, c * nc:(c + 1) * nc]) + bgu_ref[0, 0, :, c * nc:(c + 1) * nc]
            up = (_dot(x, wgu_b[:, D_FF + c * nc:D_FF + (c + 1) * nc])
                  + bgu_ref[0, 0, :, D_FF + c * nc:D_FF + (c + 1) * nc])
            gate = jnp.minimum(gate, SWIGLU_LIMIT)
            up = jnp.clip(up, -SWIGLU_LIMIT, SWIGLU_LIMIT)
            act = gate * jax.nn.sigmoid(SWIGLU_ALPHA * gate) * (up + 1.0)
            act_ref[:, c * nc:(c + 1) * nc] = act.astype(BF16)
        _to_row_tiled(ys_ref, _dot(act_ref[...], wd_b[...]) + bd_ref[0, 0], TM_MOE)

    @pl.when(jnp.logical_not(used))
    def _():
        ys_ref[...] = jnp.zeros_like(ys_ref)


def _gmm(meta, xs, w_gu, b_gu, w_down, b_down, n_tiles, layer):
    last = lambda m: jnp.maximum(m[1, 0] - 1, 0)
    return pl.pallas_call(
        _gmm_kernel,
        grid_spec=pltpu.PrefetchScalarGridSpec(
            num_scalar_prefetch=1,
            grid=(n_tiles,),
            in_specs=[
                pl.BlockSpec((TM_MOE * RT, LANES), lambda j, m: (jnp.minimum(j, last(m)), 0)),
                pl.BlockSpec((1, 1, D_MODEL, 2 * D_FF), lambda j, m: (layer, m[0, j], 0, 0)),
                pl.BlockSpec((1, 1, 1, 2 * D_FF), lambda j, m: (layer, m[0, j], 0, 0)),
                pl.BlockSpec((1, 1, D_FF, D_MODEL), lambda j, m: (layer, m[0, j], 0, 0)),
                pl.BlockSpec((1, 1, 1, D_MODEL), lambda j, m: (layer, m[0, j], 0, 0)),
            ],
            out_specs=pl.BlockSpec((TM_MOE * RT, LANES), lambda j, m: (j, 0)),
            scratch_shapes=[
                pltpu.VMEM((D_MODEL, 2 * D_FF), BF16),
                pltpu.VMEM((D_FF, D_MODEL), BF16),
                pltpu.VMEM((TM_MOE, D_MODEL), BF16),
                pltpu.VMEM((TM_MOE, D_FF), BF16),
            ],
        ),
        out_shape=jax.ShapeDtypeStruct((n_tiles * TM_MOE * RT, LANES), F32),
        compiler_params=_cparams(("arbitrary",)),
        name="gmm",
    )(meta, xs, w_gu, b_gu, w_down, b_down)


def _combine_kernel(x_ref, dest_ref, gate_ref, ys_ref, g_ref, b_ref, o_ref, ob_ref, buf_ref, sem):
    tm = TM_COMB

    def start(t, c):
        for k in range(TOP_K):
            _row_copy(ys_ref, dest_ref[k, t], buf_ref.at[k], t, sem).start(priority=k % 2)
        return c

    lax.fori_loop(0, tm, start, 0, unroll=DMA_UNROLL)
    for k in range(TOP_K):
        pltpu.make_async_copy(ys_ref.at[pl.ds(0, tm * RT)], buf_ref.at[k], sem).wait()
    gates = jnp.concatenate(
        [gate_ref[...], jnp.zeros((LANES - SUBLANES, tm), F32)], axis=0).T
    pieces = None
    for k in range(TOP_K):
        gk = gates[:, k:k + 1]
        rows = [p * gk for p in _from_row_tiled(buf_ref.at[k], tm)]
        pieces = rows if pieces is None else [a + r for a, r in zip(pieces, rows)]
    z = DN_ALPHA * x_ref[...] + jnp.concatenate(pieces, axis=1)
    out = _layernorm_rows(z, g_ref[...], b_ref[...])
    o_ref[...] = out
    ob_ref[...] = out.astype(BF16)


def _combine(x1, destT, gateT, ys, g, b):
    T = x1.shape[0]
    nt = T // TM_COMB
    full = lambda i: (0, 0)
    return pl.pallas_call(
        _combine_kernel,
        grid=(nt,),
        in_specs=[
            pl.BlockSpec((TM_COMB, D_MODEL), lambda i: (i, 0)),
            pl.BlockSpec((TOP_K, TM_COMB), lambda i: (0, i), memory_space=pltpu.SMEM),
            pl.BlockSpec((SUBLANES, TM_COMB), lambda i: (0, i)),
            pl.BlockSpec(memory_space=pl.ANY),
            pl.BlockSpec((1, D_MODEL), full),
            pl.BlockSpec((1, D_MODEL), full),
        ],
        out_specs=[
            pl.BlockSpec((TM_COMB, D_MODEL), lambda i: (i, 0)),
            pl.BlockSpec((TM_COMB, D_MODEL), lambda i: (i, 0)),
        ],
        out_shape=[
            jax.ShapeDtypeStruct((T, D_MODEL), F32),
            jax.ShapeDtypeStruct((T, D_MODEL), BF16),
        ],
        scratch_shapes=[
            pltpu.VMEM((TOP_K, TM_COMB * RT, LANES), F32),
            pltpu.SemaphoreType.DMA(()),
        ],
        compiler_params=_cparams(("arbitrary",)),
        name="combine",
    )(x1, destT, gateT, ys, g, b)


def _block_diag(w):
    g, c, _ = w.shape
    out = jnp.zeros((g * c, g * c), w.dtype)
    for i in range(g):
        out = out.at[i * c:(i + 1) * c, i * c:(i + 1) * c].set(w[i])
    return out


def kernel(x, w_in, conv_w, w_pool, pool_scale, idx_kn_g, idx_kn_b, w_out, ln1_g, ln1_b, router_w,
           router_b, w_gu, b_gu, w_down, b_down, ln2_g, ln2_b):
    B, S, D = x.shape
    T = B * S
    depth = w_in.shape[0]
    n_pairs = T * TOP_K
    n_tiles = (n_pairs + N_EXPERTS * (TM_MOE - 1)) // TM_MOE + 1
    n_tiles_pad = ((n_tiles + LANES - 1) // LANES) * LANES
    n_rows_alloc = (n_tiles + 1) * TM_MOE

    xf = x.reshape(T, D)
    xb = xf.astype(BF16)
    for l in range(depth):
        wl = w_in[l]
        wn = jnp.concatenate(
            [wl[:, 0:OFF_V], wl[:, OFF_KI:OFF_WI], jnp.zeros((D, LANES - IDX_DIM), F32)], axis=1).astype(BF16)
        wt = jnp.concatenate(
            [wl[:, OFF_V:OFF_QI], wl[:, OFF_QI:OFF_KI], wl[:, OFF_WI:D_IN],
             jnp.zeros((D, IDXT_ROWS - IDX_HEADS * IDX_DIM - IDX_HEADS), F32)], axis=1).T.astype(BF16)
        a, q, k, ki, vt, it = _inproj(xb, wn, wt, idx_kn_g[l].reshape(1, -1), idx_kn_b[l].reshape(1, -1))
        y_attn = _attn(it, ki, q, k, vt, B, S)
        x1 = _mixout(a, y_attn, xf, conv_w[l], _block_diag(w_pool[l]).astype(BF16),
                     pool_scale[l].reshape(1, -1), w_out[l].astype(BF16),
                     ln1_g[l].reshape(1, -1), ln1_b[l].reshape(1, -1), S)
        idxT, gateT, cnt = _router(x1, router_w[l].T, router_b[l].reshape(-1, 1))
        destT, meta = _ranks(idxT, cnt, n_tiles_pad)
        xs = _scatter(meta, x1, destT, n_rows_alloc)
        ys = _gmm(meta, xs, w_gu, b_gu.reshape(depth, N_EXPERTS, 1, -1), w_down,
                  b_down.reshape(depth, N_EXPERTS, 1, -1), n_tiles, l)
        xf, xb = _combine(x1, destT, gateT, ys, ln2_g[l].reshape(1, -1), ln2_b[l].reshape(1, -1))
    return xf.reshape(B, S, D)
```

```python
import functools

import jax
import jax.numpy as jnp
import numpy as np
from jax import lax
from jax.experimental import pallas as pl
from jax.experimental.pallas import tpu as pltpu

F32 = jnp.float32
BF16 = jnp.bfloat16
I32 = jnp.int32

D_MODEL = 1024
CONV_DIM = 256
CONV_WIDTH = 3
POOL_WINDOWS = (2, 4, 8, 16)
POOL_GROUP = 64
POOL_DIM = 256
ATTN_HEADS = 8
HEAD_DIM = 64
ATTN_DIM = 512
IDX_HEADS = 8
IDX_DIM = 32
INDEX_TOPK_MAX = 256
Q_BLOCK = 128
N_EXPERTS = 32
TOP_K = 4
D_FF = 1024
SWIGLU_ALPHA = 1.702
SWIGLU_LIMIT = 7.0
LN_EPS = 1e-5
DEPTH = 2
DN_ALPHA = (2.0 * DEPTH) ** 0.25

OFF_P = 3 * CONV_DIM
OFF_Q = OFF_P + POOL_DIM
OFF_K = OFF_Q + ATTN_DIM
OFF_V = OFF_K + ATTN_DIM
OFF_QI = OFF_V + ATTN_DIM
OFF_KI = OFF_QI + IDX_HEADS * IDX_DIM
OFF_WI = OFF_KI + IDX_DIM
D_IN = OFF_WI + IDX_HEADS

LANES = 128
SUBLANES = 8
VMEM_LIMIT = 56 * 1024 * 1024
KEY_CHUNK = 512
HALO = 16
TM_PROJ = 512
TM_MIX = 512
TM_ROUTE = 512
TM_MOE = 512
TM_COMB = 512
IDXT_ROWS = IDX_HEADS * IDX_DIM + 16
WT_ROWS = ATTN_DIM + IDXT_ROWS
INT_MIN = -(2 ** 31)
NEG_BIG = -1e30


def _cparams(sem, flags=None):
    return pltpu.CompilerParams(dimension_semantics=sem, vmem_limit_bytes=VMEM_LIMIT, flags=flags)


def _log2(n):
    k = n.bit_length() - 1
    assert 1 << k == n
    return k


def _dot(a, b):
    return jnp.dot(a, b, preferred_element_type=F32)


def _dot_nt(a, b):
    return lax.dot_general(a, b, (((1,), (1,)), ((), ())), preferred_element_type=F32)


def _tree(x, op):
    parts = [x[j * SUBLANES:(j + 1) * SUBLANES, :] for j in range(x.shape[0] // SUBLANES)]
    while len(parts) > 1:
        nxt = [op(parts[j], parts[j + 1]) for j in range(0, len(parts) - 1, 2)]
        if len(parts) % 2:
            nxt.append(parts[-1])
        parts = nxt
    return parts[0]


def _layernorm_rows(z, g, b):
    mu = jnp.mean(z, axis=-1, keepdims=True)
    d = z - mu
    var = jnp.mean(d * d, axis=-1, keepdims=True)
    return d * lax.rsqrt(var + LN_EPS) * g + b


def _inproj_kernel(x_ref, wn_ref, wt_ref, kg_ref, kb_ref,
                   a_ref, q_ref, k_ref, ki_ref, vt_ref, it_ref):
    x = x_ref[...]
    a_ref[...] = _dot(x, wn_ref[:, 0:OFF_Q])
    q_ref[...] = (_dot(x, wn_ref[:, OFF_Q:OFF_K]) * (HEAD_DIM ** -0.5)).astype(BF16)
    k_ref[...] = _dot(x, wn_ref[:, OFF_K:OFF_V]).astype(BF16)
    ki = _dot(x, wn_ref[:, OFF_V:OFF_V + LANES])[:, 0:IDX_DIM]
    ki_ref[...] = _layernorm_rows(ki, kg_ref[...], kb_ref[...])
    t = _dot_nt(wt_ref[...], x)
    for c in range(TM_PROJ // KEY_CHUNK):
        vt_ref[c] = t[0:ATTN_DIM, c * KEY_CHUNK:(c + 1) * KEY_CHUNK].astype(BF16)
    it_ref[...] = t[ATTN_DIM:WT_ROWS, :]


def _inproj(xb, wn, wt, kg, kb):
    T = xb.shape[0]
    nt = T // TM_PROJ
    cpt = TM_PROJ // KEY_CHUNK
    full = lambda i: (0, 0)
    return pl.pallas_call(
        _inproj_kernel,
        grid=(nt,),
        in_specs=[
            pl.BlockSpec((TM_PROJ, D_MODEL), lambda i: (i, 0)),
            pl.BlockSpec(wn.shape, full),
            pl.BlockSpec(wt.shape, full),
            pl.BlockSpec((1, IDX_DIM), full),
            pl.BlockSpec((1, IDX_DIM), full),
        ],
        out_specs=[
            pl.BlockSpec((TM_PROJ, OFF_Q), lambda i: (i, 0)),
            pl.BlockSpec((TM_PROJ, ATTN_DIM), lambda i: (i, 0)),
            pl.BlockSpec((TM_PROJ, ATTN_DIM), lambda i: (i, 0)),
            pl.BlockSpec((TM_PROJ, IDX_DIM), lambda i: (i, 0)),
            pl.BlockSpec((cpt, ATTN_DIM, KEY_CHUNK), lambda i: (i, 0, 0)),
            pl.BlockSpec((IDXT_ROWS, TM_PROJ), lambda i: (0, i)),
        ],
        out_shape=[
            jax.ShapeDtypeStruct((T, OFF_Q), F32),
            jax.ShapeDtypeStruct((T, ATTN_DIM), BF16),
            jax.ShapeDtypeStruct((T, ATTN_DIM), BF16),
            jax.ShapeDtypeStruct((T, IDX_DIM), F32),
            jax.ShapeDtypeStruct((T // KEY_CHUNK, ATTN_DIM, KEY_CHUNK), BF16),
            jax.ShapeDtypeStruct((IDXT_ROWS, T), F32),
        ],
        compiler_params=_cparams(("arbitrary",)),
        name="inproj",
    )(xb, wn, wt, kg, kb)


FIELD_BITS = 8
DIGIT_BITS = FIELD_BITS - 1
FIELDS = 32 // FIELD_BITS
DIGIT_SHIFTS = (25, 18, 11, 4, 0)
FIELD_ONES = 0x01010101
FIELD_GUARDS = 0x80808080 - (1 << 32)


def _pack_fields(f):
    q = f.shape[0] // FIELDS
    w = f[0:q]
    for j in range(1, FIELDS):
        w = w | lax.shift_left(f[j * q:(j + 1) * q], FIELD_BITS * j)
    return w | jnp.int32(FIELD_GUARDS)


def _count_fields_ge(words_ref, nch, cand):
    wq = 2 * (KEY_CHUNK // FIELDS)
    assert words_ref.shape[0] // SUBLANES < (1 << FIELD_BITS)
    cvec = cand * jnp.int32(FIELD_ONES)

    def body(c, acc):
        base = pl.multiple_of(c * wq, wq)
        w = words_ref[pl.ds(base, wq), :]
        hit = lax.shift_right_logical(w - cvec, DIGIT_BITS) & jnp.int32(FIELD_ONES)
        return acc + _tree(hit, jnp.add)

    acc = lax.fori_loop(0, lax.shift_right_logical(nch + 1, 1), body,
                        jnp.zeros((SUBLANES, words_ref.shape[1]), I32))
    tot = acc & 255
    for j in range(1, FIELDS):
        tot = tot + (lax.shift_right_logical(acc, FIELD_BITS * j) & 255)
    return jnp.sum(tot.astype(F32), axis=0, keepdims=True)


def _topk_threshold(keys_ref, words_ref, nch, n_sel, n_causal):
    lanes = keys_ref.shape[1]
    wq = KEY_CHUNK // FIELDS
    want = jnp.full((1, lanes), float(n_sel), F32)
    bucket = n_causal.astype(F32)
    prefix = None
    for idx, shift in enumerate(DIGIT_SHIFTS):
        width = (DIGIT_SHIFTS[idx - 1] if idx else 32) - shift
        if idx:
            def prep(c, carry, shift=shift, width=width, prefix=prefix):
                key = keys_ref[pl.ds(pl.multiple_of(c * KEY_CHUNK, KEY_CHUNK), KEY_CHUNK), :]
                digit = (key >> shift) & jnp.int32((1 << width) - 1)
                fields = jnp.where((key >> (shift + width)) == prefix, digit, 0)
                words_ref[pl.ds(pl.multiple_of(c * wq, wq), wq), :] = _pack_fields(fields)
                return carry

            lax.fori_loop(0, nch, prep, 0)
        dig = jnp.zeros((1, lanes), I32)
        at_dig = bucket
        above = jnp.zeros((1, lanes), F32)
        for bit in reversed(range(width)):
            cand = dig + (1 << bit)
            cnt = _count_fields_ge(words_ref, nch, cand)
            ok = cnt >= want
            dig = jnp.where(ok, cand, dig)
            at_dig = jnp.where(ok, cnt, at_dig)
            above = jnp.where(ok, above, cnt)
        want = want - above
        bucket = at_dig - above
        if idx:
            prefix = lax.shift_left(prefix, width) | dig
        else:
            prefix = dig - (1 << (DIGIT_BITS - 1))
    all_taken = n_causal <= n_sel
    tau = jnp.where(all_taken, jnp.int32(INT_MIN + 1), prefix)
    return tau, want, jnp.where(all_taken, 0.0, bucket - want)


def _attn_kernel(n_sel, it_ref, ki_ref, q_ref, k_ref, vt_ref, o_ref,
                 keys_ref, words_ref, rhs_ref, acc_ref, lg_ref):
    i = pl.program_id(1)
    nch = lax.shift_right_logical(i * Q_BLOCK + Q_BLOCK + KEY_CHUNK - 1, _log2(KEY_CHUNK))
    q0 = i * Q_BLOCK
    KC = KEY_CHUNK
    row = lax.broadcasted_iota(I32, (KC, Q_BLOCK), 0)
    lane = lax.broadcasted_iota(I32, (KC, Q_BLOCK), 1)
    qpos = q0 + lane

    qcat = jnp.concatenate(
        [it_ref[h * IDX_DIM:(h + 1) * IDX_DIM, :] for h in range(IDX_HEADS)], axis=1).astype(BF16)
    w_all = it_ref[IDX_HEADS * IDX_DIM:IDX_HEADS * IDX_DIM + IDX_HEADS, :] * (
        (IDX_HEADS ** -0.5) * (IDX_DIM ** -0.5))

    def score_body(c, carry):
        base = pl.multiple_of(c * KC, KC)
        kic = ki_ref[pl.ds(base, KC), :].astype(BF16)
        s = _dot(kic, qcat)
        sc = jnp.zeros((KC, Q_BLOCK), F32)
        for h in range(IDX_HEADS):
            sc = sc + jnp.maximum(s[:, h * Q_BLOCK:(h + 1) * Q_BLOCK], 0.0) * w_all[h:h + 1, :]
        bits = lax.bitcast_convert_type(sc, I32)
        key = bits ^ ((bits >> 31) & jnp.int32(0x7FFFFFFF))
        key = jnp.where(bits == jnp.int32(INT_MIN), jnp.int32(0), key)
        causal = base + row <= qpos
        keys_ref[pl.ds(base, KC), :] = jnp.where(causal, key, jnp.int32(INT_MIN))
        top = jnp.where(causal, (key >> DIGIT_SHIFTS[0]) + (1 << (DIGIT_BITS - 1)), 0)
        words_ref[pl.ds(pl.multiple_of(c * (KC // FIELDS), KC // FIELDS), KC // FIELDS), :] = _pack_fields(top)
        return carry

    lax.fori_loop(0, nch, score_body, 0)

    @pl.when(lax.rem(nch, 2) == 1)
    def _():
        wq = KC // FIELDS
        words_ref[pl.ds(pl.multiple_of(nch * wq, wq), wq), :] = _pack_fields(jnp.zeros((KC, Q_BLOCK), I32))

    n_causal = q0 + lax.broadcasted_iota(I32, (1, Q_BLOCK), 1) + 1
    tau, keep, excess = _topk_threshold(keys_ref, words_ref, nch, n_sel, n_causal)

    @pl.when(jnp.max(excess) > 0.0)
    def _():
        r = lax.broadcasted_iota(I32, (KC, KC), 0)
        cc = lax.broadcasted_iota(I32, (KC, KC), 1)
        tri = (cc <= r).astype(BF16)

        def retire(c, before):
            base = pl.multiple_of(c * KC, KC)
            blk = keys_ref[pl.ds(base, KC), :]
            tied = blk == tau
            upto = _dot(tri, jnp.where(tied, 1.0, 0.0).astype(BF16)) + before
            keys_ref[pl.ds(base, KC), :] = jnp.where(tied & (upto > keep), jnp.int32(INT_MIN), blk)
            return upto[KC - 1:KC, :]

        lax.fori_loop(0, nch, retire, jnp.zeros((1, Q_BLOCK), F32))

    q = q_ref[...]
    lane_q = lax.broadcasted_iota(I32, (Q_BLOCK, 2 * HEAD_DIM), 1)
    for p in range(ATTN_HEADS // 2):
        qp = q[:, p * 2 * HEAD_DIM:(p + 1) * 2 * HEAD_DIM]
        rhs_ref[p, 0:Q_BLOCK, :] = jnp.where(lane_q < HEAD_DIM, qp, jnp.zeros_like(qp))
        rhs_ref[p, Q_BLOCK:2 * Q_BLOCK, :] = jnp.where(lane_q >= HEAD_DIM, qp, jnp.zeros_like(qp))
    acc_ref[...] = jnp.zeros_like(acc_ref)

    def stage_a_prep(c):
        base = pl.multiple_of(c * KC, KC)
        tk = (base + row - (q0 + Q_BLOCK - 1)).astype(F32)
        tkm = jnp.where(keys_ref[pl.ds(base, KC), :] >= tau, tk, NEG_BIG)
        return tkm, k_ref[pl.ds(base, KC), :]

    def stage_a_pair(c, p, tkm, kc):
        l2 = _dot_nt(kc[:, p * 2 * HEAD_DIM:(p + 1) * 2 * HEAD_DIM], rhs_ref[p])
        mcs = []
        for hh in range(2):
            h = 2 * p + hh
            slope = 2.0 ** (-8.0 * (h + 1) / ATTN_HEADS)
            lg = l2[:, hh * Q_BLOCK:(hh + 1) * Q_BLOCK] + slope * tkm
            lg_ref[h] = lg
            mcs.append(jnp.max(_tree(lg, jnp.maximum), axis=0, keepdims=True))
        return mcs

    def stage_b_pair(c, p, m_new, alpha):
        sums = []
        for hh in range(2):
            h = 2 * p + hh
            pr = jnp.exp(lg_ref[h] - m_new[h:h + 1, :])
            sums.append(jnp.sum(_tree(pr, jnp.add), axis=0, keepdims=True))
            pv = _dot(vt_ref[c, h * HEAD_DIM:(h + 1) * HEAD_DIM, :], pr.astype(BF16))
            acc_ref[h * HEAD_DIM:(h + 1) * HEAD_DIM, :] = (
                acc_ref[h * HEAD_DIM:(h + 1) * HEAD_DIM, :] * alpha[h:h + 1, :] + pv)
        return sums

    def stage_b_all(c, m_old, l_old, mc):
        m_new = jnp.maximum(m_old, mc)
        alpha = jnp.exp(m_old - m_new)
        return m_new, alpha

    tkm0, kc0 = stage_a_prep(0)
    mc0 = jnp.concatenate(sum([stage_a_pair(0, p, tkm0, kc0) for p in range(ATTN_HEADS // 2)], []), axis=0)

    def attn_body(c, carry):
        m_old, l_old, mc_prev = carry
        m_new, alpha = stage_b_all(c - 1, m_old, l_old, mc_prev)
        tkm, kc = stage_a_prep(c)
        mcs, sums = [], []
        for p in range(ATTN_HEADS // 2):
            sums += stage_b_pair(c - 1, p, m_new, alpha)
            mcs += stage_a_pair(c, p, tkm, kc)
        return m_new, l_old * alpha + jnp.concatenate(sums, axis=0), jnp.concatenate(mcs, axis=0)

    m0 = jnp.full((ATTN_HEADS, Q_BLOCK), NEG_BIG, F32)
    l0 = jnp.zeros((ATTN_HEADS, Q_BLOCK), F32)
    m_old, l_old, mc_prev = lax.fori_loop(1, nch, attn_body, (m0, l0, mc0))
    m_new, alpha = stage_b_all(nch - 1, m_old, l_old, mc_prev)
    sums = sum([stage_b_pair(nch - 1, p, m_new, alpha) for p in range(ATTN_HEADS // 2)], [])
    l_fin = l_old * alpha + jnp.concatenate(sums, axis=0)

    outs = []
    for h in range(ATTN_HEADS):
        outs.append(acc_ref[h * HEAD_DIM:(h + 1) * HEAD_DIM, :] / l_fin[h:h + 1, :])
    o_ref[...] = jnp.concatenate(outs, axis=0).T


def _attn(it, ki, q, k, vt, B, S):
    T = B * S
    nb = S // Q_BLOCK
    n_sel = min(INDEX_TOPK_MAX, S // 4)
    return pl.pallas_call(
        functools.partial(_attn_kernel, n_sel),
        grid=(B, nb),
        in_specs=[
            pl.BlockSpec((IDXT_ROWS, Q_BLOCK), lambda b, i: (0, b * nb + i)),
            pl.BlockSpec((S, IDX_DIM), lambda b, i: (b, 0)),
            pl.BlockSpec((Q_BLOCK, ATTN_DIM), lambda b, i: (b * nb + i, 0)),
            pl.BlockSpec((S, ATTN_DIM), lambda b, i: (b, 0)),
            pl.BlockSpec((S // KEY_CHUNK, ATTN_DIM, KEY_CHUNK), lambda b, i: (b, 0, 0)),
        ],
        out_specs=pl.BlockSpec((Q_BLOCK, ATTN_DIM), lambda b, i: (b * nb + i, 0)),
        out_shape=jax.ShapeDtypeStruct((T, ATTN_DIM), F32),
        scratch_shapes=[
            pltpu.VMEM((S, Q_BLOCK), I32),
            pltpu.VMEM((-(-(S // KEY_CHUNK) // 2) * 2 * (KEY_CHUNK // FIELDS), Q_BLOCK), I32),
            pltpu.VMEM((ATTN_HEADS // 2, 2 * Q_BLOCK, 2 * HEAD_DIM), BF16),
            pltpu.VMEM((ATTN_DIM, Q_BLOCK), F32),
            pltpu.VMEM((ATTN_HEADS, KEY_CHUNK, Q_BLOCK), F32),
        ],
        compiler_params=_cparams(("arbitrary", "arbitrary")),
        name="attn",
    )(it, ki, q, k, vt)


def _mixout_kernel(tiles_per_seq, a_ref, halo_ref, y_ref, x_ref, cw_ref, wpool_ref, ps_ref,
                   wout_ref, g_ref, b_ref, o_ref, ext_ref):
    i = pl.program_id(0)
    tm = TM_MIX
    first = lax.rem(i, tiles_per_seq) == 0
    halo = jnp.where(first, 0.0, halo_ref[...])
    a = a_ref[...]
    h_c, gb_c, gc_c, p_c = (a[:, j * CONV_DIM:(j + 1) * CONV_DIM] for j in range(4))

    t0 = 2 * HALO
    n_ext = t0 + tm
    ext_ref[0:HALO, :] = jnp.zeros((HALO, CONV_DIM), F32)

    ext_ref[HALO:t0, :] = halo[:, 2 * CONV_DIM:3 * CONV_DIM] * halo[:, 0:CONV_DIM]
    u = gc_c * h_c
    ext_ref[t0:n_ext, :] = u
    cw = cw_ref[...]
    conv = cw[2:3, :] * u
    conv = conv + cw[1:2, :] * ext_ref[t0 - 1:n_ext - 1, :]
    conv = conv + cw[0:1, :] * ext_ref[t0 - 2:n_ext - 2, :]
    y_conv = gb_c * conv

    ext_ref[HALO:t0, :] = halo[:, OFF_P:OFF_P + POOL_DIM]
    ext_ref[t0:n_ext, :] = p_c
    sums = {}
    step = 1
    while step < POOL_WINDOWS[-1]:
        cur = ext_ref[HALO:n_ext, :] + ext_ref[HALO - step:n_ext - step, :]
        ext_ref[HALO:n_ext, :] = cur
        step *= 2
        sums[step] = ext_ref[t0:n_ext, :]
    tpos = (lax.rem(i, tiles_per_seq) * tm + lax.broadcasted_iota(I32, (tm, POOL_DIM), 0) + 1).astype(F32)
    grp = lax.shift_right_logical(lax.broadcasted_iota(I32, (tm, POOL_DIM), 1), _log2(POOL_GROUP))
    mean = jnp.zeros((tm, POOL_DIM), F32)
    for gi, w in enumerate(POOL_WINDOWS):
        mean = jnp.where(grp == gi, sums[w] / jnp.minimum(tpos, float(w)), mean)
    mixed = mean - p_c
    y_pool = _dot(mixed.astype(BF16), wpool_ref[...]) * ps_ref[...]

    mix = _dot(y_conv.astype(BF16), wout_ref[0:CONV_DIM, :])
    mix = mix + _dot(y_pool.astype(BF16), wout_ref[CONV_DIM:CONV_DIM + POOL_DIM, :])
    mix = mix + _dot(y_ref[...].astype(BF16), wout_ref[CONV_DIM + POOL_DIM:D_MODEL, :])
    z = DN_ALPHA * x_ref[...] + mix
    o_ref[...] = _layernorm_rows(z, g_ref[...], b_ref[...])


def _mixout(a, y_attn, x, cw, wpool_bd, ps, wout, g, b, S):
    T = a.shape[0]
    nt = T // TM_MIX
    tps = S // TM_MIX
    hb = TM_MIX // HALO
    full = lambda i: (0, 0)
    return pl.pallas_call(
        functools.partial(_mixout_kernel, tps),
        grid=(nt,),
        in_specs=[
            pl.BlockSpec((TM_MIX, OFF_Q), lambda i: (i, 0)),
            pl.BlockSpec((HALO, OFF_Q), lambda i: (jnp.maximum(i * hb - 1, 0), 0)),
            pl.BlockSpec((TM_MIX, ATTN_DIM), lambda i: (i, 0)),
            pl.BlockSpec((TM_MIX, D_MODEL), lambda i: (i, 0)),
            pl.BlockSpec((CONV_WIDTH, CONV_DIM), full),
            pl.BlockSpec((POOL_DIM, POOL_DIM), full),
            pl.BlockSpec((1, POOL_DIM), full),
            pl.BlockSpec((D_MODEL, D_MODEL), full),
            pl.BlockSpec((1, D_MODEL), full),
            pl.BlockSpec((1, D_MODEL), full),
        ],
        out_specs=pl.BlockSpec((TM_MIX, D_MODEL), lambda i: (i, 0)),
        out_shape=jax.ShapeDtypeStruct((T, D_MODEL), F32),
        scratch_shapes=[pltpu.VMEM((2 * HALO + TM_MIX, CONV_DIM), F32)],
        compiler_params=_cparams(("arbitrary",)),
        name="mixout",
    )(a, a, y_attn, x, cw, wpool_bd, ps, wout, g, b)


def _router_kernel(x_ref, rw_ref, rb_ref, idx_ref, gate_ref, cnt_ref):
    i = pl.program_id(0)
    tm = TM_ROUTE
    logits = lax.dot_general(rw_ref[...], x_ref[...], (((1,), (1,)), ((), ())),
                             precision=lax.Precision.HIGHEST,
                             preferred_element_type=F32) + rb_ref[...]
    erow = lax.broadcasted_iota(I32, (N_EXPERTS, tm), 0).astype(F32)
    work = logits
    vals, idxs = [], []
    multi = jnp.zeros((N_EXPERTS, tm), F32)
    for _ in range(TOP_K):
        mx = jnp.max(work, axis=0, keepdims=True)
        pick = jnp.min(jnp.where(work == mx, erow, float(N_EXPERTS)), axis=0, keepdims=True)
        hit = erow == pick
        work = jnp.where(hit, -jnp.inf, work)
        multi = multi + hit.astype(F32)
        vals.append(mx)
        idxs.append(pick)
    es = [jnp.exp(v - vals[0]) for v in vals]
    den = es[0] + es[1] + es[2] + es[3]
    idx_ref[...] = jnp.concatenate(idxs, axis=0).astype(I32)
    gate_ref[...] = jnp.concatenate([e / den for e in es] + [jnp.zeros((SUBLANES - TOP_K, tm), F32)], axis=0)
    part = multi[:, 0:LANES]
    for j in range(1, tm // LANES):
        part = part + multi[:, j * LANES:(j + 1) * LANES]

    @pl.when(i == 0)
    def _():
        cnt_ref[...] = jnp.zeros_like(cnt_ref)

    cnt_ref[...] += part


def _router(x1, rwT, rb):
    T = x1.shape[0]
    nt = T // TM_ROUTE
    full = lambda i: (0, 0)
    return pl.pallas_call(
        _router_kernel,
        grid=(nt,),
        in_specs=[
            pl.BlockSpec((TM_ROUTE, D_MODEL), lambda i: (i, 0)),
            pl.BlockSpec((N_EXPERTS, D_MODEL), full),
            pl.BlockSpec((N_EXPERTS, 1), full),
        ],
        out_specs=[
            pl.BlockSpec((TOP_K, TM_ROUTE), lambda i: (0, i)),
            pl.BlockSpec((SUBLANES, TM_ROUTE), lambda i: (0, i)),
            pl.BlockSpec((N_EXPERTS, LANES), full),
        ],
        out_shape=[
            jax.ShapeDtypeStruct((TOP_K, T), I32),
            jax.ShapeDtypeStruct((SUBLANES, T), F32),
            jax.ShapeDtypeStruct((N_EXPERTS, LANES), F32),
        ],
        compiler_params=_cparams(("arbitrary",)),
        name="router",
    )(x1, rwT, rb)


def _ranks_kernel(n_tiles_pad, idx_ref, cnt_ref, dest_ref, meta_ref, tri_ref, start_ref, carry_ref):
    i = pl.program_id(0)
    tm = TM_ROUTE
    erow = lax.broadcasted_iota(I32, (N_EXPERTS, LANES), 0)
    elane = lax.broadcasted_iota(I32, (N_EXPERTS, LANES), 1)

    @pl.when(i == 0)
    def _():
        cnt = jnp.sum(cnt_ref[...], axis=1, keepdims=True)
        cnt_i = jnp.broadcast_to(cnt, (N_EXPERTS, LANES)).astype(I32)
        padded = lax.shift_left(lax.shift_right_logical(cnt_i + (TM_MOE - 1), _log2(TM_MOE)), _log2(TM_MOE))
        r = lax.broadcasted_iota(I32, (N_EXPERTS, N_EXPERTS), 0)
        c = lax.broadcasted_iota(I32, (N_EXPERTS, N_EXPERTS), 1)
        low = (c <= r).astype(F32)
        pad_end = lax.dot_general(low, padded.astype(F32), (((1,), (0,)), ((), ())),
                                  precision=lax.Precision.HIGHEST, preferred_element_type=F32)
        pad_start = pad_end - padded.astype(F32)
        start_ref[...] = pad_start
        carry_ref[...] = jnp.zeros_like(carry_ref)
        a = lax.broadcasted_iota(I32, (tm, tm), 0)
        bcol = lax.broadcasted_iota(I32, (tm, tm), 1)
        tri_ref[...] = (a < bcol).astype(BF16)
        ntp = n_tiles_pad
        tstart = (lax.broadcasted_iota(I32, (N_EXPERTS, ntp), 1) * TM_MOE).astype(F32)
        pe = jnp.concatenate([pad_end] * (ntp // LANES), axis=1)
        texp = jnp.sum((pe <= tstart).astype(F32), axis=0, keepdims=True)
        texp = jnp.minimum(texp, float(N_EXPERTS - 1)).astype(I32)
        total = jnp.max(pad_end, axis=0, keepdims=True)
        n_used = lax.shift_right_logical(total.astype(I32), _log2(TM_MOE))
        zstart = jnp.sum(jnp.where(erow == elane, pad_start + cnt_i.astype(F32), 0.0),
                         axis=0, keepdims=True).astype(I32)
        zlen = jnp.sum(jnp.where(erow == elane, (padded - cnt_i).astype(F32), 0.0),
                       axis=0, keepdims=True).astype(I32)
        lanes_pad = jnp.zeros((1, ntp - LANES), I32)
        meta_ref[...] = jnp.concatenate(
            [texp,
             jnp.concatenate([n_used, lanes_pad], axis=1),
             jnp.concatenate([zstart, lanes_pad], axis=1),
             jnp.concatenate([zlen, lanes_pad], axis=1),
             jnp.zeros((SUBLANES - 4, ntp), I32)], axis=0)

    idx = idx_ref[...]
    erow_t = lax.broadcasted_iota(I32, (N_EXPERTS, tm), 0)
    hits = [erow_t == idx[k:k + 1, :] for k in range(TOP_K)]
    multi = hits[0].astype(F32)
    for k in range(1, TOP_K):
        multi = multi + hits[k].astype(F32)
    prefix = _dot(multi.astype(BF16), tri_ref[...])
    base = jnp.concatenate([carry_ref[...] + start_ref[...]] * (tm // LANES), axis=1)
    tot = prefix + base
    dest_ref[...] = jnp.concatenate(
        [jnp.sum(jnp.where(hits[k], tot, 0.0), axis=0, keepdims=True) for k in range(TOP_K)],
        axis=0).astype(I32)
    carry_ref[...] += jnp.broadcast_to(jnp.sum(multi, axis=1, keepdims=True), (N_EXPERTS, LANES))


def _ranks(idxT, cnt, n_tiles_pad):
    T = idxT.shape[1]
    nt = T // TM_ROUTE
    full = lambda i: (0, 0)
    return pl.pallas_call(
        functools.partial(_ranks_kernel, n_tiles_pad),
        grid=(nt,),
        in_specs=[
            pl.BlockSpec((TOP_K, TM_ROUTE), lambda i: (0, i)),
            pl.BlockSpec((N_EXPERTS, LANES), full),
        ],
        out_specs=[
            pl.BlockSpec((TOP_K, TM_ROUTE), lambda i: (0, i)),
            pl.BlockSpec((SUBLANES, n_tiles_pad), full),
        ],
        out_shape=[
            jax.ShapeDtypeStruct((TOP_K, T), I32),
            jax.ShapeDtypeStruct((SUBLANES, n_tiles_pad), I32),
        ],
        scratch_shapes=[
            pltpu.VMEM((TM_ROUTE, TM_ROUTE), BF16),
            pltpu.VMEM((N_EXPERTS, LANES), F32),
            pltpu.VMEM((N_EXPERTS, LANES), F32),
        ],
        compiler_params=_cparams(("arbitrary",)),
        name="ranks",
    )(idxT, cnt)


RT = D_MODEL // LANES
DMA_UNROLL = 8


def _row_copy(src_ref, s, dst_ref, d, sem):
    return pltpu.make_async_copy(src_ref.at[pl.ds(pl.multiple_of(s * RT, RT), RT)],
                                 dst_ref.at[pl.ds(pl.multiple_of(d * RT, RT), RT)], sem)


def _to_row_tiled(dst_ref, val, rows):
    for s in range(RT):
        dst_ref[pl.ds(s, rows, stride=RT), :] = val[:, s * LANES:(s + 1) * LANES]


def _from_row_tiled(src_ref, rows):
    return [src_ref[pl.ds(s, rows, stride=RT), :] for s in range(RT)]


def _scatter_kernel(meta_ref, x_ref, dest_ref, xs_ref, stage_ref, zero_ref, sem):
    i = pl.program_id(0)
    tm = TM_ROUTE

    @pl.when(i == 0)
    def _():
        zero_ref[...] = jnp.zeros_like(zero_ref)

        def zcopy(row, nrows):
            off = pl.multiple_of(row * RT, RT)
            return pltpu.make_async_copy(zero_ref.at[pl.ds(0, nrows * RT)],
                                         xs_ref.at[pl.ds(off, nrows * RT)], sem)

        def pad_fill(wait):
            def body(e, c):
                row = meta_ref[2, e]
                plen = meta_ref[3, e]
                for bit in reversed(range(_log2(TM_MOE))):
                    size = 1 << bit
                    has = (plen & size) != 0

                    @pl.when(has)
                    def _():
                        cp = zcopy(row, size)
                        cp.wait() if wait else cp.start()

                    row = row + jnp.where(has, size, 0)
                return c
            lax.fori_loop(0, N_EXPERTS, body, 0)

        def tail_fill(wait):
            def body(j, c):
                cp = zcopy(j * TM_MOE, TM_MOE)
                cp.wait() if wait else cp.start()
                return c
            lax.fori_loop(meta_ref[1, 0], xs_ref.shape[0] // (TM_MOE * RT), body, 0)

        pad_fill(False)
        tail_fill(False)
        pad_fill(True)
        tail_fill(True)

    _to_row_tiled(stage_ref, x_ref[...], tm)

    def start(t, c):
        for k in range(TOP_K):
            _row_copy(stage_ref, t, xs_ref, dest_ref[k, t], sem).start(priority=k % 2)
        return c

    lax.fori_loop(0, tm, start, 0, unroll=DMA_UNROLL)
    for k in range(TOP_K):
        pltpu.make_async_copy(stage_ref, xs_ref.at[pl.ds(0, tm * RT)], sem).wait()


def _scatter(meta, x1, destT, n_rows_alloc):
    T = x1.shape[0]
    nt = T // TM_ROUTE
    return pl.pallas_call(
        _scatter_kernel,
        grid_spec=pltpu.PrefetchScalarGridSpec(
            num_scalar_prefetch=1,
            grid=(nt,),
            in_specs=[
                pl.BlockSpec((TM_ROUTE, D_MODEL), lambda i, m: (i, 0)),
                pl.BlockSpec((TOP_K, TM_ROUTE), lambda i, m: (0, i), memory_space=pltpu.SMEM),
            ],
            out_specs=pl.BlockSpec(memory_space=pl.ANY),
            scratch_shapes=[
                pltpu.VMEM((TM_ROUTE * RT, LANES), F32),
                pltpu.VMEM((TM_MOE * RT, LANES), F32),
                pltpu.SemaphoreType.DMA(()),
            ],
        ),
        out_shape=jax.ShapeDtypeStruct((n_rows_alloc * RT, LANES), F32),
        compiler_params=_cparams(("arbitrary",)),
        name="scatter",
    )(meta, x1, destT)


def _gmm_kernel(meta_ref, xs_ref, wgu_ref, bgu_ref, wd_ref, bd_ref, ys_ref, wgu_b, wd_b, lhs_ref, act_ref):
    j = pl.program_id(0)
    n_used = meta_ref[1, 0]
    e_now = meta_ref[0, j]
    e_prev = meta_ref[0, jnp.maximum(j - 1, 0)]
    used = j < n_used

    @pl.when(used & ((j == 0) | (e_now != e_prev)))
    def _():
        wgu_b[...] = wgu_ref[0, 0].astype(BF16)
        wd_b[...] = wd_ref[0, 0].astype(BF16)

    @pl.when(used)
    def _():
        for s, piece in enumerate(_from_row_tiled(xs_ref, TM_MOE)):
            lhs_ref[:, s * LANES:(s + 1) * LANES] = piece.astype(BF16)
        x = lhs_ref[...]
        nc = 256
        for c in range(D_FF // nc):
            gate = _dot(x, wgu_b[:, c * nc:(c + 1) * nc]) + bgu_ref[0, 0, :, c * nc:(c + 1) * nc]
            up = (_dot(x, wgu_b[:, D_FF + c * nc:D_FF + (c + 1) * nc])
                  + bgu_ref[0, 0, :, D_FF + c * nc:D_FF + (c + 1) * nc])
            gate = jnp.minimum(gate, SWIGLU_LIMIT)
            up = jnp.clip(up, -SWIGLU_LIMIT, SWIGLU_LIMIT)
            act = gate * jax.nn.sigmoid(SWIGLU_ALPHA * gate) * (up + 1.0)
            act_ref[:, c * nc:(c + 1) * nc] = act.astype(BF16)
        _to_row_tiled(ys_ref, _dot(act_ref[...], wd_b[...]) + bd_ref[0, 0], TM_MOE)

    @pl.when(jnp.logical_not(used))
    def _():
        ys_ref[...] = jnp.zeros_like(ys_ref)


def _gmm(meta, xs, w_gu, b_gu, w_down, b_down, n_tiles, layer):
    last = lambda m: jnp.maximum(m[1, 0] - 1, 0)
    return pl.pallas_call(
        _gmm_kernel,
        grid_spec=pltpu.PrefetchScalarGridSpec(
            num_scalar_prefetch=1,
            grid=(n_tiles,),
            in_specs=[
                pl.BlockSpec((TM_MOE * RT, LANES), lambda j, m: (jnp.minimum(j, last(m)), 0)),
                pl.BlockSpec((1, 1, D_MODEL, 2 * D_FF), lambda j, m: (layer, m[0, j], 0, 0)),
                pl.BlockSpec((1, 1, 1, 2 * D_FF), lambda j, m: (layer, m[0, j], 0, 0)),
                pl.BlockSpec((1, 1, D_FF, D_MODEL), lambda j, m: (layer, m[0, j], 0, 0)),
                pl.BlockSpec((1, 1, 1, D_MODEL), lambda j, m: (layer, m[0, j], 0, 0)),
            ],
            out_specs=pl.BlockSpec((TM_MOE * RT, LANES), lambda j, m: (j, 0)),
            scratch_shapes=[
                pltpu.VMEM((D_MODEL, 2 * D_FF), BF16),
                pltpu.VMEM((D_FF, D_MODEL), BF16),
                pltpu.VMEM((TM_MOE, D_MODEL), BF16),
                pltpu.VMEM((TM_MOE, D_FF), BF16),
            ],
        ),
        out_shape=jax.ShapeDtypeStruct((n_tiles * TM_MOE * RT, LANES), F32),
        compiler_params=_cparams(("arbitrary",)),
        name="gmm",
    )(meta, xs, w_gu, b_gu, w_down, b_down)


def _combine_kernel(x_ref, dest_ref, gate_ref, ys_ref, g_ref, b_ref, o_ref, ob_ref, buf_ref, sem):
    tm = TM_COMB

    def start(t, c):
        for k in range(TOP_K):
            _row_copy(ys_ref, dest_ref[k, t], buf_ref.at[k], t, sem).start(priority=k % 2)
        return c

    lax.fori_loop(0, tm, start, 0, unroll=DMA_UNROLL)
    for k in range(TOP_K):
        pltpu.make_async_copy(ys_ref.at[pl.ds(0, tm * RT)], buf_ref.at[k], sem).wait()
    gates = jnp.concatenate(
        [gate_ref[...], jnp.zeros((LANES - SUBLANES, tm), F32)], axis=0).T
    pieces = None
    for k in range(TOP_K):
        gk = gates[:, k:k + 1]
        rows = [p * gk for p in _from_row_tiled(buf_ref.at[k], tm)]
        pieces = rows if pieces is None else [a + r for a, r in zip(pieces, rows)]
    z = DN_ALPHA * x_ref[...] + jnp.concatenate(pieces, axis=1)
    out = _layernorm_rows(z, g_ref[...], b_ref[...])
    o_ref[...] = out
    ob_ref[...] = out.astype(BF16)


def _combine(x1, destT, gateT, ys, g, b):
    T = x1.shape[0]
    nt = T // TM_COMB
    full = lambda i: (0, 0)
    return pl.pallas_call(
        _combine_kernel,
        grid=(nt,),
        in_specs=[
            pl.BlockSpec((TM_COMB, D_MODEL), lambda i: (i, 0)),
            pl.BlockSpec((TOP_K, TM_COMB), lambda i: (0, i), memory_space=pltpu.SMEM),
            pl.BlockSpec((SUBLANES, TM_COMB), lambda i: (0, i)),
            pl.BlockSpec(memory_space=pl.ANY),
            pl.BlockSpec((1, D_MODEL), full),
            pl.BlockSpec((1, D_MODEL), full),
        ],
        out_specs=[
            pl.BlockSpec((TM_COMB, D_MODEL), lambda i: (i, 0)),
            pl.BlockSpec((TM_COMB, D_MODEL), lambda i: (i, 0)),
        ],
        out_shape=[
            jax.ShapeDtypeStruct((T, D_MODEL), F32),
            jax.ShapeDtypeStruct((T, D_MODEL), BF16),
        ],
        scratch_shapes=[
            pltpu.VMEM((TOP_K, TM_COMB * RT, LANES), F32),
            pltpu.SemaphoreType.DMA(()),
        ],
        compiler_params=_cparams(("arbitrary",)),
        name="combine",
    )(x1, destT, gateT, ys, g, b)


def _block_diag(w):
    g, c, _ = w.shape
    out = jnp.zeros((g * c, g * c), w.dtype)
    for i in range(g):
        out = out.at[i * c:(i + 1) * c, i * c:(i + 1) * c].set(w[i])
    return out


def kernel(x, w_in, conv_w, w_pool, pool_scale, idx_kn_g, idx_kn_b, w_out, ln1_g, ln1_b, router_w,
           router_b, w_gu, b_gu, w_down, b_down, ln2_g, ln2_b):
    B, S, D = x.shape
    T = B * S
    depth = w_in.shape[0]
    n_pairs = T * TOP_K
    n_tiles = (n_pairs + N_EXPERTS * (TM_MOE - 1)) // TM_MOE + 1
    n_tiles_pad = ((n_tiles + LANES - 1) // LANES) * LANES
    n_rows_alloc = (n_tiles + 1) * TM_MOE

    xf = x.reshape(T, D)
    xb = xf.astype(BF16)
    for l in range(depth):
        wl = w_in[l]
        wn = jnp.concatenate(
            [wl[:, 0:OFF_V], wl[:, OFF_KI:OFF_WI], jnp.zeros((D, LANES - IDX_DIM), F32)], axis=1).astype(BF16)
        wt = jnp.concatenate(
            [wl[:, OFF_V:OFF_QI], wl[:, OFF_QI:OFF_KI], wl[:, OFF_WI:D_IN],
             jnp.zeros((D, IDXT_ROWS - IDX_HEADS * IDX_DIM - IDX_HEADS), F32)], axis=1).T.astype(BF16)
        a, q, k, ki, vt, it = _inproj(xb, wn, wt, idx_kn_g[l].reshape(1, -1), idx_kn_b[l].reshape(1, -1))
        y_attn = _attn(it, ki, q, k, vt, B, S)
        x1 = _mixout(a, y_attn, xf, conv_w[l], _block_diag(w_pool[l]).astype(BF16),
                     pool_scale[l].reshape(1, -1), w_out[l].astype(BF16),
                     ln1_g[l].reshape(1, -1), ln1_b[l].reshape(1, -1), S)
        idxT, gateT, cnt = _router(x1, router_w[l].T, router_b[l].reshape(-1, 1))
        destT, meta = _ranks(idxT, cnt, n_tiles_pad)
        xs = _scatter(meta, x1, destT, n_rows_alloc)
        ys = _gmm(meta, xs, w_gu, b_gu.reshape(depth, N_EXPERTS, 1, -1), w_down,
                  b_down.reshape(depth, N_EXPERTS, 1, -1), n_tiles, l)
        xf, xb = _combine(x1, destT, gateT, ys, ln2_g[l].reshape(1, -1), ln2_b[l].reshape(1, -1))
    return xf.reshape(B, S, D)
```

```python
import functools

import jax
import jax.numpy as jnp
import numpy as np
from jax import lax
from jax.experimental import pallas as pl
from jax.experimental.pallas import tpu as pltpu

F32 = jnp.float32
BF16 = jnp.bfloat16
I32 = jnp.int32

D_MODEL = 1024
CONV_DIM = 256
CONV_WIDTH = 3
POOL_WINDOWS = (2, 4, 8, 16)
POOL_GROUP = 64
POOL_DIM = 256
ATTN_HEADS = 8
HEAD_DIM = 64
ATTN_DIM = 512
IDX_HEADS = 8
IDX_DIM = 32
INDEX_TOPK_MAX = 256
Q_BLOCK = 128
N_EXPERTS = 32
TOP_K = 4
D_FF = 1024
SWIGLU_ALPHA = 1.702
SWIGLU_LIMIT = 7.0
LN_EPS = 1e-5
DEPTH = 2
DN_ALPHA = (2.0 * DEPTH) ** 0.25

OFF_P = 3 * CONV_DIM
OFF_Q = OFF_P + POOL_DIM
OFF_K = OFF_Q + ATTN_DIM
OFF_V = OFF_K + ATTN_DIM
OFF_QI = OFF_V + ATTN_DIM
OFF_KI = OFF_QI + IDX_HEADS * IDX_DIM
OFF_WI = OFF_KI + IDX_DIM
D_IN = OFF_WI + IDX_HEADS

LANES = 128
SUBLANES = 8
VMEM_LIMIT = 56 * 1024 * 1024
KEY_CHUNK = 512
TIE_GROUP = 128
HALO = 16
TM_PROJ = 512
TM_MIX = 512
TM_ROUTE = 512
TM_MOE = 512
TM_COMB = 512
IDXT_ROWS = IDX_HEADS * IDX_DIM + 16
WT_ROWS = ATTN_DIM + IDXT_ROWS
INT_MIN = -(2 ** 31)
NEG_BIG = -1e30


def _cparams(sem, flags=None):
    return pltpu.CompilerParams(dimension_semantics=sem, vmem_limit_bytes=VMEM_LIMIT, flags=flags)


def _log2(n):
    k = n.bit_length() - 1
    assert 1 << k == n
    return k


def _dot(a, b):
    return jnp.dot(a, b, preferred_element_type=F32)


def _dot_nt(a, b):
    return lax.dot_general(a, b, (((1,), (1,)), ((), ())), preferred_element_type=F32)


def _tree(x, op):
    parts = [x[j * SUBLANES:(j + 1) * SUBLANES, :] for j in range(x.shape[0] // SUBLANES)]
    while len(parts) > 1:
        nxt = [op(parts[j], parts[j + 1]) for j in range(0, len(parts) - 1, 2)]
        if len(parts) % 2:
            nxt.append(parts[-1])
        parts = nxt
    return parts[0]


def _layernorm_rows(z, g, b):
    mu = jnp.mean(z, axis=-1, keepdims=True)
    d = z - mu
    var = jnp.mean(d * d, axis=-1, keepdims=True)
    return d * lax.rsqrt(var + LN_EPS) * g + b


def _inproj_kernel(x_ref, wn_ref, wt_ref, kg_ref, kb_ref,
                   a_ref, q_ref, k_ref, ki_ref, vt_ref, it_ref):
    x = x_ref[...].astype(BF16)
    a_ref[...] = _dot(x, wn_ref[:, 0:OFF_Q])
    q_ref[...] = (_dot(x, wn_ref[:, OFF_Q:OFF_K]) * (HEAD_DIM ** -0.5)).astype(BF16)
    k_ref[...] = _dot(x, wn_ref[:, OFF_K:OFF_V]).astype(BF16)
    ki = _dot(x, wn_ref[:, OFF_V:OFF_V + LANES])[:, 0:IDX_DIM]
    ki_ref[...] = _layernorm_rows(ki, kg_ref[...], kb_ref[...])
    t = _dot_nt(wt_ref[...], x)
    for c in range(TM_PROJ // KEY_CHUNK):
        vt_ref[c] = t[0:ATTN_DIM, c * KEY_CHUNK:(c + 1) * KEY_CHUNK].astype(BF16)
    it_ref[...] = t[ATTN_DIM:WT_ROWS, :]


def _inproj(xb, wn, wt, kg, kb):
    T = xb.shape[0]
    nt = T // TM_PROJ
    cpt = TM_PROJ // KEY_CHUNK
    full = lambda i: (0, 0)
    return pl.pallas_call(
        _inproj_kernel,
        grid=(nt,),
        in_specs=[
            pl.BlockSpec((TM_PROJ, D_MODEL), lambda i: (i, 0)),
            pl.BlockSpec(wn.shape, full),
            pl.BlockSpec(wt.shape, full),
            pl.BlockSpec((1, IDX_DIM), full),
            pl.BlockSpec((1, IDX_DIM), full),
        ],
        out_specs=[
            pl.BlockSpec((TM_PROJ, OFF_Q), lambda i: (i, 0)),
            pl.BlockSpec((TM_PROJ, ATTN_DIM), lambda i: (i, 0)),
            pl.BlockSpec((TM_PROJ, ATTN_DIM), lambda i: (i, 0)),
            pl.BlockSpec((TM_PROJ, IDX_DIM), lambda i: (i, 0)),
            pl.BlockSpec((cpt, ATTN_DIM, KEY_CHUNK), lambda i: (i, 0, 0)),
            pl.BlockSpec((IDXT_ROWS, TM_PROJ), lambda i: (0, i)),
        ],
        out_shape=[
            jax.ShapeDtypeStruct((T, OFF_Q), F32),
            jax.ShapeDtypeStruct((T, ATTN_DIM), BF16),
            jax.ShapeDtypeStruct((T, ATTN_DIM), BF16),
            jax.ShapeDtypeStruct((T, IDX_DIM), F32),
            jax.ShapeDtypeStruct((T // KEY_CHUNK, ATTN_DIM, KEY_CHUNK), BF16),
            jax.ShapeDtypeStruct((IDXT_ROWS, T), F32),
        ],
        compiler_params=_cparams(("arbitrary",)),
        name="inproj",
    )(xb, wn, wt, kg, kb)


FIELD_BITS = 8
DIGIT_BITS = FIELD_BITS - 1
FIELDS = 32 // FIELD_BITS
DIGIT_SHIFTS = (25, 18, 11, 4, 0)
FIELD_ONES = 0x01010101
FIELD_GUARDS = 0x80808080 - (1 << 32)


def _pack_fields(f):
    q = f.shape[0] // FIELDS
    w = f[0:q]
    for j in range(1, FIELDS):
        w = w | lax.shift_left(f[j * q:(j + 1) * q], FIELD_BITS * j)
    return w | jnp.int32(FIELD_GUARDS)


def _count_fields_ge(words_ref, nch, cand):
    wq = 2 * (KEY_CHUNK // FIELDS)
    assert words_ref.shape[0] // SUBLANES < (1 << FIELD_BITS)
    cvec = cand * jnp.int32(FIELD_ONES)

    def body(c, acc):
        base = pl.multiple_of(c * wq, wq)
        w = words_ref[pl.ds(base, wq), :]
        hit = lax.shift_right_logical(w - cvec, DIGIT_BITS) & jnp.int32(FIELD_ONES)
        return acc + _tree(hit, jnp.add)

    acc = lax.fori_loop(0, lax.shift_right_logical(nch + 1, 1), body,
                        jnp.zeros((SUBLANES, words_ref.shape[1]), I32))
    tot = acc & 255
    for j in range(1, FIELDS):
        tot = tot + (lax.shift_right_logical(acc, FIELD_BITS * j) & 255)
    return jnp.sum(tot.astype(F32), axis=0, keepdims=True)


def _topk_threshold(keys_ref, words_ref, nch, n_sel, n_causal):
    lanes = keys_ref.shape[1]
    wq = KEY_CHUNK // FIELDS
    want = jnp.full((1, lanes), float(n_sel), F32)
    bucket = n_causal.astype(F32)
    prefix = None
    for idx, shift in enumerate(DIGIT_SHIFTS):
        width = (DIGIT_SHIFTS[idx - 1] if idx else 32) - shift
        if idx:
            def prep(c, carry, shift=shift, width=width, prefix=prefix):
                key = keys_ref[pl.ds(pl.multiple_of(c * KEY_CHUNK, KEY_CHUNK), KEY_CHUNK), :]
                digit = (key >> shift) & jnp.int32((1 << width) - 1)
                fields = jnp.where((key >> (shift + width)) == prefix, digit, 0)
                words_ref[pl.ds(pl.multiple_of(c * wq, wq), wq), :] = _pack_fields(fields)
                return carry

            lax.fori_loop(0, nch, prep, 0)
        dig = jnp.zeros((1, lanes), I32)
        at_dig = bucket
        above = jnp.zeros((1, lanes), F32)
        for bit in reversed(range(width)):
            cand = dig + (1 << bit)
            cnt = _count_fields_ge(words_ref, nch, cand)
            ok = cnt >= want
            dig = jnp.where(ok, cand, dig)
            at_dig = jnp.where(ok, cnt, at_dig)
            above = jnp.where(ok, above, cnt)
        want = want - above
        bucket = at_dig - above
        if idx:
            prefix = lax.shift_left(prefix, width) | dig
        else:
            prefix = dig - (1 << (DIGIT_BITS - 1))
    all_taken = n_causal <= n_sel
    tau = jnp.where(all_taken, jnp.int32(INT_MIN + 1), prefix)
    return tau, want, jnp.where(all_taken, 0.0, bucket - want)


def _attn_kernel(n_sel, it_ref, ki_ref, q_ref, k_ref, vt_ref, o_ref,
                 keys_ref, words_ref, rhs_ref, acc_ref, lg_ref):
    i = pl.program_id(1)
    nch = lax.shift_right_logical(i * Q_BLOCK + Q_BLOCK + KEY_CHUNK - 1, _log2(KEY_CHUNK))
    q0 = i * Q_BLOCK
    KC = KEY_CHUNK
    row = lax.broadcasted_iota(I32, (KC, Q_BLOCK), 0)
    lane = lax.broadcasted_iota(I32, (KC, Q_BLOCK), 1)
    qpos = q0 + lane

    qcat = jnp.concatenate(
        [it_ref[h * IDX_DIM:(h + 1) * IDX_DIM, :] for h in range(IDX_HEADS)], axis=1).astype(BF16)
    w_all = it_ref[IDX_HEADS * IDX_DIM:IDX_HEADS * IDX_DIM + IDX_HEADS, :] * (
        (IDX_HEADS ** -0.5) * (IDX_DIM ** -0.5))

    def score_body(c, carry):
        base = pl.multiple_of(c * KC, KC)
        kic = ki_ref[pl.ds(base, KC), :].astype(BF16)
        s = _dot(kic, qcat)
        sc = jnp.zeros((KC, Q_BLOCK), F32)
        for h in range(IDX_HEADS):
            sc = sc + jnp.maximum(s[:, h * Q_BLOCK:(h + 1) * Q_BLOCK], 0.0) * w_all[h:h + 1, :]
        bits = lax.bitcast_convert_type(sc, I32)
        key = bits ^ ((bits >> 31) & jnp.int32(0x7FFFFFFF))
        key = jnp.where(bits == jnp.int32(INT_MIN), jnp.int32(0), key)
        causal = base + row <= qpos
        keys_ref[pl.ds(base, KC), :] = jnp.where(causal, key, jnp.int32(INT_MIN))
        top = jnp.where(causal, (key >> DIGIT_SHIFTS[0]) + (1 << (DIGIT_BITS - 1)), 0)
        words_ref[pl.ds(pl.multiple_of(c * (KC // FIELDS), KC // FIELDS), KC // FIELDS), :] = _pack_fields(top)
        return carry

    lax.fori_loop(0, nch, score_body, 0)

    @pl.when(lax.rem(nch, 2) == 1)
    def _():
        wq = KC // FIELDS
        words_ref[pl.ds(pl.multiple_of(nch * wq, wq), wq), :] = _pack_fields(jnp.zeros((KC, Q_BLOCK), I32))

    n_causal = q0 + lax.broadcasted_iota(I32, (1, Q_BLOCK), 1) + 1
    tau, keep, excess = _topk_threshold(keys_ref, words_ref, nch, n_sel, n_causal)

    @pl.when(jnp.max(excess) > 0.0)
    def _():
        g = TIE_GROUP
        r = lax.broadcasted_iota(I32, (g, g), 0)
        cc = lax.broadcasted_iota(I32, (g, g), 1)
        tri = (cc <= r).astype(BF16)

        def retire(c, before):
            base = pl.multiple_of(c * KC, KC)
            blk = keys_ref[pl.ds(base, KC), :]
            tied = blk == tau
            ones = jnp.where(tied, 1.0, 0.0).astype(BF16)
            local = [_dot(tri, ones[j * g:(j + 1) * g, :]) for j in range(KC // g)]
            upto = []
            for lj in local:
                upto.append(lj + before)
                before = before + lj[g - 1:g, :]
            upto = jnp.concatenate(upto, axis=0)
            keys_ref[pl.ds(base, KC), :] = jnp.where(tied & (upto > keep), jnp.int32(INT_MIN), blk)
            return before

        lax.fori_loop(0, nch, retire, jnp.zeros((1, Q_BLOCK), F32))

    q = q_ref[...]
    lane_q = lax.broadcasted_iota(I32, (Q_BLOCK, 2 * HEAD_DIM), 1)
    for p in range(ATTN_HEADS // 2):
        qp = q[:, p * 2 * HEAD_DIM:(p + 1) * 2 * HEAD_DIM]
        rhs_ref[p, 0:Q_BLOCK, :] = jnp.where(lane_q < HEAD_DIM, qp, jnp.zeros_like(qp))
        rhs_ref[p, Q_BLOCK:2 * Q_BLOCK, :] = jnp.where(lane_q >= HEAD_DIM, qp, jnp.zeros_like(qp))
    acc_ref[...] = jnp.zeros_like(acc_ref)

    def stage_a_prep(c):
        base = pl.multiple_of(c * KC, KC)
        tk = (base + row - (q0 + Q_BLOCK - 1)).astype(F32)
        tkm = jnp.where(keys_ref[pl.ds(base, KC), :] >= tau, tk, NEG_BIG)
        return tkm, k_ref[pl.ds(base, KC), :]

    def stage_a_pair(c, p, tkm, kc):
        l2 = _dot_nt(kc[:, p * 2 * HEAD_DIM:(p + 1) * 2 * HEAD_DIM], rhs_ref[p])
        mcs = []
        for hh in range(2):
            h = 2 * p + hh
            slope = 2.0 ** (-8.0 * (h + 1) / ATTN_HEADS)
            lg = l2[:, hh * Q_BLOCK:(hh + 1) * Q_BLOCK] + slope * tkm
            lg_ref[h] = lg
            mcs.append(jnp.max(_tree(lg, jnp.maximum), axis=0, keepdims=True))
        return mcs

    def stage_b_pair(c, p, m_new, alpha):
        sums = []
        for hh in range(2):
            h = 2 * p + hh
            pr = jnp.exp(lg_ref[h] - m_new[h:h + 1, :])
            sums.append(jnp.sum(_tree(pr, jnp.add), axis=0, keepdims=True))
            pv = _dot(vt_ref[c, h * HEAD_DIM:(h + 1) * HEAD_DIM, :], pr.astype(BF16))
            acc_ref[h * HEAD_DIM:(h + 1) * HEAD_DIM, :] = (
                acc_ref[h * HEAD_DIM:(h + 1) * HEAD_DIM, :] * alpha[h:h + 1, :] + pv)
        return sums

    def stage_b_all(c, m_old, l_old, mc):
        m_new = jnp.maximum(m_old, mc)
        alpha = jnp.exp(m_old - m_new)
        return m_new, alpha

    tkm0, kc0 = stage_a_prep(0)
    mc0 = jnp.concatenate(sum([stage_a_pair(0, p, tkm0, kc0) for p in range(ATTN_HEADS // 2)], []), axis=0)

    def attn_body(c, carry):
        m_old, l_old, mc_prev = carry
        m_new, alpha = stage_b_all(c - 1, m_old, l_old, mc_prev)
        tkm, kc = stage_a_prep(c)
        mcs, sums = [], []
        for p in range(ATTN_HEADS // 2):
            sums += stage_b_pair(c - 1, p, m_new, alpha)
            mcs += stage_a_pair(c, p, tkm, kc)
        return m_new, l_old * alpha + jnp.concatenate(sums, axis=0), jnp.concatenate(mcs, axis=0)

    m0 = jnp.full((ATTN_HEADS, Q_BLOCK), NEG_BIG, F32)
    l0 = jnp.zeros((ATTN_HEADS, Q_BLOCK), F32)
    m_old, l_old, mc_prev = lax.fori_loop(1, nch, attn_body, (m0, l0, mc0))
    m_new, alpha = stage_b_all(nch - 1, m_old, l_old, mc_prev)
    sums = sum([stage_b_pair(nch - 1, p, m_new, alpha) for p in range(ATTN_HEADS // 2)], [])
    l_fin = l_old * alpha + jnp.concatenate(sums, axis=0)

    outs = []
    for h in range(ATTN_HEADS):
        outs.append(acc_ref[h * HEAD_DIM:(h + 1) * HEAD_DIM, :] / l_fin[h:h + 1, :])
    o_ref[...] = jnp.concatenate(outs, axis=0).T


def _attn(it, ki, q, k, vt, B, S):
    T = B * S
    nb = S // Q_BLOCK
    n_sel = min(INDEX_TOPK_MAX, S // 4)
    return pl.pallas_call(
        functools.partial(_attn_kernel, n_sel),
        grid=(B, nb),
        in_specs=[
            pl.BlockSpec((IDXT_ROWS, Q_BLOCK), lambda b, i: (0, b * nb + i)),
            pl.BlockSpec((S, IDX_DIM), lambda b, i: (b, 0)),
            pl.BlockSpec((Q_BLOCK, ATTN_DIM), lambda b, i: (b * nb + i, 0)),
            pl.BlockSpec((S, ATTN_DIM), lambda b, i: (b, 0)),
            pl.BlockSpec((S // KEY_CHUNK, ATTN_DIM, KEY_CHUNK), lambda b, i: (b, 0, 0)),
        ],
        out_specs=pl.BlockSpec((Q_BLOCK, ATTN_DIM), lambda b, i: (b * nb + i, 0)),
        out_shape=jax.ShapeDtypeStruct((T, ATTN_DIM), F32),
        scratch_shapes=[
            pltpu.VMEM((S, Q_BLOCK), I32),
            pltpu.VMEM((-(-(S // KEY_CHUNK) // 2) * 2 * (KEY_CHUNK // FIELDS), Q_BLOCK), I32),
            pltpu.VMEM((ATTN_HEADS // 2, 2 * Q_BLOCK, 2 * HEAD_DIM), BF16),
            pltpu.VMEM((ATTN_DIM, Q_BLOCK), F32),
            pltpu.VMEM((ATTN_HEADS, KEY_CHUNK, Q_BLOCK), F32),
        ],
        compiler_params=_cparams(("arbitrary", "arbitrary")),
        name="attn",
    )(it, ki, q, k, vt)


def _mixout_kernel(tiles_per_seq, a_ref, halo_ref, y_ref, x_ref, cw_ref, wpool_ref, ps_ref,
                   wout_ref, g_ref, b_ref, o_ref, ext_ref):
    i = pl.program_id(0)
    tm = TM_MIX
    first = lax.rem(i, tiles_per_seq) == 0
    halo = jnp.where(first, 0.0, halo_ref[...])
    a = a_ref[...]
    h_c, gb_c, gc_c, p_c = (a[:, j * CONV_DIM:(j + 1) * CONV_DIM] for j in range(4))

    t0 = 2 * HALO
    n_ext = t0 + tm
    ext_ref[0:HALO, :] = jnp.zeros((HALO, CONV_DIM), F32)

    ext_ref[HALO:t0, :] = halo[:, 2 * CONV_DIM:3 * CONV_DIM] * halo[:, 0:CONV_DIM]
    u = gc_c * h_c
    ext_ref[t0:n_ext, :] = u
    cw = cw_ref[...]
    conv = cw[2:3, :] * u
    conv = conv + cw[1:2, :] * ext_ref[t0 - 1:n_ext - 1, :]
    conv = conv + cw[0:1, :] * ext_ref[t0 - 2:n_ext - 2, :]
    y_conv = gb_c * conv

    ext_ref[HALO:t0, :] = halo[:, OFF_P:OFF_P + POOL_DIM]
    ext_ref[t0:n_ext, :] = p_c
    sums = {}
    step = 1
    while step < POOL_WINDOWS[-1]:
        cur = ext_ref[HALO:n_ext, :] + ext_ref[HALO - step:n_ext - step, :]
        ext_ref[HALO:n_ext, :] = cur
        step *= 2
        sums[step] = ext_ref[t0:n_ext, :]
    tpos = (lax.rem(i, tiles_per_seq) * tm + lax.broadcasted_iota(I32, (tm, POOL_DIM), 0) + 1).astype(F32)
    grp = lax.shift_right_logical(lax.broadcasted_iota(I32, (tm, POOL_DIM), 1), _log2(POOL_GROUP))
    mean = jnp.zeros((tm, POOL_DIM), F32)
    for gi, w in enumerate(POOL_WINDOWS):
        mean = jnp.where(grp == gi, sums[w] / jnp.minimum(tpos, float(w)), mean)
    mixed = mean - p_c
    y_pool = _dot(mixed.astype(BF16), wpool_ref[...]) * ps_ref[...]

    mix = _dot(y_conv.astype(BF16), wout_ref[0:CONV_DIM, :])
    mix = mix + _dot(y_pool.astype(BF16), wout_ref[CONV_DIM:CONV_DIM + POOL_DIM, :])
    mix = mix + _dot(y_ref[...].astype(BF16), wout_ref[CONV_DIM + POOL_DIM:D_MODEL, :])
    z = DN_ALPHA * x_ref[...] + mix
    o_ref[...] = _layernorm_rows(z, g_ref[...], b_ref[...])


def _mixout(a, y_attn, x, cw, wpool_bd, ps, wout, g, b, S):
    T = a.shape[0]
    nt = T // TM_MIX
    tps = S // TM_MIX
    hb = TM_MIX // HALO
    full = lambda i: (0, 0)
    return pl.pallas_call(
        functools.partial(_mixout_kernel, tps),
        grid=(nt,),
        in_specs=[
            pl.BlockSpec((TM_MIX, OFF_Q), lambda i: (i, 0)),
            pl.BlockSpec((HALO, OFF_Q), lambda i: (jnp.maximum(i * hb - 1, 0), 0)),
            pl.BlockSpec((TM_MIX, ATTN_DIM), lambda i: (i, 0)),
            pl.BlockSpec((TM_MIX, D_MODEL), lambda i: (i, 0)),
            pl.BlockSpec((CONV_WIDTH, CONV_DIM), full),
            pl.BlockSpec((POOL_DIM, POOL_DIM), full),
            pl.BlockSpec((1, POOL_DIM), full),
            pl.BlockSpec((D_MODEL, D_MODEL), full),
            pl.BlockSpec((1, D_MODEL), full),
            pl.BlockSpec((1, D_MODEL), full),
        ],
        out_specs=pl.BlockSpec((TM_MIX, D_MODEL), lambda i: (i, 0)),
        out_shape=jax.ShapeDtypeStruct((T, D_MODEL), F32),
        scratch_shapes=[pltpu.VMEM((2 * HALO + TM_MIX, CONV_DIM), F32)],
        compiler_params=_cparams(("arbitrary",)),
        name="mixout",
    )(a, a, y_attn, x, cw, wpool_bd, ps, wout, g, b)


def _router_kernel(x_ref, rw_ref, rb_ref, idx_ref, gate_ref, cnt_ref):
    i = pl.program_id(0)
    tm = TM_ROUTE
    x = x_ref[...]
    w = rw_ref[...]
    xh = x.astype(BF16)
    xl = (x - xh.astype(F32)).astype(BF16)
    wh = w.astype(BF16)
    wl = (w - wh.astype(F32)).astype(BF16)
    logits = (_dot_nt(wh, xh) + (_dot_nt(wh, xl) + _dot_nt(wl, xh))) + rb_ref[...]
    erow = lax.broadcasted_iota(I32, (N_EXPERTS, tm), 0).astype(F32)
    work = logits
    vals, idxs = [], []
    multi = jnp.zeros((N_EXPERTS, tm), F32)
    for _ in range(TOP_K):
        mx = jnp.max(work, axis=0, keepdims=True)
        pick = jnp.min(jnp.where(work == mx, erow, float(N_EXPERTS)), axis=0, keepdims=True)
        hit = erow == pick
        work = jnp.where(hit, -jnp.inf, work)
        multi = multi + hit.astype(F32)
        vals.append(mx)
        idxs.append(pick)
    es = [jnp.exp(v - vals[0]) for v in vals]
    den = es[0] + es[1] + es[2] + es[3]
    idx_ref[...] = jnp.concatenate(idxs, axis=0).astype(I32)
    gate_ref[...] = jnp.concatenate([e / den for e in es] + [jnp.zeros((SUBLANES - TOP_K, tm), F32)], axis=0)
    part = multi[:, 0:LANES]
    for j in range(1, tm // LANES):
        part = part + multi[:, j * LANES:(j + 1) * LANES]

    @pl.when(i == 0)
    def _():
        cnt_ref[...] = jnp.zeros_like(cnt_ref)

    cnt_ref[...] += part


def _router(x1, rwT, rb):
    T = x1.shape[0]
    nt = T // TM_ROUTE
    full = lambda i: (0, 0)
    return pl.pallas_call(
        _router_kernel,
        grid=(nt,),
        in_specs=[
            pl.BlockSpec((TM_ROUTE, D_MODEL), lambda i: (i, 0)),
            pl.BlockSpec((N_EXPERTS, D_MODEL), full),
            pl.BlockSpec((N_EXPERTS, 1), full),
        ],
        out_specs=[
            pl.BlockSpec((TOP_K, TM_ROUTE), lambda i: (0, i)),
            pl.BlockSpec((SUBLANES, TM_ROUTE), lambda i: (0, i)),
            pl.BlockSpec((N_EXPERTS, LANES), full),
        ],
        out_shape=[
            jax.ShapeDtypeStruct((TOP_K, T), I32),
            jax.ShapeDtypeStruct((SUBLANES, T), F32),
            jax.ShapeDtypeStruct((N_EXPERTS, LANES), F32),
        ],
        compiler_params=_cparams(("arbitrary",)),
        name="router",
    )(x1, rwT, rb)


def _ranks_kernel(n_tiles_pad, idx_ref, cnt_ref, dest_ref, meta_ref, tri_ref, start_ref, carry_ref):
    i = pl.program_id(0)
    tm = TM_ROUTE
    erow = lax.broadcasted_iota(I32, (N_EXPERTS, LANES), 0)
    elane = lax.broadcasted_iota(I32, (N_EXPERTS, LANES), 1)

    @pl.when(i == 0)
    def _():
        cnt = jnp.sum(cnt_ref[...], axis=1, keepdims=True)
        cnt_i = jnp.broadcast_to(cnt, (N_EXPERTS, LANES)).astype(I32)
        padded = lax.shift_left(lax.shift_right_logical(cnt_i + (TM_MOE - 1), _log2(TM_MOE)), _log2(TM_MOE))
        r = lax.broadcasted_iota(I32, (N_EXPERTS, N_EXPERTS), 0)
        c = lax.broadcasted_iota(I32, (N_EXPERTS, N_EXPERTS), 1)
        low = (c <= r).astype(F32)
        pad_end = lax.dot_general(low, padded.astype(F32), (((1,), (0,)), ((), ())),
                                  precision=lax.Precision.HIGHEST, preferred_element_type=F32)
        pad_start = pad_end - padded.astype(F32)
        start_ref[...] = pad_start
        carry_ref[...] = jnp.zeros_like(carry_ref)
        a = lax.broadcasted_iota(I32, (tm, tm), 0)
        bcol = lax.broadcasted_iota(I32, (tm, tm), 1)
        tri_ref[...] = (a < bcol).astype(BF16)
        ntp = n_tiles_pad
        tstart = (lax.broadcasted_iota(I32, (N_EXPERTS, ntp), 1) * TM_MOE).astype(F32)
        pe = jnp.concatenate([pad_end] * (ntp // LANES), axis=1)
        texp = jnp.sum((pe <= tstart).astype(F32), axis=0, keepdims=True)
        texp = jnp.minimum(texp, float(N_EXPERTS - 1)).astype(I32)
        total = jnp.max(pad_end, axis=0, keepdims=True)
        n_used = lax.shift_right_logical(total.astype(I32), _log2(TM_MOE))
        zstart = jnp.sum(jnp.where(erow == elane, pad_start + cnt_i.astype(F32), 0.0),
                         axis=0, keepdims=True).astype(I32)
        zlen = jnp.sum(jnp.where(erow == elane, (padded - cnt_i).astype(F32), 0.0),
                       axis=0, keepdims=True).astype(I32)
        lanes_pad = jnp.zeros((1, ntp - LANES), I32)
        meta_ref[...] = jnp.concatenate(
            [texp,
             jnp.concatenate([n_used, lanes_pad], axis=1),
             jnp.concatenate([zstart, lanes_pad], axis=1),
             jnp.concatenate([zlen, lanes_pad], axis=1),
             jnp.zeros((SUBLANES - 4, ntp), I32)], axis=0)

    idx = idx_ref[...]
    erow_t = lax.broadcasted_iota(I32, (N_EXPERTS, tm), 0)
    hits = [erow_t == idx[k:k + 1, :] for k in range(TOP_K)]
    multi = hits[0].astype(F32)
    for k in range(1, TOP_K):
        multi = multi + hits[k].astype(F32)
    prefix = _dot(multi.astype(BF16), tri_ref[...])
    base = jnp.concatenate([carry_ref[...] + start_ref[...]] * (tm // LANES), axis=1)
    tot = prefix + base
    dest_ref[...] = jnp.concatenate(
        [jnp.sum(jnp.where(hits[k], tot, 0.0), axis=0, keepdims=True) for k in range(TOP_K)],
        axis=0).astype(I32)
    carry_ref[...] += jnp.broadcast_to(jnp.sum(multi, axis=1, keepdims=True), (N_EXPERTS, LANES))


def _ranks(idxT, cnt, n_tiles_pad):
    T = idxT.shape[1]
    nt = T // TM_ROUTE
    full = lambda i: (0, 0)
    return pl.pallas_call(
        functools.partial(_ranks_kernel, n_tiles_pad),
        grid=(nt,),
        in_specs=[
            pl.BlockSpec((TOP_K, TM_ROUTE), lambda i: (0, i)),
            pl.BlockSpec((N_EXPERTS, LANES), full),
        ],
        out_specs=[
            pl.BlockSpec((TOP_K, TM_ROUTE), lambda i: (0, i)),
            pl.BlockSpec((SUBLANES, n_tiles_pad), full),
        ],
        out_shape=[
            jax.ShapeDtypeStruct((TOP_K, T), I32),
            jax.ShapeDtypeStruct((SUBLANES, n_tiles_pad), I32),
        ],
        scratch_shapes=[
            pltpu.VMEM((TM_ROUTE, TM_ROUTE), BF16),
            pltpu.VMEM((N_EXPERTS, LANES), F32),
            pltpu.VMEM((N_EXPERTS, LANES), F32),
        ],
        compiler_params=_cparams(("arbitrary",)),
        name="ranks",
    )(idxT, cnt)


RT = D_MODEL // LANES
DMA_UNROLL = 8


def _row_copy(src_ref, s, dst_ref, d, sem):
    return pltpu.make_async_copy(src_ref.at[pl.ds(pl.multiple_of(s * RT, RT), RT)],
                                 dst_ref.at[pl.ds(pl.multiple_of(d * RT, RT), RT)], sem)


def _to_row_tiled(dst_ref, val, rows):
    for s in range(RT):
        dst_ref[pl.ds(s, rows, stride=RT), :] = val[:, s * LANES:(s + 1) * LANES]


def _from_row_tiled(src_ref, rows):
    return [src_ref[pl.ds(s, rows, stride=RT), :] for s in range(RT)]


def _scatter_kernel(meta_ref, x_ref, dest_ref, xs_ref, stage_ref, zero_ref, sem):
    i = pl.program_id(0)
    tm = TM_ROUTE

    @pl.when(i == 0)
    def _():
        zero_ref[...] = jnp.zeros_like(zero_ref)

        def zcopy(row, nrows):
            off = pl.multiple_of(row * RT, RT)
            return pltpu.make_async_copy(zero_ref.at[pl.ds(0, nrows * RT)],
                                         xs_ref.at[pl.ds(off, nrows * RT)], sem)

        def pad_fill(wait):
            def body(e, c):
                row = meta_ref[2, e]
                plen = meta_ref[3, e]
                for bit in reversed(range(_log2(TM_MOE))):
                    size = 1 << bit
                    has = (plen & size) != 0

                    @pl.when(has)
                    def _():
                        cp = zcopy(row, size)
                        cp.wait() if wait else cp.start()

                    row = row + jnp.where(has, size, 0)
                return c
            lax.fori_loop(0, N_EXPERTS, body, 0)

        def tail_fill(wait):
            def body(j, c):
                cp = zcopy(j * TM_MOE, TM_MOE)
                cp.wait() if wait else cp.start()
                return c
            lax.fori_loop(meta_ref[1, 0], xs_ref.shape[0] // (TM_MOE * RT), body, 0)

        pad_fill(False)
        tail_fill(False)
        pad_fill(True)
        tail_fill(True)

    _to_row_tiled(stage_ref, x_ref[...], tm)

    def start(t, c):
        for k in range(TOP_K):
            _row_copy(stage_ref, t, xs_ref, dest_ref[k, t], sem).start(priority=k % 2)
        return c

    lax.fori_loop(0, tm, start, 0, unroll=DMA_UNROLL)
    for k in range(TOP_K):
        pltpu.make_async_copy(stage_ref, xs_ref.at[pl.ds(0, tm * RT)], sem).wait()


def _scatter(meta, x1, destT, n_rows_alloc):
    T = x1.shape[0]
    nt = T // TM_ROUTE
    return pl.pallas_call(
        _scatter_kernel,
        grid_spec=pltpu.PrefetchScalarGridSpec(
            num_scalar_prefetch=1,
            grid=(nt,),
            in_specs=[
                pl.BlockSpec((TM_ROUTE, D_MODEL), lambda i, m: (i, 0)),
                pl.BlockSpec((TOP_K, TM_ROUTE), lambda i, m: (0, i), memory_space=pltpu.SMEM),
            ],
            out_specs=pl.BlockSpec(memory_space=pl.ANY),
            scratch_shapes=[
                pltpu.VMEM((TM_ROUTE * RT, LANES), F32),
                pltpu.VMEM((TM_MOE * RT, LANES), F32),
                pltpu.SemaphoreType.DMA(()),
            ],
        ),
        out_shape=jax.ShapeDtypeStruct((n_rows_alloc * RT, LANES), F32),
        compiler_params=_cparams(("arbitrary",)),
        name="scatter",
    )(meta, x1, destT)


def _gmm_kernel(meta_ref, xs_ref, wgu_ref, bgu_ref, wd_ref, bd_ref, ys_ref, wgu_b, wd_b, lhs_ref, act_ref):
    j = pl.program_id(0)
    n_used = meta_ref[1, 0]
    e_now = meta_ref[0, j]
    e_prev = meta_ref[0, jnp.maximum(j - 1, 0)]
    used = j < n_used

    @pl.when(used & ((j == 0) | (e_now != e_prev)))
    def _():
        wgu_b[...] = wgu_ref[0, 0].astype(BF16)
        wd_b[...] = wd_ref[0, 0].astype(BF16)

    @pl.when(used)
    def _():
        for s, piece in enumerate(_from_row_tiled(xs_ref, TM_MOE)):
            lhs_ref[:, s * LANES:(s + 1) * LANES] = piece.astype(BF16)
        x = lhs_ref[...]
        nc = 256
        for c in range(D_FF // nc):
            gate = _dot(x, wgu_b[:, c * nc:(c + 1) * nc]) + bgu_ref[0, 0, :, c * nc:(c + 1) * nc]
            up = (_dot(x, wgu_b[:, D_FF + c * nc:D_FF + (c + 1) * nc])
                  + bgu_ref[0, 0, :, D_FF + c * nc:D_FF + (c + 1) * nc])
            gate = jnp.minimum(gate, SWIGLU_LIMIT)
            up = jnp.clip(up, -SWIGLU_LIMIT, SWIGLU_LIMIT)
            act = gate * jax.nn.sigmoid(SWIGLU_ALPHA * gate) * (up + 1.0)
            act_ref[:, c * nc:(c + 1) * nc] = act.astype(BF16)
        _to_row_tiled(ys_ref, _dot(act_ref[...], wd_b[...]) + bd_ref[0, 0], TM_MOE)

    @pl.when(jnp.logical_not(used))
    def _():
        ys_ref[...] = jnp.zeros_like(ys_ref)


def _gmm(meta, xs, w_gu, b_gu, w_down, b_down, n_tiles, layer):
    last = lambda m: jnp.maximum(m[1, 0] - 1, 0)
    return pl.pallas_call(
        _gmm_kernel,
        grid_spec=pltpu.PrefetchScalarGridSpec(
            num_scalar_prefetch=1,
            grid=(n_tiles,),
            in_specs=[
                pl.BlockSpec((TM_MOE * RT, LANES), lambda j, m: (jnp.minimum(j, last(m)), 0)),
                pl.BlockSpec((1, 1, D_MODEL, 2 * D_FF), lambda j, m: (layer, m[0, j], 0, 0)),
                pl.BlockSpec((1, 1, 1, 2 * D_FF), lambda j, m: (layer, m[0, j], 0, 0)),
                pl.BlockSpec((1, 1, D_FF, D_MODEL), lambda j, m: (layer, m[0, j], 0, 0)),
                pl.BlockSpec((1, 1, 1, D_MODEL), lambda j, m: (layer, m[0, j], 0, 0)),
            ],
            out_specs=pl.BlockSpec((TM_MOE * RT, LANES), lambda j, m: (j, 0)),
            scratch_shapes=[
                pltpu.VMEM((D_MODEL, 2 * D_FF), BF16),
                pltpu.VMEM((D_FF, D_MODEL), BF16),
                pltpu.VMEM((TM_MOE, D_MODEL), BF16),
                pltpu.VMEM((TM_MOE, D_FF), BF16),
            ],
        ),
        out_shape=jax.ShapeDtypeStruct((n_tiles * TM_MOE * RT, LANES), F32),
        compiler_params=_cparams(("arbitrary",)),
        name="gmm",
    )(meta, xs, w_gu, b_gu, w_down, b_down)


def _combine_kernel(x_ref, dest_ref, gate_ref, ys_ref, g_ref, b_ref, o_ref, ob_ref, buf_ref, sem):
    tm = TM_COMB

    def start(t, c):
        for k in range(TOP_K):
            _row_copy(ys_ref, dest_ref[k, t], buf_ref.at[k], t, sem).start(priority=k % 2)
        return c

    lax.fori_loop(0, tm, start, 0, unroll=DMA_UNROLL)
    for k in range(TOP_K):
        pltpu.make_async_copy(ys_ref.at[pl.ds(0, tm * RT)], buf_ref.at[k], sem).wait()
    gates = jnp.concatenate(
        [gate_ref[...], jnp.zeros((LANES - SUBLANES, tm), F32)], axis=0).T
    pieces = None
    for k in range(TOP_K):
        gk = gates[:, k:k + 1]
        rows = [p * gk for p in _from_row_tiled(buf_ref.at[k], tm)]
        pieces = rows if pieces is None else [a + r for a, r in zip(pieces, rows)]
    z = DN_ALPHA * x_ref[...] + jnp.concatenate(pieces, axis=1)
    out = _layernorm_rows(z, g_ref[...], b_ref[...])
    o_ref[...] = out
    ob_ref[...] = out.astype(BF16)


def _combine(x1, destT, gateT, ys, g, b):
    T = x1.shape[0]
    nt = T // TM_COMB
    full = lambda i: (0, 0)
    return pl.pallas_call(
        _combine_kernel,
        grid=(nt,),
        in_specs=[
            pl.BlockSpec((TM_COMB, D_MODEL), lambda i: (i, 0)),
            pl.BlockSpec((TOP_K, TM_COMB), lambda i: (0, i), memory_space=pltpu.SMEM),
            pl.BlockSpec((SUBLANES, TM_COMB), lambda i: (0, i)),
            pl.BlockSpec(memory_space=pl.ANY),
            pl.BlockSpec((1, D_MODEL), full),
            pl.BlockSpec((1, D_MODEL), full),
        ],
        out_specs=[
            pl.BlockSpec((TM_COMB, D_MODEL), lambda i: (i, 0)),
            pl.BlockSpec((TM_COMB, D_MODEL), lambda i: (i, 0)),
        ],
        out_shape=[
            jax.ShapeDtypeStruct((T, D_MODEL), F32),
            jax.ShapeDtypeStruct((T, D_MODEL), BF16),
        ],
        scratch_shapes=[
            pltpu.VMEM((TOP_K, TM_COMB * RT, LANES), F32),
            pltpu.SemaphoreType.DMA(()),
        ],
        compiler_params=_cparams(("arbitrary",)),
        name="combine",
    )(x1, destT, gateT, ys, g, b)


def _block_diag(w):
    g, c, _ = w.shape
    out = jnp.zeros((g * c, g * c), w.dtype)
    for i in range(g):
        out = out.at[i * c:(i + 1) * c, i * c:(i + 1) * c].set(w[i])
    return out


def kernel(x, w_in, conv_w, w_pool, pool_scale, idx_kn_g, idx_kn_b, w_out, ln1_g, ln1_b, router_w,
           router_b, w_gu, b_gu, w_down, b_down, ln2_g, ln2_b):
    B, S, D = x.shape
    T = B * S
    depth = w_in.shape[0]
    n_pairs = T * TOP_K
    n_tiles = (n_pairs + N_EXPERTS * (TM_MOE - 1)) // TM_MOE + 1
    n_tiles_pad = ((n_tiles + LANES - 1) // LANES) * LANES
    n_rows_alloc = (n_tiles + 1) * TM_MOE

    xf = x.reshape(T, D)
    xb = xf
    for l in range(depth):
        wl = w_in[l]
        wn = jnp.concatenate(
            [wl[:, 0:OFF_V], wl[:, OFF_KI:OFF_WI], jnp.zeros((D, LANES - IDX_DIM), F32)], axis=1).astype(BF16)
        wt = jnp.concatenate(
            [wl[:, OFF_V:OFF_QI], wl[:, OFF_QI:OFF_KI], wl[:, OFF_WI:D_IN],
             jnp.zeros((D, IDXT_ROWS - IDX_HEADS * IDX_DIM - IDX_HEADS), F32)], axis=1).T.astype(BF16)
        a, q, k, ki, vt, it = _inproj(xb, wn, wt, idx_kn_g[l].reshape(1, -1), idx_kn_b[l].reshape(1, -1))
        y_attn = _attn(it, ki, q, k, vt, B, S)
        x1 = _mixout(a, y_attn, xf, conv_w[l], _block_diag(w_pool[l]).astype(BF16),
                     pool_scale[l].reshape(1, -1), w_out[l].astype(BF16),
                     ln1_g[l].reshape(1, -1), ln1_b[l].reshape(1, -1), S)
        idxT, gateT, cnt = _router(x1, router_w[l].T, router_b[l].reshape(-1, 1))
        destT, meta = _ranks(idxT, cnt, n_tiles_pad)
        xs = _scatter(meta, x1, destT, n_rows_alloc)
        ys = _gmm(meta, xs, w_gu, b_gu.reshape(depth, N_EXPERTS, 1, -1), w_down,
                  b_down.reshape(depth, N_EXPERTS, 1, -1), n_tiles, l)
        xf, xb = _combine(x1, destT, gateT, ys, ln2_g[l].reshape(1, -1), ln2_b[l].reshape(1, -1))
    return xf.reshape(B, S, D)
```

```python
import functools

import jax
import jax.numpy as jnp
import numpy as np
from jax import lax
from jax.experimental import pallas as pl
from jax.experimental.pallas import tpu as pltpu

F32 = jnp.float32
BF16 = jnp.bfloat16
I32 = jnp.int32

D_MODEL = 1024
CONV_DIM = 256
CONV_WIDTH = 3
POOL_WINDOWS = (2, 4, 8, 16)
POOL_GROUP = 64
POOL_DIM = 256
ATTN_HEADS = 8
HEAD_DIM = 64
ATTN_DIM = 512
IDX_HEADS = 8
IDX_DIM = 32
INDEX_TOPK_MAX = 256
Q_BLOCK = 128
N_EXPERTS = 32
TOP_K = 4
D_FF = 1024
SWIGLU_ALPHA = 1.702
SWIGLU_LIMIT = 7.0
LN_EPS = 1e-5
DEPTH = 2
DN_ALPHA = (2.0 * DEPTH) ** 0.25

OFF_P = 3 * CONV_DIM
OFF_Q = OFF_P + POOL_DIM
OFF_K = OFF_Q + ATTN_DIM
OFF_V = OFF_K + ATTN_DIM
OFF_QI = OFF_V + ATTN_DIM
OFF_KI = OFF_QI + IDX_HEADS * IDX_DIM
OFF_WI = OFF_KI + IDX_DIM
D_IN = OFF_WI + IDX_HEADS

LANES = 128
SUBLANES = 8
VMEM_LIMIT = 56 * 1024 * 1024
KEY_CHUNK = 512
TIE_GROUP = 128
HALO = 16
TM_PROJ = 512
TM_MIX = 512
TM_ROUTE = 512
TM_MOE = 512
TM_COMB = 512
IDXT_ROWS = IDX_HEADS * IDX_DIM + 16
WT_ROWS = ATTN_DIM + IDXT_ROWS
INT_MIN = -(2 ** 31)
NEG_BIG = -1e30


def _cparams(sem, flags=None):
    return pltpu.CompilerParams(dimension_semantics=sem, vmem_limit_bytes=VMEM_LIMIT, flags=flags)


def _log2(n):
    k = n.bit_length() - 1
    assert 1 << k == n
    return k


def _dot(a, b):
    return jnp.dot(a, b, preferred_element_type=F32)


def _dot_nt(a, b):
    return lax.dot_general(a, b, (((1,), (1,)), ((), ())), preferred_element_type=F32)


def _tree(x, op):
    parts = [x[j * SUBLANES:(j + 1) * SUBLANES, :] for j in range(x.shape[0] // SUBLANES)]
    while len(parts) > 1:
        nxt = [op(parts[j], parts[j + 1]) for j in range(0, len(parts) - 1, 2)]
        if len(parts) % 2:
            nxt.append(parts[-1])
        parts = nxt
    return parts[0]


def _layernorm_rows(z, g, b):
    mu = jnp.mean(z, axis=-1, keepdims=True)
    d = z - mu
    var = jnp.mean(d * d, axis=-1, keepdims=True)
    return d * lax.rsqrt(var + LN_EPS) * g + b


def _inproj_kernel(x_ref, wn_ref, wt_ref, kg_ref, kb_ref,
                   a_ref, q_ref, k_ref, ki_ref, vt_ref, it_ref):
    x = x_ref[...].astype(BF16)
    a_ref[...] = _dot(x, wn_ref[:, 0:OFF_Q])
    q_ref[...] = (_dot(x, wn_ref[:, OFF_Q:OFF_K]) * (HEAD_DIM ** -0.5)).astype(BF16)
    k_ref[...] = _dot(x, wn_ref[:, OFF_K:OFF_V]).astype(BF16)
    ki = _dot(x, wn_ref[:, OFF_V:OFF_V + LANES])[:, 0:IDX_DIM]
    ki_ref[...] = _layernorm_rows(ki, kg_ref[...], kb_ref[...])
    t = _dot_nt(wt_ref[...], x)
    for c in range(TM_PROJ // KEY_CHUNK):
        vt_ref[c] = t[0:ATTN_DIM, c * KEY_CHUNK:(c + 1) * KEY_CHUNK].astype(BF16)
    it_ref[...] = t[ATTN_DIM:WT_ROWS, :]


def _inproj(xb, wn, wt, kg, kb):
    T = xb.shape[0]
    nt = T // TM_PROJ
    cpt = TM_PROJ // KEY_CHUNK
    full = lambda i: (0, 0)
    return pl.pallas_call(
        _inproj_kernel,
        grid=(nt,),
        in_specs=[
            pl.BlockSpec((TM_PROJ, D_MODEL), lambda i: (i, 0)),
            pl.BlockSpec(wn.shape, full),
            pl.BlockSpec(wt.shape, full),
            pl.BlockSpec((1, IDX_DIM), full),
            pl.BlockSpec((1, IDX_DIM), full),
        ],
        out_specs=[
            pl.BlockSpec((TM_PROJ, OFF_Q), lambda i: (i, 0)),
            pl.BlockSpec((TM_PROJ, ATTN_DIM), lambda i: (i, 0)),
            pl.BlockSpec((TM_PROJ, ATTN_DIM), lambda i: (i, 0)),
            pl.BlockSpec((TM_PROJ, IDX_DIM), lambda i: (i, 0)),
            pl.BlockSpec((cpt, ATTN_DIM, KEY_CHUNK), lambda i: (i, 0, 0)),
            pl.BlockSpec((IDXT_ROWS, TM_PROJ), lambda i: (0, i)),
        ],
        out_shape=[
            jax.ShapeDtypeStruct((T, OFF_Q), F32),
            jax.ShapeDtypeStruct((T, ATTN_DIM), BF16),
            jax.ShapeDtypeStruct((T, ATTN_DIM), BF16),
            jax.ShapeDtypeStruct((T, IDX_DIM), F32),
            jax.ShapeDtypeStruct((T // KEY_CHUNK, ATTN_DIM, KEY_CHUNK), BF16),
            jax.ShapeDtypeStruct((IDXT_ROWS, T), F32),
        ],
        compiler_params=_cparams(("arbitrary",)),
        name="inproj",
    )(xb, wn, wt, kg, kb)


FIELD_BITS = 8
DIGIT_BITS = FIELD_BITS - 1
FIELDS = 32 // FIELD_BITS
DIGIT_SHIFTS = (25, 18, 11, 4, 0)
FIELD_ONES = 0x01010101
FIELD_GUARDS = 0x80808080 - (1 << 32)


def _pack_fields(f):
    q = f.shape[0] // FIELDS
    w = f[0:q]
    for j in range(1, FIELDS):
        w = w | lax.shift_left(f[j * q:(j + 1) * q], FIELD_BITS * j)
    return w | jnp.int32(FIELD_GUARDS)


def _count_fields_ge(words_ref, nch, cand):
    wq = 2 * (KEY_CHUNK // FIELDS)
    assert words_ref.shape[0] // SUBLANES < (1 << FIELD_BITS)
    cvec = cand * jnp.int32(FIELD_ONES)

    def body(c, acc):
        base = pl.multiple_of(c * wq, wq)
        w = words_ref[pl.ds(base, wq), :]
        hit = lax.shift_right_logical(w - cvec, DIGIT_BITS) & jnp.int32(FIELD_ONES)
        return acc + _tree(hit, jnp.add)

    acc = lax.fori_loop(0, lax.shift_right_logical(nch + 1, 1), body,
                        jnp.zeros((SUBLANES, words_ref.shape[1]), I32))
    tot = acc & 255
    for j in range(1, FIELDS):
        tot = tot + (lax.shift_right_logical(acc, FIELD_BITS * j) & 255)
    return jnp.sum(tot.astype(F32), axis=0, keepdims=True)


def _topk_threshold(keys_ref, words_ref, nch, n_sel, n_causal):
    lanes = keys_ref.shape[1]
    wq = KEY_CHUNK // FIELDS
    want = jnp.full((1, lanes), float(n_sel), F32)
    bucket = n_causal.astype(F32)
    prefix = None
    for idx, shift in enumerate(DIGIT_SHIFTS):
        width = (DIGIT_SHIFTS[idx - 1] if idx else 32) - shift
        if idx:
            def prep(c, carry, shift=shift, width=width, prefix=prefix):
                key = keys_ref[pl.ds(pl.multiple_of(c * KEY_CHUNK, KEY_CHUNK), KEY_CHUNK), :]
                digit = (key >> shift) & jnp.int32((1 << width) - 1)
                fields = jnp.where((key >> (shift + width)) == prefix, digit, 0)
                words_ref[pl.ds(pl.multiple_of(c * wq, wq), wq), :] = _pack_fields(fields)
                return carry

            lax.fori_loop(0, nch, prep, 0)
        dig = jnp.zeros((1, lanes), I32)
        at_dig = bucket
        above = jnp.zeros((1, lanes), F32)
        for bit in reversed(range(width)):
            cand = dig + (1 << bit)
            cnt = _count_fields_ge(words_ref, nch, cand)
            ok = cnt >= want
            dig = jnp.where(ok, cand, dig)
            at_dig = jnp.where(ok, cnt, at_dig)
            above = jnp.where(ok, above, cnt)
        want = want - above
        bucket = at_dig - above
        if idx:
            prefix = lax.shift_left(prefix, width) | dig
        else:
            prefix = dig - (1 << (DIGIT_BITS - 1))
    all_taken = n_causal <= n_sel
    tau = jnp.where(all_taken, jnp.int32(INT_MIN + 1), prefix)
    return tau, want, jnp.where(all_taken, 0.0, bucket - want)


def _attn_kernel(n_sel, it_ref, ki_ref, q_ref, k_ref, vt_ref, o_ref,
                 keys_ref, words_ref, rhs_ref, acc_ref, lg_ref):
    i = pl.program_id(1)
    nch = lax.shift_right_logical(i * Q_BLOCK + Q_BLOCK + KEY_CHUNK - 1, _log2(KEY_CHUNK))
    q0 = i * Q_BLOCK
    KC = KEY_CHUNK
    row = lax.broadcasted_iota(I32, (KC, Q_BLOCK), 0)
    lane = lax.broadcasted_iota(I32, (KC, Q_BLOCK), 1)
    qpos = q0 + lane

    qcat = jnp.concatenate(
        [it_ref[h * IDX_DIM:(h + 1) * IDX_DIM, :] for h in range(IDX_HEADS)], axis=1).astype(BF16)
    w_all = it_ref[IDX_HEADS * IDX_DIM:IDX_HEADS * IDX_DIM + IDX_HEADS, :] * (
        (IDX_HEADS ** -0.5) * (IDX_DIM ** -0.5))

    def score_body(c, carry):
        base = pl.multiple_of(c * KC, KC)
        kic = ki_ref[pl.ds(base, KC), :].astype(BF16)
        s = _dot(kic, qcat)
        sc = jnp.zeros((KC, Q_BLOCK), F32)
        for h in range(IDX_HEADS):
            sc = sc + jnp.maximum(s[:, h * Q_BLOCK:(h + 1) * Q_BLOCK], 0.0) * w_all[h:h + 1, :]
        bits = lax.bitcast_convert_type(sc, I32)
        key = bits ^ ((bits >> 31) & jnp.int32(0x7FFFFFFF))
        key = jnp.where(bits == jnp.int32(INT_MIN), jnp.int32(0), key)
        causal = base + row <= qpos
        keys_ref[pl.ds(base, KC), :] = jnp.where(causal, key, jnp.int32(INT_MIN))
        top = jnp.where(causal, (key >> DIGIT_SHIFTS[0]) + (1 << (DIGIT_BITS - 1)), 0)
        words_ref[pl.ds(pl.multiple_of(c * (KC // FIELDS), KC // FIELDS), KC // FIELDS), :] = _pack_fields(top)
        return carry

    lax.fori_loop(0, nch, score_body, 0)

    @pl.when(lax.rem(nch, 2) == 1)
    def _():
        wq = KC // FIELDS
        words_ref[pl.ds(pl.multiple_of(nch * wq, wq), wq), :] = _pack_fields(jnp.zeros((KC, Q_BLOCK), I32))

    n_causal = q0 + lax.broadcasted_iota(I32, (1, Q_BLOCK), 1) + 1
    tau, keep, excess = _topk_threshold(keys_ref, words_ref, nch, n_sel, n_causal)

    @pl.when(jnp.max(excess) > 0.0)
    def _():
        g = TIE_GROUP
        r = lax.broadcasted_iota(I32, (g, g), 0)
        cc = lax.broadcasted_iota(I32, (g, g), 1)
        tri = (cc <= r).astype(BF16)

        def retire(c, before):
            base = pl.multiple_of(c * KC, KC)
            blk = keys_ref[pl.ds(base, KC), :]
            tied = blk == tau
            ones = jnp.where(tied, 1.0, 0.0).astype(BF16)
            local = [_dot(tri, ones[j * g:(j + 1) * g, :]) for j in range(KC // g)]
            upto = []
            for lj in local:
                upto.append(lj + before)
                before = before + lj[g - 1:g, :]
            upto = jnp.concatenate(upto, axis=0)
            keys_ref[pl.ds(base, KC), :] = jnp.where(tied & (upto > keep), jnp.int32(INT_MIN), blk)
            return before

        lax.fori_loop(0, nch, retire, jnp.zeros((1, Q_BLOCK), F32))

    q = q_ref[...]
    lane_q = lax.broadcasted_iota(I32, (Q_BLOCK, 2 * HEAD_DIM), 1)
    for p in range(ATTN_HEADS // 2):
        qp = q[:, p * 2 * HEAD_DIM:(p + 1) * 2 * HEAD_DIM]
        rhs_ref[p, 0:Q_BLOCK, :] = jnp.where(lane_q < HEAD_DIM, qp, jnp.zeros_like(qp))
        rhs_ref[p, Q_BLOCK:2 * Q_BLOCK, :] = jnp.where(lane_q >= HEAD_DIM, qp, jnp.zeros_like(qp))
    acc_ref[...] = jnp.zeros_like(acc_ref)

    def stage_a_prep(c):
        base = pl.multiple_of(c * KC, KC)
        tk = (base + row - (q0 + Q_BLOCK - 1)).astype(F32)
        tkm = jnp.where(keys_ref[pl.ds(base, KC), :] >= tau, tk, NEG_BIG)
        return tkm, k_ref[pl.ds(base, KC), :]

    def stage_a_pair(c, p, tkm, kc):
        l2 = _dot_nt(kc[:, p * 2 * HEAD_DIM:(p + 1) * 2 * HEAD_DIM], rhs_ref[p])
        mcs = []
        for hh in range(2):
            h = 2 * p + hh
            slope = 2.0 ** (-8.0 * (h + 1) / ATTN_HEADS)
            lg = l2[:, hh * Q_BLOCK:(hh + 1) * Q_BLOCK] + slope * tkm
            lg_ref[h] = lg
            mcs.append(jnp.max(_tree(lg, jnp.maximum), axis=0, keepdims=True))
        return mcs

    def stage_b_pair(c, p, m_new, alpha):
        sums = []
        for hh in range(2):
            h = 2 * p + hh
            pr = jnp.exp(lg_ref[h] - m_new[h:h + 1, :])
            sums.append(jnp.sum(_tree(pr, jnp.add), axis=0, keepdims=True))
            pv = _dot(vt_ref[c, h * HEAD_DIM:(h + 1) * HEAD_DIM, :], pr.astype(BF16))
            acc_ref[h * HEAD_DIM:(h + 1) * HEAD_DIM, :] = (
                acc_ref[h * HEAD_DIM:(h + 1) * HEAD_DIM, :] * alpha[h:h + 1, :] + pv)
        return sums

    def stage_b_all(c, m_old, l_old, mc):
        m_new = jnp.maximum(m_old, mc)
        alpha = jnp.exp(m_old - m_new)
        return m_new, alpha

    tkm0, kc0 = stage_a_prep(0)
    mc0 = jnp.concatenate(sum([stage_a_pair(0, p, tkm0, kc0) for p in range(ATTN_HEADS // 2)], []), axis=0)

    def attn_body(c, carry):
        m_old, l_old, mc_prev = carry
        m_new, alpha = stage_b_all(c - 1, m_old, l_old, mc_prev)
        tkm, kc = stage_a_prep(c)
        mcs, sums = [], []
        for p in range(ATTN_HEADS // 2):
            sums += stage_b_pair(c - 1, p, m_new, alpha)
            mcs += stage_a_pair(c, p, tkm, kc)
        return m_new, l_old * alpha + jnp.concatenate(sums, axis=0), jnp.concatenate(mcs, axis=0)

    m0 = jnp.full((ATTN_HEADS, Q_BLOCK), NEG_BIG, F32)
    l0 = jnp.zeros((ATTN_HEADS, Q_BLOCK), F32)
    m_old, l_old, mc_prev = lax.fori_loop(1, nch, attn_body, (m0, l0, mc0))
    m_new, alpha = stage_b_all(nch - 1, m_old, l_old, mc_prev)
    sums = sum([stage_b_pair(nch - 1, p, m_new, alpha) for p in range(ATTN_HEADS // 2)], [])
    l_fin = l_old * alpha + jnp.concatenate(sums, axis=0)

    outs = []
    for h in range(ATTN_HEADS):
        outs.append(acc_ref[h * HEAD_DIM:(h + 1) * HEAD_DIM, :] / l_fin[h:h + 1, :])
    o_ref[...] = jnp.concatenate(outs, axis=0).T


def _attn(it, ki, q, k, vt, B, S):
    T = B * S
    nb = S // Q_BLOCK
    n_sel = min(INDEX_TOPK_MAX, S // 4)
    return pl.pallas_call(
        functools.partial(_attn_kernel, n_sel),
        grid=(B, nb),
        in_specs=[
            pl.BlockSpec((IDXT_ROWS, Q_BLOCK), lambda b, i: (0, b * nb + i)),
            pl.BlockSpec((S, IDX_DIM), lambda b, i: (b, 0)),
            pl.BlockSpec((Q_BLOCK, ATTN_DIM), lambda b, i: (b * nb + i, 0)),
            pl.BlockSpec((S, ATTN_DIM), lambda b, i: (b, 0)),
            pl.BlockSpec((S // KEY_CHUNK, ATTN_DIM, KEY_CHUNK), lambda b, i: (b, 0, 0)),
        ],
        out_specs=pl.BlockSpec((Q_BLOCK, ATTN_DIM), lambda b, i: (b * nb + i, 0)),
        out_shape=jax.ShapeDtypeStruct((T, ATTN_DIM), F32),
        scratch_shapes=[
            pltpu.VMEM((S, Q_BLOCK), I32),
            pltpu.VMEM((-(-(S // KEY_CHUNK) // 2) * 2 * (KEY_CHUNK // FIELDS), Q_BLOCK), I32),
            pltpu.VMEM((ATTN_HEADS // 2, 2 * Q_BLOCK, 2 * HEAD_DIM), BF16),
            pltpu.VMEM((ATTN_DIM, Q_BLOCK), F32),
            pltpu.VMEM((ATTN_HEADS, KEY_CHUNK, Q_BLOCK), F32),
        ],
        compiler_params=_cparams(("arbitrary", "arbitrary")),
        name="attn",
    )(it, ki, q, k, vt)


def _mixout_kernel(tiles_per_seq, a_ref, halo_ref, y_ref, x_ref, cw_ref, wpool_ref, ps_ref,
                   wout_ref, g_ref, b_ref, o_ref, ext_ref):
    i = pl.program_id(0)
    tm = TM_MIX
    first = lax.rem(i, tiles_per_seq) == 0
    halo = jnp.where(first, 0.0, halo_ref[...])
    a = a_ref[...]
    h_c, gb_c, gc_c, p_c = (a[:, j * CONV_DIM:(j + 1) * CONV_DIM] for j in range(4))

    t0 = 2 * HALO
    n_ext = t0 + tm
    ext_ref[0:HALO, :] = jnp.zeros((HALO, CONV_DIM), F32)

    ext_ref[HALO:t0, :] = halo[:, 2 * CONV_DIM:3 * CONV_DIM] * halo[:, 0:CONV_DIM]
    u = gc_c * h_c
    ext_ref[t0:n_ext, :] = u
    cw = cw_ref[...]
    conv = cw[2:3, :] * u
    conv = conv + cw[1:2, :] * ext_ref[t0 - 1:n_ext - 1, :]
    conv = conv + cw[0:1, :] * ext_ref[t0 - 2:n_ext - 2, :]
    y_conv = gb_c * conv

    ext_ref[HALO:t0, :] = halo[:, OFF_P:OFF_P + POOL_DIM]
    ext_ref[t0:n_ext, :] = p_c
    sums = {}
    step = 1
    while step < POOL_WINDOWS[-1]:
        cur = ext_ref[HALO:n_ext, :] + ext_ref[HALO - step:n_ext - step, :]
        ext_ref[HALO:n_ext, :] = cur
        step *= 2
        sums[step] = ext_ref[t0:n_ext, :]
    tpos = (lax.rem(i, tiles_per_seq) * tm + lax.broadcasted_iota(I32, (tm, POOL_DIM), 0) + 1).astype(F32)
    grp = lax.shift_right_logical(lax.broadcasted_iota(I32, (tm, POOL_DIM), 1), _log2(POOL_GROUP))
    mean = jnp.zeros((tm, POOL_DIM), F32)
    for gi, w in enumerate(POOL_WINDOWS):
        mean = jnp.where(grp == gi, sums[w] / jnp.minimum(tpos, float(w)), mean)
    mixed = mean - p_c
    y_pool = _dot(mixed.astype(BF16), wpool_ref[...]) * ps_ref[...]

    mix = _dot(y_conv.astype(BF16), wout_ref[0:CONV_DIM, :])
    mix = mix + _dot(y_pool.astype(BF16), wout_ref[CONV_DIM:CONV_DIM + POOL_DIM, :])
    mix = mix + _dot(y_ref[...].astype(BF16), wout_ref[CONV_DIM + POOL_DIM:D_MODEL, :])
    z = DN_ALPHA * x_ref[...] + mix
    o_ref[...] = _layernorm_rows(z, g_ref[...], b_ref[...])


def _mixout(a, y_attn, x, cw, wpool_bd, ps, wout, g, b, S):
    T = a.shape[0]
    nt = T // TM_MIX
    tps = S // TM_MIX
    hb = TM_MIX // HALO
    full = lambda i: (0, 0)
    return pl.pallas_call(
        functools.partial(_mixout_kernel, tps),
        grid=(nt,),
        in_specs=[
            pl.BlockSpec((TM_MIX, OFF_Q), lambda i: (i, 0)),
            pl.BlockSpec((HALO, OFF_Q), lambda i: (jnp.maximum(i * hb - 1, 0), 0)),
            pl.BlockSpec((TM_MIX, ATTN_DIM), lambda i: (i, 0)),
            pl.BlockSpec((TM_MIX, D_MODEL), lambda i: (i, 0)),
            pl.BlockSpec((CONV_WIDTH, CONV_DIM), full),
            pl.BlockSpec((POOL_DIM, POOL_DIM), full),
            pl.BlockSpec((1, POOL_DIM), full),
            pl.BlockSpec((D_MODEL, D_MODEL), full),
            pl.BlockSpec((1, D_MODEL), full),
            pl.BlockSpec((1, D_MODEL), full),
        ],
        out_specs=pl.BlockSpec((TM_MIX, D_MODEL), lambda i: (i, 0)),
        out_shape=jax.ShapeDtypeStruct((T, D_MODEL), F32),
        scratch_shapes=[pltpu.VMEM((2 * HALO + TM_MIX, CONV_DIM), F32)],
        compiler_params=_cparams(("arbitrary",)),
        name="mixout",
    )(a, a, y_attn, x, cw, wpool_bd, ps, wout, g, b)


def _router_kernel(x_ref, rw_ref, rb_ref, idx_ref, gate_ref, cnt_ref):
    i = pl.program_id(0)
    tm = TM_ROUTE
    x = x_ref[...]
    w = rw_ref[...]
    xh = x.astype(BF16)
    xl = (x - xh.astype(F32)).astype(BF16)
    wh = w.astype(BF16)
    wl = (w - wh.astype(F32)).astype(BF16)
    logits = (_dot_nt(wh, xh) + (_dot_nt(wh, xl) + _dot_nt(wl, xh))) + rb_ref[...]
    erow = lax.broadcasted_iota(I32, (N_EXPERTS, tm), 0).astype(F32)
    work = logits
    vals, idxs = [], []
    multi = jnp.zeros((N_EXPERTS, tm), F32)
    for _ in range(TOP_K):
        mx = jnp.max(work, axis=0, keepdims=True)
        pick = jnp.min(jnp.where(work == mx, erow, float(N_EXPERTS)), axis=0, keepdims=True)
        hit = erow == pick
        work = jnp.where(hit, -jnp.inf, work)
        multi = multi + hit.astype(F32)
        vals.append(mx)
        idxs.append(pick)
    es = [jnp.exp(v - vals[0]) for v in vals]
    den = es[0] + es[1] + es[2] + es[3]
    idx_ref[...] = jnp.concatenate(idxs, axis=0).astype(I32)
    gate_ref[...] = jnp.concatenate([e / den for e in es] + [jnp.zeros((SUBLANES - TOP_K, tm), F32)], axis=0)
    part = multi[:, 0:LANES]
    for j in range(1, tm // LANES):
        part = part + multi[:, j * LANES:(j + 1) * LANES]

    @pl.when(i == 0)
    def _():
        cnt_ref[...] = jnp.zeros_like(cnt_ref)

    cnt_ref[...] += part


def _router(x1, rwT, rb):
    T = x1.shape[0]
    nt = T // TM_ROUTE
    full = lambda i: (0, 0)
    return pl.pallas_call(
        _router_kernel,
        grid=(nt,),
        in_specs=[
            pl.BlockSpec((TM_ROUTE, D_MODEL), lambda i: (i, 0)),
            pl.BlockSpec((N_EXPERTS, D_MODEL), full),
            pl.BlockSpec((N_EXPERTS, 1), full),
        ],
        out_specs=[
            pl.BlockSpec((TOP_K, TM_ROUTE), lambda i: (0, i)),
            pl.BlockSpec((SUBLANES, TM_ROUTE), lambda i: (0, i)),
            pl.BlockSpec((N_EXPERTS, LANES), full),
        ],
        out_shape=[
            jax.ShapeDtypeStruct((TOP_K, T), I32),
            jax.ShapeDtypeStruct((SUBLANES, T), F32),
            jax.ShapeDtypeStruct((N_EXPERTS, LANES), F32),
        ],
        compiler_params=_cparams(("arbitrary",)),
        name="router",
    )(x1, rwT, rb)


def _ranks_kernel(n_tiles_pad, idx_ref, cnt_ref, dest_ref, meta_ref, tri_ref, start_ref, carry_ref):
    i = pl.program_id(0)
    tm = TM_ROUTE
    erow = lax.broadcasted_iota(I32, (N_EXPERTS, LANES), 0)
    elane = lax.broadcasted_iota(I32, (N_EXPERTS, LANES), 1)

    @pl.when(i == 0)
    def _():
        cnt = jnp.sum(cnt_ref[...], axis=1, keepdims=True)
        cnt_i = jnp.broadcast_to(cnt, (N_EXPERTS, LANES)).astype(I32)
        padded = lax.shift_left(lax.shift_right_logical(cnt_i + (TM_MOE - 1), _log2(TM_MOE)), _log2(TM_MOE))
        r = lax.broadcasted_iota(I32, (N_EXPERTS, N_EXPERTS), 0)
        c = lax.broadcasted_iota(I32, (N_EXPERTS, N_EXPERTS), 1)
        low = (c <= r).astype(F32)
        pad_end = lax.dot_general(low, padded.astype(F32), (((1,), (0,)), ((), ())),
                                  precision=lax.Precision.HIGHEST, preferred_element_type=F32)
        pad_start = pad_end - padded.astype(F32)
        start_ref[...] = pad_start
        carry_ref[...] = jnp.zeros_like(carry_ref)
        a = lax.broadcasted_iota(I32, (tm, tm), 0)
        bcol = lax.broadcasted_iota(I32, (tm, tm), 1)
        tri_ref[...] = (a < bcol).astype(BF16)
        ntp = n_tiles_pad
        tstart = (lax.broadcasted_iota(I32, (N_EXPERTS, ntp), 1) * TM_MOE).astype(F32)
        pe = jnp.concatenate([pad_end] * (ntp // LANES), axis=1)
        texp = jnp.sum((pe <= tstart).astype(F32), axis=0, keepdims=True)
        texp = jnp.minimum(texp, float(N_EXPERTS - 1)).astype(I32)
        total = jnp.max(pad_end, axis=0, keepdims=True)
        n_used = lax.shift_right_logical(total.astype(I32), _log2(TM_MOE))
        zstart = jnp.sum(jnp.where(erow == elane, pad_start + cnt_i.astype(F32), 0.0),
                         axis=0, keepdims=True).astype(I32)
        zlen = jnp.sum(jnp.where(erow == elane, (padded - cnt_i).astype(F32), 0.0),
                       axis=0, keepdims=True).astype(I32)
        lanes_pad = jnp.zeros((1, ntp - LANES), I32)
        meta_ref[...] = jnp.concatenate(
            [texp,
             jnp.concatenate([n_used, lanes_pad], axis=1),
             jnp.concatenate([zstart, lanes_pad], axis=1),
             jnp.concatenate([zlen, lanes_pad], axis=1),
             jnp.zeros((SUBLANES - 4, ntp), I32)], axis=0)

    idx = idx_ref[...]
    erow_t = lax.broadcasted_iota(I32, (N_EXPERTS, tm), 0)
    hits = [erow_t == idx[k:k + 1, :] for k in range(TOP_K)]
    multi = hits[0].astype(F32)
    for k in range(1, TOP_K):
        multi = multi + hits[k].astype(F32)
    prefix = _dot(multi.astype(BF16), tri_ref[...])
    base = jnp.concatenate([carry_ref[...] + start_ref[...]] * (tm // LANES), axis=1)
    tot = prefix + base
    dest_ref[...] = jnp.concatenate(
        [jnp.sum(jnp.where(hits[k], tot, 0.0), axis=0, keepdims=True) for k in range(TOP_K)],
        axis=0).astype(I32)
    carry_ref[...] += jnp.broadcast_to(jnp.sum(multi, axis=1, keepdims=True), (N_EXPERTS, LANES))


def _ranks(idxT, cnt, n_tiles_pad):
    T = idxT.shape[1]
    nt = T // TM_ROUTE
    full = lambda i: (0, 0)
    return pl.pallas_call(
        functools.partial(_ranks_kernel, n_tiles_pad),
        grid=(nt,),
        in_specs=[
            pl.BlockSpec((TOP_K, TM_ROUTE), lambda i: (0, i)),
            pl.BlockSpec((N_EXPERTS, LANES), full),
        ],
        out_specs=[
            pl.BlockSpec((TOP_K, TM_ROUTE), lambda i: (0, i)),
            pl.BlockSpec((SUBLANES, n_tiles_pad), full),
        ],
        out_shape=[
            jax.ShapeDtypeStruct((TOP_K, T), I32),
            jax.ShapeDtypeStruct((SUBLANES, n_tiles_pad), I32),
        ],
        scratch_shapes=[
            pltpu.VMEM((TM_ROUTE, TM_ROUTE), BF16),
            pltpu.VMEM((N_EXPERTS, LANES), F32),
            pltpu.VMEM((N_EXPERTS, LANES), F32),
        ],
        compiler_params=_cparams(("arbitrary",)),
        name="ranks",
    )(idxT, cnt)


RT = D_MODEL // LANES
DMA_UNROLL = 8
COMB_SUB = 8


def _row_copy(src_ref, s, dst_ref, d, sem):
    return pltpu.make_async_copy(src_ref.at[pl.ds(pl.multiple_of(s * RT, RT), RT)],
                                 dst_ref.at[pl.ds(pl.multiple_of(d * RT, RT), RT)], sem)


def _to_row_tiled(dst_ref, val, rows):
    for s in range(RT):
        dst_ref[pl.ds(s, rows, stride=RT), :] = val[:, s * LANES:(s + 1) * LANES]


def _from_row_tiled(src_ref, rows):
    return [src_ref[pl.ds(s, rows, stride=RT), :] for s in range(RT)]


def _scatter_kernel(meta_ref, x_ref, dest_ref, xs_ref, stage_ref, zero_ref, sem):
    i = pl.program_id(0)
    tm = TM_ROUTE

    @pl.when(i == 0)
    def _():
        zero_ref[...] = jnp.zeros_like(zero_ref)

        def zcopy(row, nrows):
            off = pl.multiple_of(row * RT, RT)
            return pltpu.make_async_copy(zero_ref.at[pl.ds(0, nrows * RT)],
                                         xs_ref.at[pl.ds(off, nrows * RT)], sem)

        def pad_fill(wait):
            def body(e, c):
                row = meta_ref[2, e]
                plen = meta_ref[3, e]
                for bit in reversed(range(_log2(TM_MOE))):
                    size = 1 << bit
                    has = (plen & size) != 0

                    @pl.when(has)
                    def _():
                        cp = zcopy(row, size)
                        cp.wait() if wait else cp.start()

                    row = row + jnp.where(has, size, 0)
                return c
            lax.fori_loop(0, N_EXPERTS, body, 0)

        def tail_fill(wait):
            def body(j, c):
                cp = zcopy(j * TM_MOE, TM_MOE)
                cp.wait() if wait else cp.start()
                return c
            lax.fori_loop(meta_ref[1, 0], xs_ref.shape[0] // (TM_MOE * RT), body, 0)

        pad_fill(False)
        tail_fill(False)
        pad_fill(True)
        tail_fill(True)

    _to_row_tiled(stage_ref, x_ref[...], tm)

    def start(t, c):
        for k in range(TOP_K):
            _row_copy(stage_ref, t, xs_ref, dest_ref[k, t], sem).start(priority=k % 2)
        return c

    lax.fori_loop(0, tm, start, 0, unroll=DMA_UNROLL)
    for k in range(TOP_K):
        pltpu.make_async_copy(stage_ref, xs_ref.at[pl.ds(0, tm * RT)], sem).wait()


def _scatter(meta, x1, destT, n_rows_alloc):
    T = x1.shape[0]
    nt = T // TM_ROUTE
    return pl.pallas_call(
        _scatter_kernel,
        grid_spec=pltpu.PrefetchScalarGridSpec(
            num_scalar_prefetch=1,
            grid=(nt,),
            in_specs=[
                pl.BlockSpec((TM_ROUTE, D_MODEL), lambda i, m: (i, 0)),
                pl.BlockSpec((TOP_K, TM_ROUTE), lambda i, m: (0, i), memory_space=pltpu.SMEM),
            ],
            out_specs=pl.BlockSpec(memory_space=pl.ANY),
            scratch_shapes=[
                pltpu.VMEM((TM_ROUTE * RT, LANES), F32),
                pltpu.VMEM((TM_MOE * RT, LANES), F32),
                pltpu.SemaphoreType.DMA(()),
            ],
        ),
        out_shape=jax.ShapeDtypeStruct((n_rows_alloc * RT, LANES), F32),
        compiler_params=_cparams(("arbitrary",)),
        name="scatter",
    )(meta, x1, destT)


def _gmm_kernel(meta_ref, xs_ref, wgu_ref, bgu_ref, wd_ref, bd_ref, ys_ref, wgu_b, wd_b, lhs_ref, act_ref):
    j = pl.program_id(0)
    n_used = meta_ref[1, 0]
    e_now = meta_ref[0, j]
    e_prev = meta_ref[0, jnp.maximum(j - 1, 0)]
    used = j < n_used

    @pl.when(used & ((j == 0) | (e_now != e_prev)))
    def _():
        wgu_b[...] = wgu_ref[0, 0].astype(BF16)
        wd_b[...] = wd_ref[0, 0].astype(BF16)

    @pl.when(used)
    def _():
        for s, piece in enumerate(_from_row_tiled(xs_ref, TM_MOE)):
            lhs_ref[:, s * LANES:(s + 1) * LANES] = piece.astype(BF16)
        x = lhs_ref[...]
        nc = 256
        for c in range(D_FF // nc):
            gate = _dot(x, wgu_b[:, c * nc:(c + 1) * nc]) + bgu_ref[0, 0, :, c * nc:(c + 1) * nc]
            up = (_dot(x, wgu_b[:, D_FF + c * nc:D_FF + (c + 1) * nc])
                  + bgu_ref[0, 0, :, D_FF + c * nc:D_FF + (c + 1) * nc])
            gate = jnp.minimum(gate, SWIGLU_LIMIT)
            up = jnp.clip(up, -SWIGLU_LIMIT, SWIGLU_LIMIT)
            act = gate * jax.nn.sigmoid(SWIGLU_ALPHA * gate) * (up + 1.0)
            act_ref[:, c * nc:(c + 1) * nc] = act.astype(BF16)
        _to_row_tiled(ys_ref, _dot(act_ref[...], wd_b[...]) + bd_ref[0, 0], TM_MOE)

    @pl.when(jnp.logical_not(used))
    def _():
        ys_ref[...] = jnp.zeros_like(ys_ref)


def _gmm(meta, xs, w_gu, b_gu, w_down, b_down, n_tiles, layer):
    last = lambda m: jnp.maximum(m[1, 0] - 1, 0)
    return pl.pallas_call(
        _gmm_kernel,
        grid_spec=pltpu.PrefetchScalarGridSpec(
            num_scalar_prefetch=1,
            grid=(n_tiles,),
            in_specs=[
                pl.BlockSpec((TM_MOE * RT, LANES), lambda j, m: (jnp.minimum(j, last(m)), 0)),
                pl.BlockSpec((1, 1, D_MODEL, 2 * D_FF), lambda j, m: (layer, m[0, j], 0, 0)),
                pl.BlockSpec((1, 1, 1, 2 * D_FF), lambda j, m: (layer, m[0, j], 0, 0)),
                pl.BlockSpec((1, 1, D_FF, D_MODEL), lambda j, m: (layer, m[0, j], 0, 0)),
                pl.BlockSpec((1, 1, 1, D_MODEL), lambda j, m: (layer, m[0, j], 0, 0)),
            ],
            out_specs=pl.BlockSpec((TM_MOE * RT, LANES), lambda j, m: (j, 0)),
            scratch_shapes=[
                pltpu.VMEM((D_MODEL, 2 * D_FF), BF16),
                pltpu.VMEM((D_FF, D_MODEL), BF16),
                pltpu.VMEM((TM_MOE, D_MODEL), BF16),
                pltpu.VMEM((TM_MOE, D_FF), BF16),
            ],
        ),
        out_shape=jax.ShapeDtypeStruct((n_tiles * TM_MOE * RT, LANES), F32),
        compiler_params=_cparams(("arbitrary",)),
        name="gmm",
    )(meta, xs, w_gu, b_gu, w_down, b_down)


def _combine_kernel(x_ref, dest_ref, gate_ref, ys_ref, g_ref, b_ref, o_ref, ob_ref, buf_ref, sem):
    tm = TM_COMB
    sub = tm // COMB_SUB

    def issue(sb):
        def start(t, c):
            for k in range(TOP_K):
                _row_copy(ys_ref, dest_ref[k, t], buf_ref.at[k], t, sem.at[sb]).start(priority=k % 2)
            return c
        lax.fori_loop(sb * sub, (sb + 1) * sub, start, 0, unroll=True)

    def wait(sb):
        for k in range(TOP_K):
            pltpu.make_async_copy(ys_ref.at[pl.ds(0, sub * RT)],
                                  buf_ref.at[k, pl.ds(sb * sub * RT, sub * RT)], sem.at[sb]).wait()

    gates = jnp.concatenate(
        [gate_ref[...], jnp.zeros((LANES - SUBLANES, tm), F32)], axis=0).T
    issue(0)
    for sb in range(COMB_SUB):
        wait(sb)
        rows = slice(sb * sub, (sb + 1) * sub)
        x = x_ref[rows, :]
        loaded = [[buf_ref[k, pl.ds(sb * sub * RT + s, sub, stride=RT), :] for s in range(RT)]
                  for k in range(TOP_K)]
        if sb + 1 < COMB_SUB:
            issue(sb + 1)
        pieces = None
        for k in range(TOP_K):
            gk = gates[rows, k:k + 1]
            scaled = [p * gk for p in loaded[k]]
            pieces = scaled if pieces is None else [a + r for a, r in zip(pieces, scaled)]
        z = DN_ALPHA * x + jnp.concatenate(pieces, axis=1)
        out = _layernorm_rows(z, g_ref[...], b_ref[...])
        o_ref[rows, :] = out
        ob_ref[rows, :] = out.astype(BF16)


def _combine(x1, destT, gateT, ys, g, b):
    T = x1.shape[0]
    nt = T // TM_COMB
    full = lambda i: (0, 0)
    return pl.pallas_call(
        _combine_kernel,
        grid=(nt,),
        in_specs=[
            pl.BlockSpec((TM_COMB, D_MODEL), lambda i: (i, 0)),
            pl.BlockSpec((TOP_K, TM_COMB), lambda i: (0, i), memory_space=pltpu.SMEM),
            pl.BlockSpec((SUBLANES, TM_COMB), lambda i: (0, i)),
            pl.BlockSpec(memory_space=pl.ANY),
            pl.BlockSpec((1, D_MODEL), full),
            pl.BlockSpec((1, D_MODEL), full),
        ],
        out_specs=[
            pl.BlockSpec((TM_COMB, D_MODEL), lambda i: (i, 0)),
            pl.BlockSpec((TM_COMB, D_MODEL), lambda i: (i, 0)),
        ],
        out_shape=[
            jax.ShapeDtypeStruct((T, D_MODEL), F32),
            jax.ShapeDtypeStruct((T, D_MODEL), BF16),
        ],
        scratch_shapes=[
            pltpu.VMEM((TOP_K, TM_COMB * RT, LANES), F32),
            pltpu.SemaphoreType.DMA((COMB_SUB,)),
        ],
        compiler_params=_cparams(("arbitrary",)),
        name="combine",
    )(x1, destT, gateT, ys, g, b)


def _block_diag(w):
    g, c, _ = w.shape
    out = jnp.zeros((g * c, g * c), w.dtype)
    for i in range(g):
        out = out.at[i * c:(i + 1) * c, i * c:(i + 1) * c].set(w[i])
    return out


def kernel(x, w_in, conv_w, w_pool, pool_scale, idx_kn_g, idx_kn_b, w_out, ln1_g, ln1_b, router_w,
           router_b, w_gu, b_gu, w_down, b_down, ln2_g, ln2_b):
    B, S, D = x.shape
    T = B * S
    depth = w_in.shape[0]
    n_pairs = T * TOP_K
    n_tiles = (n_pairs + N_EXPERTS * (TM_MOE - 1)) // TM_MOE + 1
    n_tiles_pad = ((n_tiles + LANES - 1) // LANES) * LANES
    n_rows_alloc = (n_tiles + 1) * TM_MOE

    xf = x.reshape(T, D)
    xb = xf
    for l in range(depth):
        wl = w_in[l]
        wn = jnp.concatenate(
            [wl[:, 0:OFF_V], wl[:, OFF_KI:OFF_WI], jnp.zeros((D, LANES - IDX_DIM), F32)], axis=1).astype(BF16)
        wt = jnp.concatenate(
            [wl[:, OFF_V:OFF_QI], wl[:, OFF_QI:OFF_KI], wl[:, OFF_WI:D_IN],
             jnp.zeros((D, IDXT_ROWS - IDX_HEADS * IDX_DIM - IDX_HEADS), F32)], axis=1).T.astype(BF16)
        a, q, k, ki, vt, it = _inproj(xb, wn, wt, idx_kn_g[l].reshape(1, -1), idx_kn_b[l].reshape(1, -1))
        y_attn = _attn(it, ki, q, k, vt, B, S)
        x1 = _mixout(a, y_attn, xf, conv_w[l], _block_diag(w_pool[l]).astype(BF16),
                     pool_scale[l].reshape(1, -1), w_out[l].astype(BF16),
                     ln1_g[l].reshape(1, -1), ln1_b[l].reshape(1, -1), S)
        idxT, gateT, cnt = _router(x1, router_w[l].T, router_b[l].reshape(-1, 1))
        destT, meta = _ranks(idxT, cnt, n_tiles_pad)
        xs = _scatter(meta, x1, destT, n_rows_alloc)
        ys = _gmm(meta, xs, w_gu, b_gu.reshape(depth, N_EXPERTS, 1, -1), w_down,
                  b_down.reshape(depth, N_EXPERTS, 1, -1), n_tiles, l)
        xf, xb = _combine(x1, destT, gateT, ys, ln2_g[l].reshape(1, -1), ln2_b[l].reshape(1, -1))
    return xf.reshape(B, S, D)
```

```python
import functools

import jax
import jax.numpy as jnp
import numpy as np
from jax import lax
from jax.experimental import pallas as pl
from jax.experimental.pallas import tpu as pltpu

F32 = jnp.float32
BF16 = jnp.bfloat16
I32 = jnp.int32

D_MODEL = 1024
CONV_DIM = 256
CONV_WIDTH = 3
POOL_WINDOWS = (2, 4, 8, 16)
POOL_GROUP = 64
POOL_DIM = 256
ATTN_HEADS = 8
HEAD_DIM = 64
ATTN_DIM = 512
IDX_HEADS = 8
IDX_DIM = 32
INDEX_TOPK_MAX = 256
Q_BLOCK = 128
N_EXPERTS = 32
TOP_K = 4
D_FF = 1024
SWIGLU_ALPHA = 1.702
SWIGLU_LIMIT = 7.0
LN_EPS = 1e-5
DEPTH = 2
DN_ALPHA = (2.0 * DEPTH) ** 0.25

OFF_P = 3 * CONV_DIM
OFF_Q = OFF_P + POOL_DIM
OFF_K = OFF_Q + ATTN_DIM
OFF_V = OFF_K + ATTN_DIM
OFF_QI = OFF_V + ATTN_DIM
OFF_KI = OFF_QI + IDX_HEADS * IDX_DIM
OFF_WI = OFF_KI + IDX_DIM
D_IN = OFF_WI + IDX_HEADS

LANES = 128
SUBLANES = 8
VMEM_LIMIT = 56 * 1024 * 1024
KEY_CHUNK = 512
TIE_GROUP = 128
HALO = 16
TM_PROJ = 512
TM_MIX = 512
TM_ROUTE = 512
TM_MOE = 512
TM_COMB = 512
IDXT_ROWS = IDX_HEADS * IDX_DIM + 16
WT_ROWS = ATTN_DIM + IDXT_ROWS
INT_MIN = -(2 ** 31)
NEG_BIG = -1e30


def _cparams(sem, flags=None):
    return pltpu.CompilerParams(dimension_semantics=sem, vmem_limit_bytes=VMEM_LIMIT, flags=flags)


def _log2(n):
    k = n.bit_length() - 1
    assert 1 << k == n
    return k


def _dot(a, b):
    return jnp.dot(a, b, preferred_element_type=F32)


def _dot_nt(a, b):
    return lax.dot_general(a, b, (((1,), (1,)), ((), ())), preferred_element_type=F32)


def _tree(x, op):
    parts = [x[j * SUBLANES:(j + 1) * SUBLANES, :] for j in range(x.shape[0] // SUBLANES)]
    while len(parts) > 1:
        nxt = [op(parts[j], parts[j + 1]) for j in range(0, len(parts) - 1, 2)]
        if len(parts) % 2:
            nxt.append(parts[-1])
        parts = nxt
    return parts[0]


def _layernorm_rows(z, g, b):
    mu = jnp.mean(z, axis=-1, keepdims=True)
    d = z - mu
    var = jnp.mean(d * d, axis=-1, keepdims=True)
    return d * lax.rsqrt(var + LN_EPS) * g + b


def _inproj_kernel(x_ref, wn_ref, wt_ref, kg_ref, kb_ref,
                   a_ref, q_ref, k_ref, ki_ref, vt_ref, it_ref):
    x = x_ref[...].astype(BF16)
    a_ref[...] = _dot(x, wn_ref[:, 0:OFF_Q])
    q_ref[...] = (_dot(x, wn_ref[:, OFF_Q:OFF_K]) * (HEAD_DIM ** -0.5)).astype(BF16)
    k_ref[...] = _dot(x, wn_ref[:, OFF_K:OFF_V]).astype(BF16)
    ki = _dot(x, wn_ref[:, OFF_V:OFF_V + LANES])[:, 0:IDX_DIM]
    ki_ref[...] = _layernorm_rows(ki, kg_ref[...], kb_ref[...])
    t = _dot_nt(wt_ref[...], x)
    for c in range(TM_PROJ // KEY_CHUNK):
        vt_ref[c] = t[0:ATTN_DIM, c * KEY_CHUNK:(c + 1) * KEY_CHUNK].astype(BF16)
    it_ref[...] = t[ATTN_DIM:WT_ROWS, :]


def _inproj(xb, wn, wt, kg, kb):
    T = xb.shape[0]
    nt = T // TM_PROJ
    cpt = TM_PROJ // KEY_CHUNK
    full = lambda i: (0, 0)
    return pl.pallas_call(
        _inproj_kernel,
        grid=(nt,),
        in_specs=[
            pl.BlockSpec((TM_PROJ, D_MODEL), lambda i: (i, 0)),
            pl.BlockSpec(wn.shape, full),
            pl.BlockSpec(wt.shape, full),
            pl.BlockSpec((1, IDX_DIM), full),
            pl.BlockSpec((1, IDX_DIM), full),
        ],
        out_specs=[
            pl.BlockSpec((TM_PROJ, OFF_Q), lambda i: (i, 0)),
            pl.BlockSpec((TM_PROJ, ATTN_DIM), lambda i: (i, 0)),
            pl.BlockSpec((TM_PROJ, ATTN_DIM), lambda i: (i, 0)),
            pl.BlockSpec((TM_PROJ, IDX_DIM), lambda i: (i, 0)),
            pl.BlockSpec((cpt, ATTN_DIM, KEY_CHUNK), lambda i: (i, 0, 0)),
            pl.BlockSpec((IDXT_ROWS, TM_PROJ), lambda i: (0, i)),
        ],
        out_shape=[
            jax.ShapeDtypeStruct((T, OFF_Q), F32),
            jax.ShapeDtypeStruct((T, ATTN_DIM), BF16),
            jax.ShapeDtypeStruct((T, ATTN_DIM), BF16),
            jax.ShapeDtypeStruct((T, IDX_DIM), F32),
            jax.ShapeDtypeStruct((T // KEY_CHUNK, ATTN_DIM, KEY_CHUNK), BF16),
            jax.ShapeDtypeStruct((IDXT_ROWS, T), F32),
        ],
        compiler_params=_cparams(("arbitrary",)),
        name="inproj",
    )(xb, wn, wt, kg, kb)


FIELD_BITS = 8
DIGIT_BITS = FIELD_BITS - 1
FIELDS = 32 // FIELD_BITS
DIGIT_SHIFTS = (25, 18, 11, 4, 0)
FIELD_ONES = 0x01010101
FIELD_GUARDS = 0x80808080 - (1 << 32)


def _pack_fields(f):
    q = f.shape[0] // FIELDS
    w = f[0:q]
    for j in range(1, FIELDS):
        w = w | lax.shift_left(f[j * q:(j + 1) * q], FIELD_BITS * j)
    return w | jnp.int32(FIELD_GUARDS)


def _count_fields_ge(words_ref, nch, cand):
    wq = 2 * (KEY_CHUNK // FIELDS)
    assert words_ref.shape[0] // SUBLANES < (1 << FIELD_BITS)
    cvec = cand * jnp.int32(FIELD_ONES)

    def body(c, acc):
        base = pl.multiple_of(c * wq, wq)
        w = words_ref[pl.ds(base, wq), :]
        hit = lax.shift_right_logical(w - cvec, DIGIT_BITS) & jnp.int32(FIELD_ONES)
        return acc + _tree(hit, jnp.add)

    acc = lax.fori_loop(0, lax.shift_right_logical(nch + 1, 1), body,
                        jnp.zeros((SUBLANES, words_ref.shape[1]), I32))
    tot = acc & 255
    for j in range(1, FIELDS):
        tot = tot + (lax.shift_right_logical(acc, FIELD_BITS * j) & 255)
    return jnp.sum(tot.astype(F32), axis=0, keepdims=True)


def _topk_threshold(keys_ref, words_ref, nch, n_sel, n_causal):
    lanes = keys_ref.shape[1]
    wq = KEY_CHUNK // FIELDS
    want = jnp.full((1, lanes), float(n_sel), F32)
    bucket = n_causal.astype(F32)
    prefix = None
    for idx, shift in enumerate(DIGIT_SHIFTS):
        width = (DIGIT_SHIFTS[idx - 1] if idx else 32) - shift
        if idx:
            def prep(c, carry, shift=shift, width=width, prefix=prefix):
                key = keys_ref[pl.ds(pl.multiple_of(c * KEY_CHUNK, KEY_CHUNK), KEY_CHUNK), :]
                digit = (key >> shift) & jnp.int32((1 << width) - 1)
                fields = jnp.where((key >> (shift + width)) == prefix, digit, 0)
                words_ref[pl.ds(pl.multiple_of(c * wq, wq), wq), :] = _pack_fields(fields)
                return carry

            lax.fori_loop(0, nch, prep, 0)
        dig = jnp.zeros((1, lanes), I32)
        at_dig = bucket
        above = jnp.zeros((1, lanes), F32)
        for bit in reversed(range(width)):
            cand = dig + (1 << bit)
            cnt = _count_fields_ge(words_ref, nch, cand)
            ok = cnt >= want
            dig = jnp.where(ok, cand, dig)
            at_dig = jnp.where(ok, cnt, at_dig)
            above = jnp.where(ok, above, cnt)
        want = want - above
        bucket = at_dig - above
        if idx:
            prefix = lax.shift_left(prefix, width) | dig
        else:
            prefix = dig - (1 << (DIGIT_BITS - 1))
    all_taken = n_causal <= n_sel
    tau = jnp.where(all_taken, jnp.int32(INT_MIN + 1), prefix)
    return tau, want, jnp.where(all_taken, 0.0, bucket - want)


def _attn_kernel(n_sel, it_ref, ki_ref, q_ref, k_ref, vt_ref, o_ref,
                 keys_ref, words_ref, rhs_ref, acc_ref, lg_ref):
    i = pl.program_id(1)
    nch = lax.shift_right_logical(i * Q_BLOCK + Q_BLOCK + KEY_CHUNK - 1, _log2(KEY_CHUNK))
    q0 = i * Q_BLOCK
    KC = KEY_CHUNK
    row = lax.broadcasted_iota(I32, (KC, Q_BLOCK), 0)
    lane = lax.broadcasted_iota(I32, (KC, Q_BLOCK), 1)
    qpos = q0 + lane

    qcat = jnp.concatenate(
        [it_ref[h * IDX_DIM:(h + 1) * IDX_DIM, :] for h in range(IDX_HEADS)], axis=1).astype(BF16)
    w_all = it_ref[IDX_HEADS * IDX_DIM:IDX_HEADS * IDX_DIM + IDX_HEADS, :] * (
        (IDX_HEADS ** -0.5) * (IDX_DIM ** -0.5))

    def score_body(c, carry):
        base = pl.multiple_of(c * KC, KC)
        kic = ki_ref[pl.ds(base, KC), :].astype(BF16)
        s = _dot(kic, qcat)
        sc = jnp.zeros((KC, Q_BLOCK), F32)
        for h in range(IDX_HEADS):
            sc = sc + jnp.maximum(s[:, h * Q_BLOCK:(h + 1) * Q_BLOCK], 0.0) * w_all[h:h + 1, :]
        bits = lax.bitcast_convert_type(sc, I32)
        key = bits ^ ((bits >> 31) & jnp.int32(0x7FFFFFFF))
        key = jnp.where(bits == jnp.int32(INT_MIN), jnp.int32(0), key)
        causal = base + row <= qpos
        keys_ref[pl.ds(base, KC), :] = jnp.where(causal, key, jnp.int32(INT_MIN))
        top = jnp.where(causal, (key >> DIGIT_SHIFTS[0]) + (1 << (DIGIT_BITS - 1)), 0)
        words_ref[pl.ds(pl.multiple_of(c * (KC // FIELDS), KC // FIELDS), KC // FIELDS), :] = _pack_fields(top)
        return carry

    lax.fori_loop(0, nch, score_body, 0)

    @pl.when(lax.rem(nch, 2) == 1)
    def _():
        wq = KC // FIELDS
        words_ref[pl.ds(pl.multiple_of(nch * wq, wq), wq), :] = _pack_fields(jnp.zeros((KC, Q_BLOCK), I32))

    n_causal = q0 + lax.broadcasted_iota(I32, (1, Q_BLOCK), 1) + 1
    tau, keep, excess = _topk_threshold(keys_ref, words_ref, nch, n_sel, n_causal)

    @pl.when(jnp.max(excess) > 0.0)
    def _():
        g = TIE_GROUP
        r = lax.broadcasted_iota(I32, (g, g), 0)
        cc = lax.broadcasted_iota(I32, (g, g), 1)
        tri = (cc <= r).astype(BF16)

        def retire(c, before):
            base = pl.multiple_of(c * KC, KC)
            blk = keys_ref[pl.ds(base, KC), :]
            tied = blk == tau
            ones = jnp.where(tied, 1.0, 0.0).astype(BF16)
            local = [_dot(tri, ones[j * g:(j + 1) * g, :]) for j in range(KC // g)]
            upto = []
            for lj in local:
                upto.append(lj + before)
                before = before + lj[g - 1:g, :]
            upto = jnp.concatenate(upto, axis=0)
            keys_ref[pl.ds(base, KC), :] = jnp.where(tied & (upto > keep), jnp.int32(INT_MIN), blk)
            return before

        lax.fori_loop(0, nch, retire, jnp.zeros((1, Q_BLOCK), F32))

    q = q_ref[...]
    lane_q = lax.broadcasted_iota(I32, (Q_BLOCK, 2 * HEAD_DIM), 1)
    for p in range(ATTN_HEADS // 2):
        qp = q[:, p * 2 * HEAD_DIM:(p + 1) * 2 * HEAD_DIM]
        rhs_ref[p, 0:Q_BLOCK, :] = jnp.where(lane_q < HEAD_DIM, qp, jnp.zeros_like(qp))
        rhs_ref[p, Q_BLOCK:2 * Q_BLOCK, :] = jnp.where(lane_q >= HEAD_DIM, qp, jnp.zeros_like(qp))
    acc_ref[...] = jnp.zeros_like(acc_ref)

    def stage_a_prep(c):
        base = pl.multiple_of(c * KC, KC)
        tk = (base + row - (q0 + Q_BLOCK - 1)).astype(F32)
        tkm = jnp.where(keys_ref[pl.ds(base, KC), :] >= tau, tk, NEG_BIG)
        return tkm, k_ref[pl.ds(base, KC), :]

    def stage_a_pair(c, p, tkm, kc):
        l2 = _dot_nt(kc[:, p * 2 * HEAD_DIM:(p + 1) * 2 * HEAD_DIM], rhs_ref[p])
        mcs = []
        for hh in range(2):
            h = 2 * p + hh
            slope = 2.0 ** (-8.0 * (h + 1) / ATTN_HEADS)
            lg = l2[:, hh * Q_BLOCK:(hh + 1) * Q_BLOCK] + slope * tkm
            lg_ref[h] = lg
            mcs.append(jnp.max(_tree(lg, jnp.maximum), axis=0, keepdims=True))
        return mcs

    def stage_b_pair(c, p, m_new, alpha):
        sums = []
        for hh in range(2):
            h = 2 * p + hh
            pr = jnp.exp(lg_ref[h] - m_new[h:h + 1, :])
            sums.append(jnp.sum(_tree(pr, jnp.add), axis=0, keepdims=True))
            pv = _dot(vt_ref[c, h * HEAD_DIM:(h + 1) * HEAD_DIM, :], pr.astype(BF16))
            acc_ref[h * HEAD_DIM:(h + 1) * HEAD_DIM, :] = (
                acc_ref[h * HEAD_DIM:(h + 1) * HEAD_DIM, :] * alpha[h:h + 1, :] + pv)
        return sums

    def stage_b_all(c, m_old, l_old, mc):
        m_new = jnp.maximum(m_old, mc)
        alpha = jnp.exp(m_old - m_new)
        return m_new, alpha

    tkm0, kc0 = stage_a_prep(0)
    mc0 = jnp.concatenate(sum([stage_a_pair(0, p, tkm0, kc0) for p in range(ATTN_HEADS // 2)], []), axis=0)

    def attn_body(c, carry):
        m_old, l_old, mc_prev = carry
        m_new, alpha = stage_b_all(c - 1, m_old, l_old, mc_prev)
        tkm, kc = stage_a_prep(c)
        mcs, sums = [], []
        for p in range(ATTN_HEADS // 2):
            sums += stage_b_pair(c - 1, p, m_new, alpha)
            mcs += stage_a_pair(c, p, tkm, kc)
        return m_new, l_old * alpha + jnp.concatenate(sums, axis=0), jnp.concatenate(mcs, axis=0)

    m0 = jnp.full((ATTN_HEADS, Q_BLOCK), NEG_BIG, F32)
    l0 = jnp.zeros((ATTN_HEADS, Q_BLOCK), F32)
    m_old, l_old, mc_prev = lax.fori_loop(1, nch, attn_body, (m0, l0, mc0))
    m_new, alpha = stage_b_all(nch - 1, m_old, l_old, mc_prev)
    sums = sum([stage_b_pair(nch - 1, p, m_new, alpha) for p in range(ATTN_HEADS // 2)], [])
    l_fin = l_old * alpha + jnp.concatenate(sums, axis=0)

    outs = []
    for h in range(ATTN_HEADS):
        outs.append(acc_ref[h * HEAD_DIM:(h + 1) * HEAD_DIM, :] / l_fin[h:h + 1, :])
    o_ref[...] = jnp.concatenate(outs, axis=0).T


def _attn(it, ki, q, k, vt, B, S):
    T = B * S
    nb = S // Q_BLOCK
    n_sel = min(INDEX_TOPK_MAX, S // 4)
    return pl.pallas_call(
        functools.partial(_attn_kernel, n_sel),
        grid=(B, nb),
        in_specs=[
            pl.BlockSpec((IDXT_ROWS, Q_BLOCK), lambda b, i: (0, b * nb + i)),
            pl.BlockSpec((S, IDX_DIM), lambda b, i: (b, 0)),
            pl.BlockSpec((Q_BLOCK, ATTN_DIM), lambda b, i: (b * nb + i, 0)),
            pl.BlockSpec((S, ATTN_DIM), lambda b, i: (b, 0)),
            pl.BlockSpec((S // KEY_CHUNK, ATTN_DIM, KEY_CHUNK), lambda b, i: (b, 0, 0)),
        ],
        out_specs=pl.BlockSpec((Q_BLOCK, ATTN_DIM), lambda b, i: (b * nb + i, 0)),
        out_shape=jax.ShapeDtypeStruct((T, ATTN_DIM), F32),
        scratch_shapes=[
            pltpu.VMEM((S, Q_BLOCK), I32),
            pltpu.VMEM((-(-(S // KEY_CHUNK) // 2) * 2 * (KEY_CHUNK // FIELDS), Q_BLOCK), I32),
            pltpu.VMEM((ATTN_HEADS // 2, 2 * Q_BLOCK, 2 * HEAD_DIM), BF16),
            pltpu.VMEM((ATTN_DIM, Q_BLOCK), F32),
            pltpu.VMEM((ATTN_HEADS, KEY_CHUNK, Q_BLOCK), F32),
        ],
        compiler_params=_cparams(("arbitrary", "arbitrary")),
        name="attn",
    )(it, ki, q, k, vt)


def _mixout_kernel(tiles_per_seq, a_ref, halo_ref, y_ref, x_ref, cw_ref, wpool_ref, ps_ref,
                   wout_ref, g_ref, b_ref, o_ref, ext_ref):
    i = pl.program_id(0)
    tm = TM_MIX
    first = lax.rem(i, tiles_per_seq) == 0
    halo = jnp.where(first, 0.0, halo_ref[...])
    a = a_ref[...]
    h_c, gb_c, gc_c, p_c = (a[:, j * CONV_DIM:(j + 1) * CONV_DIM] for j in range(4))

    t0 = 2 * HALO
    n_ext = t0 + tm
    ext_ref[0:HALO, :] = jnp.zeros((HALO, CONV_DIM), F32)

    ext_ref[HALO:t0, :] = halo[:, 2 * CONV_DIM:3 * CONV_DIM] * halo[:, 0:CONV_DIM]
    u = gc_c * h_c
    ext_ref[t0:n_ext, :] = u
    cw = cw_ref[...]
    conv = cw[2:3, :] * u
    conv = conv + cw[1:2, :] * ext_ref[t0 - 1:n_ext - 1, :]
    conv = conv + cw[0:1, :] * ext_ref[t0 - 2:n_ext - 2, :]
    y_conv = gb_c * conv

    ext_ref[HALO:t0, :] = halo[:, OFF_P:OFF_P + POOL_DIM]
    ext_ref[t0:n_ext, :] = p_c
    sums = {}
    step = 1
    while step < POOL_WINDOWS[-1]:
        cur = ext_ref[HALO:n_ext, :] + ext_ref[HALO - step:n_ext - step, :]
        ext_ref[HALO:n_ext, :] = cur
        step *= 2
        sums[step] = ext_ref[t0:n_ext, :]
    tpos = (lax.rem(i, tiles_per_seq) * tm + lax.broadcasted_iota(I32, (tm, POOL_DIM), 0) + 1).astype(F32)
    grp = lax.shift_right_logical(lax.broadcasted_iota(I32, (tm, POOL_DIM), 1), _log2(POOL_GROUP))
    mean = jnp.zeros((tm, POOL_DIM), F32)
    for gi, w in enumerate(POOL_WINDOWS):
        mean = jnp.where(grp == gi, sums[w] / jnp.minimum(tpos, float(w)), mean)
    mixed = mean - p_c
    y_pool = _dot(mixed.astype(BF16), wpool_ref[...]) * ps_ref[...]

    mix = _dot(y_conv.astype(BF16), wout_ref[0:CONV_DIM, :])
    mix = mix + _dot(y_pool.astype(BF16), wout_ref[CONV_DIM:CONV_DIM + POOL_DIM, :])
    mix = mix + _dot(y_ref[...].astype(BF16), wout_ref[CONV_DIM + POOL_DIM:D_MODEL, :])
    z = DN_ALPHA * x_ref[...] + mix
    o_ref[...] = _layernorm_rows(z, g_ref[...], b_ref[...])


def _mixout(a, y_attn, x, cw, wpool_bd, ps, wout, g, b, S):
    T = a.shape[0]
    nt = T // TM_MIX
    tps = S // TM_MIX
    hb = TM_MIX // HALO
    full = lambda i: (0, 0)
    return pl.pallas_call(
        functools.partial(_mixout_kernel, tps),
        grid=(nt,),
        in_specs=[
            pl.BlockSpec((TM_MIX, OFF_Q), lambda i: (i, 0)),
            pl.BlockSpec((HALO, OFF_Q), lambda i: (jnp.maximum(i * hb - 1, 0), 0)),
            pl.BlockSpec((TM_MIX, ATTN_DIM), lambda i: (i, 0)),
            pl.BlockSpec((TM_MIX, D_MODEL), lambda i: (i, 0)),
            pl.BlockSpec((CONV_WIDTH, CONV_DIM), full),
            pl.BlockSpec((POOL_DIM, POOL_DIM), full),
            pl.BlockSpec((1, POOL_DIM), full),
            pl.BlockSpec((D_MODEL, D_MODEL), full),
            pl.BlockSpec((1, D_MODEL), full),
            pl.BlockSpec((1, D_MODEL), full),
        ],
        out_specs=pl.BlockSpec((TM_MIX, D_MODEL), lambda i: (i, 0)),
        out_shape=jax.ShapeDtypeStruct((T, D_MODEL), F32),
        scratch_shapes=[pltpu.VMEM((2 * HALO + TM_MIX, CONV_DIM), F32)],
        compiler_params=_cparams(("arbitrary",)),
        name="mixout",
    )(a, a, y_attn, x, cw, wpool_bd, ps, wout, g, b)


def _router_kernel(x_ref, rw_ref, rb_ref, idx_ref, gate_ref, cnt_ref):
    i = pl.program_id(0)
    tm = TM_ROUTE
    x = x_ref[...]
    w = rw_ref[...]
    xh = x.astype(BF16)
    xl = (x - xh.astype(F32)).astype(BF16)
    wh = w.astype(BF16)
    wl = (w - wh.astype(F32)).astype(BF16)
    logits = (_dot_nt(wh, xh) + (_dot_nt(wh, xl) + _dot_nt(wl, xh))) + rb_ref[...]
    erow = lax.broadcasted_iota(I32, (N_EXPERTS, tm), 0).astype(F32)
    work = logits
    vals, idxs = [], []
    multi = jnp.zeros((N_EXPERTS, tm), F32)
    for _ in range(TOP_K):
        mx = jnp.max(work, axis=0, keepdims=True)
        pick = jnp.min(jnp.where(work == mx, erow, float(N_EXPERTS)), axis=0, keepdims=True)
        hit = erow == pick
        work = jnp.where(hit, -jnp.inf, work)
        multi = multi + hit.astype(F32)
        vals.append(mx)
        idxs.append(pick)
    es = [jnp.exp(v - vals[0]) for v in vals]
    den = es[0] + es[1] + es[2] + es[3]
    idx_ref[...] = jnp.concatenate(idxs, axis=0).astype(I32)
    gate_ref[...] = jnp.concatenate([e / den for e in es] + [jnp.zeros((SUBLANES - TOP_K, tm), F32)], axis=0)
    part = multi[:, 0:LANES]
    for j in range(1, tm // LANES):
        part = part + multi[:, j * LANES:(j + 1) * LANES]

    @pl.when(i == 0)
    def _():
        cnt_ref[...] = jnp.zeros_like(cnt_ref)

    cnt_ref[...] += part


def _router(x1, rwT, rb):
    T = x1.shape[0]
    nt = T // TM_ROUTE
    full = lambda i: (0, 0)
    return pl.pallas_call(
        _router_kernel,
        grid=(nt,),
        in_specs=[
            pl.BlockSpec((TM_ROUTE, D_MODEL), lambda i: (i, 0)),
            pl.BlockSpec((N_EXPERTS, D_MODEL), full),
            pl.BlockSpec((N_EXPERTS, 1), full),
        ],
        out_specs=[
            pl.BlockSpec((TOP_K, TM_ROUTE), lambda i: (0, i)),
            pl.BlockSpec((SUBLANES, TM_ROUTE), lambda i: (0, i)),
            pl.BlockSpec((N_EXPERTS, LANES), full),
        ],
        out_shape=[
            jax.ShapeDtypeStruct((TOP_K, T), I32),
            jax.ShapeDtypeStruct((SUBLANES, T), F32),
            jax.ShapeDtypeStruct((N_EXPERTS, LANES), F32),
        ],
        compiler_params=_cparams(("arbitrary",)),
        name="router",
    )(x1, rwT, rb)


def _ranks_kernel(n_tiles_pad, idx_ref, cnt_ref, dest_ref, meta_ref, tri_ref, start_ref, carry_ref):
    i = pl.program_id(0)
    tm = TM_ROUTE
    erow = lax.broadcasted_iota(I32, (N_EXPERTS, LANES), 0)
    elane = lax.broadcasted_iota(I32, (N_EXPERTS, LANES), 1)

    @pl.when(i == 0)
    def _():
        cnt = jnp.sum(cnt_ref[...], axis=1, keepdims=True)
        cnt_i = jnp.broadcast_to(cnt, (N_EXPERTS, LANES)).astype(I32)
        padded = lax.shift_left(lax.shift_right_logical(cnt_i + (TM_MOE - 1), _log2(TM_MOE)), _log2(TM_MOE))
        r = lax.broadcasted_iota(I32, (N_EXPERTS, N_EXPERTS), 0)
        c = lax.broadcasted_iota(I32, (N_EXPERTS, N_EXPERTS), 1)
        low = (c <= r).astype(F32)
        pad_end = lax.dot_general(low, padded.astype(F32), (((1,), (0,)), ((), ())),
                                  precision=lax.Precision.HIGHEST, preferred_element_type=F32)
        pad_start = pad_end - padded.astype(F32)
        start_ref[...] = pad_start
        carry_ref[...] = jnp.zeros_like(carry_ref)
        a = lax.broadcasted_iota(I32, (tm, tm), 0)
        bcol = lax.broadcasted_iota(I32, (tm, tm), 1)
        tri_ref[...] = (a < bcol).astype(BF16)
        ntp = n_tiles_pad
        tstart = (lax.broadcasted_iota(I32, (N_EXPERTS, ntp), 1) * TM_MOE).astype(F32)
        pe = jnp.concatenate([pad_end] * (ntp // LANES), axis=1)
        texp = jnp.sum((pe <= tstart).astype(F32), axis=0, keepdims=True)
        texp = jnp.minimum(texp, float(N_EXPERTS - 1)).astype(I32)
        total = jnp.max(pad_end, axis=0, keepdims=True)
        n_used = lax.shift_right_logical(total.astype(I32), _log2(TM_MOE))
        zstart = jnp.sum(jnp.where(erow == elane, pad_start + cnt_i.astype(F32), 0.0),
                         axis=0, keepdims=True).astype(I32)
        zlen = jnp.sum(jnp.where(erow == elane, (padded - cnt_i).astype(F32), 0.0),
                       axis=0, keepdims=True).astype(I32)
        etiles = lax.shift_right_logical(
            jnp.sum(jnp.where(erow == elane, padded.astype(F32), 0.0), axis=0, keepdims=True).astype(I32),
            _log2(TM_MOE))
        lanes_pad = jnp.zeros((1, ntp - LANES), I32)
        meta_ref[...] = jnp.concatenate(
            [texp,
             jnp.concatenate([n_used, lanes_pad], axis=1),
             jnp.concatenate([zstart, lanes_pad], axis=1),
             jnp.concatenate([zlen, lanes_pad], axis=1),
             jnp.concatenate([etiles, lanes_pad], axis=1),
             jnp.zeros((SUBLANES - 5, ntp), I32)], axis=0)

    idx = idx_ref[...]
    erow_t = lax.broadcasted_iota(I32, (N_EXPERTS, tm), 0)
    hits = [erow_t == idx[k:k + 1, :] for k in range(TOP_K)]
    multi = hits[0].astype(F32)
    for k in range(1, TOP_K):
        multi = multi + hits[k].astype(F32)
    prefix = _dot(multi.astype(BF16), tri_ref[...])
    base = jnp.concatenate([carry_ref[...] + start_ref[...]] * (tm // LANES), axis=1)
    tot = prefix + base
    dest_ref[...] = jnp.concatenate(
        [jnp.sum(jnp.where(hits[k], tot, 0.0), axis=0, keepdims=True) for k in range(TOP_K)],
        axis=0).astype(I32)
    carry_ref[...] += jnp.broadcast_to(jnp.sum(multi, axis=1, keepdims=True), (N_EXPERTS, LANES))


def _ranks(idxT, cnt, n_tiles_pad):
    T = idxT.shape[1]
    nt = T // TM_ROUTE
    full = lambda i: (0, 0)
    return pl.pallas_call(
        functools.partial(_ranks_kernel, n_tiles_pad),
        grid=(nt,),
        in_specs=[
            pl.BlockSpec((TOP_K, TM_ROUTE), lambda i: (0, i)),
            pl.BlockSpec((N_EXPERTS, LANES), full),
        ],
        out_specs=[
            pl.BlockSpec((TOP_K, TM_ROUTE), lambda i: (0, i)),
            pl.BlockSpec((SUBLANES, n_tiles_pad), full),
        ],
        out_shape=[
            jax.ShapeDtypeStruct((TOP_K, T), I32),
            jax.ShapeDtypeStruct((SUBLANES, n_tiles_pad), I32),
        ],
        scratch_shapes=[
            pltpu.VMEM((TM_ROUTE, TM_ROUTE), BF16),
            pltpu.VMEM((N_EXPERTS, LANES), F32),
            pltpu.VMEM((N_EXPERTS, LANES), F32),
        ],
        compiler_params=_cparams(("arbitrary",)),
        name="ranks",
    )(idxT, cnt)


RT = D_MODEL // LANES
DMA_UNROLL = 8


def _row_copy(src_ref, s, dst_ref, d, sem):
    return pltpu.make_async_copy(src_ref.at[pl.ds(pl.multiple_of(s * RT, RT), RT)],
                                 dst_ref.at[pl.ds(pl.multiple_of(d * RT, RT), RT)], sem)


def _to_row_tiled(dst_ref, val, rows):
    for s in range(RT):
        dst_ref[pl.ds(s, rows, stride=RT), :] = val[:, s * LANES:(s + 1) * LANES]


def _from_row_tiled(src_ref, rows):
    return [src_ref[pl.ds(s, rows, stride=RT), :] for s in range(RT)]


def _scatter_kernel(meta_ref, x_ref, dest_ref, xs_ref, stage_ref, zero_ref, sem):
    i = pl.program_id(0)
    tm = TM_ROUTE

    @pl.when(i == 0)
    def _():
        zero_ref[...] = jnp.zeros_like(zero_ref)

        def zcopy(row, nrows):
            off = pl.multiple_of(row * RT, RT)
            return pltpu.make_async_copy(zero_ref.at[pl.ds(0, nrows * RT)],
                                         xs_ref.at[pl.ds(off, nrows * RT)], sem)

        def pad_fill(wait):
            def body(e, c):
                row = meta_ref[2, e]
                plen = meta_ref[3, e]
                for bit in reversed(range(_log2(TM_MOE))):
                    size = 1 << bit
                    has = (plen & size) != 0

                    @pl.when(has)
                    def _():
                        cp = zcopy(row, size)
                        cp.wait() if wait else cp.start()

                    row = row + jnp.where(has, size, 0)
                return c
            lax.fori_loop(0, N_EXPERTS, body, 0)

        def tail_fill(wait):
            def body(j, c):
                cp = zcopy(j * TM_MOE, TM_MOE)
                cp.wait() if wait else cp.start()
                return c
            lax.fori_loop(meta_ref[1, 0], xs_ref.shape[0] // (TM_MOE * RT), body, 0)

        pad_fill(False)
        tail_fill(False)
        pad_fill(True)
        tail_fill(True)

    _to_row_tiled(stage_ref, x_ref[...], tm)

    def start(t, c):
        for k in range(TOP_K):
            _row_copy(stage_ref, t, xs_ref, dest_ref[k, t], sem).start(priority=k % 2)
        return c

    lax.fori_loop(0, tm, start, 0, unroll=DMA_UNROLL)
    for k in range(TOP_K):
        pltpu.make_async_copy(stage_ref, xs_ref.at[pl.ds(0, tm * RT)], sem).wait()


def _scatter(meta, x1, destT, n_rows_alloc):
    T = x1.shape[0]
    nt = T // TM_ROUTE
    return pl.pallas_call(
        _scatter_kernel,
        grid_spec=pltpu.PrefetchScalarGridSpec(
            num_scalar_prefetch=1,
            grid=(nt,),
            in_specs=[
                pl.BlockSpec((TM_ROUTE, D_MODEL), lambda i, m: (i, 0)),
                pl.BlockSpec((TOP_K, TM_ROUTE), lambda i, m: (0, i), memory_space=pltpu.SMEM),
            ],
            out_specs=pl.BlockSpec(memory_space=pl.ANY),
            scratch_shapes=[
                pltpu.VMEM((TM_ROUTE * RT, LANES), F32),
                pltpu.VMEM((TM_MOE * RT, LANES), F32),
                pltpu.SemaphoreType.DMA(()),
            ],
        ),
        out_shape=jax.ShapeDtypeStruct((n_rows_alloc * RT, LANES), F32),
        compiler_params=_cparams(("arbitrary",)),
        name="scatter",
    )(meta, x1, destT)


def _gmm_kernel(layer, meta_ref, xs_ref, wgu_hbm, bgu_ref, wd_hbm, bd_ref, ys_ref,
                wgu_f, wd_f, wgu_b, wd_b, lhs_ref, act_ref, grp_ref, sems):
    j = pl.program_id(0)
    n_used = meta_ref[1, 0]
    e_now = meta_ref[0, j]
    e_prev = meta_ref[0, jnp.maximum(j - 1, 0)]
    used = j < n_used

    def weight_copies(e, slot):
        return (pltpu.make_async_copy(wgu_hbm.at[layer, e], wgu_f.at[slot], sems.at[0, slot]),
                pltpu.make_async_copy(wd_hbm.at[layer, e], wd_f.at[slot], sems.at[1, slot]))

    @pl.when(used & (j == 0))
    def _():
        grp_ref[0] = 0
        for cp in weight_copies(e_now, 0):
            cp.start()

    @pl.when(used & ((j == 0) | (e_now != e_prev)))
    def _():
        slot = grp_ref[0]
        j_next = j + meta_ref[4, e_now]

        @pl.when(j_next < n_used)
        def _():
            for cp in weight_copies(meta_ref[0, jnp.minimum(j_next, meta_ref.shape[1] - 1)], 1 - slot):
                cp.start()

        for cp in weight_copies(e_now, slot):
            cp.wait()
        wgu_b[...] = wgu_f[slot].astype(BF16)
        wd_b[...] = wd_f[slot].astype(BF16)
        grp_ref[0] = 1 - slot

    @pl.when(used)
    def _():
        for s, piece in enumerate(_from_row_tiled(xs_ref, TM_MOE)):
            lhs_ref[:, s * LANES:(s + 1) * LANES] = piece.astype(BF16)
        x = lhs_ref[...]
        nc = 256
        for c in range(D_FF // nc):
            gate = _dot(x, wgu_b[:, c * nc:(c + 1) * nc]) + bgu_ref[0, 0, :, c * nc:(c + 1) * nc]
            up = (_dot(x, wgu_b[:, D_FF + c * nc:D_FF + (c + 1) * nc])
                  + bgu_ref[0, 0, :, D_FF + c * nc:D_FF + (c + 1) * nc])
            gate = jnp.minimum(gate, SWIGLU_LIMIT)
            up = jnp.clip(up, -SWIGLU_LIMIT, SWIGLU_LIMIT)
            act = gate * jax.nn.sigmoid(SWIGLU_ALPHA * gate) * (up + 1.0)
            act_ref[:, c * nc:(c + 1) * nc] = act.astype(BF16)
        _to_row_tiled(ys_ref, _dot(act_ref[...], wd_b[...]) + bd_ref[0, 0], TM_MOE)

    @pl.when(jnp.logical_not(used))
    def _():
        ys_ref[...] = jnp.zeros_like(ys_ref)


def _gmm(meta, xs, w_gu, b_gu, w_down, b_down, n_tiles, layer):
    last = lambda m: jnp.maximum(m[1, 0] - 1, 0)
    return pl.pallas_call(
        functools.partial(_gmm_kernel, layer),
        grid_spec=pltpu.PrefetchScalarGridSpec(
            num_scalar_prefetch=1,
            grid=(n_tiles,),
            in_specs=[
                pl.BlockSpec((TM_MOE * RT, LANES), lambda j, m: (jnp.minimum(j, last(m)), 0)),
                pl.BlockSpec(memory_space=pl.ANY),
                pl.BlockSpec((1, 1, 1, 2 * D_FF), lambda j, m: (layer, m[0, j], 0, 0)),
                pl.BlockSpec(memory_space=pl.ANY),
                pl.BlockSpec((1, 1, 1, D_MODEL), lambda j, m: (layer, m[0, j], 0, 0)),
            ],
            out_specs=pl.BlockSpec((TM_MOE * RT, LANES), lambda j, m: (j, 0)),
            scratch_shapes=[
                pltpu.VMEM((2, D_MODEL, 2 * D_FF), F32),
                pltpu.VMEM((2, D_FF, D_MODEL), F32),
                pltpu.VMEM((D_MODEL, 2 * D_FF), BF16),
                pltpu.VMEM((D_FF, D_MODEL), BF16),
                pltpu.VMEM((TM_MOE, D_MODEL), BF16),
                pltpu.VMEM((TM_MOE, D_FF), BF16),
                pltpu.SMEM((1,), I32),
                pltpu.SemaphoreType.DMA((2, 2)),
            ],
        ),
        out_shape=jax.ShapeDtypeStruct((n_tiles * TM_MOE * RT, LANES), F32),
        compiler_params=_cparams(("arbitrary",)),
        name="gmm",
    )(meta, xs, w_gu, b_gu, w_down, b_down)


def _combine_kernel(x_ref, dest_ref, gate_ref, ys_ref, g_ref, b_ref, o_ref, ob_ref, buf_ref, sem):
    tm = TM_COMB

    def start(t, c):
        for k in range(TOP_K):
            _row_copy(ys_ref, dest_ref[k, t], buf_ref.at[k], t, sem).start(priority=k % 2)
        return c

    lax.fori_loop(0, tm, start, 0, unroll=DMA_UNROLL)
    for k in range(TOP_K):
        pltpu.make_async_copy(ys_ref.at[pl.ds(0, tm * RT)], buf_ref.at[k], sem).wait()
    gates = jnp.concatenate(
        [gate_ref[...], jnp.zeros((LANES - SUBLANES, tm), F32)], axis=0).T
    pieces = None
    for k in range(TOP_K):
        gk = gates[:, k:k + 1]
        rows = [p * gk for p in _from_row_tiled(buf_ref.at[k], tm)]
        pieces = rows if pieces is None else [a + r for a, r in zip(pieces, rows)]
    z = DN_ALPHA * x_ref[...] + jnp.concatenate(pieces, axis=1)
    out = _layernorm_rows(z, g_ref[...], b_ref[...])
    o_ref[...] = out
    ob_ref[...] = out.astype(BF16)


def _combine(x1, destT, gateT, ys, g, b):
    T = x1.shape[0]
    nt = T // TM_COMB
    full = lambda i: (0, 0)
    return pl.pallas_call(
        _combine_kernel,
        grid=(nt,),
        in_specs=[
            pl.BlockSpec((TM_COMB, D_MODEL), lambda i: (i, 0)),
            pl.BlockSpec((TOP_K, TM_COMB), lambda i: (0, i), memory_space=pltpu.SMEM),
            pl.BlockSpec((SUBLANES, TM_COMB), lambda i: (0, i)),
            pl.BlockSpec(memory_space=pl.ANY),
            pl.BlockSpec((1, D_MODEL), full),
            pl.BlockSpec((1, D_MODEL), full),
        ],
        out_specs=[
            pl.BlockSpec((TM_COMB, D_MODEL), lambda i: (i, 0)),
            pl.BlockSpec((TM_COMB, D_MODEL), lambda i: (i, 0)),
        ],
        out_shape=[
            jax.ShapeDtypeStruct((T, D_MODEL), F32),
            jax.ShapeDtypeStruct((T, D_MODEL), BF16),
        ],
        scratch_shapes=[
            pltpu.VMEM((TOP_K, TM_COMB * RT, LANES), F32),
            pltpu.SemaphoreType.DMA(()),
        ],
        compiler_params=_cparams(("arbitrary",)),
        name="combine",
    )(x1, destT, gateT, ys, g, b)


def _block_diag(w):
    g, c, _ = w.shape
    out = jnp.zeros((g * c, g * c), w.dtype)
    for i in range(g):
        out = out.at[i * c:(i + 1) * c, i * c:(i + 1) * c].set(w[i])
    return out


def kernel(x, w_in, conv_w, w_pool, pool_scale, idx_kn_g, idx_kn_b, w_out, ln1_g, ln1_b, router_w,
           router_b, w_gu, b_gu, w_down, b_down, ln2_g, ln2_b):
    B, S, D = x.shape
    T = B * S
    depth = w_in.shape[0]
    n_pairs = T * TOP_K
    n_tiles = (n_pairs + N_EXPERTS * (TM_MOE - 1)) // TM_MOE + 1
    n_tiles_pad = ((n_tiles + LANES - 1) // LANES) * LANES
    n_rows_alloc = (n_tiles + 1) * TM_MOE

    xf = x.reshape(T, D)
    xb = xf
    for l in range(depth):
        wl = w_in[l]
        wn = jnp.concatenate(
            [wl[:, 0:OFF_V], wl[:, OFF_KI:OFF_WI], jnp.zeros((D, LANES - IDX_DIM), F32)], axis=1).astype(BF16)
        wt = jnp.concatenate(
            [wl[:, OFF_V:OFF_QI], wl[:, OFF_QI:OFF_KI], wl[:, OFF_WI:D_IN],
             jnp.zeros((D, IDXT_ROWS - IDX_HEADS * IDX_DIM - IDX_HEADS), F32)], axis=1).T.astype(BF16)
        a, q, k, ki, vt, it = _inproj(xb, wn, wt, idx_kn_g[l].reshape(1, -1), idx_kn_b[l].reshape(1, -1))
        y_attn = _attn(it, ki, q, k, vt, B, S)
        x1 = _mixout(a, y_attn, xf, conv_w[l], _block_diag(w_pool[l]).astype(BF16),
                     pool_scale[l].reshape(1, -1), w_out[l].astype(BF16),
                     ln1_g[l].reshape(1, -1), ln1_b[l].reshape(1, -1), S)
        idxT, gateT, cnt = _router(x1, router_w[l].T, router_b[l].reshape(-1, 1))
        destT, meta = _ranks(idxT, cnt, n_tiles_pad)
        xs = _scatter(meta, x1, destT, n_rows_alloc)
        ys = _gmm(meta, xs, w_gu, b_gu.reshape(depth, N_EXPERTS, 1, -1), w_down,
                  b_down.reshape(depth, N_EXPERTS, 1, -1), n_tiles, l)
        xf, xb = _combine(x1, destT, gateT, ys, ln2_g[l].reshape(1, -1), ln2_b[l].reshape(1, -1))
    return xf.reshape(B, S, D)
```

```python
import functools

import jax
import jax.numpy as jnp
import numpy as np
from jax import lax
from jax.experimental import pallas as pl
from jax.experimental.pallas import tpu as pltpu

F32 = jnp.float32
BF16 = jnp.bfloat16
I32 = jnp.int32

D_MODEL = 1024
CONV_DIM = 256
CONV_WIDTH = 3
POOL_WINDOWS = (2, 4, 8, 16)
POOL_GROUP = 64
POOL_DIM = 256
ATTN_HEADS = 8
HEAD_DIM = 64
ATTN_DIM = 512
IDX_HEADS = 8
IDX_DIM = 32
INDEX_TOPK_MAX = 256
Q_BLOCK = 128
N_EXPERTS = 32
TOP_K = 4
D_FF = 1024
SWIGLU_ALPHA = 1.702
SWIGLU_LIMIT = 7.0
LN_EPS = 1e-5
DEPTH = 2
DN_ALPHA = (2.0 * DEPTH) ** 0.25

OFF_P = 3 * CONV_DIM
OFF_Q = OFF_P + POOL_DIM
OFF_K = OFF_Q + ATTN_DIM
OFF_V = OFF_K + ATTN_DIM
OFF_QI = OFF_V + ATTN_DIM
OFF_KI = OFF_QI + IDX_HEADS * IDX_DIM
OFF_WI = OFF_KI + IDX_DIM
D_IN = OFF_WI + IDX_HEADS

LANES = 128
SUBLANES = 8
VMEM_LIMIT = 56 * 1024 * 1024
KEY_CHUNK = 512
TIE_GROUP = 128
HALO = 16
TM_PROJ = 512
TM_MIX = 1024
TM_ROUTE = 1024
TM_MOE = 512
TM_COMB = 1024
IDXT_ROWS = IDX_HEADS * IDX_DIM + 16
WT_ROWS = ATTN_DIM + IDXT_ROWS
INT_MIN = -(2 ** 31)
NEG_BIG = -1e30


def _cparams(sem, flags=None):
    return pltpu.CompilerParams(dimension_semantics=sem, vmem_limit_bytes=VMEM_LIMIT, flags=flags)


def _log2(n):
    k = n.bit_length() - 1
    assert 1 << k == n
    return k


def _dot(a, b):
    return jnp.dot(a, b, preferred_element_type=F32)


def _dot_nt(a, b):
    return lax.dot_general(a, b, (((1,), (1,)), ((), ())), preferred_element_type=F32)


def _tree(x, op):
    parts = [x[j * SUBLANES:(j + 1) * SUBLANES, :] for j in range(x.shape[0] // SUBLANES)]
    while len(parts) > 1:
        nxt = [op(parts[j], parts[j + 1]) for j in range(0, len(parts) - 1, 2)]
        if len(parts) % 2:
            nxt.append(parts[-1])
        parts = nxt
    return parts[0]


def _layernorm_rows(z, g, b):
    mu = jnp.mean(z, axis=-1, keepdims=True)
    d = z - mu
    var = jnp.mean(d * d, axis=-1, keepdims=True)
    return d * lax.rsqrt(var + LN_EPS) * g + b


def _inproj_kernel(x_ref, wn_ref, wt_ref, kg_ref, kb_ref,
                   a_ref, q_ref, k_ref, ki_ref, vt_ref, it_ref):
    x = x_ref[...].astype(BF16)
    a_ref[...] = _dot(x, wn_ref[:, 0:OFF_Q])
    q_ref[...] = (_dot(x, wn_ref[:, OFF_Q:OFF_K]) * (HEAD_DIM ** -0.5)).astype(BF16)
    k_ref[...] = _dot(x, wn_ref[:, OFF_K:OFF_V]).astype(BF16)
    ki = _dot(x, wn_ref[:, OFF_V:OFF_V + LANES])[:, 0:IDX_DIM]
    ki_ref[...] = _layernorm_rows(ki, kg_ref[...], kb_ref[...])
    t = _dot_nt(wt_ref[...], x)
    for c in range(TM_PROJ // KEY_CHUNK):
        vt_ref[c] = t[0:ATTN_DIM, c * KEY_CHUNK:(c + 1) * KEY_CHUNK].astype(BF16)
    it_ref[...] = t[ATTN_DIM:WT_ROWS, :]


def _inproj(xb, wn, wt, kg, kb):
    T = xb.shape[0]
    nt = T // TM_PROJ
    cpt = TM_PROJ // KEY_CHUNK
    full = lambda i: (0, 0)
    return pl.pallas_call(
        _inproj_kernel,
        grid=(nt,),
        in_specs=[
            pl.BlockSpec((TM_PROJ, D_MODEL), lambda i: (i, 0)),
            pl.BlockSpec(wn.shape, full),
            pl.BlockSpec(wt.shape, full),
            pl.BlockSpec((1, IDX_DIM), full),
            pl.BlockSpec((1, IDX_DIM), full),
        ],
        out_specs=[
            pl.BlockSpec((TM_PROJ, OFF_Q), lambda i: (i, 0)),
            pl.BlockSpec((TM_PROJ, ATTN_DIM), lambda i: (i, 0)),
            pl.BlockSpec((TM_PROJ, ATTN_DIM), lambda i: (i, 0)),
            pl.BlockSpec((TM_PROJ, IDX_DIM), lambda i: (i, 0)),
            pl.BlockSpec((cpt, ATTN_DIM, KEY_CHUNK), lambda i: (i, 0, 0)),
            pl.BlockSpec((IDXT_ROWS, TM_PROJ), lambda i: (0, i)),
        ],
        out_shape=[
            jax.ShapeDtypeStruct((T, OFF_Q), F32),
            jax.ShapeDtypeStruct((T, ATTN_DIM), BF16),
            jax.ShapeDtypeStruct((T, ATTN_DIM), BF16),
            jax.ShapeDtypeStruct((T, IDX_DIM), F32),
            jax.ShapeDtypeStruct((T // KEY_CHUNK, ATTN_DIM, KEY_CHUNK), BF16),
            jax.ShapeDtypeStruct((IDXT_ROWS, T), F32),
        ],
        compiler_params=_cparams(("arbitrary",)),
        name="inproj",
    )(xb, wn, wt, kg, kb)


FIELD_BITS = 8
DIGIT_BITS = FIELD_BITS - 1
FIELDS = 32 // FIELD_BITS
DIGIT_SHIFTS = (25, 18, 11, 4, 0)
FIELD_ONES = 0x01010101
FIELD_GUARDS = 0x80808080 - (1 << 32)


def _pack_fields(f):
    q = f.shape[0] // FIELDS
    w = f[0:q]
    for j in range(1, FIELDS):
        w = w | lax.shift_left(f[j * q:(j + 1) * q], FIELD_BITS * j)
    return w | jnp.int32(FIELD_GUARDS)


def _count_fields_ge(words_ref, nch, cand):
    wq = 2 * (KEY_CHUNK // FIELDS)
    assert words_ref.shape[0] // SUBLANES < (1 << FIELD_BITS)
    cvec = cand * jnp.int32(FIELD_ONES)

    def body(c, acc):
        base = pl.multiple_of(c * wq, wq)
        w = words_ref[pl.ds(base, wq), :]
        hit = lax.shift_right_logical(w - cvec, DIGIT_BITS) & jnp.int32(FIELD_ONES)
        return acc + _tree(hit, jnp.add)

    acc = lax.fori_loop(0, lax.shift_right_logical(nch + 1, 1), body,
                        jnp.zeros((SUBLANES, words_ref.shape[1]), I32))
    tot = acc & 255
    for j in range(1, FIELDS):
        tot = tot + (lax.shift_right_logical(acc, FIELD_BITS * j) & 255)
    return jnp.sum(tot.astype(F32), axis=0, keepdims=True)


def _topk_threshold(keys_ref, words_ref, nch, n_sel, n_causal):
    lanes = keys_ref.shape[1]
    wq = KEY_CHUNK // FIELDS
    want = jnp.full((1, lanes), float(n_sel), F32)
    bucket = n_causal.astype(F32)
    prefix = None
    for idx, shift in enumerate(DIGIT_SHIFTS):
        width = (DIGIT_SHIFTS[idx - 1] if idx else 32) - shift
        if idx:
            def prep(c, carry, shift=shift, width=width, prefix=prefix):
                key = keys_ref[pl.ds(pl.multiple_of(c * KEY_CHUNK, KEY_CHUNK), KEY_CHUNK), :]
                digit = (key >> shift) & jnp.int32((1 << width) - 1)
                fields = jnp.where((key >> (shift + width)) == prefix, digit, 0)
                words_ref[pl.ds(pl.multiple_of(c * wq, wq), wq), :] = _pack_fields(fields)
                return carry

            lax.fori_loop(0, nch, prep, 0)
        dig = jnp.zeros((1, lanes), I32)
        at_dig = bucket
        above = jnp.zeros((1, lanes), F32)
        for bit in reversed(range(width)):
            cand = dig + (1 << bit)
            cnt = _count_fields_ge(words_ref, nch, cand)
            ok = cnt >= want
            dig = jnp.where(ok, cand, dig)
            at_dig = jnp.where(ok, cnt, at_dig)
            above = jnp.where(ok, above, cnt)
        want = want - above
        bucket = at_dig - above
        if idx:
            prefix = lax.shift_left(prefix, width) | dig
        else:
            prefix = dig - (1 << (DIGIT_BITS - 1))
    all_taken = n_causal <= n_sel
    tau = jnp.where(all_taken, jnp.int32(INT_MIN + 1), prefix)
    return tau, want, jnp.where(all_taken, 0.0, bucket - want)


def _attn_kernel(n_sel, it_ref, ki_ref, q_ref, k_ref, vt_ref, o_ref,
                 keys_ref, words_ref, rhs_ref, acc_ref, lg_ref):
    i = pl.program_id(1)
    nch = lax.shift_right_logical(i * Q_BLOCK + Q_BLOCK + KEY_CHUNK - 1, _log2(KEY_CHUNK))
    q0 = i * Q_BLOCK
    KC = KEY_CHUNK
    row = lax.broadcasted_iota(I32, (KC, Q_BLOCK), 0)
    lane = lax.broadcasted_iota(I32, (KC, Q_BLOCK), 1)
    qpos = q0 + lane

    qcat = jnp.concatenate(
        [it_ref[h * IDX_DIM:(h + 1) * IDX_DIM, :] for h in range(IDX_HEADS)], axis=1).astype(BF16)
    w_all = it_ref[IDX_HEADS * IDX_DIM:IDX_HEADS * IDX_DIM + IDX_HEADS, :] * (
        (IDX_HEADS ** -0.5) * (IDX_DIM ** -0.5))

    def score_body(c, carry):
        base = pl.multiple_of(c * KC, KC)
        kic = ki_ref[pl.ds(base, KC), :].astype(BF16)
        s = _dot(kic, qcat)
        sc = jnp.zeros((KC, Q_BLOCK), F32)
        for h in range(IDX_HEADS):
            sc = sc + jnp.maximum(s[:, h * Q_BLOCK:(h + 1) * Q_BLOCK], 0.0) * w_all[h:h + 1, :]
        bits = lax.bitcast_convert_type(sc, I32)
        key = bits ^ ((bits >> 31) & jnp.int32(0x7FFFFFFF))
        key = jnp.where(bits == jnp.int32(INT_MIN), jnp.int32(0), key)
        causal = base + row <= qpos
        keys_ref[pl.ds(base, KC), :] = jnp.where(causal, key, jnp.int32(INT_MIN))
        top = jnp.where(causal, (key >> DIGIT_SHIFTS[0]) + (1 << (DIGIT_BITS - 1)), 0)
        words_ref[pl.ds(pl.multiple_of(c * (KC // FIELDS), KC // FIELDS), KC // FIELDS), :] = _pack_fields(top)
        return carry

    lax.fori_loop(0, nch, score_body, 0)

    @pl.when(lax.rem(nch, 2) == 1)
    def _():
        wq = KC // FIELDS
        words_ref[pl.ds(pl.multiple_of(nch * wq, wq), wq), :] = _pack_fields(jnp.zeros((KC, Q_BLOCK), I32))

    n_causal = q0 + lax.broadcasted_iota(I32, (1, Q_BLOCK), 1) + 1
    tau, keep, excess = _topk_threshold(keys_ref, words_ref, nch, n_sel, n_causal)

    @pl.when(jnp.max(excess) > 0.0)
    def _():
        g = TIE_GROUP
        r = lax.broadcasted_iota(I32, (g, g), 0)
        cc = lax.broadcasted_iota(I32, (g, g), 1)
        tri = (cc <= r).astype(BF16)

        def retire(c, before):
            base = pl.multiple_of(c * KC, KC)
            blk = keys_ref[pl.ds(base, KC), :]
            tied = blk == tau
            ones = jnp.where(tied, 1.0, 0.0).astype(BF16)
            local = [_dot(tri, ones[j * g:(j + 1) * g, :]) for j in range(KC // g)]
            upto = []
            for lj in local:
                upto.append(lj + before)
                before = before + lj[g - 1:g, :]
            upto = jnp.concatenate(upto, axis=0)
            keys_ref[pl.ds(base, KC), :] = jnp.where(tied & (upto > keep), jnp.int32(INT_MIN), blk)
            return before

        lax.fori_loop(0, nch, retire, jnp.zeros((1, Q_BLOCK), F32))

    q = q_ref[...]
    lane_q = lax.broadcasted_iota(I32, (Q_BLOCK, 2 * HEAD_DIM), 1)
    for p in range(ATTN_HEADS // 2):
        qp = q[:, p * 2 * HEAD_DIM:(p + 1) * 2 * HEAD_DIM]
        rhs_ref[p, 0:Q_BLOCK, :] = jnp.where(lane_q < HEAD_DIM, qp, jnp.zeros_like(qp))
        rhs_ref[p, Q_BLOCK:2 * Q_BLOCK, :] = jnp.where(lane_q >= HEAD_DIM, qp, jnp.zeros_like(qp))
    acc_ref[...] = jnp.zeros_like(acc_ref)

    def stage_a_prep(c):
        base = pl.multiple_of(c * KC, KC)
        tk = (base + row - (q0 + Q_BLOCK - 1)).astype(F32)
        tkm = jnp.where(keys_ref[pl.ds(base, KC), :] >= tau, tk, NEG_BIG)
        return tkm, k_ref[pl.ds(base, KC), :]

    def stage_a_pair(c, p, tkm, kc):
        l2 = _dot_nt(kc[:, p * 2 * HEAD_DIM:(p + 1) * 2 * HEAD_DIM], rhs_ref[p])
        mcs = []
        for hh in range(2):
            h = 2 * p + hh
            slope = 2.0 ** (-8.0 * (h + 1) / ATTN_HEADS)
            lg = l2[:, hh * Q_BLOCK:(hh + 1) * Q_BLOCK] + slope * tkm
            lg_ref[h] = lg
            mcs.append(jnp.max(_tree(lg, jnp.maximum), axis=0, keepdims=True))
        return mcs

    def stage_b_pair(c, p, m_new, alpha):
        sums = []
        for hh in range(2):
            h = 2 * p + hh
            pr = jnp.exp(lg_ref[h] - m_new[h:h + 1, :])
            sums.append(jnp.sum(_tree(pr, jnp.add), axis=0, keepdims=True))
            pv = _dot(vt_ref[c, h * HEAD_DIM:(h + 1) * HEAD_DIM, :], pr.astype(BF16))
            acc_ref[h * HEAD_DIM:(h + 1) * HEAD_DIM, :] = (
                acc_ref[h * HEAD_DIM:(h + 1) * HEAD_DIM, :] * alpha[h:h + 1, :] + pv)
        return sums

    def stage_b_all(c, m_old, l_old, mc):
        m_new = jnp.maximum(m_old, mc)
        alpha = jnp.exp(m_old - m_new)
        return m_new, alpha

    tkm0, kc0 = stage_a_prep(0)
    mc0 = jnp.concatenate(sum([stage_a_pair(0, p, tkm0, kc0) for p in range(ATTN_HEADS // 2)], []), axis=0)

    def attn_body(c, carry):
        m_old, l_old, mc_prev = carry
        m_new, alpha = stage_b_all(c - 1, m_old, l_old, mc_prev)
        tkm, kc = stage_a_prep(c)
        mcs, sums = [], []
        for p in range(ATTN_HEADS // 2):
            sums += stage_b_pair(c - 1, p, m_new, alpha)
            mcs += stage_a_pair(c, p, tkm, kc)
        return m_new, l_old * alpha + jnp.concatenate(sums, axis=0), jnp.concatenate(mcs, axis=0)

    m0 = jnp.full((ATTN_HEADS, Q_BLOCK), NEG_BIG, F32)
    l0 = jnp.zeros((ATTN_HEADS, Q_BLOCK), F32)
    m_old, l_old, mc_prev = lax.fori_loop(1, nch, attn_body, (m0, l0, mc0))
    m_new, alpha = stage_b_all(nch - 1, m_old, l_old, mc_prev)
    sums = sum([stage_b_pair(nch - 1, p, m_new, alpha) for p in range(ATTN_HEADS // 2)], [])
    l_fin = l_old * alpha + jnp.concatenate(sums, axis=0)

    outs = []
    for h in range(ATTN_HEADS):
        outs.append(acc_ref[h * HEAD_DIM:(h + 1) * HEAD_DIM, :] / l_fin[h:h + 1, :])
    o_ref[...] = jnp.concatenate(outs, axis=0).T


def _attn(it, ki, q, k, vt, B, S):
    T = B * S
    nb = S // Q_BLOCK
    n_sel = min(INDEX_TOPK_MAX, S // 4)
    return pl.pallas_call(
        functools.partial(_attn_kernel, n_sel),
        grid=(B, nb),
        in_specs=[
            pl.BlockSpec((IDXT_ROWS, Q_BLOCK), lambda b, i: (0, b * nb + i)),
            pl.BlockSpec((S, IDX_DIM), lambda b, i: (b, 0)),
            pl.BlockSpec((Q_BLOCK, ATTN_DIM), lambda b, i: (b * nb + i, 0)),
            pl.BlockSpec((S, ATTN_DIM), lambda b, i: (b, 0)),
            pl.BlockSpec((S // KEY_CHUNK, ATTN_DIM, KEY_CHUNK), lambda b, i: (b, 0, 0)),
        ],
        out_specs=pl.BlockSpec((Q_BLOCK, ATTN_DIM), lambda b, i: (b * nb + i, 0)),
        out_shape=jax.ShapeDtypeStruct((T, ATTN_DIM), F32),
        scratch_shapes=[
            pltpu.VMEM((S, Q_BLOCK), I32),
            pltpu.VMEM((-(-(S // KEY_CHUNK) // 2) * 2 * (KEY_CHUNK // FIELDS), Q_BLOCK), I32),
            pltpu.VMEM((ATTN_HEADS // 2, 2 * Q_BLOCK, 2 * HEAD_DIM), BF16),
            pltpu.VMEM((ATTN_DIM, Q_BLOCK), F32),
            pltpu.VMEM((ATTN_HEADS, KEY_CHUNK, Q_BLOCK), F32),
        ],
        compiler_params=_cparams(("arbitrary", "arbitrary")),
        name="attn",
    )(it, ki, q, k, vt)


def _mixout_kernel(tiles_per_seq, a_ref, halo_ref, y_ref, x_ref, cw_ref, wpool_ref, ps_ref,
                   wout_ref, g_ref, b_ref, o_ref, ext_ref):
    i = pl.program_id(0)
    tm = TM_MIX
    first = lax.rem(i, tiles_per_seq) == 0
    halo = jnp.where(first, 0.0, halo_ref[...])
    a = a_ref[...]
    h_c, gb_c, gc_c, p_c = (a[:, j * CONV_DIM:(j + 1) * CONV_DIM] for j in range(4))

    t0 = 2 * HALO
    n_ext = t0 + tm
    ext_ref[0:HALO, :] = jnp.zeros((HALO, CONV_DIM), F32)

    ext_ref[HALO:t0, :] = halo[:, 2 * CONV_DIM:3 * CONV_DIM] * halo[:, 0:CONV_DIM]
    u = gc_c * h_c
    ext_ref[t0:n_ext, :] = u
    cw = cw_ref[...]
    conv = cw[2:3, :] * u
    conv = conv + cw[1:2, :] * ext_ref[t0 - 1:n_ext - 1, :]
    conv = conv + cw[0:1, :] * ext_ref[t0 - 2:n_ext - 2, :]
    y_conv = gb_c * conv

    ext_ref[HALO:t0, :] = halo[:, OFF_P:OFF_P + POOL_DIM]
    ext_ref[t0:n_ext, :] = p_c
    sums = {}
    step = 1
    while step < POOL_WINDOWS[-1]:
        cur = ext_ref[HALO:n_ext, :] + ext_ref[HALO - step:n_ext - step, :]
        ext_ref[HALO:n_ext, :] = cur
        step *= 2
        sums[step] = ext_ref[t0:n_ext, :]
    tpos = (lax.rem(i, tiles_per_seq) * tm + lax.broadcasted_iota(I32, (tm, POOL_DIM), 0) + 1).astype(F32)
    grp = lax.shift_right_logical(lax.broadcasted_iota(I32, (tm, POOL_DIM), 1), _log2(POOL_GROUP))
    mean = jnp.zeros((tm, POOL_DIM), F32)
    for gi, w in enumerate(POOL_WINDOWS):
        mean = jnp.where(grp == gi, sums[w] / jnp.minimum(tpos, float(w)), mean)
    mixed = mean - p_c
    y_pool = _dot(mixed.astype(BF16), wpool_ref[...]) * ps_ref[...]

    mix = _dot(y_conv.astype(BF16), wout_ref[0:CONV_DIM, :])
    mix = mix + _dot(y_pool.astype(BF16), wout_ref[CONV_DIM:CONV_DIM + POOL_DIM, :])
    mix = mix + _dot(y_ref[...].astype(BF16), wout_ref[CONV_DIM + POOL_DIM:D_MODEL, :])
    z = DN_ALPHA * x_ref[...] + mix
    o_ref[...] = _layernorm_rows(z, g_ref[...], b_ref[...])


def _mixout(a, y_attn, x, cw, wpool_bd, ps, wout, g, b, S):
    T = a.shape[0]
    nt = T // TM_MIX
    tps = S // TM_MIX
    hb = TM_MIX // HALO
    full = lambda i: (0, 0)
    return pl.pallas_call(
        functools.partial(_mixout_kernel, tps),
        grid=(nt,),
        in_specs=[
            pl.BlockSpec((TM_MIX, OFF_Q), lambda i: (i, 0)),
            pl.BlockSpec((HALO, OFF_Q), lambda i: (jnp.maximum(i * hb - 1, 0), 0)),
            pl.BlockSpec((TM_MIX, ATTN_DIM), lambda i: (i, 0)),
            pl.BlockSpec((TM_MIX, D_MODEL), lambda i: (i, 0)),
            pl.BlockSpec((CONV_WIDTH, CONV_DIM), full),
            pl.BlockSpec((POOL_DIM, POOL_DIM), full),
            pl.BlockSpec((1, POOL_DIM), full),
            pl.BlockSpec((D_MODEL, D_MODEL), full),
            pl.BlockSpec((1, D_MODEL), full),
            pl.BlockSpec((1, D_MODEL), full),
        ],
        out_specs=pl.BlockSpec((TM_MIX, D_MODEL), lambda i: (i, 0)),
        out_shape=jax.ShapeDtypeStruct((T, D_MODEL), F32),
        scratch_shapes=[pltpu.VMEM((2 * HALO + TM_MIX, CONV_DIM), F32)],
        compiler_params=_cparams(("arbitrary",)),
        name="mixout",
    )(a, a, y_attn, x, cw, wpool_bd, ps, wout, g, b)


def _router_kernel(x_ref, rw_ref, rb_ref, idx_ref, gate_ref, cnt_ref):
    i = pl.program_id(0)
    tm = TM_ROUTE
    x = x_ref[...]
    w = rw_ref[...]
    xh = x.astype(BF16)
    xl = (x - xh.astype(F32)).astype(BF16)
    wh = w.astype(BF16)
    wl = (w - wh.astype(F32)).astype(BF16)
    logits = (_dot_nt(wh, xh) + (_dot_nt(wh, xl) + _dot_nt(wl, xh))) + rb_ref[...]
    erow = lax.broadcasted_iota(I32, (N_EXPERTS, tm), 0).astype(F32)
    work = logits
    vals, idxs = [], []
    multi = jnp.zeros((N_EXPERTS, tm), F32)
    for _ in range(TOP_K):
        mx = jnp.max(work, axis=0, keepdims=True)
        pick = jnp.min(jnp.where(work == mx, erow, float(N_EXPERTS)), axis=0, keepdims=True)
        hit = erow == pick
        work = jnp.where(hit, -jnp.inf, work)
        multi = multi + hit.astype(F32)
        vals.append(mx)
        idxs.append(pick)
    es = [jnp.exp(v - vals[0]) for v in vals]
    den = es[0] + es[1] + es[2] + es[3]
    idx_ref[...] = jnp.concatenate(idxs, axis=0).astype(I32)
    gate_ref[...] = jnp.concatenate([e / den for e in es] + [jnp.zeros((SUBLANES - TOP_K, tm), F32)], axis=0)
    part = multi[:, 0:LANES]
    for j in range(1, tm // LANES):
        part = part + multi[:, j * LANES:(j + 1) * LANES]

    @pl.when(i == 0)
    def _():
        cnt_ref[...] = jnp.zeros_like(cnt_ref)

    cnt_ref[...] += part


def _router(x1, rwT, rb):
    T = x1.shape[0]
    nt = T // TM_ROUTE
    full = lambda i: (0, 0)
    return pl.pallas_call(
        _router_kernel,
        grid=(nt,),
        in_specs=[
            pl.BlockSpec((TM_ROUTE, D_MODEL), lambda i: (i, 0)),
            pl.BlockSpec((N_EXPERTS, D_MODEL), full),
            pl.BlockSpec((N_EXPERTS, 1), full),
        ],
        out_specs=[
            pl.BlockSpec((TOP_K, TM_ROUTE), lambda i: (0, i)),
            pl.BlockSpec((SUBLANES, TM_ROUTE), lambda i: (0, i)),
            pl.BlockSpec((N_EXPERTS, LANES), full),
        ],
        out_shape=[
            jax.ShapeDtypeStruct((TOP_K, T), I32),
            jax.ShapeDtypeStruct((SUBLANES, T), F32),
            jax.ShapeDtypeStruct((N_EXPERTS, LANES), F32),
        ],
        compiler_params=_cparams(("arbitrary",)),
        name="router",
    )(x1, rwT, rb)


def _ranks_kernel(n_tiles_pad, idx_ref, cnt_ref, dest_ref, meta_ref, tri_ref, start_ref, carry_ref):
    i = pl.program_id(0)
    tm = TM_ROUTE
    erow = lax.broadcasted_iota(I32, (N_EXPERTS, LANES), 0)
    elane = lax.broadcasted_iota(I32, (N_EXPERTS, LANES), 1)

    @pl.when(i == 0)
    def _():
        cnt = jnp.sum(cnt_ref[...], axis=1, keepdims=True)
        cnt_i = jnp.broadcast_to(cnt, (N_EXPERTS, LANES)).astype(I32)
        padded = lax.shift_left(lax.shift_right_logical(cnt_i + (TM_MOE - 1), _log2(TM_MOE)), _log2(TM_MOE))
        r = lax.broadcasted_iota(I32, (N_EXPERTS, N_EXPERTS), 0)
        c = lax.broadcasted_iota(I32, (N_EXPERTS, N_EXPERTS), 1)
        low = (c <= r).astype(F32)
        pad_end = lax.dot_general(low, padded.astype(F32), (((1,), (0,)), ((), ())),
                                  precision=lax.Precision.HIGHEST, preferred_element_type=F32)
        pad_start = pad_end - padded.astype(F32)
        start_ref[...] = pad_start
        carry_ref[...] = jnp.zeros_like(carry_ref)
        a = lax.broadcasted_iota(I32, (tm, tm), 0)
        bcol = lax.broadcasted_iota(I32, (tm, tm), 1)
        tri_ref[...] = (a < bcol).astype(BF16)
        ntp = n_tiles_pad
        tstart = (lax.broadcasted_iota(I32, (N_EXPERTS, ntp), 1) * TM_MOE).astype(F32)
        pe = jnp.concatenate([pad_end] * (ntp // LANES), axis=1)
        texp = jnp.sum((pe <= tstart).astype(F32), axis=0, keepdims=True)
        texp = jnp.minimum(texp, float(N_EXPERTS - 1)).astype(I32)
        total = jnp.max(pad_end, axis=0, keepdims=True)
        n_used = lax.shift_right_logical(total.astype(I32), _log2(TM_MOE))
        zstart = jnp.sum(jnp.where(erow == elane, pad_start + cnt_i.astype(F32), 0.0),
                         axis=0, keepdims=True).astype(I32)
        zlen = jnp.sum(jnp.where(erow == elane, (padded - cnt_i).astype(F32), 0.0),
                       axis=0, keepdims=True).astype(I32)
        etiles = lax.shift_right_logical(
            jnp.sum(jnp.where(erow == elane, padded.astype(F32), 0.0), axis=0, keepdims=True).astype(I32),
            _log2(TM_MOE))
        lanes_pad = jnp.zeros((1, ntp - LANES), I32)
        meta_ref[...] = jnp.concatenate(
            [texp,
             jnp.concatenate([n_used, lanes_pad], axis=1),
             jnp.concatenate([zstart, lanes_pad], axis=1),
             jnp.concatenate([zlen, lanes_pad], axis=1),
             jnp.concatenate([etiles, lanes_pad], axis=1),
             jnp.zeros((SUBLANES - 5, ntp), I32)], axis=0)

    idx = idx_ref[...]
    erow_t = lax.broadcasted_iota(I32, (N_EXPERTS, tm), 0)
    hits = [erow_t == idx[k:k + 1, :] for k in range(TOP_K)]
    multi = hits[0].astype(F32)
    for k in range(1, TOP_K):
        multi = multi + hits[k].astype(F32)
    prefix = _dot(multi.astype(BF16), tri_ref[...])
    base = jnp.concatenate([carry_ref[...] + start_ref[...]] * (tm // LANES), axis=1)
    tot = prefix + base
    dest_ref[...] = jnp.concatenate(
        [jnp.sum(jnp.where(hits[k], tot, 0.0), axis=0, keepdims=True) for k in range(TOP_K)],
        axis=0).astype(I32)
    carry_ref[...] += jnp.broadcast_to(jnp.sum(multi, axis=1, keepdims=True), (N_EXPERTS, LANES))


def _ranks(idxT, cnt, n_tiles_pad):
    T = idxT.shape[1]
    nt = T // TM_ROUTE
    full = lambda i: (0, 0)
    return pl.pallas_call(
        functools.partial(_ranks_kernel, n_tiles_pad),
        grid=(nt,),
        in_specs=[
            pl.BlockSpec((TOP_K, TM_ROUTE), lambda i: (0, i)),
            pl.BlockSpec((N_EXPERTS, LANES), full),
        ],
        out_specs=[
            pl.BlockSpec((TOP_K, TM_ROUTE), lambda i: (0, i)),
            pl.BlockSpec((SUBLANES, n_tiles_pad), full),
        ],
        out_shape=[
            jax.ShapeDtypeStruct((TOP_K, T), I32),
            jax.ShapeDtypeStruct((SUBLANES, n_tiles_pad), I32),
        ],
        scratch_shapes=[
            pltpu.VMEM((TM_ROUTE, TM_ROUTE), BF16),
            pltpu.VMEM((N_EXPERTS, LANES), F32),
            pltpu.VMEM((N_EXPERTS, LANES), F32),
        ],
        compiler_params=_cparams(("arbitrary",)),
        name="ranks",
    )(idxT, cnt)


RT = D_MODEL // LANES
DMA_UNROLL = 8


def _row_copy(src_ref, s, dst_ref, d, sem):
    return pltpu.make_async_copy(src_ref.at[pl.ds(pl.multiple_of(s * RT, RT), RT)],
                                 dst_ref.at[pl.ds(pl.multiple_of(d * RT, RT), RT)], sem)


def _to_row_tiled(dst_ref, val, rows):
    for s in range(RT):
        dst_ref[pl.ds(s, rows, stride=RT), :] = val[:, s * LANES:(s + 1) * LANES]


def _from_row_tiled(src_ref, rows):
    return [src_ref[pl.ds(s, rows, stride=RT), :] for s in range(RT)]


def _scatter_kernel(meta_ref, x_ref, dest_ref, xs_ref, stage_ref, zero_ref, sem):
    i = pl.program_id(0)
    tm = TM_ROUTE

    @pl.when(i == 0)
    def _():
        zero_ref[...] = jnp.zeros_like(zero_ref)

        def zcopy(row, nrows):
            off = pl.multiple_of(row * RT, RT)
            return pltpu.make_async_copy(zero_ref.at[pl.ds(0, nrows * RT)],
                                         xs_ref.at[pl.ds(off, nrows * RT)], sem)

        def pad_fill(wait):
            def body(e, c):
                row = meta_ref[2, e]
                plen = meta_ref[3, e]
                for bit in reversed(range(_log2(TM_MOE))):
                    size = 1 << bit
                    has = (plen & size) != 0

                    @pl.when(has)
                    def _():
                        cp = zcopy(row, size)
                        cp.wait() if wait else cp.start()

                    row = row + jnp.where(has, size, 0)
                return c
            lax.fori_loop(0, N_EXPERTS, body, 0)

        def tail_fill(wait):
            def body(j, c):
                cp = zcopy(j * TM_MOE, TM_MOE)
                cp.wait() if wait else cp.start()
                return c
            lax.fori_loop(meta_ref[1, 0], xs_ref.shape[0] // (TM_MOE * RT), body, 0)

        pad_fill(False)
        tail_fill(False)
        pad_fill(True)
        tail_fill(True)

    _to_row_tiled(stage_ref, x_ref[...], tm)

    def start(t, c):
        for k in range(TOP_K):
            _row_copy(stage_ref, t, xs_ref, dest_ref[k, t], sem).start(priority=k % 2)
        return c

    lax.fori_loop(0, tm, start, 0, unroll=DMA_UNROLL)
    for k in range(TOP_K):
        pltpu.make_async_copy(stage_ref, xs_ref.at[pl.ds(0, tm * RT)], sem).wait()


def _scatter(meta, x1, destT, n_rows_alloc):
    T = x1.shape[0]
    nt = T // TM_ROUTE
    return pl.pallas_call(
        _scatter_kernel,
        grid_spec=pltpu.PrefetchScalarGridSpec(
            num_scalar_prefetch=1,
            grid=(nt,),
            in_specs=[
                pl.BlockSpec((TM_ROUTE, D_MODEL), lambda i, m: (i, 0)),
                pl.BlockSpec((TOP_K, TM_ROUTE), lambda i, m: (0, i), memory_space=pltpu.SMEM),
            ],
            out_specs=pl.BlockSpec(memory_space=pl.ANY),
            scratch_shapes=[
                pltpu.VMEM((TM_ROUTE * RT, LANES), F32),
                pltpu.VMEM((TM_MOE * RT, LANES), F32),
                pltpu.SemaphoreType.DMA(()),
            ],
        ),
        out_shape=jax.ShapeDtypeStruct((n_rows_alloc * RT, LANES), F32),
        compiler_params=_cparams(("arbitrary",)),
        name="scatter",
    )(meta, x1, destT)


def _gmm_kernel(layer, meta_ref, xs_ref, wgu_hbm, bgu_ref, wd_hbm, bd_ref, ys_ref,
                wgu_f, wd_f, wgu_b, wd_b, lhs_ref, act_ref, grp_ref, sems):
    j = pl.program_id(0)
    n_used = meta_ref[1, 0]
    e_now = meta_ref[0, j]
    e_prev = meta_ref[0, jnp.maximum(j - 1, 0)]
    used = j < n_used

    def weight_copies(e, slot):
        return (pltpu.make_async_copy(wgu_hbm.at[layer, e], wgu_f.at[slot], sems.at[0, slot]),
                pltpu.make_async_copy(wd_hbm.at[layer, e], wd_f.at[slot], sems.at[1, slot]))

    @pl.when(used & (j == 0))
    def _():
        grp_ref[0] = 0
        for cp in weight_copies(e_now, 0):
            cp.start()

    @pl.when(used & ((j == 0) | (e_now != e_prev)))
    def _():
        slot = grp_ref[0]
        j_next = j + meta_ref[4, e_now]

        @pl.when(j_next < n_used)
        def _():
            for cp in weight_copies(meta_ref[0, jnp.minimum(j_next, meta_ref.shape[1] - 1)], 1 - slot):
                cp.start()

        for cp in weight_copies(e_now, slot):
            cp.wait()
        wgu_b[...] = wgu_f[slot].astype(BF16)
        wd_b[...] = wd_f[slot].astype(BF16)
        grp_ref[0] = 1 - slot

    @pl.when(used)
    def _():
        for s, piece in enumerate(_from_row_tiled(xs_ref, TM_MOE)):
            lhs_ref[:, s * LANES:(s + 1) * LANES] = piece.astype(BF16)
        x = lhs_ref[...]
        nc = 512
        for c in range(D_FF // nc):
            gate = _dot(x, wgu_b[:, c * nc:(c + 1) * nc]) + bgu_ref[0, 0, :, c * nc:(c + 1) * nc]
            up = (_dot(x, wgu_b[:, D_FF + c * nc:D_FF + (c + 1) * nc])
                  + bgu_ref[0, 0, :, D_FF + c * nc:D_FF + (c + 1) * nc])
            gate = jnp.minimum(gate, SWIGLU_LIMIT)
            up = jnp.clip(up, -SWIGLU_LIMIT, SWIGLU_LIMIT)
            act = gate * jax.nn.sigmoid(SWIGLU_ALPHA * gate) * (up + 1.0)
            act_ref[:, c * nc:(c + 1) * nc] = act.astype(BF16)
        _to_row_tiled(ys_ref, _dot(act_ref[...], wd_b[...]) + bd_ref[0, 0], TM_MOE)

    @pl.when(jnp.logical_not(used))
    def _():
        ys_ref[...] = jnp.zeros_like(ys_ref)


def _gmm(meta, xs, w_gu, b_gu, w_down, b_down, n_tiles, layer):
    last = lambda m: jnp.maximum(m[1, 0] - 1, 0)
    return pl.pallas_call(
        functools.partial(_gmm_kernel, layer),
        grid_spec=pltpu.PrefetchScalarGridSpec(
            num_scalar_prefetch=1,
            grid=(n_tiles,),
            in_specs=[
                pl.BlockSpec((TM_MOE * RT, LANES), lambda j, m: (jnp.minimum(j, last(m)), 0)),
                pl.BlockSpec(memory_space=pl.ANY),
                pl.BlockSpec((1, 1, 1, 2 * D_FF), lambda j, m: (layer, m[0, j], 0, 0)),
                pl.BlockSpec(memory_space=pl.ANY),
                pl.BlockSpec((1, 1, 1, D_MODEL), lambda j, m: (layer, m[0, j], 0, 0)),
            ],
            out_specs=pl.BlockSpec((TM_MOE * RT, LANES), lambda j, m: (j, 0)),
            scratch_shapes=[
                pltpu.VMEM((2, D_MODEL, 2 * D_FF), F32),
                pltpu.VMEM((2, D_FF, D_MODEL), F32),
                pltpu.VMEM((D_MODEL, 2 * D_FF), BF16),
                pltpu.VMEM((D_FF, D_MODEL), BF16),
                pltpu.VMEM((TM_MOE, D_MODEL), BF16),
                pltpu.VMEM((TM_MOE, D_FF), BF16),
                pltpu.SMEM((1,), I32),
                pltpu.SemaphoreType.DMA((2, 2)),
            ],
        ),
        out_shape=jax.ShapeDtypeStruct((n_tiles * TM_MOE * RT, LANES), F32),
        compiler_params=_cparams(("arbitrary",)),
        name="gmm",
    )(meta, xs, w_gu, b_gu, w_down, b_down)


def _combine_kernel(x_ref, dest_ref, gate_ref, ys_ref, g_ref, b_ref, o_ref, ob_ref, buf_ref, sem):
    tm = TM_COMB

    def start(t, c):
        for k in range(TOP_K):
            _row_copy(ys_ref, dest_ref[k, t], buf_ref.at[k], t, sem).start(priority=k % 2)
        return c

    lax.fori_loop(0, tm, start, 0, unroll=DMA_UNROLL)
    for k in range(TOP_K):
        pltpu.make_async_copy(ys_ref.at[pl.ds(0, tm * RT)], buf_ref.at[k], sem).wait()
    gates = jnp.concatenate(
        [gate_ref[...], jnp.zeros((LANES - SUBLANES, tm), F32)], axis=0).T
    pieces = None
    for k in range(TOP_K):
        gk = gates[:, k:k + 1]
        rows = [p * gk for p in _from_row_tiled(buf_ref.at[k], tm)]
        pieces = rows if pieces is None else [a + r for a, r in zip(pieces, rows)]
    z = DN_ALPHA * x_ref[...] + jnp.concatenate(pieces, axis=1)
    out = _layernorm_rows(z, g_ref[...], b_ref[...])
    o_ref[...] = out
    ob_ref[...] = out.astype(BF16)


def _combine(x1, destT, gateT, ys, g, b):
    T = x1.shape[0]
    nt = T // TM_COMB
    full = lambda i: (0, 0)
    return pl.pallas_call(
        _combine_kernel,
        grid=(nt,),
        in_specs=[
            pl.BlockSpec((TM_COMB, D_MODEL), lambda i: (i, 0)),
            pl.BlockSpec((TOP_K, TM_COMB), lambda i: (0, i), memory_space=pltpu.SMEM),
            pl.BlockSpec((SUBLANES, TM_COMB), lambda i: (0, i)),
            pl.BlockSpec(memory_space=pl.ANY),
            pl.BlockSpec((1, D_MODEL), full),
            pl.BlockSpec((1, D_MODEL), full),
        ],
        out_specs=[
            pl.BlockSpec((TM_COMB, D_MODEL), lambda i: (i, 0)),
            pl.BlockSpec((TM_COMB, D_MODEL), lambda i: (i, 0)),
        ],
        out_shape=[
            jax.ShapeDtypeStruct((T, D_MODEL), F32),
            jax.ShapeDtypeStruct((T, D_MODEL), BF16),
        ],
        scratch_shapes=[
            pltpu.VMEM((TOP_K, TM_COMB * RT, LANES), F32),
            pltpu.SemaphoreType.DMA(()),
        ],
        compiler_params=_cparams(("arbitrary",)),
        name="combine",
    )(x1, destT, gateT, ys, g, b)


def _block_diag(w):
    g, c, _ = w.shape
    out = jnp.zeros((g * c, g * c), w.dtype)
    for i in range(g):
        out = out.at[i * c:(i + 1) * c, i * c:(i + 1) * c].set(w[i])
    return out


def kernel(x, w_in, conv_w, w_pool, pool_scale, idx_kn_g, idx_kn_b, w_out, ln1_g, ln1_b, router_w,
           router_b, w_gu, b_gu, w_down, b_down, ln2_g, ln2_b):
    B, S, D = x.shape
    T = B * S
    depth = w_in.shape[0]
    n_pairs = T * TOP_K
    n_tiles = (n_pairs + N_EXPERTS * (TM_MOE - 1)) // TM_MOE + 1
    n_tiles_pad = ((n_tiles + LANES - 1) // LANES) * LANES
    n_rows_alloc = (n_tiles + 1) * TM_MOE

    xf = x.reshape(T, D)
    xb = xf
    for l in range(depth):
        wl = w_in[l]
        wn = jnp.concatenate(
            [wl[:, 0:OFF_V], wl[:, OFF_KI:OFF_WI], jnp.zeros((D, LANES - IDX_DIM), F32)], axis=1).astype(BF16)
        wt = jnp.concatenate(
            [wl[:, OFF_V:OFF_QI], wl[:, OFF_QI:OFF_KI], wl[:, OFF_WI:D_IN],
             jnp.zeros((D, IDXT_ROWS - IDX_HEADS * IDX_DIM - IDX_HEADS), F32)], axis=1).T.astype(BF16)
        a, q, k, ki, vt, it = _inproj(xb, wn, wt, idx_kn_g[l].reshape(1, -1), idx_kn_b[l].reshape(1, -1))
        y_attn = _attn(it, ki, q, k, vt, B, S)
        x1 = _mixout(a, y_attn, xf, conv_w[l], _block_diag(w_pool[l]).astype(BF16),
                     pool_scale[l].reshape(1, -1), w_out[l].astype(BF16),
                     ln1_g[l].reshape(1, -1), ln1_b[l].reshape(1, -1), S)
        idxT, gateT, cnt = _router(x1, router_w[l].T, router_b[l].reshape(-1, 1))
        destT, meta = _ranks(idxT, cnt, n_tiles_pad)
        xs = _scatter(meta, x1, destT, n_rows_alloc)
        ys = _gmm(meta, xs, w_gu, b_gu.reshape(depth, N_EXPERTS, 1, -1), w_down,
                  b_down.reshape(depth, N_EXPERTS, 1, -1), n_tiles, l)
        xf, xb = _combine(x1, destT, gateT, ys, ln2_g[l].reshape(1, -1), ln2_b[l].reshape(1, -1))
    return xf.reshape(B, S, D)
```

```python
import functools

import jax
import jax.numpy as jnp
import numpy as np
from jax import lax
from jax.experimental import pallas as pl
from jax.experimental.pallas import tpu as pltpu

F32 = jnp.float32
BF16 = jnp.bfloat16
I32 = jnp.int32

D_MODEL = 1024
CONV_DIM = 256
CONV_WIDTH = 3
POOL_WINDOWS = (2, 4, 8, 16)
POOL_GROUP = 64
POOL_DIM = 256
ATTN_HEADS = 8
HEAD_DIM = 64
ATTN_DIM = 512
IDX_HEADS = 8
IDX_DIM = 32
INDEX_TOPK_MAX = 256
Q_BLOCK = 128
N_EXPERTS = 32
TOP_K = 4
D_FF = 1024
SWIGLU_ALPHA = 1.702
SWIGLU_LIMIT = 7.0
LN_EPS = 1e-5
DEPTH = 2
DN_ALPHA = (2.0 * DEPTH) ** 0.25

OFF_P = 3 * CONV_DIM
OFF_Q = OFF_P + POOL_DIM
OFF_K = OFF_Q + ATTN_DIM
OFF_V = OFF_K + ATTN_DIM
OFF_QI = OFF_V + ATTN_DIM
OFF_KI = OFF_QI + IDX_HEADS * IDX_DIM
OFF_WI = OFF_KI + IDX_DIM
D_IN = OFF_WI + IDX_HEADS

LANES = 128
SUBLANES = 8
VMEM_LIMIT = 56 * 1024 * 1024
KEY_CHUNK = 512
TIE_GROUP = 128
HALO = 16
TM_PROJ = 512
TM_MIX = 1024
TM_ROUTE = 1024
TM_MOE = 512
TM_COMB = 1024
IDXT_ROWS = IDX_HEADS * IDX_DIM + 16
WT_ROWS = ATTN_DIM + IDXT_ROWS
INT_MIN = -(2 ** 31)
NEG_BIG = -1e30


def _cparams(sem, flags=None):
    return pltpu.CompilerParams(dimension_semantics=sem, vmem_limit_bytes=VMEM_LIMIT, flags=flags)


def _log2(n):
    k = n.bit_length() - 1
    assert 1 << k == n
    return k


def _dot(a, b):
    return jnp.dot(a, b, preferred_element_type=F32)


def _dot_nt(a, b):
    return lax.dot_general(a, b, (((1,), (1,)), ((), ())), preferred_element_type=F32)


def _tree(x, op):
    parts = [x[j * SUBLANES:(j + 1) * SUBLANES, :] for j in range(x.shape[0] // SUBLANES)]
    while len(parts) > 1:
        nxt = [op(parts[j], parts[j + 1]) for j in range(0, len(parts) - 1, 2)]
        if len(parts) % 2:
            nxt.append(parts[-1])
        parts = nxt
    return parts[0]


def _layernorm_rows(z, g, b):
    mu = jnp.mean(z, axis=-1, keepdims=True)
    d = z - mu
    var = jnp.mean(d * d, axis=-1, keepdims=True)
    return d * lax.rsqrt(var + LN_EPS) * g + b


def _inproj_kernel(x_ref, wn_ref, wt_ref, kg_ref, kb_ref,
                   a_ref, q_ref, k_ref, ki_ref, vt_ref, it_ref):
    x = x_ref[...].astype(BF16)
    a_ref[...] = _dot(x, wn_ref[:, 0:OFF_Q])
    q_ref[...] = (_dot(x, wn_ref[:, OFF_Q:OFF_K]) * (HEAD_DIM ** -0.5)).astype(BF16)
    k_ref[...] = _dot(x, wn_ref[:, OFF_K:OFF_V]).astype(BF16)
    ki = _dot(x, wn_ref[:, OFF_V:OFF_V + LANES])[:, 0:IDX_DIM]
    ki_ref[...] = _layernorm_rows(ki, kg_ref[...], kb_ref[...])
    t = _dot_nt(wt_ref[...], x)
    for c in range(TM_PROJ // KEY_CHUNK):
        vt_ref[c] = t[0:ATTN_DIM, c * KEY_CHUNK:(c + 1) * KEY_CHUNK].astype(BF16)
    it_ref[...] = t[ATTN_DIM:WT_ROWS, :]


def _inproj(xb, wn, wt, kg, kb):
    T = xb.shape[0]
    nt = T // TM_PROJ
    cpt = TM_PROJ // KEY_CHUNK
    full = lambda i: (0, 0)
    return pl.pallas_call(
        _inproj_kernel,
        grid=(nt,),
        in_specs=[
            pl.BlockSpec((TM_PROJ, D_MODEL), lambda i: (i, 0)),
            pl.BlockSpec(wn.shape, full),
            pl.BlockSpec(wt.shape, full),
            pl.BlockSpec((1, IDX_DIM), full),
            pl.BlockSpec((1, IDX_DIM), full),
        ],
        out_specs=[
            pl.BlockSpec((TM_PROJ, OFF_Q), lambda i: (i, 0)),
            pl.BlockSpec((TM_PROJ, ATTN_DIM), lambda i: (i, 0)),
            pl.BlockSpec((TM_PROJ, ATTN_DIM), lambda i: (i, 0)),
            pl.BlockSpec((TM_PROJ, IDX_DIM), lambda i: (i, 0)),
            pl.BlockSpec((cpt, ATTN_DIM, KEY_CHUNK), lambda i: (i, 0, 0)),
            pl.BlockSpec((IDXT_ROWS, TM_PROJ), lambda i: (0, i)),
        ],
        out_shape=[
            jax.ShapeDtypeStruct((T, OFF_Q), F32),
            jax.ShapeDtypeStruct((T, ATTN_DIM), BF16),
            jax.ShapeDtypeStruct((T, ATTN_DIM), BF16),
            jax.ShapeDtypeStruct((T, IDX_DIM), F32),
            jax.ShapeDtypeStruct((T // KEY_CHUNK, ATTN_DIM, KEY_CHUNK), BF16),
            jax.ShapeDtypeStruct((IDXT_ROWS, T), F32),
        ],
        compiler_params=_cparams(("arbitrary",)),
        name="inproj",
    )(xb, wn, wt, kg, kb)


FIELD_BITS = 8
DIGIT_BITS = FIELD_BITS - 1
FIELDS = 32 // FIELD_BITS
DIGIT_SHIFTS = (25, 18, 11, 4, 0)
EARLY_EXIT_FROM = 4
FIELD_ONES = 0x01010101
FIELD_GUARDS = 0x80808080 - (1 << 32)


def _pack_fields(f):
    q = f.shape[0] // FIELDS
    w = f[0:q]
    for j in range(1, FIELDS):
        w = w | lax.shift_left(f[j * q:(j + 1) * q], FIELD_BITS * j)
    return w | jnp.int32(FIELD_GUARDS)


def _count_fields_ge(words_ref, nch, cand):
    wq = 2 * (KEY_CHUNK // FIELDS)
    assert words_ref.shape[0] // SUBLANES < (1 << FIELD_BITS)
    cvec = cand * jnp.int32(FIELD_ONES)

    def body(c, acc):
        base = pl.multiple_of(c * wq, wq)
        w = words_ref[pl.ds(base, wq), :]
        hit = lax.shift_right_logical(w - cvec, DIGIT_BITS) & jnp.int32(FIELD_ONES)
        return acc + _tree(hit, jnp.add)

    acc = lax.fori_loop(0, lax.shift_right_logical(nch + 1, 1), body,
                        jnp.zeros((SUBLANES, words_ref.shape[1]), I32))
    tot = acc & 255
    for j in range(1, FIELDS):
        tot = tot + (lax.shift_right_logical(acc, FIELD_BITS * j) & 255)
    return jnp.sum(tot.astype(F32), axis=0, keepdims=True)


def _topk_threshold(keys_ref, words_ref, nch, n_sel, n_causal):
    lanes = keys_ref.shape[1]
    wq = KEY_CHUNK // FIELDS
    all_taken = n_causal <= n_sel

    def run_digit(idx, state):
        prefix, want, bucket = state
        shift = DIGIT_SHIFTS[idx]
        width = (DIGIT_SHIFTS[idx - 1] if idx else 32) - shift
        if idx:
            def prep(c, carry):
                key = keys_ref[pl.ds(pl.multiple_of(c * KEY_CHUNK, KEY_CHUNK), KEY_CHUNK), :]
                digit = (key >> shift) & jnp.int32((1 << width) - 1)
                fields = jnp.where((key >> (shift + width)) == prefix, digit, 0)
                words_ref[pl.ds(pl.multiple_of(c * wq, wq), wq), :] = _pack_fields(fields)
                return carry

            lax.fori_loop(0, nch, prep, 0)
        dig = jnp.zeros((1, lanes), I32)
        at_dig = bucket
        above = jnp.zeros((1, lanes), F32)
        for bit in reversed(range(width)):
            cand = dig + (1 << bit)
            cnt = _count_fields_ge(words_ref, nch, cand)
            ok = cnt >= want
            dig = jnp.where(ok, cand, dig)
            at_dig = jnp.where(ok, cnt, at_dig)
            above = jnp.where(ok, above, cnt)
        prefix = (lax.shift_left(prefix, width) | dig) if idx else dig - (1 << (DIGIT_BITS - 1))
        return prefix, want - above, at_dig - above

    def finish(state, low_bits):
        prefix, want, bucket = state
        tau = jnp.where(all_taken, jnp.int32(INT_MIN + 1), lax.shift_left(prefix, low_bits))
        return tau, want, jnp.where(all_taken, 0.0, bucket - want)

    def settled(state):
        _, want, bucket = state
        return jnp.min(jnp.where((bucket == want) | all_taken, 1.0, 0.0)) > 0.0

    def from_digit(idx, state):
        if idx == len(DIGIT_SHIFTS):
            return finish(state, 0)
        if idx < EARLY_EXIT_FROM:
            return from_digit(idx + 1, run_digit(idx, state))
        return lax.cond(settled(state),
                        lambda s: finish(s, DIGIT_SHIFTS[idx - 1]),
                        lambda s: from_digit(idx + 1, run_digit(idx, s)), state)

    state = (jnp.zeros((1, lanes), I32), jnp.full((1, lanes), float(n_sel), F32), n_causal.astype(F32))
    return from_digit(0, state)


def _attn_kernel(n_sel, it_ref, ki_ref, q_ref, k_ref, vt_ref, o_ref,
                 keys_ref, words_ref, rhs_ref, acc_ref, lg_ref):
    i = pl.program_id(1)
    nch = lax.shift_right_logical(i * Q_BLOCK + Q_BLOCK + KEY_CHUNK - 1, _log2(KEY_CHUNK))
    q0 = i * Q_BLOCK
    KC = KEY_CHUNK
    row = lax.broadcasted_iota(I32, (KC, Q_BLOCK), 0)
    lane = lax.broadcasted_iota(I32, (KC, Q_BLOCK), 1)
    qpos = q0 + lane

    qcat = jnp.concatenate(
        [it_ref[h * IDX_DIM:(h + 1) * IDX_DIM, :] for h in range(IDX_HEADS)], axis=1).astype(BF16)
    w_all = it_ref[IDX_HEADS * IDX_DIM:IDX_HEADS * IDX_DIM + IDX_HEADS, :] * (
        (IDX_HEADS ** -0.5) * (IDX_DIM ** -0.5))

    def score_body(c, carry):
        base = pl.multiple_of(c * KC, KC)
        kic = ki_ref[pl.ds(base, KC), :].astype(BF16)
        s = _dot(kic, qcat)
        sc = jnp.zeros((KC, Q_BLOCK), F32)
        for h in range(IDX_HEADS):
            sc = sc + jnp.maximum(s[:, h * Q_BLOCK:(h + 1) * Q_BLOCK], 0.0) * w_all[h:h + 1, :]
        bits = lax.bitcast_convert_type(sc, I32)
        key = bits ^ ((bits >> 31) & jnp.int32(0x7FFFFFFF))
        key = jnp.where(bits == jnp.int32(INT_MIN), jnp.int32(0), key)
        causal = base + row <= qpos
        keys_ref[pl.ds(base, KC), :] = jnp.where(causal, key, jnp.int32(INT_MIN))
        top = jnp.where(causal, (key >> DIGIT_SHIFTS[0]) + (1 << (DIGIT_BITS - 1)), 0)
        words_ref[pl.ds(pl.multiple_of(c * (KC // FIELDS), KC // FIELDS), KC // FIELDS), :] = _pack_fields(top)
        return carry

    lax.fori_loop(0, nch, score_body, 0)

    @pl.when(lax.rem(nch, 2) == 1)
    def _():
        wq = KC // FIELDS
        words_ref[pl.ds(pl.multiple_of(nch * wq, wq), wq), :] = _pack_fields(jnp.zeros((KC, Q_BLOCK), I32))

    n_causal = q0 + lax.broadcasted_iota(I32, (1, Q_BLOCK), 1) + 1
    tau, keep, excess = _topk_threshold(keys_ref, words_ref, nch, n_sel, n_causal)

    @pl.when(jnp.max(excess) > 0.0)
    def _():
        g = TIE_GROUP
        r = lax.broadcasted_iota(I32, (g, g), 0)
        cc = lax.broadcasted_iota(I32, (g, g), 1)
        tri = (cc <= r).astype(BF16)

        def retire(c, before):
            base = pl.multiple_of(c * KC, KC)
            blk = keys_ref[pl.ds(base, KC), :]
            tied = blk == tau
            ones = jnp.where(tied, 1.0, 0.0).astype(BF16)
            local = [_dot(tri, ones[j * g:(j + 1) * g, :]) for j in range(KC // g)]
            upto = []
            for lj in local:
                upto.append(lj + before)
                before = before + lj[g - 1:g, :]
            upto = jnp.concatenate(upto, axis=0)
            keys_ref[pl.ds(base, KC), :] = jnp.where(tied & (upto > keep), jnp.int32(INT_MIN), blk)
            return before

        lax.fori_loop(0, nch, retire, jnp.zeros((1, Q_BLOCK), F32))

    q = q_ref[...]
    lane_q = lax.broadcasted_iota(I32, (Q_BLOCK, 2 * HEAD_DIM), 1)
    for p in range(ATTN_HEADS // 2):
        qp = q[:, p * 2 * HEAD_DIM:(p + 1) * 2 * HEAD_DIM]
        rhs_ref[p, 0:Q_BLOCK, :] = jnp.where(lane_q < HEAD_DIM, qp, jnp.zeros_like(qp))
        rhs_ref[p, Q_BLOCK:2 * Q_BLOCK, :] = jnp.where(lane_q >= HEAD_DIM, qp, jnp.zeros_like(qp))
    acc_ref[...] = jnp.zeros_like(acc_ref)

    def stage_a_prep(c):
        base = pl.multiple_of(c * KC, KC)
        tk = (base + row - (q0 + Q_BLOCK - 1)).astype(F32)
        tkm = jnp.where(keys_ref[pl.ds(base, KC), :] >= tau, tk, NEG_BIG)
        return tkm, k_ref[pl.ds(base, KC), :]

    def stage_a_pair(c, p, tkm, kc):
        l2 = _dot_nt(kc[:, p * 2 * HEAD_DIM:(p + 1) * 2 * HEAD_DIM], rhs_ref[p])
        mcs = []
        for hh in range(2):
            h = 2 * p + hh
            slope = 2.0 ** (-8.0 * (h + 1) / ATTN_HEADS)
            lg = l2[:, hh * Q_BLOCK:(hh + 1) * Q_BLOCK] + slope * tkm
            lg_ref[h] = lg
            mcs.append(jnp.max(_tree(lg, jnp.maximum), axis=0, keepdims=True))
        return mcs

    def stage_b_pair(c, p, m_new, alpha):
        sums = []
        for hh in range(2):
            h = 2 * p + hh
            pr = jnp.exp(lg_ref[h] - m_new[h:h + 1, :])
            sums.append(jnp.sum(_tree(pr, jnp.add), axis=0, keepdims=True))
            pv = _dot(vt_ref[c, h * HEAD_DIM:(h + 1) * HEAD_DIM, :], pr.astype(BF16))
            acc_ref[h * HEAD_DIM:(h + 1) * HEAD_DIM, :] = (
                acc_ref[h * HEAD_DIM:(h + 1) * HEAD_DIM, :] * alpha[h:h + 1, :] + pv)
        return sums

    def stage_b_all(c, m_old, l_old, mc):
        m_new = jnp.maximum(m_old, mc)
        alpha = jnp.exp(m_old - m_new)
        return m_new, alpha

    tkm0, kc0 = stage_a_prep(0)
    mc0 = jnp.concatenate(sum([stage_a_pair(0, p, tkm0, kc0) for p in range(ATTN_HEADS // 2)], []), axis=0)

    def attn_body(c, carry):
        m_old, l_old, mc_prev = carry
        m_new, alpha = stage_b_all(c - 1, m_old, l_old, mc_prev)
        tkm, kc = stage_a_prep(c)
        mcs, sums = [], []
        for p in range(ATTN_HEADS // 2):
            sums += stage_b_pair(c - 1, p, m_new, alpha)
            mcs += stage_a_pair(c, p, tkm, kc)
        return m_new, l_old * alpha + jnp.concatenate(sums, axis=0), jnp.concatenate(mcs, axis=0)

    m0 = jnp.full((ATTN_HEADS, Q_BLOCK), NEG_BIG, F32)
    l0 = jnp.zeros((ATTN_HEADS, Q_BLOCK), F32)
    m_old, l_old, mc_prev = lax.fori_loop(1, nch, attn_body, (m0, l0, mc0))
    m_new, alpha = stage_b_all(nch - 1, m_old, l_old, mc_prev)
    sums = sum([stage_b_pair(nch - 1, p, m_new, alpha) for p in range(ATTN_HEADS // 2)], [])
    l_fin = l_old * alpha + jnp.concatenate(sums, axis=0)

    outs = []
    for h in range(ATTN_HEADS):
        outs.append(acc_ref[h * HEAD_DIM:(h + 1) * HEAD_DIM, :] / l_fin[h:h + 1, :])
    o_ref[...] = jnp.concatenate(outs, axis=0).T


def _attn(it, ki, q, k, vt, B, S):
    T = B * S
    nb = S // Q_BLOCK
    n_sel = min(INDEX_TOPK_MAX, S // 4)
    return pl.pallas_call(
        functools.partial(_attn_kernel, n_sel),
        grid=(B, nb),
        in_specs=[
            pl.BlockSpec((IDXT_ROWS, Q_BLOCK), lambda b, i: (0, b * nb + i)),
            pl.BlockSpec((S, IDX_DIM), lambda b, i: (b, 0)),
            pl.BlockSpec((Q_BLOCK, ATTN_DIM), lambda b, i: (b * nb + i, 0)),
            pl.BlockSpec((S, ATTN_DIM), lambda b, i: (b, 0)),
            pl.BlockSpec((S // KEY_CHUNK, ATTN_DIM, KEY_CHUNK), lambda b, i: (b, 0, 0)),
        ],
        out_specs=pl.BlockSpec((Q_BLOCK, ATTN_DIM), lambda b, i: (b * nb + i, 0)),
        out_shape=jax.ShapeDtypeStruct((T, ATTN_DIM), F32),
        scratch_shapes=[
            pltpu.VMEM((S, Q_BLOCK), I32),
            pltpu.VMEM((-(-(S // KEY_CHUNK) // 2) * 2 * (KEY_CHUNK // FIELDS), Q_BLOCK), I32),
            pltpu.VMEM((ATTN_HEADS // 2, 2 * Q_BLOCK, 2 * HEAD_DIM), BF16),
            pltpu.VMEM((ATTN_DIM, Q_BLOCK), F32),
            pltpu.VMEM((ATTN_HEADS, KEY_CHUNK, Q_BLOCK), F32),
        ],
        compiler_params=_cparams(("arbitrary", "arbitrary")),
        name="attn",
    )(it, ki, q, k, vt)


def _mixout_kernel(tiles_per_seq, a_ref, halo_ref, y_ref, x_ref, cw_ref, wpool_ref, ps_ref,
                   wout_ref, g_ref, b_ref, o_ref, ext_ref):
    i = pl.program_id(0)
    tm = TM_MIX
    first = lax.rem(i, tiles_per_seq) == 0
    halo = jnp.where(first, 0.0, halo_ref[...])
    a = a_ref[...]
    h_c, gb_c, gc_c, p_c = (a[:, j * CONV_DIM:(j + 1) * CONV_DIM] for j in range(4))

    t0 = 2 * HALO
    n_ext = t0 + tm
    ext_ref[0:HALO, :] = jnp.zeros((HALO, CONV_DIM), F32)

    ext_ref[HALO:t0, :] = halo[:, 2 * CONV_DIM:3 * CONV_DIM] * halo[:, 0:CONV_DIM]
    u = gc_c * h_c
    ext_ref[t0:n_ext, :] = u
    cw = cw_ref[...]
    conv = cw[2:3, :] * u
    conv = conv + cw[1:2, :] * ext_ref[t0 - 1:n_ext - 1, :]
    conv = conv + cw[0:1, :] * ext_ref[t0 - 2:n_ext - 2, :]
    y_conv = gb_c * conv

    ext_ref[HALO:t0, :] = halo[:, OFF_P:OFF_P + POOL_DIM]
    ext_ref[t0:n_ext, :] = p_c
    sums = {}
    step = 1
    while step < POOL_WINDOWS[-1]:
        cur = ext_ref[HALO:n_ext, :] + ext_ref[HALO - step:n_ext - step, :]
        ext_ref[HALO:n_ext, :] = cur
        step *= 2
        sums[step] = ext_ref[t0:n_ext, :]
    tpos = (lax.rem(i, tiles_per_seq) * tm + lax.broadcasted_iota(I32, (tm, POOL_DIM), 0) + 1).astype(F32)
    grp = lax.shift_right_logical(lax.broadcasted_iota(I32, (tm, POOL_DIM), 1), _log2(POOL_GROUP))
    mean = jnp.zeros((tm, POOL_DIM), F32)
    for gi, w in enumerate(POOL_WINDOWS):
        mean = jnp.where(grp == gi, sums[w] / jnp.minimum(tpos, float(w)), mean)
    mixed = mean - p_c
    y_pool = _dot(mixed.astype(BF16), wpool_ref[...]) * ps_ref[...]

    mix = _dot(y_conv.astype(BF16), wout_ref[0:CONV_DIM, :])
    mix = mix + _dot(y_pool.astype(BF16), wout_ref[CONV_DIM:CONV_DIM + POOL_DIM, :])
    mix = mix + _dot(y_ref[...].astype(BF16), wout_ref[CONV_DIM + POOL_DIM:D_MODEL, :])
    z = DN_ALPHA * x_ref[...] + mix
    o_ref[...] = _layernorm_rows(z, g_ref[...], b_ref[...])


def _mixout(a, y_attn, x, cw, wpool_bd, ps, wout, g, b, S):
    T = a.shape[0]
    nt = T // TM_MIX
    tps = S // TM_MIX
    hb = TM_MIX // HALO
    full = lambda i: (0, 0)
    return pl.pallas_call(
        functools.partial(_mixout_kernel, tps),
        grid=(nt,),
        in_specs=[
            pl.BlockSpec((TM_MIX, OFF_Q), lambda i: (i, 0)),
            pl.BlockSpec((HALO, OFF_Q), lambda i: (jnp.maximum(i * hb - 1, 0), 0)),
            pl.BlockSpec((TM_MIX, ATTN_DIM), lambda i: (i, 0)),
            pl.BlockSpec((TM_MIX, D_MODEL), lambda i: (i, 0)),
            pl.BlockSpec((CONV_WIDTH, CONV_DIM), full),
            pl.BlockSpec((POOL_DIM, POOL_DIM), full),
            pl.BlockSpec((1, POOL_DIM), full),
            pl.BlockSpec((D_MODEL, D_MODEL), full),
            pl.BlockSpec((1, D_MODEL), full),
            pl.BlockSpec((1, D_MODEL), full),
        ],
        out_specs=pl.BlockSpec((TM_MIX, D_MODEL), lambda i: (i, 0)),
        out_shape=jax.ShapeDtypeStruct((T, D_MODEL), F32),
        scratch_shapes=[pltpu.VMEM((2 * HALO + TM_MIX, CONV_DIM), F32)],
        compiler_params=_cparams(("arbitrary",)),
        name="mixout",
    )(a, a, y_attn, x, cw, wpool_bd, ps, wout, g, b)


def _router_kernel(x_ref, rw_ref, rb_ref, idx_ref, gate_ref, cnt_ref):
    i = pl.program_id(0)
    tm = TM_ROUTE
    x = x_ref[...]
    w = rw_ref[...]
    xh = x.astype(BF16)
    xl = (x - xh.astype(F32)).astype(BF16)
    wh = w.astype(BF16)
    wl = (w - wh.astype(F32)).astype(BF16)
    logits = (_dot_nt(wh, xh) + (_dot_nt(wh, xl) + _dot_nt(wl, xh))) + rb_ref[...]
    erow = lax.broadcasted_iota(I32, (N_EXPERTS, tm), 0).astype(F32)
    work = logits
    vals, idxs = [], []
    multi = jnp.zeros((N_EXPERTS, tm), F32)
    for _ in range(TOP_K):
        mx = jnp.max(work, axis=0, keepdims=True)
        pick = jnp.min(jnp.where(work == mx, erow, float(N_EXPERTS)), axis=0, keepdims=True)
        hit = erow == pick
        work = jnp.where(hit, -jnp.inf, work)
        multi = multi + hit.astype(F32)
        vals.append(mx)
        idxs.append(pick)
    es = [jnp.exp(v - vals[0]) for v in vals]
    den = es[0] + es[1] + es[2] + es[3]
    idx_ref[...] = jnp.concatenate(idxs, axis=0).astype(I32)
    gate_ref[...] = jnp.concatenate([e / den for e in es] + [jnp.zeros((SUBLANES - TOP_K, tm), F32)], axis=0)
    part = multi[:, 0:LANES]
    for j in range(1, tm // LANES):
        part = part + multi[:, j * LANES:(j + 1) * LANES]

    @pl.when(i == 0)
    def _():
        cnt_ref[...] = jnp.zeros_like(cnt_ref)

    cnt_ref[...] += part


def _router(x1, rwT, rb):
    T = x1.shape[0]
    nt = T // TM_ROUTE
    full = lambda i: (0, 0)
    return pl.pallas_call(
        _router_kernel,
        grid=(nt,),
        in_specs=[
            pl.BlockSpec((TM_ROUTE, D_MODEL), lambda i: (i, 0)),
            pl.BlockSpec((N_EXPERTS, D_MODEL), full),
            pl.BlockSpec((N_EXPERTS, 1), full),
        ],
        out_specs=[
            pl.BlockSpec((TOP_K, TM_ROUTE), lambda i: (0, i)),
            pl.BlockSpec((SUBLANES, TM_ROUTE), lambda i: (0, i)),
            pl.BlockSpec((N_EXPERTS, LANES), full),
        ],
        out_shape=[
            jax.ShapeDtypeStruct((TOP_K, T), I32),
            jax.ShapeDtypeStruct((SUBLANES, T), F32),
            jax.ShapeDtypeStruct((N_EXPERTS, LANES), F32),
        ],
        compiler_params=_cparams(("arbitrary",)),
        name="router",
    )(x1, rwT, rb)


def _ranks_kernel(n_tiles_pad, idx_ref, cnt_ref, dest_ref, meta_ref, tri_ref, start_ref, carry_ref):
    i = pl.program_id(0)
    tm = TM_ROUTE
    erow = lax.broadcasted_iota(I32, (N_EXPERTS, LANES), 0)
    elane = lax.broadcasted_iota(I32, (N_EXPERTS, LANES), 1)

    @pl.when(i == 0)
    def _():
        cnt = jnp.sum(cnt_ref[...], axis=1, keepdims=True)
        cnt_i = jnp.broadcast_to(cnt, (N_EXPERTS, LANES)).astype(I32)
        padded = lax.shift_left(lax.shift_right_logical(cnt_i + (TM_MOE - 1), _log2(TM_MOE)), _log2(TM_MOE))
        r = lax.broadcasted_iota(I32, (N_EXPERTS, N_EXPERTS), 0)
        c = lax.broadcasted_iota(I32, (N_EXPERTS, N_EXPERTS), 1)
        low = (c <= r).astype(F32)
        pad_end = lax.dot_general(low, padded.astype(F32), (((1,), (0,)), ((), ())),
                                  precision=lax.Precision.HIGHEST, preferred_element_type=F32)
        pad_start = pad_end - padded.astype(F32)
        start_ref[...] = pad_start
        carry_ref[...] = jnp.zeros_like(carry_ref)
        a = lax.broadcasted_iota(I32, (tm, tm), 0)
        bcol = lax.broadcasted_iota(I32, (tm, tm), 1)
        tri_ref[...] = (a < bcol).astype(BF16)
        ntp = n_tiles_pad
        tstart = (lax.broadcasted_iota(I32, (N_EXPERTS, ntp), 1) * TM_MOE).astype(F32)
        pe = jnp.concatenate([pad_end] * (ntp // LANES), axis=1)
        texp = jnp.sum((pe <= tstart).astype(F32), axis=0, keepdims=True)
        texp = jnp.minimum(texp, float(N_EXPERTS - 1)).astype(I32)
        total = jnp.max(pad_end, axis=0, keepdims=True)
        n_used = lax.shift_right_logical(total.astype(I32), _log2(TM_MOE))
        zstart = jnp.sum(jnp.where(erow == elane, pad_start + cnt_i.astype(F32), 0.0),
                         axis=0, keepdims=True).astype(I32)
        zlen = jnp.sum(jnp.where(erow == elane, (padded - cnt_i).astype(F32), 0.0),
                       axis=0, keepdims=True).astype(I32)
        etiles = lax.shift_right_logical(
            jnp.sum(jnp.where(erow == elane, padded.astype(F32), 0.0), axis=0, keepdims=True).astype(I32),
            _log2(TM_MOE))
        lanes_pad = jnp.zeros((1, ntp - LANES), I32)
        meta_ref[...] = jnp.concatenate(
            [texp,
             jnp.concatenate([n_used, lanes_pad], axis=1),
             jnp.concatenate([zstart, lanes_pad], axis=1),
             jnp.concatenate([zlen, lanes_pad], axis=1),
             jnp.concatenate([etiles, lanes_pad], axis=1),
             jnp.zeros((SUBLANES - 5, ntp), I32)], axis=0)

    idx = idx_ref[...]
    erow_t = lax.broadcasted_iota(I32, (N_EXPERTS, tm), 0)
    hits = [erow_t == idx[k:k + 1, :] for k in range(TOP_K)]
    multi = hits[0].astype(F32)
    for k in range(1, TOP_K):
        multi = multi + hits[k].astype(F32)
    prefix = _dot(multi.astype(BF16), tri_ref[...])
    base = jnp.concatenate([carry_ref[...] + start_ref[...]] * (tm // LANES), axis=1)
    tot = prefix + base
    dest_ref[...] = jnp.concatenate(
        [jnp.sum(jnp.where(hits[k], tot, 0.0), axis=0, keepdims=True) for k in range(TOP_K)],
        axis=0).astype(I32)
    carry_ref[...] += jnp.broadcast_to(jnp.sum(multi, axis=1, keepdims=True), (N_EXPERTS, LANES))


def _ranks(idxT, cnt, n_tiles_pad):
    T = idxT.shape[1]
    nt = T // TM_ROUTE
    full = lambda i: (0, 0)
    return pl.pallas_call(
        functools.partial(_ranks_kernel, n_tiles_pad),
        grid=(nt,),
        in_specs=[
            pl.BlockSpec((TOP_K, TM_ROUTE), lambda i: (0, i)),
            pl.BlockSpec((N_EXPERTS, LANES), full),
        ],
        out_specs=[
            pl.BlockSpec((TOP_K, TM_ROUTE), lambda i: (0, i)),
            pl.BlockSpec((SUBLANES, n_tiles_pad), full),
        ],
        out_shape=[
            jax.ShapeDtypeStruct((TOP_K, T), I32),
            jax.ShapeDtypeStruct((SUBLANES, n_tiles_pad), I32),
        ],
        scratch_shapes=[
            pltpu.VMEM((TM_ROUTE, TM_ROUTE), BF16),
            pltpu.VMEM((N_EXPERTS, LANES), F32),
            pltpu.VMEM((N_EXPERTS, LANES), F32),
        ],
        compiler_params=_cparams(("arbitrary",)),
        name="ranks",
    )(idxT, cnt)


RT = D_MODEL // LANES
DMA_UNROLL = 8


def _row_copy(src_ref, s, dst_ref, d, sem):
    return pltpu.make_async_copy(src_ref.at[pl.ds(pl.multiple_of(s * RT, RT), RT)],
                                 dst_ref.at[pl.ds(pl.multiple_of(d * RT, RT), RT)], sem)


def _to_row_tiled(dst_ref, val, rows):
    for s in range(RT):
        dst_ref[pl.ds(s, rows, stride=RT), :] = val[:, s * LANES:(s + 1) * LANES]


def _from_row_tiled(src_ref, rows):
    return [src_ref[pl.ds(s, rows, stride=RT), :] for s in range(RT)]


def _scatter_kernel(meta_ref, x_ref, dest_ref, xs_ref, stage_ref, zero_ref, sem):
    i = pl.program_id(0)
    tm = TM_ROUTE

    @pl.when(i == 0)
    def _():
        zero_ref[...] = jnp.zeros_like(zero_ref)

        def zcopy(row, nrows):
            off = pl.multiple_of(row * RT, RT)
            return pltpu.make_async_copy(zero_ref.at[pl.ds(0, nrows * RT)],
                                         xs_ref.at[pl.ds(off, nrows * RT)], sem)

        def pad_fill(wait):
            def body(e, c):
                row = meta_ref[2, e]
                plen = meta_ref[3, e]
                for bit in reversed(range(_log2(TM_MOE))):
                    size = 1 << bit
                    has = (plen & size) != 0

                    @pl.when(has)
                    def _():
                        cp = zcopy(row, size)
                        cp.wait() if wait else cp.start()

                    row = row + jnp.where(has, size, 0)
                return c
            lax.fori_loop(0, N_EXPERTS, body, 0)

        def tail_fill(wait):
            def body(j, c):
                cp = zcopy(j * TM_MOE, TM_MOE)
                cp.wait() if wait else cp.start()
                return c
            lax.fori_loop(meta_ref[1, 0], xs_ref.shape[0] // (TM_MOE * RT), body, 0)

        pad_fill(False)
        tail_fill(False)
        pad_fill(True)
        tail_fill(True)

    _to_row_tiled(stage_ref, x_ref[...], tm)

    def start(t, c):
        for k in range(TOP_K):
            _row_copy(stage_ref, t, xs_ref, dest_ref[k, t], sem).start(priority=k % 2)
        return c

    lax.fori_loop(0, tm, start, 0, unroll=DMA_UNROLL)
    for k in range(TOP_K):
        pltpu.make_async_copy(stage_ref, xs_ref.at[pl.ds(0, tm * RT)], sem).wait()


def _scatter(meta, x1, destT, n_rows_alloc):
    T = x1.shape[0]
    nt = T // TM_ROUTE
    return pl.pallas_call(
        _scatter_kernel,
        grid_spec=pltpu.PrefetchScalarGridSpec(
            num_scalar_prefetch=1,
            grid=(nt,),
            in_specs=[
                pl.BlockSpec((TM_ROUTE, D_MODEL), lambda i, m: (i, 0)),
                pl.BlockSpec((TOP_K, TM_ROUTE), lambda i, m: (0, i), memory_space=pltpu.SMEM),
            ],
            out_specs=pl.BlockSpec(memory_space=pl.ANY),
            scratch_shapes=[
                pltpu.VMEM((TM_ROUTE * RT, LANES), F32),
                pltpu.VMEM((TM_MOE * RT, LANES), F32),
                pltpu.SemaphoreType.DMA(()),
            ],
        ),
        out_shape=jax.ShapeDtypeStruct((n_rows_alloc * RT, LANES), F32),
        compiler_params=_cparams(("arbitrary",)),
        name="scatter",
    )(meta, x1, destT)


def _gmm_kernel(layer, meta_ref, xs_ref, wgu_hbm, bgu_ref, wd_hbm, bd_ref, ys_ref,
                wgu_f, wd_f, wgu_b, wd_b, lhs_ref, act_ref, grp_ref, sems):
    j = pl.program_id(0)
    n_used = meta_ref[1, 0]
    e_now = meta_ref[0, j]
    e_prev = meta_ref[0, jnp.maximum(j - 1, 0)]
    used = j < n_used

    def weight_copies(e, slot):
        return (pltpu.make_async_copy(wgu_hbm.at[layer, e], wgu_f.at[slot], sems.at[0, slot]),
                pltpu.make_async_copy(wd_hbm.at[layer, e], wd_f.at[slot], sems.at[1, slot]))

    @pl.when(used & (j == 0))
    def _():
        grp_ref[0] = 0
        for cp in weight_copies(e_now, 0):
            cp.start()

    @pl.when(used & ((j == 0) | (e_now != e_prev)))
    def _():
        slot = grp_ref[0]
        j_next = j + meta_ref[4, e_now]

        @pl.when(j_next < n_used)
        def _():
            for cp in weight_copies(meta_ref[0, jnp.minimum(j_next, meta_ref.shape[1] - 1)], 1 - slot):
                cp.start()

        for cp in weight_copies(e_now, slot):
            cp.wait()
        wgu_b[...] = wgu_f[slot].astype(BF16)
        wd_b[...] = wd_f[slot].astype(BF16)
        grp_ref[0] = 1 - slot

    @pl.when(used)
    def _():
        for s, piece in enumerate(_from_row_tiled(xs_ref, TM_MOE)):
            lhs_ref[:, s * LANES:(s + 1) * LANES] = piece.astype(BF16)
        x = lhs_ref[...]
        nc = 512
        for c in range(D_FF // nc):
            gate = _dot(x, wgu_b[:, c * nc:(c + 1) * nc]) + bgu_ref[0, 0, :, c * nc:(c + 1) * nc]
            up = (_dot(x, wgu_b[:, D_FF + c * nc:D_FF + (c + 1) * nc])
                  + bgu_ref[0, 0, :, D_FF + c * nc:D_FF + (c + 1) * nc])
            gate = jnp.minimum(gate, SWIGLU_LIMIT)
            up = jnp.clip(up, -SWIGLU_LIMIT, SWIGLU_LIMIT)
            act = gate * jax.nn.sigmoid(SWIGLU_ALPHA * gate) * (up + 1.0)
            act_ref[:, c * nc:(c + 1) * nc] = act.astype(BF16)
        _to_row_tiled(ys_ref, _dot(act_ref[...], wd_b[...]) + bd_ref[0, 0], TM_MOE)

    @pl.when(jnp.logical_not(used))
    def _():
        ys_ref[...] = jnp.zeros_like(ys_ref)


def _gmm(meta, xs, w_gu, b_gu, w_down, b_down, n_tiles, layer):
    last = lambda m: jnp.maximum(m[1, 0] - 1, 0)
    return pl.pallas_call(
        functools.partial(_gmm_kernel, layer),
        grid_spec=pltpu.PrefetchScalarGridSpec(
            num_scalar_prefetch=1,
            grid=(n_tiles,),
            in_specs=[
                pl.BlockSpec((TM_MOE * RT, LANES), lambda j, m: (jnp.minimum(j, last(m)), 0)),
                pl.BlockSpec(memory_space=pl.ANY),
                pl.BlockSpec((1, 1, 1, 2 * D_FF), lambda j, m: (layer, m[0, j], 0, 0)),
                pl.BlockSpec(memory_space=pl.ANY),
                pl.BlockSpec((1, 1, 1, D_MODEL), lambda j, m: (layer, m[0, j], 0, 0)),
            ],
            out_specs=pl.BlockSpec((TM_MOE * RT, LANES), lambda j, m: (j, 0)),
            scratch_shapes=[
                pltpu.VMEM((2, D_MODEL, 2 * D_FF), F32),
                pltpu.VMEM((2, D_FF, D_MODEL), F32),
                pltpu.VMEM((D_MODEL, 2 * D_FF), BF16),
                pltpu.VMEM((D_FF, D_MODEL), BF16),
                pltpu.VMEM((TM_MOE, D_MODEL), BF16),
                pltpu.VMEM((TM_MOE, D_FF), BF16),
                pltpu.SMEM((1,), I32),
                pltpu.SemaphoreType.DMA((2, 2)),
            ],
        ),
        out_shape=jax.ShapeDtypeStruct((n_tiles * TM_MOE * RT, LANES), F32),
        compiler_params=_cparams(("arbitrary",)),
        name="gmm",
    )(meta, xs, w_gu, b_gu, w_down, b_down)


def _combine_kernel(x_ref, dest_ref, gate_ref, ys_ref, g_ref, b_ref, o_ref, ob_ref, buf_ref, sem):
    tm = TM_COMB

    def start(t, c):
        for k in range(TOP_K):
            _row_copy(ys_ref, dest_ref[k, t], buf_ref.at[k], t, sem).start(priority=k % 2)
        return c

    lax.fori_loop(0, tm, start, 0, unroll=DMA_UNROLL)
    for k in range(TOP_K):
        pltpu.make_async_copy(ys_ref.at[pl.ds(0, tm * RT)], buf_ref.at[k], sem).wait()
    gates = jnp.concatenate(
        [gate_ref[...], jnp.zeros((LANES - SUBLANES, tm), F32)], axis=0).T
    pieces = None
    for k in range(TOP_K):
        gk = gates[:, k:k + 1]
        rows = [p * gk for p in _from_row_tiled(buf_ref.at[k], tm)]
        pieces = rows if pieces is None else [a + r for a, r in zip(pieces, rows)]
    z = DN_ALPHA * x_ref[...] + jnp.concatenate(pieces, axis=1)
    out = _layernorm_rows(z, g_ref[...], b_ref[...])
    o_ref[...] = out
    ob_ref[...] = out.astype(BF16)


def _combine(x1, destT, gateT, ys, g, b):
    T = x1.shape[0]
    nt = T // TM_COMB
    full = lambda i: (0, 0)
    return pl.pallas_call(
        _combine_kernel,
        grid=(nt,),
        in_specs=[
            pl.BlockSpec((TM_COMB, D_MODEL), lambda i: (i, 0)),
            pl.BlockSpec((TOP_K, TM_COMB), lambda i: (0, i), memory_space=pltpu.SMEM),
            pl.BlockSpec((SUBLANES, TM_COMB), lambda i: (0, i)),
            pl.BlockSpec(memory_space=pl.ANY),
            pl.BlockSpec((1, D_MODEL), full),
            pl.BlockSpec((1, D_MODEL), full),
        ],
        out_specs=[
            pl.BlockSpec((TM_COMB, D_MODEL), lambda i: (i, 0)),
            pl.BlockSpec((TM_COMB, D_MODEL), lambda i: (i, 0)),
        ],
        out_shape=[
            jax.ShapeDtypeStruct((T, D_MODEL), F32),
            jax.ShapeDtypeStruct((T, D_MODEL), BF16),
        ],
        scratch_shapes=[
            pltpu.VMEM((TOP_K, TM_COMB * RT, LANES), F32),
            pltpu.SemaphoreType.DMA(()),
        ],
        compiler_params=_cparams(("arbitrary",)),
        name="combine",
    )(x1, destT, gateT, ys, g, b)


def _block_diag(w):
    g, c, _ = w.shape
    out = jnp.zeros((g * c, g * c), w.dtype)
    for i in range(g):
        out = out.at[i * c:(i + 1) * c, i * c:(i + 1) * c].set(w[i])
    return out


def kernel(x, w_in, conv_w, w_pool, pool_scale, idx_kn_g, idx_kn_b, w_out, ln1_g, ln1_b, router_w,
           router_b, w_gu, b_gu, w_down, b_down, ln2_g, ln2_b):
    B, S, D = x.shape
    T = B * S
    depth = w_in.shape[0]
    n_pairs = T * TOP_K
    n_tiles = (n_pairs + N_EXPERTS * (TM_MOE - 1)) // TM_MOE + 1
    n_tiles_pad = ((n_tiles + LANES - 1) // LANES) * LANES
    n_rows_alloc = (n_tiles + 1) * TM_MOE

    xf = x.reshape(T, D)
    xb = xf
    for l in range(depth):
        wl = w_in[l]
        wn = jnp.concatenate(
            [wl[:, 0:OFF_V], wl[:, OFF_KI:OFF_WI], jnp.zeros((D, LANES - IDX_DIM), F32)], axis=1).astype(BF16)
        wt = jnp.concatenate(
            [wl[:, OFF_V:OFF_QI], wl[:, OFF_QI:OFF_KI], wl[:, OFF_WI:D_IN],
             jnp.zeros((D, IDXT_ROWS - IDX_HEADS * IDX_DIM - IDX_HEADS), F32)], axis=1).T.astype(BF16)
        a, q, k, ki, vt, it = _inproj(xb, wn, wt, idx_kn_g[l].reshape(1, -1), idx_kn_b[l].reshape(1, -1))
        y_attn = _attn(it, ki, q, k, vt, B, S)
        x1 = _mixout(a, y_attn, xf, conv_w[l], _block_diag(w_pool[l]).astype(BF16),
                     pool_scale[l].reshape(1, -1), w_out[l].astype(BF16),
                     ln1_g[l].reshape(1, -1), ln1_b[l].reshape(1, -1), S)
        idxT, gateT, cnt = _router(x1, router_w[l].T, router_b[l].reshape(-1, 1))
        destT, meta = _ranks(idxT, cnt, n_tiles_pad)
        xs = _scatter(meta, x1, destT, n_rows_alloc)
        ys = _gmm(meta, xs, w_gu, b_gu.reshape(depth, N_EXPERTS, 1, -1), w_down,
                  b_down.reshape(depth, N_EXPERTS, 1, -1), n_tiles, l)
        xf, xb = _combine(x1, destT, gateT, ys, ln2_g[l].reshape(1, -1), ln2_b[l].reshape(1, -1))
    return xf.reshape(B, S, D)
```

```python
import functools

import jax
import jax.numpy as jnp
import numpy as np
from jax import lax
from jax.experimental import pallas as pl
from jax.experimental.pallas import tpu as pltpu

F32 = jnp.float32
BF16 = jnp.bfloat16
I32 = jnp.int32

D_MODEL = 1024
CONV_DIM = 256
CONV_WIDTH = 3
POOL_WINDOWS = (2, 4, 8, 16)
POOL_GROUP = 64
POOL_DIM = 256
ATTN_HEADS = 8
HEAD_DIM = 64
ATTN_DIM = 512
IDX_HEADS = 8
IDX_DIM = 32
INDEX_TOPK_MAX = 256
Q_BLOCK = 128
N_EXPERTS = 32
TOP_K = 4
D_FF = 1024
SWIGLU_ALPHA = 1.702
SWIGLU_LIMIT = 7.0
LN_EPS = 1e-5
DEPTH = 2
DN_ALPHA = (2.0 * DEPTH) ** 0.25

OFF_P = 3 * CONV_DIM
OFF_Q = OFF_P + POOL_DIM
OFF_K = OFF_Q + ATTN_DIM
OFF_V = OFF_K + ATTN_DIM
OFF_QI = OFF_V + ATTN_DIM
OFF_KI = OFF_QI + IDX_HEADS * IDX_DIM
OFF_WI = OFF_KI + IDX_DIM
D_IN = OFF_WI + IDX_HEADS

LANES = 128
SUBLANES = 8
VMEM_LIMIT = 56 * 1024 * 1024
KEY_CHUNK = 512
TIE_GROUP = 128
HALO = 16
TM_PROJ = 512
TM_MIX = 1024
TM_ROUTE = 1024
TM_MOE = 512
TM_COMB = 1024
IDXT_ROWS = IDX_HEADS * IDX_DIM + 16
WT_ROWS = ATTN_DIM + IDXT_ROWS
INT_MIN = -(2 ** 31)
NEG_BIG = -1e30


def _cparams(sem, flags=None):
    return pltpu.CompilerParams(dimension_semantics=sem, vmem_limit_bytes=VMEM_LIMIT, flags=flags)


def _log2(n):
    k = n.bit_length() - 1
    assert 1 << k == n
    return k


def _dot(a, b):
    return jnp.dot(a, b, preferred_element_type=F32)


def _dot_nt(a, b):
    return lax.dot_general(a, b, (((1,), (1,)), ((), ())), preferred_element_type=F32)


def _tree(x, op):
    parts = [x[j * SUBLANES:(j + 1) * SUBLANES, :] for j in range(x.shape[0] // SUBLANES)]
    while len(parts) > 1:
        nxt = [op(parts[j], parts[j + 1]) for j in range(0, len(parts) - 1, 2)]
        if len(parts) % 2:
            nxt.append(parts[-1])
        parts = nxt
    return parts[0]


def _layernorm_rows(z, g, b):
    mu = jnp.mean(z, axis=-1, keepdims=True)
    d = z - mu
    var = jnp.mean(d * d, axis=-1, keepdims=True)
    return d * lax.rsqrt(var + LN_EPS) * g + b


def _inproj_kernel(x_ref, wn_ref, wt_ref, kg_ref, kb_ref,
                   a_ref, q_ref, k_ref, ki_ref, vt_ref, it_ref):
    x = x_ref[...].astype(BF16)
    a_ref[...] = _dot(x, wn_ref[:, 0:OFF_Q])
    q_ref[...] = (_dot(x, wn_ref[:, OFF_Q:OFF_K]) * (HEAD_DIM ** -0.5)).astype(BF16)
    k_ref[...] = _dot(x, wn_ref[:, OFF_K:OFF_V]).astype(BF16)
    ki = _dot(x, wn_ref[:, OFF_V:OFF_V + LANES])[:, 0:IDX_DIM]
    ki_ref[...] = _layernorm_rows(ki, kg_ref[...], kb_ref[...])
    t = _dot_nt(wt_ref[...], x)
    for c in range(TM_PROJ // KEY_CHUNK):
        vt_ref[c] = t[0:ATTN_DIM, c * KEY_CHUNK:(c + 1) * KEY_CHUNK].astype(BF16)
    it_ref[...] = t[ATTN_DIM:WT_ROWS, :]


def _inproj(xb, wn, wt, kg, kb):
    T = xb.shape[0]
    nt = T // TM_PROJ
    cpt = TM_PROJ // KEY_CHUNK
    full = lambda i: (0, 0)
    return pl.pallas_call(
        _inproj_kernel,
        grid=(nt,),
        in_specs=[
            pl.BlockSpec((TM_PROJ, D_MODEL), lambda i: (i, 0)),
            pl.BlockSpec(wn.shape, full),
            pl.BlockSpec(wt.shape, full),
            pl.BlockSpec((1, IDX_DIM), full),
            pl.BlockSpec((1, IDX_DIM), full),
        ],
        out_specs=[
            pl.BlockSpec((TM_PROJ, OFF_Q), lambda i: (i, 0)),
            pl.BlockSpec((TM_PROJ, ATTN_DIM), lambda i: (i, 0)),
            pl.BlockSpec((TM_PROJ, ATTN_DIM), lambda i: (i, 0)),
            pl.BlockSpec((TM_PROJ, IDX_DIM), lambda i: (i, 0)),
            pl.BlockSpec((cpt, ATTN_DIM, KEY_CHUNK), lambda i: (i, 0, 0)),
            pl.BlockSpec((IDXT_ROWS, TM_PROJ), lambda i: (0, i)),
        ],
        out_shape=[
            jax.ShapeDtypeStruct((T, OFF_Q), F32),
            jax.ShapeDtypeStruct((T, ATTN_DIM), BF16),
            jax.ShapeDtypeStruct((T, ATTN_DIM), BF16),
            jax.ShapeDtypeStruct((T, IDX_DIM), F32),
            jax.ShapeDtypeStruct((T // KEY_CHUNK, ATTN_DIM, KEY_CHUNK), BF16),
            jax.ShapeDtypeStruct((IDXT_ROWS, T), F32),
        ],
        compiler_params=_cparams(("arbitrary",)),
        name="inproj",
    )(xb, wn, wt, kg, kb)


FIELD_BITS = 8
DIGIT_BITS = FIELD_BITS - 1
FIELDS = 32 // FIELD_BITS
DIGIT_SHIFTS = (25, 18, 11, 4, 0)
EARLY_EXIT_FROM = 4
FIELD_ONES = 0x01010101
FIELD_GUARDS = 0x80808080 - (1 << 32)


def _pack_fields(f):
    q = f.shape[0] // FIELDS
    w = f[0:q]
    for j in range(1, FIELDS):
        w = w | lax.shift_left(f[j * q:(j + 1) * q], FIELD_BITS * j)
    return w | jnp.int32(FIELD_GUARDS)


def _count_fields_ge(words_ref, nch, cand):
    wq = 2 * (KEY_CHUNK // FIELDS)
    assert words_ref.shape[0] // SUBLANES < (1 << FIELD_BITS)
    cvec = cand * jnp.int32(FIELD_ONES)

    def body(c, acc):
        base = pl.multiple_of(c * wq, wq)
        w = words_ref[pl.ds(base, wq), :]
        hit = lax.shift_right_logical(w - cvec, DIGIT_BITS) & jnp.int32(FIELD_ONES)
        return acc + _tree(hit, jnp.add)

    acc = lax.fori_loop(0, lax.shift_right_logical(nch + 1, 1), body,
                        jnp.zeros((SUBLANES, words_ref.shape[1]), I32))
    tot = acc & 255
    for j in range(1, FIELDS):
        tot = tot + (lax.shift_right_logical(acc, FIELD_BITS * j) & 255)
    return jnp.sum(tot.astype(F32), axis=0, keepdims=True)


def _topk_threshold(keys_ref, words_ref, nch, n_sel, n_causal):
    lanes = keys_ref.shape[1]
    wq = KEY_CHUNK // FIELDS
    all_taken = n_causal <= n_sel

    def run_digit(idx, state):
        prefix, want, bucket = state
        shift = DIGIT_SHIFTS[idx]
        width = (DIGIT_SHIFTS[idx - 1] if idx else 32) - shift
        if idx:
            def prep(c, carry):
                key = keys_ref[pl.ds(pl.multiple_of(c * KEY_CHUNK, KEY_CHUNK), KEY_CHUNK), :]
                digit = (key >> shift) & jnp.int32((1 << width) - 1)
                fields = jnp.where((key >> (shift + width)) == prefix, digit, 0)
                words_ref[pl.ds(pl.multiple_of(c * wq, wq), wq), :] = _pack_fields(fields)
                return carry

            lax.fori_loop(0, nch, prep, 0)
        dig = jnp.zeros((1, lanes), I32)
        at_dig = bucket
        above = jnp.zeros((1, lanes), F32)
        for bit in reversed(range(width)):
            cand = dig + (1 << bit)
            cnt = _count_fields_ge(words_ref, nch, cand)
            ok = cnt >= want
            dig = jnp.where(ok, cand, dig)
            at_dig = jnp.where(ok, cnt, at_dig)
            above = jnp.where(ok, above, cnt)
        prefix = (lax.shift_left(prefix, width) | dig) if idx else dig - (1 << (DIGIT_BITS - 1))
        return prefix, want - above, at_dig - above

    def finish(state, low_bits):
        prefix, want, bucket = state
        tau = jnp.where(all_taken, jnp.int32(INT_MIN + 1), lax.shift_left(prefix, low_bits))
        return tau, want, jnp.where(all_taken, 0.0, bucket - want)

    def settled(state):
        _, want, bucket = state
        return jnp.min(jnp.where((bucket == want) | all_taken, 1.0, 0.0)) > 0.0

    def from_digit(idx, state):
        if idx == len(DIGIT_SHIFTS):
            return finish(state, 0)
        if idx < EARLY_EXIT_FROM:
            return from_digit(idx + 1, run_digit(idx, state))
        return lax.cond(settled(state),
                        lambda s: finish(s, DIGIT_SHIFTS[idx - 1]),
                        lambda s: from_digit(idx + 1, run_digit(idx, s)), state)

    state = (jnp.zeros((1, lanes), I32), jnp.full((1, lanes), float(n_sel), F32), n_causal.astype(F32))
    return from_digit(0, state)


def _attn_kernel(n_sel, it_ref, ki_ref, q_ref, k_ref, vt_ref, o_ref,
                 keys_ref, words_ref, rhs_ref, acc_ref, lg_ref):
    i = pl.program_id(1)
    nch = lax.shift_right_logical(i * Q_BLOCK + Q_BLOCK + KEY_CHUNK - 1, _log2(KEY_CHUNK))
    q0 = i * Q_BLOCK
    KC = KEY_CHUNK
    row = lax.broadcasted_iota(I32, (KC, Q_BLOCK), 0)
    lane = lax.broadcasted_iota(I32, (KC, Q_BLOCK), 1)
    qpos = q0 + lane

    qcat = jnp.concatenate(
        [it_ref[h * IDX_DIM:(h + 1) * IDX_DIM, :] for h in range(IDX_HEADS)], axis=1).astype(BF16)
    w_all = it_ref[IDX_HEADS * IDX_DIM:IDX_HEADS * IDX_DIM + IDX_HEADS, :] * (
        (IDX_HEADS ** -0.5) * (IDX_DIM ** -0.5))

    def score_chunk(c, has_future_keys):
        base = pl.multiple_of(c * KC, KC)
        kic = ki_ref[pl.ds(base, KC), :].astype(BF16)
        s = _dot(kic, qcat)
        sc = jnp.zeros((KC, Q_BLOCK), F32)
        for h in range(IDX_HEADS):
            sc = sc + jnp.maximum(s[:, h * Q_BLOCK:(h + 1) * Q_BLOCK], 0.0) * w_all[h:h + 1, :]
        bits = lax.bitcast_convert_type(sc, I32)
        key = bits ^ ((bits >> 31) & jnp.int32(0x7FFFFFFF))
        key = jnp.where(bits == jnp.int32(INT_MIN), jnp.int32(0), key)
        top = (key >> DIGIT_SHIFTS[0]) + (1 << (DIGIT_BITS - 1))
        if has_future_keys:
            causal = base + row <= qpos
            key = jnp.where(causal, key, jnp.int32(INT_MIN))
            top = jnp.where(causal, top, 0)
        keys_ref[pl.ds(base, KC), :] = key
        words_ref[pl.ds(pl.multiple_of(c * (KC // FIELDS), KC // FIELDS), KC // FIELDS), :] = _pack_fields(top)

    def score_body(c, carry):
        score_chunk(c, False)
        return carry

    lax.fori_loop(0, nch - 1, score_body, 0)
    score_chunk(nch - 1, True)

    @pl.when(lax.rem(nch, 2) == 1)
    def _():
        wq = KC // FIELDS
        words_ref[pl.ds(pl.multiple_of(nch * wq, wq), wq), :] = _pack_fields(jnp.zeros((KC, Q_BLOCK), I32))

    n_causal = q0 + lax.broadcasted_iota(I32, (1, Q_BLOCK), 1) + 1
    tau, keep, excess = _topk_threshold(keys_ref, words_ref, nch, n_sel, n_causal)

    @pl.when(jnp.max(excess) > 0.0)
    def _():
        g = TIE_GROUP
        r = lax.broadcasted_iota(I32, (g, g), 0)
        cc = lax.broadcasted_iota(I32, (g, g), 1)
        tri = (cc <= r).astype(BF16)

        def retire(c, before):
            base = pl.multiple_of(c * KC, KC)
            blk = keys_ref[pl.ds(base, KC), :]
            tied = blk == tau
            ones = jnp.where(tied, 1.0, 0.0).astype(BF16)
            local = [_dot(tri, ones[j * g:(j + 1) * g, :]) for j in range(KC // g)]
            upto = []
            for lj in local:
                upto.append(lj + before)
                before = before + lj[g - 1:g, :]
            upto = jnp.concatenate(upto, axis=0)
            keys_ref[pl.ds(base, KC), :] = jnp.where(tied & (upto > keep), jnp.int32(INT_MIN), blk)
            return before

        lax.fori_loop(0, nch, retire, jnp.zeros((1, Q_BLOCK), F32))

    q = q_ref[...]
    lane_q = lax.broadcasted_iota(I32, (Q_BLOCK, 2 * HEAD_DIM), 1)
    for p in range(ATTN_HEADS // 2):
        qp = q[:, p * 2 * HEAD_DIM:(p + 1) * 2 * HEAD_DIM]
        rhs_ref[p, 0:Q_BLOCK, :] = jnp.where(lane_q < HEAD_DIM, qp, jnp.zeros_like(qp))
        rhs_ref[p, Q_BLOCK:2 * Q_BLOCK, :] = jnp.where(lane_q >= HEAD_DIM, qp, jnp.zeros_like(qp))
    acc_ref[...] = jnp.zeros_like(acc_ref)

    def stage_a_prep(c):
        base = pl.multiple_of(c * KC, KC)
        tk = (base + row - (q0 + Q_BLOCK - 1)).astype(F32)
        tkm = jnp.where(keys_ref[pl.ds(base, KC), :] >= tau, tk, NEG_BIG)
        return tkm, k_ref[pl.ds(base, KC), :]

    def stage_a_pair(c, p, tkm, kc):
        l2 = _dot_nt(kc[:, p * 2 * HEAD_DIM:(p + 1) * 2 * HEAD_DIM], rhs_ref[p])
        mcs = []
        for hh in range(2):
            h = 2 * p + hh
            slope = 2.0 ** (-8.0 * (h + 1) / ATTN_HEADS)
            lg = l2[:, hh * Q_BLOCK:(hh + 1) * Q_BLOCK] + slope * tkm
            lg_ref[h] = lg
            mcs.append(jnp.max(_tree(lg, jnp.maximum), axis=0, keepdims=True))
        return mcs

    def stage_b_pair(c, p, m_new, alpha):
        sums = []
        for hh in range(2):
            h = 2 * p + hh
            pr = jnp.exp(lg_ref[h] - m_new[h:h + 1, :])
            sums.append(jnp.sum(_tree(pr, jnp.add), axis=0, keepdims=True))
            pv = _dot(vt_ref[c, h * HEAD_DIM:(h + 1) * HEAD_DIM, :], pr.astype(BF16))
            acc_ref[h * HEAD_DIM:(h + 1) * HEAD_DIM, :] = (
                acc_ref[h * HEAD_DIM:(h + 1) * HEAD_DIM, :] * alpha[h:h + 1, :] + pv)
        return sums

    def stage_b_all(c, m_old, l_old, mc):
        m_new = jnp.maximum(m_old, mc)
        alpha = jnp.exp(m_old - m_new)
        return m_new, alpha

    tkm0, kc0 = stage_a_prep(0)
    mc0 = jnp.concatenate(sum([stage_a_pair(0, p, tkm0, kc0) for p in range(ATTN_HEADS // 2)], []), axis=0)

    def attn_body(c, carry):
        m_old, l_old, mc_prev = carry
        m_new, alpha = stage_b_all(c - 1, m_old, l_old, mc_prev)
        tkm, kc = stage_a_prep(c)
        mcs, sums = [], []
        for p in range(ATTN_HEADS // 2):
            sums += stage_b_pair(c - 1, p, m_new, alpha)
            mcs += stage_a_pair(c, p, tkm, kc)
        return m_new, l_old * alpha + jnp.concatenate(sums, axis=0), jnp.concatenate(mcs, axis=0)

    m0 = jnp.full((ATTN_HEADS, Q_BLOCK), NEG_BIG, F32)
    l0 = jnp.zeros((ATTN_HEADS, Q_BLOCK), F32)
    m_old, l_old, mc_prev = lax.fori_loop(1, nch, attn_body, (m0, l0, mc0))
    m_new, alpha = stage_b_all(nch - 1, m_old, l_old, mc_prev)
    sums = sum([stage_b_pair(nch - 1, p, m_new, alpha) for p in range(ATTN_HEADS // 2)], [])
    l_fin = l_old * alpha + jnp.concatenate(sums, axis=0)

    outs = []
    for h in range(ATTN_HEADS):
        outs.append(acc_ref[h * HEAD_DIM:(h + 1) * HEAD_DIM, :] / l_fin[h:h + 1, :])
    o_ref[...] = jnp.concatenate(outs, axis=0).T


def _attn(it, ki, q, k, vt, B, S):
    T = B * S
    nb = S // Q_BLOCK
    n_sel = min(INDEX_TOPK_MAX, S // 4)
    return pl.pallas_call(
        functools.partial(_attn_kernel, n_sel),
        grid=(B, nb),
        in_specs=[
            pl.BlockSpec((IDXT_ROWS, Q_BLOCK), lambda b, i: (0, b * nb + i)),
            pl.BlockSpec((S, IDX_DIM), lambda b, i: (b, 0)),
            pl.BlockSpec((Q_BLOCK, ATTN_DIM), lambda b, i: (b * nb + i, 0)),
            pl.BlockSpec((S, ATTN_DIM), lambda b, i: (b, 0)),
            pl.BlockSpec((S // KEY_CHUNK, ATTN_DIM, KEY_CHUNK), lambda b, i: (b, 0, 0)),
        ],
        out_specs=pl.BlockSpec((Q_BLOCK, ATTN_DIM), lambda b, i: (b * nb + i, 0)),
        out_shape=jax.ShapeDtypeStruct((T, ATTN_DIM), F32),
        scratch_shapes=[
            pltpu.VMEM((S, Q_BLOCK), I32),
            pltpu.VMEM((-(-(S // KEY_CHUNK) // 2) * 2 * (KEY_CHUNK // FIELDS), Q_BLOCK), I32),
            pltpu.VMEM((ATTN_HEADS // 2, 2 * Q_BLOCK, 2 * HEAD_DIM), BF16),
            pltpu.VMEM((ATTN_DIM, Q_BLOCK), F32),
            pltpu.VMEM((ATTN_HEADS, KEY_CHUNK, Q_BLOCK), F32),
        ],
        compiler_params=_cparams(("arbitrary", "arbitrary")),
        name="attn",
    )(it, ki, q, k, vt)


def _mixout_kernel(tiles_per_seq, a_ref, halo_ref, y_ref, x_ref, cw_ref, wpool_ref, ps_ref,
                   wout_ref, g_ref, b_ref, o_ref, ext_ref):
    i = pl.program_id(0)
    tm = TM_MIX
    first = lax.rem(i, tiles_per_seq) == 0
    halo = jnp.where(first, 0.0, halo_ref[...])
    a = a_ref[...]
    h_c, gb_c, gc_c, p_c = (a[:, j * CONV_DIM:(j + 1) * CONV_DIM] for j in range(4))

    t0 = 2 * HALO
    n_ext = t0 + tm
    ext_ref[0:HALO, :] = jnp.zeros((HALO, CONV_DIM), F32)

    ext_ref[HALO:t0, :] = halo[:, 2 * CONV_DIM:3 * CONV_DIM] * halo[:, 0:CONV_DIM]
    u = gc_c * h_c
    ext_ref[t0:n_ext, :] = u
    cw = cw_ref[...]
    conv = cw[2:3, :] * u
    conv = conv + cw[1:2, :] * ext_ref[t0 - 1:n_ext - 1, :]
    conv = conv + cw[0:1, :] * ext_ref[t0 - 2:n_ext - 2, :]
    y_conv = gb_c * conv

    ext_ref[HALO:t0, :] = halo[:, OFF_P:OFF_P + POOL_DIM]
    ext_ref[t0:n_ext, :] = p_c
    sums = {}
    step = 1
    while step < POOL_WINDOWS[-1]:
        cur = ext_ref[HALO:n_ext, :] + ext_ref[HALO - step:n_ext - step, :]
        ext_ref[HALO:n_ext, :] = cur
        step *= 2
        sums[step] = ext_ref[t0:n_ext, :]
    tpos = (lax.rem(i, tiles_per_seq) * tm + lax.broadcasted_iota(I32, (tm, POOL_DIM), 0) + 1).astype(F32)
    grp = lax.shift_right_logical(lax.broadcasted_iota(I32, (tm, POOL_DIM), 1), _log2(POOL_GROUP))
    mean = jnp.zeros((tm, POOL_DIM), F32)
    for gi, w in enumerate(POOL_WINDOWS):
        mean = jnp.where(grp == gi, sums[w] / jnp.minimum(tpos, float(w)), mean)
    mixed = mean - p_c
    y_pool = _dot(mixed.astype(BF16), wpool_ref[...]) * ps_ref[...]

    mix = _dot(y_conv.astype(BF16), wout_ref[0:CONV_DIM, :])
    mix = mix + _dot(y_pool.astype(BF16), wout_ref[CONV_DIM:CONV_DIM + POOL_DIM, :])
    mix = mix + _dot(y_ref[...].astype(BF16), wout_ref[CONV_DIM + POOL_DIM:D_MODEL, :])
    z = DN_ALPHA * x_ref[...] + mix
    o_ref[...] = _layernorm_rows(z, g_ref[...], b_ref[...])


def _mixout(a, y_attn, x, cw, wpool_bd, ps, wout, g, b, S):
    T = a.shape[0]
    nt = T // TM_MIX
    tps = S // TM_MIX
    hb = TM_MIX // HALO
    full = lambda i: (0, 0)
    return pl.pallas_call(
        functools.partial(_mixout_kernel, tps),
        grid=(nt,),
        in_specs=[
            pl.BlockSpec((TM_MIX, OFF_Q), lambda i: (i, 0)),
            pl.BlockSpec((HALO, OFF_Q), lambda i: (jnp.maximum(i * hb - 1, 0), 0)),
            pl.BlockSpec((TM_MIX, ATTN_DIM), lambda i: (i, 0)),
            pl.BlockSpec((TM_MIX, D_MODEL), lambda i: (i, 0)),
            pl.BlockSpec((CONV_WIDTH, CONV_DIM), full),
            pl.BlockSpec((POOL_DIM, POOL_DIM), full),
            pl.BlockSpec((1, POOL_DIM), full),
            pl.BlockSpec((D_MODEL, D_MODEL), full),
            pl.BlockSpec((1, D_MODEL), full),
            pl.BlockSpec((1, D_MODEL), full),
        ],
        out_specs=pl.BlockSpec((TM_MIX, D_MODEL), lambda i: (i, 0)),
        out_shape=jax.ShapeDtypeStruct((T, D_MODEL), F32),
        scratch_shapes=[pltpu.VMEM((2 * HALO + TM_MIX, CONV_DIM), F32)],
        compiler_params=_cparams(("arbitrary",)),
        name="mixout",
    )(a, a, y_attn, x, cw, wpool_bd, ps, wout, g, b)


def _router_kernel(x_ref, rw_ref, rb_ref, idx_ref, gate_ref, cnt_ref):
    i = pl.program_id(0)
    tm = TM_ROUTE
    x = x_ref[...]
    w = rw_ref[...]
    xh = x.astype(BF16)
    xl = (x - xh.astype(F32)).astype(BF16)
    wh = w.astype(BF16)
    wl = (w - wh.astype(F32)).astype(BF16)
    logits = (_dot_nt(wh, xh) + (_dot_nt(wh, xl) + _dot_nt(wl, xh))) + rb_ref[...]
    erow = lax.broadcasted_iota(I32, (N_EXPERTS, tm), 0).astype(F32)
    work = logits
    vals, idxs = [], []
    multi = jnp.zeros((N_EXPERTS, tm), F32)
    for _ in range(TOP_K):
        mx = jnp.max(work, axis=0, keepdims=True)
        pick = jnp.min(jnp.where(work == mx, erow, float(N_EXPERTS)), axis=0, keepdims=True)
        hit = erow == pick
        work = jnp.where(hit, -jnp.inf, work)
        multi = multi + hit.astype(F32)
        vals.append(mx)
        idxs.append(pick)
    es = [jnp.exp(v - vals[0]) for v in vals]
    den = es[0] + es[1] + es[2] + es[3]
    idx_ref[...] = jnp.concatenate(idxs, axis=0).astype(I32)
    gate_ref[...] = jnp.concatenate([e / den for e in es] + [jnp.zeros((SUBLANES - TOP_K, tm), F32)], axis=0)
    part = multi[:, 0:LANES]
    for j in range(1, tm // LANES):
        part = part + multi[:, j * LANES:(j + 1) * LANES]

    @pl.when(i == 0)
    def _():
        cnt_ref[...] = jnp.zeros_like(cnt_ref)

    cnt_ref[...] += part


def _router(x1, rwT, rb):
    T = x1.shape[0]
    nt = T // TM_ROUTE
    full = lambda i: (0, 0)
    return pl.pallas_call(
        _router_kernel,
        grid=(nt,),
        in_specs=[
            pl.BlockSpec((TM_ROUTE, D_MODEL), lambda i: (i, 0)),
            pl.BlockSpec((N_EXPERTS, D_MODEL), full),
            pl.BlockSpec((N_EXPERTS, 1), full),
        ],
        out_specs=[
            pl.BlockSpec((TOP_K, TM_ROUTE), lambda i: (0, i)),
            pl.BlockSpec((SUBLANES, TM_ROUTE), lambda i: (0, i)),
            pl.BlockSpec((N_EXPERTS, LANES), full),
        ],
        out_shape=[
            jax.ShapeDtypeStruct((TOP_K, T), I32),
            jax.ShapeDtypeStruct((SUBLANES, T), F32),
            jax.ShapeDtypeStruct((N_EXPERTS, LANES), F32),
        ],
        compiler_params=_cparams(("arbitrary",)),
        name="router",
    )(x1, rwT, rb)


def _ranks_kernel(n_tiles_pad, idx_ref, cnt_ref, dest_ref, meta_ref, tri_ref, start_ref, carry_ref):
    i = pl.program_id(0)
    tm = TM_ROUTE
    erow = lax.broadcasted_iota(I32, (N_EXPERTS, LANES), 0)
    elane = lax.broadcasted_iota(I32, (N_EXPERTS, LANES), 1)

    @pl.when(i == 0)
    def _():
        cnt = jnp.sum(cnt_ref[...], axis=1, keepdims=True)
        cnt_i = jnp.broadcast_to(cnt, (N_EXPERTS, LANES)).astype(I32)
        padded = lax.shift_left(lax.shift_right_logical(cnt_i + (TM_MOE - 1), _log2(TM_MOE)), _log2(TM_MOE))
        r = lax.broadcasted_iota(I32, (N_EXPERTS, N_EXPERTS), 0)
        c = lax.broadcasted_iota(I32, (N_EXPERTS, N_EXPERTS), 1)
        low = (c <= r).astype(F32)
        pad_end = lax.dot_general(low, padded.astype(F32), (((1,), (0,)), ((), ())),
                                  precision=lax.Precision.HIGHEST, preferred_element_type=F32)
        pad_start = pad_end - padded.astype(F32)
        start_ref[...] = pad_start
        carry_ref[...] = jnp.zeros_like(carry_ref)
        a = lax.broadcasted_iota(I32, (tm, tm), 0)
        bcol = lax.broadcasted_iota(I32, (tm, tm), 1)
        tri_ref[...] = (a < bcol).astype(BF16)
        ntp = n_tiles_pad
        tstart = (lax.broadcasted_iota(I32, (N_EXPERTS, ntp), 1) * TM_MOE).astype(F32)
        pe = jnp.concatenate([pad_end] * (ntp // LANES), axis=1)
        texp = jnp.sum((pe <= tstart).astype(F32), axis=0, keepdims=True)
        texp = jnp.minimum(texp, float(N_EXPERTS - 1)).astype(I32)
        total = jnp.max(pad_end, axis=0, keepdims=True)
        n_used = lax.shift_right_logical(total.astype(I32), _log2(TM_MOE))
        zstart = jnp.sum(jnp.where(erow == elane, pad_start + cnt_i.astype(F32), 0.0),
                         axis=0, keepdims=True).astype(I32)
        zlen = jnp.sum(jnp.where(erow == elane, (padded - cnt_i).astype(F32), 0.0),
                       axis=0, keepdims=True).astype(I32)
        etiles = lax.shift_right_logical(
            jnp.sum(jnp.where(erow == elane, padded.astype(F32), 0.0), axis=0, keepdims=True).astype(I32),
            _log2(TM_MOE))
        lanes_pad = jnp.zeros((1, ntp - LANES), I32)
        meta_ref[...] = jnp.concatenate(
            [texp,
             jnp.concatenate([n_used, lanes_pad], axis=1),
             jnp.concatenate([zstart, lanes_pad], axis=1),
             jnp.concatenate([zlen, lanes_pad], axis=1),
             jnp.concatenate([etiles, lanes_pad], axis=1),
             jnp.zeros((SUBLANES - 5, ntp), I32)], axis=0)

    idx = idx_ref[...]
    erow_t = lax.broadcasted_iota(I32, (N_EXPERTS, tm), 0)
    hits = [erow_t == idx[k:k + 1, :] for k in range(TOP_K)]
    multi = hits[0].astype(F32)
    for k in range(1, TOP_K):
        multi = multi + hits[k].astype(F32)
    prefix = _dot(multi.astype(BF16), tri_ref[...])
    base = jnp.concatenate([carry_ref[...] + start_ref[...]] * (tm // LANES), axis=1)
    tot = prefix + base
    dest_ref[...] = jnp.concatenate(
        [jnp.sum(jnp.where(hits[k], tot, 0.0), axis=0, keepdims=True) for k in range(TOP_K)],
        axis=0).astype(I32)
    carry_ref[...] += jnp.broadcast_to(jnp.sum(multi, axis=1, keepdims=True), (N_EXPERTS, LANES))


def _ranks(idxT, cnt, n_tiles_pad):
    T = idxT.shape[1]
    nt = T // TM_ROUTE
    full = lambda i: (0, 0)
    return pl.pallas_call(
        functools.partial(_ranks_kernel, n_tiles_pad),
        grid=(nt,),
        in_specs=[
            pl.BlockSpec((TOP_K, TM_ROUTE), lambda i: (0, i)),
            pl.BlockSpec((N_EXPERTS, LANES), full),
        ],
        out_specs=[
            pl.BlockSpec((TOP_K, TM_ROUTE), lambda i: (0, i)),
            pl.BlockSpec((SUBLANES, n_tiles_pad), full),
        ],
        out_shape=[
            jax.ShapeDtypeStruct((TOP_K, T), I32),
            jax.ShapeDtypeStruct((SUBLANES, n_tiles_pad), I32),
        ],
        scratch_shapes=[
            pltpu.VMEM((TM_ROUTE, TM_ROUTE), BF16),
            pltpu.VMEM((N_EXPERTS, LANES), F32),
            pltpu.VMEM((N_EXPERTS, LANES), F32),
        ],
        compiler_params=_cparams(("arbitrary",)),
        name="ranks",
    )(idxT, cnt)


RT = D_MODEL // LANES
DMA_UNROLL = 8


def _row_copy(src_ref, s, dst_ref, d, sem):
    return pltpu.make_async_copy(src_ref.at[pl.ds(pl.multiple_of(s * RT, RT), RT)],
                                 dst_ref.at[pl.ds(pl.multiple_of(d * RT, RT), RT)], sem)


def _to_row_tiled(dst_ref, val, rows):
    for s in range(RT):
        dst_ref[pl.ds(s, rows, stride=RT), :] = val[:, s * LANES:(s + 1) * LANES]


def _from_row_tiled(src_ref, rows):
    return [src_ref[pl.ds(s, rows, stride=RT), :] for s in range(RT)]


def _scatter_kernel(meta_ref, x_ref, dest_ref, xs_ref, stage_ref, zero_ref, sem):
    i = pl.program_id(0)
    tm = TM_ROUTE

    @pl.when(i == 0)
    def _():
        zero_ref[...] = jnp.zeros_like(zero_ref)

        def zcopy(row, nrows):
            off = pl.multiple_of(row * RT, RT)
            return pltpu.make_async_copy(zero_ref.at[pl.ds(0, nrows * RT)],
                                         xs_ref.at[pl.ds(off, nrows * RT)], sem)

        def pad_fill(wait):
            def body(e, c):
                row = meta_ref[2, e]
                plen = meta_ref[3, e]
                for bit in reversed(range(_log2(TM_MOE))):
                    size = 1 << bit
                    has = (plen & size) != 0

                    @pl.when(has)
                    def _():
                        cp = zcopy(row, size)
                        cp.wait() if wait else cp.start()

                    row = row + jnp.where(has, size, 0)
                return c
            lax.fori_loop(0, N_EXPERTS, body, 0)

        def tail_fill(wait):
            def body(j, c):
                cp = zcopy(j * TM_MOE, TM_MOE)
                cp.wait() if wait else cp.start()
                return c
            lax.fori_loop(meta_ref[1, 0], xs_ref.shape[0] // (TM_MOE * RT), body, 0)

        pad_fill(False)
        tail_fill(False)
        pad_fill(True)
        tail_fill(True)

    _to_row_tiled(stage_ref, x_ref[...], tm)

    def start(t, c):
        for k in range(TOP_K):
            _row_copy(stage_ref, t, xs_ref, dest_ref[k, t], sem).start(priority=k % 2)
        return c

    lax.fori_loop(0, tm, start, 0, unroll=DMA_UNROLL)
    for k in range(TOP_K):
        pltpu.make_async_copy(stage_ref, xs_ref.at[pl.ds(0, tm * RT)], sem).wait()


def _scatter(meta, x1, destT, n_rows_alloc):
    T = x1.shape[0]
    nt = T // TM_ROUTE
    return pl.pallas_call(
        _scatter_kernel,
        grid_spec=pltpu.PrefetchScalarGridSpec(
            num_scalar_prefetch=1,
            grid=(nt,),
            in_specs=[
                pl.BlockSpec((TM_ROUTE, D_MODEL), lambda i, m: (i, 0)),
                pl.BlockSpec((TOP_K, TM_ROUTE), lambda i, m: (0, i), memory_space=pltpu.SMEM),
            ],
            out_specs=pl.BlockSpec(memory_space=pl.ANY),
            scratch_shapes=[
                pltpu.VMEM((TM_ROUTE * RT, LANES), F32),
                pltpu.VMEM((TM_MOE * RT, LANES), F32),
                pltpu.SemaphoreType.DMA(()),
            ],
        ),
        out_shape=jax.ShapeDtypeStruct((n_rows_alloc * RT, LANES), F32),
        compiler_params=_cparams(("arbitrary",)),
        name="scatter",
    )(meta, x1, destT)


def _gmm_kernel(layer, meta_ref, xs_ref, wgu_hbm, bgu_ref, wd_hbm, bd_ref, ys_ref,
                wgu_f, wd_f, wgu_b, wd_b, lhs_ref, act_ref, grp_ref, sems):
    j = pl.program_id(0)
    n_used = meta_ref[1, 0]
    e_now = meta_ref[0, j]
    e_prev = meta_ref[0, jnp.maximum(j - 1, 0)]
    used = j < n_used

    def weight_copies(e, slot):
        return (pltpu.make_async_copy(wgu_hbm.at[layer, e], wgu_f.at[slot], sems.at[0, slot]),
                pltpu.make_async_copy(wd_hbm.at[layer, e], wd_f.at[slot], sems.at[1, slot]))

    @pl.when(used & (j == 0))
    def _():
        grp_ref[0] = 0
        for cp in weight_copies(e_now, 0):
            cp.start()

    @pl.when(used & ((j == 0) | (e_now != e_prev)))
    def _():
        slot = grp_ref[0]
        j_next = j + meta_ref[4, e_now]

        @pl.when(j_next < n_used)
        def _():
            for cp in weight_copies(meta_ref[0, jnp.minimum(j_next, meta_ref.shape[1] - 1)], 1 - slot):
                cp.start()

        for cp in weight_copies(e_now, slot):
            cp.wait()
        wgu_b[...] = wgu_f[slot].astype(BF16)
        wd_b[...] = wd_f[slot].astype(BF16)
        grp_ref[0] = 1 - slot

    @pl.when(used)
    def _():
        for s, piece in enumerate(_from_row_tiled(xs_ref, TM_MOE)):
            lhs_ref[:, s * LANES:(s + 1) * LANES] = piece.astype(BF16)
        x = lhs_ref[...]
        nc = 512
        for c in range(D_FF // nc):
            gate = _dot(x, wgu_b[:, c * nc:(c + 1) * nc]) + bgu_ref[0, 0, :, c * nc:(c + 1) * nc]
            up = (_dot(x, wgu_b[:, D_FF + c * nc:D_FF + (c + 1) * nc])
                  + bgu_ref[0, 0, :, D_FF + c * nc:D_FF + (c + 1) * nc])
            gate = jnp.minimum(gate, SWIGLU_LIMIT)
            up = jnp.clip(up, -SWIGLU_LIMIT, SWIGLU_LIMIT)
            act = gate * jax.nn.sigmoid(SWIGLU_ALPHA * gate) * (up + 1.0)
            act_ref[:, c * nc:(c + 1) * nc] = act.astype(BF16)
        _to_row_tiled(ys_ref, _dot(act_ref[...], wd_b[...]) + bd_ref[0, 0], TM_MOE)

    @pl.when(jnp.logical_not(used))
    def _():
        ys_ref[...] = jnp.zeros_like(ys_ref)


def _gmm(meta, xs, w_gu, b_gu, w_down, b_down, n_tiles, layer):
    last = lambda m: jnp.maximum(m[1, 0] - 1, 0)
    return pl.pallas_call(
        functools.partial(_gmm_kernel, layer),
        grid_spec=pltpu.PrefetchScalarGridSpec(
            num_scalar_prefetch=1,
            grid=(n_tiles,),
            in_specs=[
                pl.BlockSpec((TM_MOE * RT, LANES), lambda j, m: (jnp.minimum(j, last(m)), 0)),
                pl.BlockSpec(memory_space=pl.ANY),
                pl.BlockSpec((1, 1, 1, 2 * D_FF), lambda j, m: (layer, m[0, j], 0, 0)),
                pl.BlockSpec(memory_space=pl.ANY),
                pl.BlockSpec((1, 1, 1, D_MODEL), lambda j, m: (layer, m[0, j], 0, 0)),
            ],
            out_specs=pl.BlockSpec((TM_MOE * RT, LANES), lambda j, m: (j, 0)),
            scratch_shapes=[
                pltpu.VMEM((2, D_MODEL, 2 * D_FF), F32),
                pltpu.VMEM((2, D_FF, D_MODEL), F32),
                pltpu.VMEM((D_MODEL, 2 * D_FF), BF16),
                pltpu.VMEM((D_FF, D_MODEL), BF16),
                pltpu.VMEM((TM_MOE, D_MODEL), BF16),
                pltpu.VMEM((TM_MOE, D_FF), BF16),
                pltpu.SMEM((1,), I32),
                pltpu.SemaphoreType.DMA((2, 2)),
            ],
        ),
        out_shape=jax.ShapeDtypeStruct((n_tiles * TM_MOE * RT, LANES), F32),
        compiler_params=_cparams(("arbitrary",)),
        name="gmm",
    )(meta, xs, w_gu, b_gu, w_down, b_down)


def _combine_kernel(x_ref, dest_ref, gate_ref, ys_ref, g_ref, b_ref, o_ref, ob_ref, buf_ref, sem):
    tm = TM_COMB

    def start(t, c):
        for k in range(TOP_K):
            _row_copy(ys_ref, dest_ref[k, t], buf_ref.at[k], t, sem).start(priority=k % 2)
        return c

    lax.fori_loop(0, tm, start, 0, unroll=DMA_UNROLL)
    for k in range(TOP_K):
        pltpu.make_async_copy(ys_ref.at[pl.ds(0, tm * RT)], buf_ref.at[k], sem).wait()
    gates = jnp.concatenate(
        [gate_ref[...], jnp.zeros((LANES - SUBLANES, tm), F32)], axis=0).T
    pieces = None
    for k in range(TOP_K):
        gk = gates[:, k:k + 1]
        rows = [p * gk for p in _from_row_tiled(buf_ref.at[k], tm)]
        pieces = rows if pieces is None else [a + r for a, r in zip(pieces, rows)]
    z = DN_ALPHA * x_ref[...] + jnp.concatenate(pieces, axis=1)
    out = _layernorm_rows(z, g_ref[...], b_ref[...])
    o_ref[...] = out
    ob_ref[...] = out.astype(BF16)


def _combine(x1, destT, gateT, ys, g, b):
    T = x1.shape[0]
    nt = T // TM_COMB
    full = lambda i: (0, 0)
    return pl.pallas_call(
        _combine_kernel,
        grid=(nt,),
        in_specs=[
            pl.BlockSpec((TM_COMB, D_MODEL), lambda i: (i, 0)),
            pl.BlockSpec((TOP_K, TM_COMB), lambda i: (0, i), memory_space=pltpu.SMEM),
            pl.BlockSpec((SUBLANES, TM_COMB), lambda i: (0, i)),
            pl.BlockSpec(memory_space=pl.ANY),
            pl.BlockSpec((1, D_MODEL), full),
            pl.BlockSpec((1, D_MODEL), full),
        ],
        out_specs=[
            pl.BlockSpec((TM_COMB, D_MODEL), lambda i: (i, 0)),
            pl.BlockSpec((TM_COMB, D_MODEL), lambda i: (i, 0)),
        ],
        out_shape=[
            jax.ShapeDtypeStruct((T, D_MODEL), F32),
            jax.ShapeDtypeStruct((T, D_MODEL), BF16),
        ],
        scratch_shapes=[
            pltpu.VMEM((TOP_K, TM_COMB * RT, LANES), F32),
            pltpu.SemaphoreType.DMA(()),
        ],
        compiler_params=_cparams(("arbitrary",)),
        name="combine",
    )(x1, destT, gateT, ys, g, b)


def _block_diag(w):
    g, c, _ = w.shape
    out = jnp.zeros((g * c, g * c), w.dtype)
    for i in range(g):
        out = out.at[i * c:(i + 1) * c, i * c:(i + 1) * c].set(w[i])
    return out


def kernel(x, w_in, conv_w, w_pool, pool_scale, idx_kn_g, idx_kn_b, w_out, ln1_g, ln1_b, router_w,
           router_b, w_gu, b_gu, w_down, b_down, ln2_g, ln2_b):
    B, S, D = x.shape
    T = B * S
    depth = w_in.shape[0]
    n_pairs = T * TOP_K
    n_tiles = (n_pairs + N_EXPERTS * (TM_MOE - 1)) // TM_MOE + 1
    n_tiles_pad = ((n_tiles + LANES - 1) // LANES) * LANES
    n_rows_alloc = (n_tiles + 1) * TM_MOE

    xf = x.reshape(T, D)
    xb = xf
    for l in range(depth):
        wl = w_in[l]
        wn = jnp.concatenate(
            [wl[:, 0:OFF_V], wl[:, OFF_KI:OFF_WI], jnp.zeros((D, LANES - IDX_DIM), F32)], axis=1).astype(BF16)
        wt = jnp.concatenate(
            [wl[:, OFF_V:OFF_QI], wl[:, OFF_QI:OFF_KI], wl[:, OFF_WI:D_IN],
             jnp.zeros((D, IDXT_ROWS - IDX_HEADS * IDX_DIM - IDX_HEADS), F32)], axis=1).T.astype(BF16)
        a, q, k, ki, vt, it = _inproj(xb, wn, wt, idx_kn_g[l].reshape(1, -1), idx_kn_b[l].reshape(1, -1))
        y_attn = _attn(it, ki, q, k, vt, B, S)
        x1 = _mixout(a, y_attn, xf, conv_w[l], _block_diag(w_pool[l]).astype(BF16),
                     pool_scale[l].reshape(1, -1), w_out[l].astype(BF16),
                     ln1_g[l].reshape(1, -1), ln1_b[l].reshape(1, -1), S)
        idxT, gateT, cnt = _router(x1, router_w[l].T, router_b[l].reshape(-1, 1))
        destT, meta = _ranks(idxT, cnt, n_tiles_pad)
        xs = _scatter(meta, x1, destT, n_rows_alloc)
        ys = _gmm(meta, xs, w_gu, b_gu.reshape(depth, N_EXPERTS, 1, -1), w_down,
                  b_down.reshape(depth, N_EXPERTS, 1, -1), n_tiles, l)
        xf, xb = _combine(x1, destT, gateT, ys, ln2_g[l].reshape(1, -1), ln2_b[l].reshape(1, -1))
    return xf.reshape(B, S, D)
```

```python
import functools

import jax
import jax.numpy as jnp
from jax import lax
from jax.experimental import pallas as pl
from jax.experimental.pallas import tpu as pltpu

F32 = jnp.float32
BF16 = jnp.bfloat16
I32 = jnp.int32

D_MODEL = 1024
CONV_DIM = 256
CONV_WIDTH = 3
POOL_WINDOWS = (2, 4, 8, 16)
POOL_GROUP = 64
POOL_DIM = 256
ATTN_HEADS = 8
HEAD_DIM = 64
ATTN_DIM = 512
IDX_HEADS = 8
IDX_DIM = 32
INDEX_TOPK_MAX = 256
Q_BLOCK = 128
N_EXPERTS = 32
TOP_K = 4
D_FF = 1024
SWIGLU_ALPHA = 1.702
SWIGLU_LIMIT = 7.0
LN_EPS = 1e-5
DEPTH = 2
DN_ALPHA = (2.0 * DEPTH) ** 0.25

OFF_P = 3 * CONV_DIM
OFF_Q = OFF_P + POOL_DIM
OFF_K = OFF_Q + ATTN_DIM
OFF_V = OFF_K + ATTN_DIM
OFF_QI = OFF_V + ATTN_DIM
OFF_KI = OFF_QI + IDX_HEADS * IDX_DIM
OFF_WI = OFF_KI + IDX_DIM
D_IN = OFF_WI + IDX_HEADS

LANES = 128
SUBLANES = 8
VMEM_LIMIT = 56 * 1024 * 1024
KEY_CHUNK = 512
TIE_GROUP = 128
HALO = 16
TM_PROJ = 512
TM_MIX = 1024
TM_ROUTE = 1024
TM_MOE = 512
TM_COMB = 1024
IDXT_ROWS = IDX_HEADS * IDX_DIM + 16
WT_ROWS = ATTN_DIM + IDXT_ROWS
INT_MIN = -(2 ** 31)
NEG_BIG = -1e30


def _cparams(sem):
    return pltpu.CompilerParams(dimension_semantics=sem, vmem_limit_bytes=VMEM_LIMIT)


def _log2(n):
    k = n.bit_length() - 1
    assert 1 << k == n
    return k


def _dot(a, b):
    return jnp.dot(a, b, preferred_element_type=F32)


def _dot_nt(a, b):
    return lax.dot_general(a, b, (((1,), (1,)), ((), ())), preferred_element_type=F32)


def _tree(x, op):
    parts = [x[j * SUBLANES:(j + 1) * SUBLANES, :] for j in range(x.shape[0] // SUBLANES)]
    while len(parts) > 1:
        nxt = [op(parts[j], parts[j + 1]) for j in range(0, len(parts) - 1, 2)]
        if len(parts) % 2:
            nxt.append(parts[-1])
        parts = nxt
    return parts[0]


def _layernorm_rows(z, g, b):
    mu = jnp.mean(z, axis=-1, keepdims=True)
    d = z - mu
    var = jnp.mean(d * d, axis=-1, keepdims=True)
    return d * lax.rsqrt(var + LN_EPS) * g + b


def _inproj_kernel(x_ref, wn_ref, wt_ref, kg_ref, kb_ref,
                   a_ref, q_ref, k_ref, ki_ref, vt_ref, it_ref):
    x = x_ref[...].astype(BF16)
    a_ref[...] = _dot(x, wn_ref[:, 0:OFF_Q])
    q_ref[...] = (_dot(x, wn_ref[:, OFF_Q:OFF_K]) * (HEAD_DIM ** -0.5)).astype(BF16)
    k_ref[...] = _dot(x, wn_ref[:, OFF_K:OFF_V]).astype(BF16)
    ki = _dot(x, wn_ref[:, OFF_V:OFF_V + LANES])[:, 0:IDX_DIM]
    ki_ref[...] = _layernorm_rows(ki, kg_ref[...], kb_ref[...])
    t = _dot_nt(wt_ref[...], x)
    for c in range(TM_PROJ // KEY_CHUNK):
        vt_ref[c] = t[0:ATTN_DIM, c * KEY_CHUNK:(c + 1) * KEY_CHUNK].astype(BF16)
    it_ref[...] = t[ATTN_DIM:WT_ROWS, :]


def _inproj(xb, wn, wt, kg, kb):
    T = xb.shape[0]
    nt = T // TM_PROJ
    cpt = TM_PROJ // KEY_CHUNK
    full = lambda i: (0, 0)
    return pl.pallas_call(
        _inproj_kernel,
        grid=(nt,),
        in_specs=[
            pl.BlockSpec((TM_PROJ, D_MODEL), lambda i: (i, 0)),
            pl.BlockSpec(wn.shape, full),
            pl.BlockSpec(wt.shape, full),
            pl.BlockSpec((1, IDX_DIM), full),
            pl.BlockSpec((1, IDX_DIM), full),
        ],
        out_specs=[
            pl.BlockSpec((TM_PROJ, OFF_Q), lambda i: (i, 0)),
            pl.BlockSpec((TM_PROJ, ATTN_DIM), lambda i: (i, 0)),
            pl.BlockSpec((TM_PROJ, ATTN_DIM), lambda i: (i, 0)),
            pl.BlockSpec((TM_PROJ, IDX_DIM), lambda i: (i, 0)),
            pl.BlockSpec((cpt, ATTN_DIM, KEY_CHUNK), lambda i: (i, 0, 0)),
            pl.BlockSpec((IDXT_ROWS, TM_PROJ), lambda i: (0, i)),
        ],
        out_shape=[
            jax.ShapeDtypeStruct((T, OFF_Q), F32),
            jax.ShapeDtypeStruct((T, ATTN_DIM), BF16),
            jax.ShapeDtypeStruct((T, ATTN_DIM), BF16),
            jax.ShapeDtypeStruct((T, IDX_DIM), F32),
            jax.ShapeDtypeStruct((T // KEY_CHUNK, ATTN_DIM, KEY_CHUNK), BF16),
            jax.ShapeDtypeStruct((IDXT_ROWS, T), F32),
        ],
        compiler_params=_cparams(("arbitrary",)),
        name="inproj",
    )(xb, wn, wt, kg, kb)


FIELD_BITS = 8
DIGIT_BITS = FIELD_BITS - 1
FIELDS = 32 // FIELD_BITS
DIGIT_SHIFTS = (25, 18, 11, 4, 0)
EARLY_EXIT_FROM = 4
FIELD_ONES = 0x01010101
FIELD_GUARDS = 0x80808080 - (1 << 32)


def _pack_fields(f):
    q = f.shape[0] // FIELDS
    w = f[0:q]
    for j in range(1, FIELDS):
        w = w | lax.shift_left(f[j * q:(j + 1) * q], FIELD_BITS * j)
    return w | jnp.int32(FIELD_GUARDS)


def _count_fields_ge(words_ref, nch, cand):
    wq = 2 * (KEY_CHUNK // FIELDS)
    assert words_ref.shape[0] // SUBLANES < (1 << FIELD_BITS)
    cvec = cand * jnp.int32(FIELD_ONES)

    def body(c, acc):
        base = pl.multiple_of(c * wq, wq)
        w = words_ref[pl.ds(base, wq), :]
        hit = lax.shift_right_logical(w - cvec, DIGIT_BITS) & jnp.int32(FIELD_ONES)
        return acc + _tree(hit, jnp.add)

    acc = lax.fori_loop(0, lax.shift_right_logical(nch + 1, 1), body,
                        jnp.zeros((SUBLANES, words_ref.shape[1]), I32))
    tot = acc & 255
    for j in range(1, FIELDS):
        tot = tot + (lax.shift_right_logical(acc, FIELD_BITS * j) & 255)
    return jnp.sum(tot.astype(F32), axis=0, keepdims=True)


def _topk_threshold(keys_ref, words_ref, nch, n_sel, n_causal):
    lanes = keys_ref.shape[1]
    wq = KEY_CHUNK // FIELDS
    all_taken = n_causal <= n_sel

    def run_digit(idx, state):
        prefix, want, bucket = state
        shift = DIGIT_SHIFTS[idx]
        width = (DIGIT_SHIFTS[idx - 1] if idx else 32) - shift
        if idx:
            def prep(c, carry):
                key = keys_ref[pl.ds(pl.multiple_of(c * KEY_CHUNK, KEY_CHUNK), KEY_CHUNK), :]
                digit = (key >> shift) & jnp.int32((1 << width) - 1)
                fields = jnp.where((key >> (shift + width)) == prefix, digit, 0)
                words_ref[pl.ds(pl.multiple_of(c * wq, wq), wq), :] = _pack_fields(fields)
                return carry

            lax.fori_loop(0, nch, prep, 0)
        dig = jnp.zeros((1, lanes), I32)
        at_dig = bucket
        above = jnp.zeros((1, lanes), F32)
        for bit in reversed(range(width)):
            cand = dig + (1 << bit)
            cnt = _count_fields_ge(words_ref, nch, cand)
            ok = cnt >= want
            dig = jnp.where(ok, cand, dig)
            at_dig = jnp.where(ok, cnt, at_dig)
            above = jnp.where(ok, above, cnt)
        prefix = (lax.shift_left(prefix, width) | dig) if idx else dig - (1 << (DIGIT_BITS - 1))
        return prefix, want - above, at_dig - above

    def finish(state, low_bits):
        prefix, want, bucket = state
        tau = jnp.where(all_taken, jnp.int32(INT_MIN + 1), lax.shift_left(prefix, low_bits))
        return tau, want, jnp.where(all_taken, 0.0, bucket - want)

    def settled(state):
        _, want, bucket = state
        return jnp.min(jnp.where((bucket == want) | all_taken, 1.0, 0.0)) > 0.0

    def from_digit(idx, state):
        if idx == len(DIGIT_SHIFTS):
            return finish(state, 0)
        if idx < EARLY_EXIT_FROM:
            return from_digit(idx + 1, run_digit(idx, state))
        return lax.cond(settled(state),
                        lambda s: finish(s, DIGIT_SHIFTS[idx - 1]),
                        lambda s: from_digit(idx + 1, run_digit(idx, s)), state)

    state = (jnp.zeros((1, lanes), I32), jnp.full((1, lanes), float(n_sel), F32), n_causal.astype(F32))
    return from_digit(0, state)


def _attn_kernel(n_sel, it_ref, ki_ref, q_ref, k_ref, vt_ref, o_ref,
                 keys_ref, words_ref, rhs_ref, acc_ref, lg_ref):
    i = pl.program_id(1)
    nch = lax.shift_right_logical(i * Q_BLOCK + Q_BLOCK + KEY_CHUNK - 1, _log2(KEY_CHUNK))
    q0 = i * Q_BLOCK
    KC = KEY_CHUNK
    row = lax.broadcasted_iota(I32, (KC, Q_BLOCK), 0)
    lane = lax.broadcasted_iota(I32, (KC, Q_BLOCK), 1)
    qpos = q0 + lane

    qcat = jnp.concatenate(
        [it_ref[h * IDX_DIM:(h + 1) * IDX_DIM, :] for h in range(IDX_HEADS)], axis=1).astype(BF16)
    w_all = it_ref[IDX_HEADS * IDX_DIM:IDX_HEADS * IDX_DIM + IDX_HEADS, :] * (
        (IDX_HEADS ** -0.5) * (IDX_DIM ** -0.5))

    def score_chunk(c, has_future_keys):
        base = pl.multiple_of(c * KC, KC)
        kic = ki_ref[pl.ds(base, KC), :].astype(BF16)
        s = _dot(kic, qcat)
        sc = jnp.zeros((KC, Q_BLOCK), F32)
        for h in range(IDX_HEADS):
            sc = sc + jnp.maximum(s[:, h * Q_BLOCK:(h + 1) * Q_BLOCK], 0.0) * w_all[h:h + 1, :]
        bits = lax.bitcast_convert_type(sc, I32)
        key = bits ^ ((bits >> 31) & jnp.int32(0x7FFFFFFF))
        key = jnp.where(bits == jnp.int32(INT_MIN), jnp.int32(0), key)
        top = (key >> DIGIT_SHIFTS[0]) + (1 << (DIGIT_BITS - 1))
        if has_future_keys:
            causal = base + row <= qpos
            key = jnp.where(causal, key, jnp.int32(INT_MIN))
            top = jnp.where(causal, top, 0)
        keys_ref[pl.ds(base, KC), :] = key
        words_ref[pl.ds(pl.multiple_of(c * (KC // FIELDS), KC // FIELDS), KC // FIELDS), :] = _pack_fields(top)

    def score_body(c, carry):
        score_chunk(c, False)
        return carry

    lax.fori_loop(0, nch - 1, score_body, 0)
    score_chunk(nch - 1, True)

    @pl.when(lax.rem(nch, 2) == 1)
    def _():
        wq = KC // FIELDS
        words_ref[pl.ds(pl.multiple_of(nch * wq, wq), wq), :] = _pack_fields(jnp.zeros((KC, Q_BLOCK), I32))

    n_causal = q0 + lax.broadcasted_iota(I32, (1, Q_BLOCK), 1) + 1
    tau, keep, excess = _topk_threshold(keys_ref, words_ref, nch, n_sel, n_causal)

    @pl.when(jnp.max(excess) > 0.0)
    def _():
        g = TIE_GROUP
        r = lax.broadcasted_iota(I32, (g, g), 0)
        cc = lax.broadcasted_iota(I32, (g, g), 1)
        tri = (cc <= r).astype(BF16)

        def retire(c, before):
            base = pl.multiple_of(c * KC, KC)
            blk = keys_ref[pl.ds(base, KC), :]
            tied = blk == tau
            ones = jnp.where(tied, 1.0, 0.0).astype(BF16)
            local = [_dot(tri, ones[j * g:(j + 1) * g, :]) for j in range(KC // g)]
            upto = []
            for lj in local:
                upto.append(lj + before)
                before = before + lj[g - 1:g, :]
            upto = jnp.concatenate(upto, axis=0)
            keys_ref[pl.ds(base, KC), :] = jnp.where(tied & (upto > keep), jnp.int32(INT_MIN), blk)
            return before

        lax.fori_loop(0, nch, retire, jnp.zeros((1, Q_BLOCK), F32))

    q = q_ref[...]
    lane_q = lax.broadcasted_iota(I32, (Q_BLOCK, 2 * HEAD_DIM), 1)
    for p in range(ATTN_HEADS // 2):
        qp = q[:, p * 2 * HEAD_DIM:(p + 1) * 2 * HEAD_DIM]
        rhs_ref[p, 0:Q_BLOCK, :] = jnp.where(lane_q < HEAD_DIM, qp, jnp.zeros_like(qp))
        rhs_ref[p, Q_BLOCK:2 * Q_BLOCK, :] = jnp.where(lane_q >= HEAD_DIM, qp, jnp.zeros_like(qp))
    acc_ref[...] = jnp.zeros_like(acc_ref)

    def stage_a_prep(c):
        base = pl.multiple_of(c * KC, KC)
        tk = (base + row - (q0 + Q_BLOCK - 1)).astype(F32)
        tkm = jnp.where(keys_ref[pl.ds(base, KC), :] >= tau, tk, NEG_BIG)
        return tkm, k_ref[pl.ds(base, KC), :]

    def stage_a_pair(c, p, tkm, kc):
        l2 = _dot_nt(kc[:, p * 2 * HEAD_DIM:(p + 1) * 2 * HEAD_DIM], rhs_ref[p])
        mcs = []
        for hh in range(2):
            h = 2 * p + hh
            slope = 2.0 ** (-8.0 * (h + 1) / ATTN_HEADS)
            lg = l2[:, hh * Q_BLOCK:(hh + 1) * Q_BLOCK] + slope * tkm
            lg_ref[h] = lg
            mcs.append(jnp.max(_tree(lg, jnp.maximum), axis=0, keepdims=True))
        return mcs

    def stage_b_pair(c, p, m_new, alpha):
        sums = []
        for hh in range(2):
            h = 2 * p + hh
            pr = jnp.exp(lg_ref[h] - m_new[h:h + 1, :])
            sums.append(jnp.sum(_tree(pr, jnp.add), axis=0, keepdims=True))
            pv = _dot(vt_ref[c, h * HEAD_DIM:(h + 1) * HEAD_DIM, :], pr.astype(BF16))
            acc_ref[h * HEAD_DIM:(h + 1) * HEAD_DIM, :] = (
                acc_ref[h * HEAD_DIM:(h + 1) * HEAD_DIM, :] * alpha[h:h + 1, :] + pv)
        return sums

    def stage_b_all(c, m_old, l_old, mc):
        m_new = jnp.maximum(m_old, mc)
        alpha = jnp.exp(m_old - m_new)
        return m_new, alpha

    tkm0, kc0 = stage_a_prep(0)
    mc0 = jnp.concatenate(sum([stage_a_pair(0, p, tkm0, kc0) for p in range(ATTN_HEADS // 2)], []), axis=0)

    def attn_body(c, carry):
        m_old, l_old, mc_prev = carry
        m_new, alpha = stage_b_all(c - 1, m_old, l_old, mc_prev)
        tkm, kc = stage_a_prep(c)
        mcs, sums = [], []
        for p in range(ATTN_HEADS // 2):
            sums += stage_b_pair(c - 1, p, m_new, alpha)
            mcs += stage_a_pair(c, p, tkm, kc)
        return m_new, l_old * alpha + jnp.concatenate(sums, axis=0), jnp.concatenate(mcs, axis=0)

    m0 = jnp.full((ATTN_HEADS, Q_BLOCK), NEG_BIG, F32)
    l0 = jnp.zeros((ATTN_HEADS, Q_BLOCK), F32)
    m_old, l_old, mc_prev = lax.fori_loop(1, nch, attn_body, (m0, l0, mc0))
    m_new, alpha = stage_b_all(nch - 1, m_old, l_old, mc_prev)
    sums = sum([stage_b_pair(nch - 1, p, m_new, alpha) for p in range(ATTN_HEADS // 2)], [])
    l_fin = l_old * alpha + jnp.concatenate(sums, axis=0)

    outs = []
    for h in range(ATTN_HEADS):
        outs.append(acc_ref[h * HEAD_DIM:(h + 1) * HEAD_DIM, :] / l_fin[h:h + 1, :])
    o_ref[...] = jnp.concatenate(outs, axis=0).T


def _attn(it, ki, q, k, vt, B, S):
    T = B * S
    nb = S // Q_BLOCK
    n_sel = min(INDEX_TOPK_MAX, S // 4)
    return pl.pallas_call(
        functools.partial(_attn_kernel, n_sel),
        grid=(B, nb),
        in_specs=[
            pl.BlockSpec((IDXT_ROWS, Q_BLOCK), lambda b, i: (0, b * nb + i)),
            pl.BlockSpec((S, IDX_DIM), lambda b, i: (b, 0)),
            pl.BlockSpec((Q_BLOCK, ATTN_DIM), lambda b, i: (b * nb + i, 0)),
            pl.BlockSpec((S, ATTN_DIM), lambda b, i: (b, 0)),
            pl.BlockSpec((S // KEY_CHUNK, ATTN_DIM, KEY_CHUNK), lambda b, i: (b, 0, 0)),
        ],
        out_specs=pl.BlockSpec((Q_BLOCK, ATTN_DIM), lambda b, i: (b * nb + i, 0)),
        out_shape=jax.ShapeDtypeStruct((T, ATTN_DIM), F32),
        scratch_shapes=[
            pltpu.VMEM((S, Q_BLOCK), I32),
            pltpu.VMEM((-(-(S // KEY_CHUNK) // 2) * 2 * (KEY_CHUNK // FIELDS), Q_BLOCK), I32),
            pltpu.VMEM((ATTN_HEADS // 2, 2 * Q_BLOCK, 2 * HEAD_DIM), BF16),
            pltpu.VMEM((ATTN_DIM, Q_BLOCK), F32),
            pltpu.VMEM((ATTN_HEADS, KEY_CHUNK, Q_BLOCK), F32),
        ],
        compiler_params=_cparams(("arbitrary", "arbitrary")),
        name="attn",
    )(it, ki, q, k, vt)


def _mixout_kernel(tiles_per_seq, a_ref, halo_ref, y_ref, x_ref, cw_ref, wpool_ref, ps_ref,
                   wout_ref, g_ref, b_ref, o_ref, ext_ref):
    i = pl.program_id(0)
    tm = TM_MIX
    first = lax.rem(i, tiles_per_seq) == 0
    halo = jnp.where(first, 0.0, halo_ref[...])
    a = a_ref[...]
    h_c, gb_c, gc_c, p_c = (a[:, j * CONV_DIM:(j + 1) * CONV_DIM] for j in range(4))

    t0 = 2 * HALO
    n_ext = t0 + tm
    ext_ref[0:HALO, :] = jnp.zeros((HALO, CONV_DIM), F32)

    ext_ref[HALO:t0, :] = halo[:, 2 * CONV_DIM:3 * CONV_DIM] * halo[:, 0:CONV_DIM]
    u = gc_c * h_c
    ext_ref[t0:n_ext, :] = u
    cw = cw_ref[...]
    conv = cw[2:3, :] * u
    conv = conv + cw[1:2, :] * ext_ref[t0 - 1:n_ext - 1, :]
    conv = conv + cw[0:1, :] * ext_ref[t0 - 2:n_ext - 2, :]
    y_conv = gb_c * conv

    ext_ref[HALO:t0, :] = halo[:, OFF_P:OFF_P + POOL_DIM]
    ext_ref[t0:n_ext, :] = p_c
    sums = {}
    step = 1
    while step < POOL_WINDOWS[-1]:
        cur = ext_ref[HALO:n_ext, :] + ext_ref[HALO - step:n_ext - step, :]
        ext_ref[HALO:n_ext, :] = cur
        step *= 2
        sums[step] = ext_ref[t0:n_ext, :]
    tpos = (lax.rem(i, tiles_per_seq) * tm + lax.broadcasted_iota(I32, (tm, POOL_DIM), 0) + 1).astype(F32)
    grp = lax.shift_right_logical(lax.broadcasted_iota(I32, (tm, POOL_DIM), 1), _log2(POOL_GROUP))
    mean = jnp.zeros((tm, POOL_DIM), F32)
    for gi, w in enumerate(POOL_WINDOWS):
        mean = jnp.where(grp == gi, sums[w] / jnp.minimum(tpos, float(w)), mean)
    mixed = mean - p_c
    y_pool = _dot(mixed.astype(BF16), wpool_ref[...]) * ps_ref[...]

    mix = _dot(y_conv.astype(BF16), wout_ref[0:CONV_DIM, :])
    mix = mix + _dot(y_pool.astype(BF16), wout_ref[CONV_DIM:CONV_DIM + POOL_DIM, :])
    mix = mix + _dot(y_ref[...].astype(BF16), wout_ref[CONV_DIM + POOL_DIM:D_MODEL, :])
    z = DN_ALPHA * x_ref[...] + mix
    o_ref[...] = _layernorm_rows(z, g_ref[...], b_ref[...])


def _mixout(a, y_attn, x, cw, wpool_bd, ps, wout, g, b, S):
    T = a.shape[0]
    nt = T // TM_MIX
    tps = S // TM_MIX
    hb = TM_MIX // HALO
    full = lambda i: (0, 0)
    return pl.pallas_call(
        functools.partial(_mixout_kernel, tps),
        grid=(nt,),
        in_specs=[
            pl.BlockSpec((TM_MIX, OFF_Q), lambda i: (i, 0)),
            pl.BlockSpec((HALO, OFF_Q), lambda i: (jnp.maximum(i * hb - 1, 0), 0)),
            pl.BlockSpec((TM_MIX, ATTN_DIM), lambda i: (i, 0)),
            pl.BlockSpec((TM_MIX, D_MODEL), lambda i: (i, 0)),
            pl.BlockSpec((CONV_WIDTH, CONV_DIM), full),
            pl.BlockSpec((POOL_DIM, POOL_DIM), full),
            pl.BlockSpec((1, POOL_DIM), full),
            pl.BlockSpec((D_MODEL, D_MODEL), full),
            pl.BlockSpec((1, D_MODEL), full),
            pl.BlockSpec((1, D_MODEL), full),
        ],
        out_specs=pl.BlockSpec((TM_MIX, D_MODEL), lambda i: (i, 0)),
        out_shape=jax.ShapeDtypeStruct((T, D_MODEL), F32),
        scratch_shapes=[pltpu.VMEM((2 * HALO + TM_MIX, CONV_DIM), F32)],
        compiler_params=_cparams(("arbitrary",)),
        name="mixout",
    )(a, a, y_attn, x, cw, wpool_bd, ps, wout, g, b)


def _router_kernel(x_ref, rw_ref, rb_ref, idx_ref, gate_ref, cnt_ref):
    i = pl.program_id(0)
    tm = TM_ROUTE
    x = x_ref[...]
    w = rw_ref[...]
    xh = x.astype(BF16)
    xl = (x - xh.astype(F32)).astype(BF16)
    wh = w.astype(BF16)
    wl = (w - wh.astype(F32)).astype(BF16)
    logits = (_dot_nt(wh, xh) + (_dot_nt(wh, xl) + _dot_nt(wl, xh))) + rb_ref[...]
    erow = lax.broadcasted_iota(I32, (N_EXPERTS, tm), 0).astype(F32)
    work = logits
    vals, idxs = [], []
    multi = jnp.zeros((N_EXPERTS, tm), F32)
    for _ in range(TOP_K):
        mx = jnp.max(work, axis=0, keepdims=True)
        pick = jnp.min(jnp.where(work == mx, erow, float(N_EXPERTS)), axis=0, keepdims=True)
        hit = erow == pick
        work = jnp.where(hit, -jnp.inf, work)
        multi = multi + hit.astype(F32)
        vals.append(mx)
        idxs.append(pick)
    es = [jnp.exp(v - vals[0]) for v in vals]
    den = es[0] + es[1] + es[2] + es[3]
    idx_ref[...] = jnp.concatenate(idxs, axis=0).astype(I32)
    gate_ref[...] = jnp.concatenate([e / den for e in es] + [jnp.zeros((SUBLANES - TOP_K, tm), F32)], axis=0)
    part = multi[:, 0:LANES]
    for j in range(1, tm // LANES):
        part = part + multi[:, j * LANES:(j + 1) * LANES]

    @pl.when(i == 0)
    def _():
        cnt_ref[...] = jnp.zeros_like(cnt_ref)

    cnt_ref[...] += part


def _router(x1, rwT, rb):
    T = x1.shape[0]
    nt = T // TM_ROUTE
    full = lambda i: (0, 0)
    return pl.pallas_call(
        _router_kernel,
        grid=(nt,),
        in_specs=[
            pl.BlockSpec((TM_ROUTE, D_MODEL), lambda i: (i, 0)),
            pl.BlockSpec((N_EXPERTS, D_MODEL), full),
            pl.BlockSpec((N_EXPERTS, 1), full),
        ],
        out_specs=[
            pl.BlockSpec((TOP_K, TM_ROUTE), lambda i: (0, i)),
            pl.BlockSpec((SUBLANES, TM_ROUTE), lambda i: (0, i)),
            pl.BlockSpec((N_EXPERTS, LANES), full),
        ],
        out_shape=[
            jax.ShapeDtypeStruct((TOP_K, T), I32),
            jax.ShapeDtypeStruct((SUBLANES, T), F32),
            jax.ShapeDtypeStruct((N_EXPERTS, LANES), F32),
        ],
        compiler_params=_cparams(("arbitrary",)),
        name="router",
    )(x1, rwT, rb)


def _ranks_kernel(n_tiles_pad, idx_ref, cnt_ref, dest_ref, meta_ref, tri_ref, start_ref, carry_ref):
    i = pl.program_id(0)
    tm = TM_ROUTE
    erow = lax.broadcasted_iota(I32, (N_EXPERTS, LANES), 0)
    elane = lax.broadcasted_iota(I32, (N_EXPERTS, LANES), 1)

    @pl.when(i == 0)
    def _():
        cnt = jnp.sum(cnt_ref[...], axis=1, keepdims=True)
        cnt_i = jnp.broadcast_to(cnt, (N_EXPERTS, LANES)).astype(I32)
        padded = lax.shift_left(lax.shift_right_logical(cnt_i + (TM_MOE - 1), _log2(TM_MOE)), _log2(TM_MOE))
        r = lax.broadcasted_iota(I32, (N_EXPERTS, N_EXPERTS), 0)
        c = lax.broadcasted_iota(I32, (N_EXPERTS, N_EXPERTS), 1)
        low = (c <= r).astype(F32)
        pad_end = lax.dot_general(low, padded.astype(F32), (((1,), (0,)), ((), ())),
                                  precision=lax.Precision.HIGHEST, preferred_element_type=F32)
        pad_start = pad_end - padded.astype(F32)
        start_ref[...] = pad_start
        carry_ref[...] = jnp.zeros_like(carry_ref)
        a = lax.broadcasted_iota(I32, (tm, tm), 0)
        bcol = lax.broadcasted_iota(I32, (tm, tm), 1)
        tri_ref[...] = (a < bcol).astype(BF16)
        ntp = n_tiles_pad
        tstart = (lax.broadcasted_iota(I32, (N_EXPERTS, ntp), 1) * TM_MOE).astype(F32)
        pe = jnp.concatenate([pad_end] * (ntp // LANES), axis=1)
        texp = jnp.sum((pe <= tstart).astype(F32), axis=0, keepdims=True)
        texp = jnp.minimum(texp, float(N_EXPERTS - 1)).astype(I32)
        total = jnp.max(pad_end, axis=0, keepdims=True)
        n_used = lax.shift_right_logical(total.astype(I32), _log2(TM_MOE))
        zstart = jnp.sum(jnp.where(erow == elane, pad_start + cnt_i.astype(F32), 0.0),
                         axis=0, keepdims=True).astype(I32)
        zlen = jnp.sum(jnp.where(erow == elane, (padded - cnt_i).astype(F32), 0.0),
                       axis=0, keepdims=True).astype(I32)
        etiles = lax.shift_right_logical(
            jnp.sum(jnp.where(erow == elane, padded.astype(F32), 0.0), axis=0, keepdims=True).astype(I32),
            _log2(TM_MOE))
        lanes_pad = jnp.zeros((1, ntp - LANES), I32)
        meta_ref[...] = jnp.concatenate(
            [texp,
             jnp.concatenate([n_used, lanes_pad], axis=1),
             jnp.concatenate([zstart, lanes_pad], axis=1),
             jnp.concatenate([zlen, lanes_pad], axis=1),
             jnp.concatenate([etiles, lanes_pad], axis=1),
             jnp.zeros((SUBLANES - 5, ntp), I32)], axis=0)

    idx = idx_ref[...]
    erow_t = lax.broadcasted_iota(I32, (N_EXPERTS, tm), 0)
    hits = [erow_t == idx[k:k + 1, :] for k in range(TOP_K)]
    multi = hits[0].astype(F32)
    for k in range(1, TOP_K):
        multi = multi + hits[k].astype(F32)
    prefix = _dot(multi.astype(BF16), tri_ref[...])
    base = jnp.concatenate([carry_ref[...] + start_ref[...]] * (tm // LANES), axis=1)
    tot = prefix + base
    dest_ref[...] = jnp.concatenate(
        [jnp.sum(jnp.where(hits[k], tot, 0.0), axis=0, keepdims=True) for k in range(TOP_K)],
        axis=0).astype(I32)
    carry_ref[...] += jnp.broadcast_to(jnp.sum(multi, axis=1, keepdims=True), (N_EXPERTS, LANES))


def _ranks(idxT, cnt, n_tiles_pad):
    T = idxT.shape[1]
    nt = T // TM_ROUTE
    full = lambda i: (0, 0)
    return pl.pallas_call(
        functools.partial(_ranks_kernel, n_tiles_pad),
        grid=(nt,),
        in_specs=[
            pl.BlockSpec((TOP_K, TM_ROUTE), lambda i: (0, i)),
            pl.BlockSpec((N_EXPERTS, LANES), full),
        ],
        out_specs=[
            pl.BlockSpec((TOP_K, TM_ROUTE), lambda i: (0, i)),
            pl.BlockSpec((SUBLANES, n_tiles_pad), full),
        ],
        out_shape=[
            jax.ShapeDtypeStruct((TOP_K, T), I32),
            jax.ShapeDtypeStruct((SUBLANES, n_tiles_pad), I32),
        ],
        scratch_shapes=[
            pltpu.VMEM((TM_ROUTE, TM_ROUTE), BF16),
            pltpu.VMEM((N_EXPERTS, LANES), F32),
            pltpu.VMEM((N_EXPERTS, LANES), F32),
        ],
        compiler_params=_cparams(("arbitrary",)),
        name="ranks",
    )(idxT, cnt)


RT = D_MODEL // LANES
DMA_UNROLL = 8


def _row_copy(src_ref, s, dst_ref, d, sem):
    return pltpu.make_async_copy(src_ref.at[pl.ds(pl.multiple_of(s * RT, RT), RT)],
                                 dst_ref.at[pl.ds(pl.multiple_of(d * RT, RT), RT)], sem)


def _to_row_tiled(dst_ref, val, rows):
    for s in range(RT):
        dst_ref[pl.ds(s, rows, stride=RT), :] = val[:, s * LANES:(s + 1) * LANES]


def _from_row_tiled(src_ref, rows):
    return [src_ref[pl.ds(s, rows, stride=RT), :] for s in range(RT)]


def _scatter_kernel(meta_ref, x_ref, dest_ref, xs_ref, stage_ref, zero_ref, sem):
    i = pl.program_id(0)
    tm = TM_ROUTE

    @pl.when(i == 0)
    def _():
        zero_ref[...] = jnp.zeros_like(zero_ref)

        def zcopy(row, nrows):
            off = pl.multiple_of(row * RT, RT)
            return pltpu.make_async_copy(zero_ref.at[pl.ds(0, nrows * RT)],
                                         xs_ref.at[pl.ds(off, nrows * RT)], sem)

        def pad_fill(wait):
            def body(e, c):
                row = meta_ref[2, e]
                plen = meta_ref[3, e]
                for bit in reversed(range(_log2(TM_MOE))):
                    size = 1 << bit
                    has = (plen & size) != 0

                    @pl.when(has)
                    def _():
                        cp = zcopy(row, size)
                        cp.wait() if wait else cp.start()

                    row = row + jnp.where(has, size, 0)
                return c
            lax.fori_loop(0, N_EXPERTS, body, 0)

        def tail_fill(wait):
            def body(j, c):
                cp = zcopy(j * TM_MOE, TM_MOE)
                cp.wait() if wait else cp.start()
                return c
            lax.fori_loop(meta_ref[1, 0], xs_ref.shape[0] // (TM_MOE * RT), body, 0)

        pad_fill(False)
        tail_fill(False)
        pad_fill(True)
        tail_fill(True)

    _to_row_tiled(stage_ref, x_ref[...], tm)

    def start(t, c):
        for k in range(TOP_K):
            _row_copy(stage_ref, t, xs_ref, dest_ref[k, t], sem).start(priority=k % 2)
        return c

    lax.fori_loop(0, tm, start, 0, unroll=DMA_UNROLL)
    for k in range(TOP_K):
        pltpu.make_async_copy(stage_ref, xs_ref.at[pl.ds(0, tm * RT)], sem).wait()


def _scatter(meta, x1, destT, n_rows_alloc):
    T = x1.shape[0]
    nt = T // TM_ROUTE
    return pl.pallas_call(
        _scatter_kernel,
        grid_spec=pltpu.PrefetchScalarGridSpec(
            num_scalar_prefetch=1,
            grid=(nt,),
            in_specs=[
                pl.BlockSpec((TM_ROUTE, D_MODEL), lambda i, m: (i, 0)),
                pl.BlockSpec((TOP_K, TM_ROUTE), lambda i, m: (0, i), memory_space=pltpu.SMEM),
            ],
            out_specs=pl.BlockSpec(memory_space=pl.ANY),
            scratch_shapes=[
                pltpu.VMEM((TM_ROUTE * RT, LANES), F32),
                pltpu.VMEM((TM_MOE * RT, LANES), F32),
                pltpu.SemaphoreType.DMA(()),
            ],
        ),
        out_shape=jax.ShapeDtypeStruct((n_rows_alloc * RT, LANES), F32),
        compiler_params=_cparams(("arbitrary",)),
        name="scatter",
    )(meta, x1, destT)


def _gmm_kernel(layer, meta_ref, xs_ref, wgu_hbm, bgu_ref, wd_hbm, bd_ref, ys_ref,
                wgu_f, wd_f, wgu_b, wd_b, lhs_ref, act_ref, grp_ref, sems):
    j = pl.program_id(0)
    n_used = meta_ref[1, 0]
    e_now = meta_ref[0, j]
    e_prev = meta_ref[0, jnp.maximum(j - 1, 0)]
    used = j < n_used

    def weight_copies(e, slot):
        return (pltpu.make_async_copy(wgu_hbm.at[layer, e], wgu_f.at[slot], sems.at[0, slot]),
                pltpu.make_async_copy(wd_hbm.at[layer, e], wd_f.at[slot], sems.at[1, slot]))

    @pl.when(used & (j == 0))
    def _():
        grp_ref[0] = 0
        for cp in weight_copies(e_now, 0):
            cp.start()

    @pl.when(used & ((j == 0) | (e_now != e_prev)))
    def _():
        slot = grp_ref[0]
        j_next = j + meta_ref[4, e_now]

        @pl.when(j_next < n_used)
        def _():
            for cp in weight_copies(meta_ref[0, jnp.minimum(j_next, meta_ref.shape[1] - 1)], 1 - slot):
                cp.start()

        for cp in weight_copies(e_now, slot):
            cp.wait()
        wgu_b[...] = wgu_f[slot].astype(BF16)
        wd_b[...] = wd_f[slot].astype(BF16)
        grp_ref[0] = 1 - slot

    @pl.when(used)
    def _():
        for s, piece in enumerate(_from_row_tiled(xs_ref, TM_MOE)):
            lhs_ref[:, s * LANES:(s + 1) * LANES] = piece.astype(BF16)
        x = lhs_ref[...]
        nc = 512
        for c in range(D_FF // nc):
            gate = _dot(x, wgu_b[:, c * nc:(c + 1) * nc]) + bgu_ref[0, 0, :, c * nc:(c + 1) * nc]
            up = (_dot(x, wgu_b[:, D_FF + c * nc:D_FF + (c + 1) * nc])
                  + bgu_ref[0, 0, :, D_FF + c * nc:D_FF + (c + 1) * nc])
            gate = jnp.minimum(gate, SWIGLU_LIMIT)
            up = jnp.clip(up, -SWIGLU_LIMIT, SWIGLU_LIMIT)
            act = gate * jax.nn.sigmoid(SWIGLU_ALPHA * gate) * (up + 1.0)
            act_ref[:, c * nc:(c + 1) * nc] = act.astype(BF16)
        _to_row_tiled(ys_ref, _dot(act_ref[...], wd_b[...]) + bd_ref[0, 0], TM_MOE)

    @pl.when(jnp.logical_not(used))
    def _():
        ys_ref[...] = jnp.zeros_like(ys_ref)


def _gmm(meta, xs, w_gu, b_gu, w_down, b_down, n_tiles, layer):
    last = lambda m: jnp.maximum(m[1, 0] - 1, 0)
    return pl.pallas_call(
        functools.partial(_gmm_kernel, layer),
        grid_spec=pltpu.PrefetchScalarGridSpec(
            num_scalar_prefetch=1,
            grid=(n_tiles,),
            in_specs=[
                pl.BlockSpec((TM_MOE * RT, LANES), lambda j, m: (jnp.minimum(j, last(m)), 0)),
                pl.BlockSpec(memory_space=pl.ANY),
                pl.BlockSpec((1, 1, 1, 2 * D_FF), lambda j, m: (layer, m[0, j], 0, 0)),
                pl.BlockSpec(memory_space=pl.ANY),
                pl.BlockSpec((1, 1, 1, D_MODEL), lambda j, m: (layer, m[0, j], 0, 0)),
            ],
            out_specs=pl.BlockSpec((TM_MOE * RT, LANES), lambda j, m: (j, 0)),
            scratch_shapes=[
                pltpu.VMEM((2, D_MODEL, 2 * D_FF), F32),
                pltpu.VMEM((2, D_FF, D_MODEL), F32),
                pltpu.VMEM((D_MODEL, 2 * D_FF), BF16),
                pltpu.VMEM((D_FF, D_MODEL), BF16),
                pltpu.VMEM((TM_MOE, D_MODEL), BF16),
                pltpu.VMEM((TM_MOE, D_FF), BF16),
                pltpu.SMEM((1,), I32),
                pltpu.SemaphoreType.DMA((2, 2)),
            ],
        ),
        out_shape=jax.ShapeDtypeStruct((n_tiles * TM_MOE * RT, LANES), F32),
        compiler_params=_cparams(("arbitrary",)),
        name="gmm",
    )(meta, xs, w_gu, b_gu, w_down, b_down)


def _combine_kernel(x_ref, dest_ref, gate_ref, ys_ref, g_ref, b_ref, o_ref, ob_ref, buf_ref, sem):
    tm = TM_COMB

    def start(t, c):
        for k in range(TOP_K):
            _row_copy(ys_ref, dest_ref[k, t], buf_ref.at[k], t, sem).start(priority=k % 2)
        return c

    lax.fori_loop(0, tm, start, 0, unroll=DMA_UNROLL)
    for k in range(TOP_K):
        pltpu.make_async_copy(ys_ref.at[pl.ds(0, tm * RT)], buf_ref.at[k], sem).wait()
    gates = jnp.concatenate(
        [gate_ref[...], jnp.zeros((LANES - SUBLANES, tm), F32)], axis=0).T
    pieces = None
    for k in range(TOP_K):
        gk = gates[:, k:k + 1]
        rows = [p * gk for p in _from_row_tiled(buf_ref.at[k], tm)]
        pieces = rows if pieces is None else [a + r for a, r in zip(pieces, rows)]
    z = DN_ALPHA * x_ref[...] + jnp.concatenate(pieces, axis=1)
    out = _layernorm_rows(z, g_ref[...], b_ref[...])
    o_ref[...] = out
    ob_ref[...] = out.astype(BF16)


def _combine(x1, destT, gateT, ys, g, b):
    T = x1.shape[0]
    nt = T // TM_COMB
    full = lambda i: (0, 0)
    return pl.pallas_call(
        _combine_kernel,
        grid=(nt,),
        in_specs=[
            pl.BlockSpec((TM_COMB, D_MODEL), lambda i: (i, 0)),
            pl.BlockSpec((TOP_K, TM_COMB), lambda i: (0, i), memory_space=pltpu.SMEM),
            pl.BlockSpec((SUBLANES, TM_COMB), lambda i: (0, i)),
            pl.BlockSpec(memory_space=pl.ANY),
            pl.BlockSpec((1, D_MODEL), full),
            pl.BlockSpec((1, D_MODEL), full),
        ],
        out_specs=[
            pl.BlockSpec((TM_COMB, D_MODEL), lambda i: (i, 0)),
            pl.BlockSpec((TM_COMB, D_MODEL), lambda i: (i, 0)),
        ],
        out_shape=[
            jax.ShapeDtypeStruct((T, D_MODEL), F32),
            jax.ShapeDtypeStruct((T, D_MODEL), BF16),
        ],
        scratch_shapes=[
            pltpu.VMEM((TOP_K, TM_COMB * RT, LANES), F32),
            pltpu.SemaphoreType.DMA(()),
        ],
        compiler_params=_cparams(("arbitrary",)),
        name="combine",
    )(x1, destT, gateT, ys, g, b)


def _block_diag(w):
    g, c, _ = w.shape
    out = jnp.zeros((g * c, g * c), w.dtype)
    for i in range(g):
        out = out.at[i * c:(i + 1) * c, i * c:(i + 1) * c].set(w[i])
    return out


def kernel(x, w_in, conv_w, w_pool, pool_scale, idx_kn_g, idx_kn_b, w_out, ln1_g, ln1_b, router_w,
           router_b, w_gu, b_gu, w_down, b_down, ln2_g, ln2_b):
    B, S, D = x.shape
    T = B * S
    depth = w_in.shape[0]
    n_pairs = T * TOP_K
    n_tiles = (n_pairs + N_EXPERTS * (TM_MOE - 1)) // TM_MOE + 1
    n_tiles_pad = ((n_tiles + LANES - 1) // LANES) * LANES
    n_rows_alloc = (n_tiles + 1) * TM_MOE

    xf = x.reshape(T, D)
    xb = xf
    for l in range(depth):
        wl = w_in[l]
        wn = jnp.concatenate(
            [wl[:, 0:OFF_V], wl[:, OFF_KI:OFF_WI], jnp.zeros((D, LANES - IDX_DIM), F32)], axis=1).astype(BF16)
        wt = jnp.concatenate(
            [wl[:, OFF_V:OFF_QI], wl[:, OFF_QI:OFF_KI], wl[:, OFF_WI:D_IN],
             jnp.zeros((D, IDXT_ROWS - IDX_HEADS * IDX_DIM - IDX_HEADS), F32)], axis=1).T.astype(BF16)
        a, q, k, ki, vt, it = _inproj(xb, wn, wt, idx_kn_g[l].reshape(1, -1), idx_kn_b[l].reshape(1, -1))
        y_attn = _attn(it, ki, q, k, vt, B, S)
        x1 = _mixout(a, y_attn, xf, conv_w[l], _block_diag(w_pool[l]).astype(BF16),
                     pool_scale[l].reshape(1, -1), w_out[l].astype(BF16),
                     ln1_g[l].reshape(1, -1), ln1_b[l].reshape(1, -1), S)
        idxT, gateT, cnt = _router(x1, router_w[l].T, router_b[l].reshape(-1, 1))
        destT, meta = _ranks(idxT, cnt, n_tiles_pad)
        xs = _scatter(meta, x1, destT, n_rows_alloc)
        ys = _gmm(meta, xs, w_gu, b_gu.reshape(depth, N_EXPERTS, 1, -1), w_down,
                  b_down.reshape(depth, N_EXPERTS, 1, -1), n_tiles, l)
        xf, xb = _combine(x1, destT, gateT, ys, ln2_g[l].reshape(1, -1), ln2_b[l].reshape(1, -1))
    return xf.reshape(B, S, D)
```

```python
import functools

import jax
import jax.numpy as jnp
from jax import lax
from jax.experimental import pallas as pl
from jax.experimental.pallas import tpu as pltpu

F32 = jnp.float32
BF16 = jnp.bfloat16
I32 = jnp.int32

D_MODEL = 1024
CONV_DIM = 256
CONV_WIDTH = 3
POOL_WINDOWS = (2, 4, 8, 16)
POOL_GROUP = 64
POOL_DIM = 256
ATTN_HEADS = 8
HEAD_DIM = 64
ATTN_DIM = 512
IDX_HEADS = 8
IDX_DIM = 32
INDEX_TOPK_MAX = 256
Q_BLOCK = 128
N_EXPERTS = 32
TOP_K = 4
D_FF = 1024
SWIGLU_ALPHA = 1.702
SWIGLU_LIMIT = 7.0
LN_EPS = 1e-5
DEPTH = 2
DN_ALPHA = (2.0 * DEPTH) ** 0.25

OFF_P = 3 * CONV_DIM
OFF_Q = OFF_P + POOL_DIM
OFF_K = OFF_Q + ATTN_DIM
OFF_V = OFF_K + ATTN_DIM
OFF_QI = OFF_V + ATTN_DIM
OFF_KI = OFF_QI + IDX_HEADS * IDX_DIM
OFF_WI = OFF_KI + IDX_DIM
D_IN = OFF_WI + IDX_HEADS

LANES = 128
SUBLANES = 8
VMEM_LIMIT = 56 * 1024 * 1024
KEY_CHUNK = 512
TIE_GROUP = 128
HALO = 16
TM_PROJ = 512
TM_MIX = 1024
TM_ROUTE = 1024
TM_MOE = 512
TM_COMB = 1024
IDXT_ROWS = IDX_HEADS * IDX_DIM + 16
WT_ROWS = ATTN_DIM + IDXT_ROWS
INT_MIN = -(2 ** 31)
NEG_BIG = -1e30


def _cparams(sem):
    return pltpu.CompilerParams(dimension_semantics=sem, vmem_limit_bytes=VMEM_LIMIT)


def _log2(n):
    k = n.bit_length() - 1
    assert 1 << k == n
    return k


def _dot(a, b):
    return jnp.dot(a, b, preferred_element_type=F32)


def _dot_nt(a, b):
    return lax.dot_general(a, b, (((1,), (1,)), ((), ())), preferred_element_type=F32)


def _tree(x, op):
    parts = [x[j * SUBLANES:(j + 1) * SUBLANES, :] for j in range(x.shape[0] // SUBLANES)]
    while len(parts) > 1:
        nxt = [op(parts[j], parts[j + 1]) for j in range(0, len(parts) - 1, 2)]
        if len(parts) % 2:
            nxt.append(parts[-1])
        parts = nxt
    return parts[0]


def _layernorm_rows(z, g, b):
    mu = jnp.mean(z, axis=-1, keepdims=True)
    d = z - mu
    var = jnp.mean(d * d, axis=-1, keepdims=True)
    return d * lax.rsqrt(var + LN_EPS) * g + b


def _inproj_kernel(x_ref, wn_ref, wt_ref, kg_ref, kb_ref,
                   a_ref, q_ref, k_ref, ki_ref, vt_ref, it_ref):
    x = x_ref[...].astype(BF16)
    a_ref[...] = _dot(x, wn_ref[:, 0:OFF_Q])
    q_ref[...] = (_dot(x, wn_ref[:, OFF_Q:OFF_K]) * (HEAD_DIM ** -0.5)).astype(BF16)
    k_ref[...] = _dot(x, wn_ref[:, OFF_K:OFF_V]).astype(BF16)
    ki = _dot(x, wn_ref[:, OFF_V:OFF_V + LANES])[:, 0:IDX_DIM]
    ki_ref[...] = _layernorm_rows(ki, kg_ref[...], kb_ref[...])
    t = _dot_nt(wt_ref[...], x)
    for c in range(TM_PROJ // KEY_CHUNK):
        vt_ref[c] = t[0:ATTN_DIM, c * KEY_CHUNK:(c + 1) * KEY_CHUNK].astype(BF16)
    it_ref[...] = t[ATTN_DIM:WT_ROWS, :]


def _inproj(xb, wn, wt, kg, kb):
    T = xb.shape[0]
    nt = T // TM_PROJ
    cpt = TM_PROJ // KEY_CHUNK
    full = lambda i: (0, 0)
    return pl.pallas_call(
        _inproj_kernel,
        grid=(nt,),
        in_specs=[
            pl.BlockSpec((TM_PROJ, D_MODEL), lambda i: (i, 0)),
            pl.BlockSpec(wn.shape, full),
            pl.BlockSpec(wt.shape, full),
            pl.BlockSpec((1, IDX_DIM), full),
            pl.BlockSpec((1, IDX_DIM), full),
        ],
        out_specs=[
            pl.BlockSpec((TM_PROJ, OFF_Q), lambda i: (i, 0)),
            pl.BlockSpec((TM_PROJ, ATTN_DIM), lambda i: (i, 0)),
            pl.BlockSpec((TM_PROJ, ATTN_DIM), lambda i: (i, 0)),
            pl.BlockSpec((TM_PROJ, IDX_DIM), lambda i: (i, 0)),
            pl.BlockSpec((cpt, ATTN_DIM, KEY_CHUNK), lambda i: (i, 0, 0)),
            pl.BlockSpec((IDXT_ROWS, TM_PROJ), lambda i: (0, i)),
        ],
        out_shape=[
            jax.ShapeDtypeStruct((T, OFF_Q), F32),
            jax.ShapeDtypeStruct((T, ATTN_DIM), BF16),
            jax.ShapeDtypeStruct((T, ATTN_DIM), BF16),
            jax.ShapeDtypeStruct((T, IDX_DIM), F32),
            jax.ShapeDtypeStruct((T // KEY_CHUNK, ATTN_DIM, KEY_CHUNK), BF16),
            jax.ShapeDtypeStruct((IDXT_ROWS, T), F32),
        ],
        compiler_params=_cparams(("arbitrary",)),
        name="inproj",
    )(xb, wn, wt, kg, kb)


FIELD_BITS = 8
DIGIT_BITS = FIELD_BITS - 1
FIELDS = 32 // FIELD_BITS
DIGIT_SHIFTS = (25, 18, 11, 4, 0)
EARLY_EXIT_FROM = 4
FIELD_ONES = 0x01010101
FIELD_GUARDS = 0x80808080 - (1 << 32)


def _pack_fields(f):
    q = f.shape[0] // FIELDS
    w = f[0:q]
    for j in range(1, FIELDS):
        w = w | lax.shift_left(f[j * q:(j + 1) * q], FIELD_BITS * j)
    return w | jnp.int32(FIELD_GUARDS)


def _count_fields_ge(words_ref, nch, cand):
    wq = 2 * (KEY_CHUNK // FIELDS)
    assert words_ref.shape[0] // SUBLANES < (1 << FIELD_BITS)
    cvec = cand * jnp.int32(FIELD_ONES)

    def body(c, acc):
        base = pl.multiple_of(c * wq, wq)
        w = words_ref[pl.ds(base, wq), :]
        hit = lax.shift_right_logical(w - cvec, DIGIT_BITS) & jnp.int32(FIELD_ONES)
        return acc + _tree(hit, jnp.add)

    acc = lax.fori_loop(0, lax.shift_right_logical(nch + 1, 1), body,
                        jnp.zeros((SUBLANES, words_ref.shape[1]), I32))
    tot = acc & 255
    for j in range(1, FIELDS):
        tot = tot + (lax.shift_right_logical(acc, FIELD_BITS * j) & 255)
    return jnp.sum(tot.astype(F32), axis=0, keepdims=True)


def _topk_threshold(keys_ref, words_ref, nch, n_sel, n_causal):
    lanes = keys_ref.shape[1]
    wq = KEY_CHUNK // FIELDS
    all_taken = n_causal <= n_sel

    def run_digit(idx, state):
        prefix, want, bucket = state
        shift = DIGIT_SHIFTS[idx]
        width = (DIGIT_SHIFTS[idx - 1] if idx else 32) - shift
        if idx:
            def prep(c, carry):
                key = keys_ref[pl.ds(pl.multiple_of(c * KEY_CHUNK, KEY_CHUNK), KEY_CHUNK), :]
                digit = (key >> shift) & jnp.int32((1 << width) - 1)
                fields = jnp.where((key >> (shift + width)) == prefix, digit, 0)
                words_ref[pl.ds(pl.multiple_of(c * wq, wq), wq), :] = _pack_fields(fields)
                return carry

            lax.fori_loop(0, nch, prep, 0)
        dig = jnp.zeros((1, lanes), I32)
        at_dig = bucket
        above = jnp.zeros((1, lanes), F32)
        for bit in reversed(range(width)):
            cand = dig + (1 << bit)
            cnt = _count_fields_ge(words_ref, nch, cand)
            ok = cnt >= want
            dig = jnp.where(ok, cand, dig)
            at_dig = jnp.where(ok, cnt, at_dig)
            above = jnp.where(ok, above, cnt)
        prefix = (lax.shift_left(prefix, width) | dig) if idx else dig - (1 << (DIGIT_BITS - 1))
        return prefix, want - above, at_dig - above

    def finish(state, low_bits):
        prefix, want, bucket = state
        tau = jnp.where(all_taken, jnp.int32(INT_MIN + 1), lax.shift_left(prefix, low_bits))
        return tau, want, jnp.where(all_taken, 0.0, bucket - want)

    def settled(state):
        _, want, bucket = state
        return jnp.min(jnp.where((bucket == want) | all_taken, 1.0, 0.0)) > 0.0

    def from_digit(idx, state):
        if idx == len(DIGIT_SHIFTS):
            return finish(state, 0)
        if idx < EARLY_EXIT_FROM:
            return from_digit(idx + 1, run_digit(idx, state))
        return lax.cond(settled(state),
                        lambda s: finish(s, DIGIT_SHIFTS[idx - 1]),
                        lambda s: from_digit(idx + 1, run_digit(idx, s)), state)

    state = (jnp.zeros((1, lanes), I32), jnp.full((1, lanes), float(n_sel), F32), n_causal.astype(F32))
    return from_digit(0, state)


def _attn_kernel(n_sel, it_ref, ki_ref, q_ref, k_ref, vt_ref, o_ref,
                 keys_ref, words_ref, rhs_ref, acc_ref, lg_ref):
    i = pl.program_id(1)
    nch = lax.shift_right_logical(i * Q_BLOCK + Q_BLOCK + KEY_CHUNK - 1, _log2(KEY_CHUNK))
    q0 = i * Q_BLOCK
    KC = KEY_CHUNK
    row = lax.broadcasted_iota(I32, (KC, Q_BLOCK), 0)
    lane = lax.broadcasted_iota(I32, (KC, Q_BLOCK), 1)
    qpos = q0 + lane

    qcat = jnp.concatenate(
        [it_ref[h * IDX_DIM:(h + 1) * IDX_DIM, :] for h in range(IDX_HEADS)], axis=1).astype(BF16)
    w_all = it_ref[IDX_HEADS * IDX_DIM:IDX_HEADS * IDX_DIM + IDX_HEADS, :] * (
        (IDX_HEADS ** -0.5) * (IDX_DIM ** -0.5))

    def score_chunk(c, has_future_keys):
        base = pl.multiple_of(c * KC, KC)
        kic = ki_ref[pl.ds(base, KC), :].astype(BF16)
        s = _dot(kic, qcat)
        sc = jnp.zeros((KC, Q_BLOCK), F32)
        for h in range(IDX_HEADS):
            sc = sc + jnp.maximum(s[:, h * Q_BLOCK:(h + 1) * Q_BLOCK], 0.0) * w_all[h:h + 1, :]
        bits = lax.bitcast_convert_type(sc, I32)
        key = bits ^ ((bits >> 31) & jnp.int32(0x7FFFFFFF))
        key = jnp.where(bits == jnp.int32(INT_MIN), jnp.int32(0), key)
        top = (key >> DIGIT_SHIFTS[0]) + (1 << (DIGIT_BITS - 1))
        if has_future_keys:
            causal = base + row <= qpos
            key = jnp.where(causal, key, jnp.int32(INT_MIN))
            top = jnp.where(causal, top, 0)
        keys_ref[pl.ds(base, KC), :] = key
        words_ref[pl.ds(pl.multiple_of(c * (KC // FIELDS), KC // FIELDS), KC // FIELDS), :] = _pack_fields(top)

    def score_body(c, carry):
        score_chunk(c, False)
        return carry

    lax.fori_loop(0, nch - 1, score_body, 0)
    score_chunk(nch - 1, True)

    @pl.when(lax.rem(nch, 2) == 1)
    def _():
        wq = KC // FIELDS
        words_ref[pl.ds(pl.multiple_of(nch * wq, wq), wq), :] = _pack_fields(jnp.zeros((KC, Q_BLOCK), I32))

    n_causal = q0 + lax.broadcasted_iota(I32, (1, Q_BLOCK), 1) + 1
    tau, keep, excess = _topk_threshold(keys_ref, words_ref, nch, n_sel, n_causal)

    @pl.when(jnp.max(excess) > 0.0)
    def _():
        g = TIE_GROUP
        r = lax.broadcasted_iota(I32, (g, g), 0)
        cc = lax.broadcasted_iota(I32, (g, g), 1)
        tri = (cc <= r).astype(BF16)

        def retire(c, before):
            base = pl.multiple_of(c * KC, KC)
            blk = keys_ref[pl.ds(base, KC), :]
            tied = blk == tau
            ones = jnp.where(tied, 1.0, 0.0).astype(BF16)
            local = [_dot(tri, ones[j * g:(j + 1) * g, :]) for j in range(KC // g)]
            upto = []
            for lj in local:
                upto.append(lj + before)
                before = before + lj[g - 1:g, :]
            upto = jnp.concatenate(upto, axis=0)
            keys_ref[pl.ds(base, KC), :] = jnp.where(tied & (upto > keep), jnp.int32(INT_MIN), blk)
            return before

        lax.fori_loop(0, nch, retire, jnp.zeros((1, Q_BLOCK), F32))

    q = q_ref[...]
    lane_q = lax.broadcasted_iota(I32, (Q_BLOCK, 2 * HEAD_DIM), 1)
    for p in range(ATTN_HEADS // 2):
        qp = q[:, p * 2 * HEAD_DIM:(p + 1) * 2 * HEAD_DIM]
        rhs_ref[p, 0:Q_BLOCK, :] = jnp.where(lane_q < HEAD_DIM, qp, jnp.zeros_like(qp))
        rhs_ref[p, Q_BLOCK:2 * Q_BLOCK, :] = jnp.where(lane_q >= HEAD_DIM, qp, jnp.zeros_like(qp))
    acc_ref[...] = jnp.zeros_like(acc_ref)

    def stage_a_prep(c):
        base = pl.multiple_of(c * KC, KC)
        tk = (base + row - (q0 + Q_BLOCK - 1)).astype(F32)
        tkm = jnp.where(keys_ref[pl.ds(base, KC), :] >= tau, tk, NEG_BIG)
        return tkm, k_ref[pl.ds(base, KC), :]

    def stage_a_pair(c, p, tkm, kc):
        l2 = _dot_nt(kc[:, p * 2 * HEAD_DIM:(p + 1) * 2 * HEAD_DIM], rhs_ref[p])
        mcs = []
        for hh in range(2):
            h = 2 * p + hh
            slope = 2.0 ** (-8.0 * (h + 1) / ATTN_HEADS)
            lg = l2[:, hh * Q_BLOCK:(hh + 1) * Q_BLOCK] + slope * tkm
            lg_ref[h] = lg
            mcs.append(jnp.max(_tree(lg, jnp.maximum), axis=0, keepdims=True))
        return mcs

    def stage_b_pair(c, p, m_new, alpha):
        sums = []
        for hh in range(2):
            h = 2 * p + hh
            pr = jnp.exp(lg_ref[h] - m_new[h:h + 1, :])
            sums.append(jnp.sum(_tree(pr, jnp.add), axis=0, keepdims=True))
            pv = _dot(vt_ref[c, h * HEAD_DIM:(h + 1) * HEAD_DIM, :], pr.astype(BF16))
            acc_ref[h * HEAD_DIM:(h + 1) * HEAD_DIM, :] = (
                acc_ref[h * HEAD_DIM:(h + 1) * HEAD_DIM, :] * alpha[h:h + 1, :] + pv)
        return sums

    def stage_b_all(c, m_old, l_old, mc):
        m_new = jnp.maximum(m_old, mc)
        alpha = jnp.exp(m_old - m_new)
        return m_new, alpha

    tkm0, kc0 = stage_a_prep(0)
    mc0 = jnp.concatenate(sum([stage_a_pair(0, p, tkm0, kc0) for p in range(ATTN_HEADS // 2)], []), axis=0)

    def attn_body(c, carry):
        m_old, l_old, mc_prev = carry
        m_new, alpha = stage_b_all(c - 1, m_old, l_old, mc_prev)
        tkm, kc = stage_a_prep(c)
        mcs, sums = [], []
        for p in range(ATTN_HEADS // 2):
            sums += stage_b_pair(c - 1, p, m_new, alpha)
            mcs += stage_a_pair(c, p, tkm, kc)
        return m_new, l_old * alpha + jnp.concatenate(sums, axis=0), jnp.concatenate(mcs, axis=0)

    m0 = jnp.full((ATTN_HEADS, Q_BLOCK), NEG_BIG, F32)
    l0 = jnp.zeros((ATTN_HEADS, Q_BLOCK), F32)
    m_old, l_old, mc_prev = lax.fori_loop(1, nch, attn_body, (m0, l0, mc0))
    m_new, alpha = stage_b_all(nch - 1, m_old, l_old, mc_prev)
    sums = sum([stage_b_pair(nch - 1, p, m_new, alpha) for p in range(ATTN_HEADS // 2)], [])
    l_fin = l_old * alpha + jnp.concatenate(sums, axis=0)

    outs = []
    for h in range(ATTN_HEADS):
        outs.append(acc_ref[h * HEAD_DIM:(h + 1) * HEAD_DIM, :] / l_fin[h:h + 1, :])
    o_ref[...] = jnp.concatenate(outs, axis=0).T


def _attn(it, ki, q, k, vt, B, S):
    T = B * S
    nb = S // Q_BLOCK
    n_sel = min(INDEX_TOPK_MAX, S // 4)
    return pl.pallas_call(
        functools.partial(_attn_kernel, n_sel),
        grid=(B, nb),
        in_specs=[
            pl.BlockSpec((IDXT_ROWS, Q_BLOCK), lambda b, i: (0, b * nb + i)),
            pl.BlockSpec((S, IDX_DIM), lambda b, i: (b, 0)),
            pl.BlockSpec((Q_BLOCK, ATTN_DIM), lambda b, i: (b * nb + i, 0)),
            pl.BlockSpec((S, ATTN_DIM), lambda b, i: (b, 0)),
            pl.BlockSpec((S // KEY_CHUNK, ATTN_DIM, KEY_CHUNK), lambda b, i: (b, 0, 0)),
        ],
        out_specs=pl.BlockSpec((Q_BLOCK, ATTN_DIM), lambda b, i: (b * nb + i, 0)),
        out_shape=jax.ShapeDtypeStruct((T, ATTN_DIM), F32),
        scratch_shapes=[
            pltpu.VMEM((S, Q_BLOCK), I32),
            pltpu.VMEM((-(-(S // KEY_CHUNK) // 2) * 2 * (KEY_CHUNK // FIELDS), Q_BLOCK), I32),
            pltpu.VMEM((ATTN_HEADS // 2, 2 * Q_BLOCK, 2 * HEAD_DIM), BF16),
            pltpu.VMEM((ATTN_DIM, Q_BLOCK), F32),
            pltpu.VMEM((ATTN_HEADS, KEY_CHUNK, Q_BLOCK), F32),
        ],
        compiler_params=_cparams(("arbitrary", "arbitrary")),
        name="attn",
    )(it, ki, q, k, vt)


def _mixout_kernel(tiles_per_seq, a_ref, halo_ref, y_ref, x_ref, cw_ref, wpool_ref, ps_ref,
                   wout_ref, g_ref, b_ref, o_ref, ext_ref):
    i = pl.program_id(0)
    tm = TM_MIX
    first = lax.rem(i, tiles_per_seq) == 0
    halo = jnp.where(first, 0.0, halo_ref[...])
    a = a_ref[...]
    h_c, gb_c, gc_c, p_c = (a[:, j * CONV_DIM:(j + 1) * CONV_DIM] for j in range(4))

    t0 = 2 * HALO
    n_ext = t0 + tm
    ext_ref[0:HALO, :] = jnp.zeros((HALO, CONV_DIM), F32)

    ext_ref[HALO:t0, :] = halo[:, 2 * CONV_DIM:3 * CONV_DIM] * halo[:, 0:CONV_DIM]
    u = gc_c * h_c
    ext_ref[t0:n_ext, :] = u
    cw = cw_ref[...]
    conv = cw[2:3, :] * u
    conv = conv + cw[1:2, :] * ext_ref[t0 - 1:n_ext - 1, :]
    conv = conv + cw[0:1, :] * ext_ref[t0 - 2:n_ext - 2, :]
    y_conv = gb_c * conv

    ext_ref[HALO:t0, :] = halo[:, OFF_P:OFF_P + POOL_DIM]
    ext_ref[t0:n_ext, :] = p_c
    sums = {}
    step = 1
    while step < POOL_WINDOWS[-1]:
        cur = ext_ref[HALO:n_ext, :] + ext_ref[HALO - step:n_ext - step, :]
        ext_ref[HALO:n_ext, :] = cur
        step *= 2
        sums[step] = ext_ref[t0:n_ext, :]
    tpos = (lax.rem(i, tiles_per_seq) * tm + lax.broadcasted_iota(I32, (tm, POOL_DIM), 0) + 1).astype(F32)
    grp = lax.shift_right_logical(lax.broadcasted_iota(I32, (tm, POOL_DIM), 1), _log2(POOL_GROUP))
    mean = jnp.zeros((tm, POOL_DIM), F32)
    for gi, w in enumerate(POOL_WINDOWS):
        mean = jnp.where(grp == gi, sums[w] / jnp.minimum(tpos, float(w)), mean)
    mixed = mean - p_c
    y_pool = _dot(mixed.astype(BF16), wpool_ref[...]) * ps_ref[...]

    mix = _dot(y_conv.astype(BF16), wout_ref[0:CONV_DIM, :])
    mix = mix + _dot(y_pool.astype(BF16), wout_ref[CONV_DIM:CONV_DIM + POOL_DIM, :])
    mix = mix + _dot(y_ref[...].astype(BF16), wout_ref[CONV_DIM + POOL_DIM:D_MODEL, :])
    z = DN_ALPHA * x_ref[...] + mix
    o_ref[...] = _layernorm_rows(z, g_ref[...], b_ref[...])


def _mixout(a, y_attn, x, cw, wpool_bd, ps, wout, g, b, S):
    T = a.shape[0]
    nt = T // TM_MIX
    tps = S // TM_MIX
    hb = TM_MIX // HALO
    full = lambda i: (0, 0)
    return pl.pallas_call(
        functools.partial(_mixout_kernel, tps),
        grid=(nt,),
        in_specs=[
            pl.BlockSpec((TM_MIX, OFF_Q), lambda i: (i, 0)),
            pl.BlockSpec((HALO, OFF_Q), lambda i: (jnp.maximum(i * hb - 1, 0), 0)),
            pl.BlockSpec((TM_MIX, ATTN_DIM), lambda i: (i, 0)),
            pl.BlockSpec((TM_MIX, D_MODEL), lambda i: (i, 0)),
            pl.BlockSpec((CONV_WIDTH, CONV_DIM), full),
            pl.BlockSpec((POOL_DIM, POOL_DIM), full),
            pl.BlockSpec((1, POOL_DIM), full),
            pl.BlockSpec((D_MODEL, D_MODEL), full),
            pl.BlockSpec((1, D_MODEL), full),
            pl.BlockSpec((1, D_MODEL), full),
        ],
        out_specs=pl.BlockSpec((TM_MIX, D_MODEL), lambda i: (i, 0)),
        out_shape=jax.ShapeDtypeStruct((T, D_MODEL), F32),
        scratch_shapes=[pltpu.VMEM((2 * HALO + TM_MIX, CONV_DIM), F32)],
        compiler_params=_cparams(("arbitrary",)),
        name="mixout",
    )(a, a, y_attn, x, cw, wpool_bd, ps, wout, g, b)


def _router_kernel(x_ref, rw_ref, rb_ref, idx_ref, gate_ref, cnt_ref):
    i = pl.program_id(0)
    tm = TM_ROUTE
    x = x_ref[...]
    w = rw_ref[...]
    xh = x.astype(BF16)
    xl = (x - xh.astype(F32)).astype(BF16)
    wh = w.astype(BF16)
    wl = (w - wh.astype(F32)).astype(BF16)
    logits = (_dot_nt(wh, xh) + (_dot_nt(wh, xl) + _dot_nt(wl, xh))) + rb_ref[...]
    erow = lax.broadcasted_iota(I32, (N_EXPERTS, tm), 0).astype(F32)
    work = logits
    vals, idxs = [], []
    multi = jnp.zeros((N_EXPERTS, tm), F32)
    for _ in range(TOP_K):
        mx = jnp.max(work, axis=0, keepdims=True)
        pick = jnp.min(jnp.where(work == mx, erow, float(N_EXPERTS)), axis=0, keepdims=True)
        hit = erow == pick
        work = jnp.where(hit, -jnp.inf, work)
        multi = multi + hit.astype(F32)
        vals.append(mx)
        idxs.append(pick)
    es = [jnp.exp(v - vals[0]) for v in vals]
    den = es[0] + es[1] + es[2] + es[3]
    idx_ref[...] = jnp.concatenate(idxs, axis=0).astype(I32)
    gate_ref[...] = jnp.concatenate([e / den for e in es] + [jnp.zeros((SUBLANES - TOP_K, tm), F32)], axis=0)
    part = multi[:, 0:LANES]
    for j in range(1, tm // LANES):
        part = part + multi[:, j * LANES:(j + 1) * LANES]

    @pl.when(i == 0)
    def _():
        cnt_ref[...] = jnp.zeros_like(cnt_ref)

    cnt_ref[...] += part


def _router(x1, rwT, rb):
    T = x1.shape[0]
    nt = T // TM_ROUTE
    full = lambda i: (0, 0)
    return pl.pallas_call(
        _router_kernel,
        grid=(nt,),
        in_specs=[
            pl.BlockSpec((TM_ROUTE, D_MODEL), lambda i: (i, 0)),
            pl.BlockSpec((N_EXPERTS, D_MODEL), full),
            pl.BlockSpec((N_EXPERTS, 1), full),
        ],
        out_specs=[
            pl.BlockSpec((TOP_K, TM_ROUTE), lambda i: (0, i)),
            pl.BlockSpec((SUBLANES, TM_ROUTE), lambda i: (0, i)),
            pl.BlockSpec((N_EXPERTS, LANES), full),
        ],
        out_shape=[
            jax.ShapeDtypeStruct((TOP_K, T), I32),
            jax.ShapeDtypeStruct((SUBLANES, T), F32),
            jax.ShapeDtypeStruct((N_EXPERTS, LANES), F32),
        ],
        compiler_params=_cparams(("arbitrary",)),
        name="router",
    )(x1, rwT, rb)


def _ranks_kernel(n_tiles_pad, idx_ref, cnt_ref, dest_ref, meta_ref, tri_ref, start_ref, carry_ref):
    i = pl.program_id(0)
    tm = TM_ROUTE
    erow = lax.broadcasted_iota(I32, (N_EXPERTS, LANES), 0)
    elane = lax.broadcasted_iota(I32, (N_EXPERTS, LANES), 1)

    @pl.when(i == 0)
    def _():
        cnt = jnp.sum(cnt_ref[...], axis=1, keepdims=True)
        cnt_i = jnp.broadcast_to(cnt, (N_EXPERTS, LANES)).astype(I32)
        padded = lax.shift_left(lax.shift_right_logical(cnt_i + (TM_MOE - 1), _log2(TM_MOE)), _log2(TM_MOE))
        r = lax.broadcasted_iota(I32, (N_EXPERTS, N_EXPERTS), 0)
        c = lax.broadcasted_iota(I32, (N_EXPERTS, N_EXPERTS), 1)
        low = (c <= r).astype(F32)
        pad_end = lax.dot_general(low, padded.astype(F32), (((1,), (0,)), ((), ())),
                                  precision=lax.Precision.HIGHEST, preferred_element_type=F32)
        pad_start = pad_end - padded.astype(F32)
        start_ref[...] = pad_start
        carry_ref[...] = jnp.zeros_like(carry_ref)
        a = lax.broadcasted_iota(I32, (tm, tm), 0)
        bcol = lax.broadcasted_iota(I32, (tm, tm), 1)
        tri_ref[...] = (a < bcol).astype(BF16)
        ntp = n_tiles_pad
        tstart = (lax.broadcasted_iota(I32, (N_EXPERTS, ntp), 1) * TM_MOE).astype(F32)
        pe = jnp.concatenate([pad_end] * (ntp // LANES), axis=1)
        texp = jnp.sum((pe <= tstart).astype(F32), axis=0, keepdims=True)
        texp = jnp.minimum(texp, float(N_EXPERTS - 1)).astype(I32)
        total = jnp.max(pad_end, axis=0, keepdims=True)
        n_used = lax.shift_right_logical(total.astype(I32), _log2(TM_MOE))
        zstart = jnp.sum(jnp.where(erow == elane, pad_start + cnt_i.astype(F32), 0.0),
                         axis=0, keepdims=True).astype(I32)
        zlen = jnp.sum(jnp.where(erow == elane, (padded - cnt_i).astype(F32), 0.0),
                       axis=0, keepdims=True).astype(I32)
        etiles = lax.shift_right_logical(
            jnp.sum(jnp.where(erow == elane, padded.astype(F32), 0.0), axis=0, keepdims=True).astype(I32),
            _log2(TM_MOE))
        lanes_pad = jnp.zeros((1, ntp - LANES), I32)
        meta_ref[...] = jnp.concatenate(
            [texp,
             jnp.concatenate([n_used, lanes_pad], axis=1),
             jnp.concatenate([zstart, lanes_pad], axis=1),
             jnp.concatenate([zlen, lanes_pad], axis=1),
             jnp.concatenate([etiles, lanes_pad], axis=1),
             jnp.zeros((SUBLANES - 5, ntp), I32)], axis=0)

    idx = idx_ref[...]
    erow_t = lax.broadcasted_iota(I32, (N_EXPERTS, tm), 0)
    hits = [erow_t == idx[k:k + 1, :] for k in range(TOP_K)]
    multi = hits[0].astype(F32)
    for k in range(1, TOP_K):
        multi = multi + hits[k].astype(F32)
    prefix = _dot(multi.astype(BF16), tri_ref[...])
    base = jnp.concatenate([carry_ref[...] + start_ref[...]] * (tm // LANES), axis=1)
    tot = prefix + base
    dest_ref[...] = jnp.concatenate(
        [jnp.sum(jnp.where(hits[k], tot, 0.0), axis=0, keepdims=True) for k in range(TOP_K)],
        axis=0).astype(I32)
    carry_ref[...] += jnp.broadcast_to(jnp.sum(multi, axis=1, keepdims=True), (N_EXPERTS, LANES))


def _ranks(idxT, cnt, n_tiles_pad):
    T = idxT.shape[1]
    nt = T // TM_ROUTE
    full = lambda i: (0, 0)
    return pl.pallas_call(
        functools.partial(_ranks_kernel, n_tiles_pad),
        grid=(nt,),
        in_specs=[
            pl.BlockSpec((TOP_K, TM_ROUTE), lambda i: (0, i)),
            pl.BlockSpec((N_EXPERTS, LANES), full),
        ],
        out_specs=[
            pl.BlockSpec((TOP_K, TM_ROUTE), lambda i: (0, i)),
            pl.BlockSpec((SUBLANES, n_tiles_pad), full),
        ],
        out_shape=[
            jax.ShapeDtypeStruct((TOP_K, T), I32),
            jax.ShapeDtypeStruct((SUBLANES, n_tiles_pad), I32),
        ],
        scratch_shapes=[
            pltpu.VMEM((TM_ROUTE, TM_ROUTE), BF16),
            pltpu.VMEM((N_EXPERTS, LANES), F32),
            pltpu.VMEM((N_EXPERTS, LANES), F32),
        ],
        compiler_params=_cparams(("arbitrary",)),
        name="ranks",
    )(idxT, cnt)


RT = D_MODEL // LANES
DMA_UNROLL = 8
SCATTER_PARTS = 4
COMBINE_PARTS = 4


def _row_copy(src_ref, s, dst_ref, d, sem):
    return pltpu.make_async_copy(src_ref.at[pl.ds(pl.multiple_of(s * RT, RT), RT)],
                                 dst_ref.at[pl.ds(pl.multiple_of(d * RT, RT), RT)], sem)


def _to_row_tiled(dst_ref, val, rows):
    for s in range(RT):
        dst_ref[pl.ds(s, rows, stride=RT), :] = val[:, s * LANES:(s + 1) * LANES]


def _from_row_tiled(src_ref, rows):
    return [src_ref[pl.ds(s, rows, stride=RT), :] for s in range(RT)]


def _scatter_kernel(meta_ref, x_ref, dest_ref, xs_ref, stage_ref, zero_ref, sem):
    i = pl.program_id(0)
    tm = TM_ROUTE

    @pl.when(i == 0)
    def _():
        zero_ref[...] = jnp.zeros_like(zero_ref)

        def zcopy(row, nrows):
            off = pl.multiple_of(row * RT, RT)
            return pltpu.make_async_copy(zero_ref.at[pl.ds(0, nrows * RT)],
                                         xs_ref.at[pl.ds(off, nrows * RT)], sem)

        def pad_fill(wait):
            def body(e, c):
                row = meta_ref[2, e]
                plen = meta_ref[3, e]
                for bit in reversed(range(_log2(TM_MOE))):
                    size = 1 << bit
                    has = (plen & size) != 0

                    @pl.when(has)
                    def _():
                        cp = zcopy(row, size)
                        cp.wait() if wait else cp.start()

                    row = row + jnp.where(has, size, 0)
                return c
            lax.fori_loop(0, N_EXPERTS, body, 0)

        def tail_fill(wait):
            def body(j, c):
                cp = zcopy(j * TM_MOE, TM_MOE)
                cp.wait() if wait else cp.start()
                return c
            lax.fori_loop(meta_ref[1, 0], xs_ref.shape[0] // (TM_MOE * RT), body, 0)

        pad_fill(False)
        tail_fill(False)
        pad_fill(True)
        tail_fill(True)

    def start(t, c):
        for k in range(TOP_K):
            _row_copy(stage_ref, t, xs_ref, dest_ref[k, t], sem).start(priority=k % 2)
        return c

    part = tm // SCATTER_PARTS
    for p in range(SCATTER_PARTS):
        for s in range(RT):
            stage_ref[pl.ds(p * part * RT + s, part, stride=RT), :] = (
                x_ref[p * part:(p + 1) * part, s * LANES:(s + 1) * LANES])
        lax.fori_loop(p * part, (p + 1) * part, start, 0, unroll=DMA_UNROLL)
    for k in range(TOP_K):
        pltpu.make_async_copy(stage_ref, xs_ref.at[pl.ds(0, tm * RT)], sem).wait()


def _scatter(meta, x1, destT, n_rows_alloc):
    T = x1.shape[0]
    nt = T // TM_ROUTE
    return pl.pallas_call(
        _scatter_kernel,
        grid_spec=pltpu.PrefetchScalarGridSpec(
            num_scalar_prefetch=1,
            grid=(nt,),
            in_specs=[
                pl.BlockSpec((TM_ROUTE, D_MODEL), lambda i, m: (i, 0)),
                pl.BlockSpec((TOP_K, TM_ROUTE), lambda i, m: (0, i), memory_space=pltpu.SMEM),
            ],
            out_specs=pl.BlockSpec(memory_space=pl.ANY),
            scratch_shapes=[
                pltpu.VMEM((TM_ROUTE * RT, LANES), F32),
                pltpu.VMEM((TM_MOE * RT, LANES), F32),
                pltpu.SemaphoreType.DMA(()),
            ],
        ),
        out_shape=jax.ShapeDtypeStruct((n_rows_alloc * RT, LANES), F32),
        compiler_params=_cparams(("arbitrary",)),
        name="scatter",
    )(meta, x1, destT)


def _gmm_kernel(layer, meta_ref, xs_ref, wgu_hbm, bgu_ref, wd_hbm, bd_ref, ys_ref,
                wgu_f, wd_f, wgu_b, wd_b, lhs_ref, act_ref, grp_ref, sems):
    j = pl.program_id(0)
    n_used = meta_ref[1, 0]
    e_now = meta_ref[0, j]
    e_prev = meta_ref[0, jnp.maximum(j - 1, 0)]
    used = j < n_used

    def weight_copies(e, slot):
        return (pltpu.make_async_copy(wgu_hbm.at[layer, e], wgu_f.at[slot], sems.at[0, slot]),
                pltpu.make_async_copy(wd_hbm.at[layer, e], wd_f.at[slot], sems.at[1, slot]))

    @pl.when(used & (j == 0))
    def _():
        grp_ref[0] = 0
        for cp in weight_copies(e_now, 0):
            cp.start()

    @pl.when(used & ((j == 0) | (e_now != e_prev)))
    def _():
        slot = grp_ref[0]
        j_next = j + meta_ref[4, e_now]

        @pl.when(j_next < n_used)
        def _():
            for cp in weight_copies(meta_ref[0, jnp.minimum(j_next, meta_ref.shape[1] - 1)], 1 - slot):
                cp.start()

        for cp in weight_copies(e_now, slot):
            cp.wait()
        wgu_b[...] = wgu_f[slot].astype(BF16)
        wd_b[...] = wd_f[slot].astype(BF16)
        grp_ref[0] = 1 - slot

    @pl.when(used)
    def _():
        for s, piece in enumerate(_from_row_tiled(xs_ref, TM_MOE)):
            lhs_ref[:, s * LANES:(s + 1) * LANES] = piece.astype(BF16)
        x = lhs_ref[...]
        nc = 512
        for c in range(D_FF // nc):
            gate = _dot(x, wgu_b[:, c * nc:(c + 1) * nc]) + bgu_ref[0, 0, :, c * nc:(c + 1) * nc]
            up = (_dot(x, wgu_b[:, D_FF + c * nc:D_FF + (c + 1) * nc])
                  + bgu_ref[0, 0, :, D_FF + c * nc:D_FF + (c + 1) * nc])
            gate = jnp.minimum(gate, SWIGLU_LIMIT)
            up = jnp.clip(up, -SWIGLU_LIMIT, SWIGLU_LIMIT)
            act = gate * jax.nn.sigmoid(SWIGLU_ALPHA * gate) * (up + 1.0)
            act_ref[:, c * nc:(c + 1) * nc] = act.astype(BF16)
        _to_row_tiled(ys_ref, _dot(act_ref[...], wd_b[...]) + bd_ref[0, 0], TM_MOE)

    @pl.when(jnp.logical_not(used))
    def _():
        ys_ref[...] = jnp.zeros_like(ys_ref)


def _gmm(meta, xs, w_gu, b_gu, w_down, b_down, n_tiles, layer):
    last = lambda m: jnp.maximum(m[1, 0] - 1, 0)
    return pl.pallas_call(
        functools.partial(_gmm_kernel, layer),
        grid_spec=pltpu.PrefetchScalarGridSpec(
            num_scalar_prefetch=1,
            grid=(n_tiles,),
            in_specs=[
                pl.BlockSpec((TM_MOE * RT, LANES), lambda j, m: (jnp.minimum(j, last(m)), 0)),
                pl.BlockSpec(memory_space=pl.ANY),
                pl.BlockSpec((1, 1, 1, 2 * D_FF), lambda j, m: (layer, m[0, j], 0, 0)),
                pl.BlockSpec(memory_space=pl.ANY),
                pl.BlockSpec((1, 1, 1, D_MODEL), lambda j, m: (layer, m[0, j], 0, 0)),
            ],
            out_specs=pl.BlockSpec((TM_MOE * RT, LANES), lambda j, m: (j, 0)),
            scratch_shapes=[
                pltpu.VMEM((2, D_MODEL, 2 * D_FF), F32),
                pltpu.VMEM((2, D_FF, D_MODEL), F32),
                pltpu.VMEM((D_MODEL, 2 * D_FF), BF16),
                pltpu.VMEM((D_FF, D_MODEL), BF16),
                pltpu.VMEM((TM_MOE, D_MODEL), BF16),
                pltpu.VMEM((TM_MOE, D_FF), BF16),
                pltpu.SMEM((1,), I32),
                pltpu.SemaphoreType.DMA((2, 2)),
            ],
        ),
        out_shape=jax.ShapeDtypeStruct((n_tiles * TM_MOE * RT, LANES), F32),
        compiler_params=_cparams(("arbitrary",)),
        name="gmm",
    )(meta, xs, w_gu, b_gu, w_down, b_down)


def _combine_kernel(x_ref, dest_ref, gate_ref, ys_ref, g_ref, b_ref, o_ref, ob_ref, buf_ref, sem):
    tm = TM_COMB
    part = tm // COMBINE_PARTS

    for p in range(COMBINE_PARTS):
        def start(t, c, p=p):
            for k in range(TOP_K):
                _row_copy(ys_ref, dest_ref[k, t], buf_ref.at[k], t, sem.at[p]).start(priority=k % 2)
            return c

        lax.fori_loop(p * part, (p + 1) * part, start, 0, unroll=DMA_UNROLL)
    gates = jnp.concatenate(
        [gate_ref[...], jnp.zeros((LANES - SUBLANES, tm), F32)], axis=0).T
    for p in range(COMBINE_PARTS):
        rows = slice(p * part, (p + 1) * part)
        for k in range(TOP_K):
            pltpu.make_async_copy(ys_ref.at[pl.ds(0, part * RT)],
                                  buf_ref.at[k, pl.ds(p * part * RT, part * RT)], sem.at[p]).wait()
        pieces = None
        for k in range(TOP_K):
            gk = gates[rows, k:k + 1]
            scaled = [buf_ref[k, pl.ds(p * part * RT + s, part, stride=RT), :] * gk for s in range(RT)]
            pieces = scaled if pieces is None else [a + r for a, r in zip(pieces, scaled)]
        z = DN_ALPHA * x_ref[rows, :] + jnp.concatenate(pieces, axis=1)
        out = _layernorm_rows(z, g_ref[...], b_ref[...])
        o_ref[rows, :] = out
        ob_ref[rows, :] = out.astype(BF16)


def _combine(x1, destT, gateT, ys, g, b):
    T = x1.shape[0]
    nt = T // TM_COMB
    full = lambda i: (0, 0)
    return pl.pallas_call(
        _combine_kernel,
        grid=(nt,),
        in_specs=[
            pl.BlockSpec((TM_COMB, D_MODEL), lambda i: (i, 0)),
            pl.BlockSpec((TOP_K, TM_COMB), lambda i: (0, i), memory_space=pltpu.SMEM),
            pl.BlockSpec((SUBLANES, TM_COMB), lambda i: (0, i)),
            pl.BlockSpec(memory_space=pl.ANY),
            pl.BlockSpec((1, D_MODEL), full),
            pl.BlockSpec((1, D_MODEL), full),
        ],
        out_specs=[
            pl.BlockSpec((TM_COMB, D_MODEL), lambda i: (i, 0)),
            pl.BlockSpec((TM_COMB, D_MODEL), lambda i: (i, 0)),
        ],
        out_shape=[
            jax.ShapeDtypeStruct((T, D_MODEL), F32),
            jax.ShapeDtypeStruct((T, D_MODEL), BF16),
        ],
        scratch_shapes=[
            pltpu.VMEM((TOP_K, TM_COMB * RT, LANES), F32),
            pltpu.SemaphoreType.DMA((COMBINE_PARTS,)),
        ],
        compiler_params=_cparams(("arbitrary",)),
        name="combine",
    )(x1, destT, gateT, ys, g, b)


def _block_diag(w):
    g, c, _ = w.shape
    out = jnp.zeros((g * c, g * c), w.dtype)
    for i in range(g):
        out = out.at[i * c:(i + 1) * c, i * c:(i + 1) * c].set(w[i])
    return out


def kernel(x, w_in, conv_w, w_pool, pool_scale, idx_kn_g, idx_kn_b, w_out, ln1_g, ln1_b, router_w,
           router_b, w_gu, b_gu, w_down, b_down, ln2_g, ln2_b):
    B, S, D = x.shape
    T = B * S
    depth = w_in.shape[0]
    n_pairs = T * TOP_K
    n_tiles = (n_pairs + N_EXPERTS * (TM_MOE - 1)) // TM_MOE + 1
    n_tiles_pad = ((n_tiles + LANES - 1) // LANES) * LANES
    n_rows_alloc = (n_tiles + 1) * TM_MOE

    xf = x.reshape(T, D)
    xb = xf
    for l in range(depth):
        wl = w_in[l]
        wn = jnp.concatenate(
            [wl[:, 0:OFF_V], wl[:, OFF_KI:OFF_WI], jnp.zeros((D, LANES - IDX_DIM), F32)], axis=1).astype(BF16)
        wt = jnp.concatenate(
            [wl[:, OFF_V:OFF_QI], wl[:, OFF_QI:OFF_KI], wl[:, OFF_WI:D_IN],
             jnp.zeros((D, IDXT_ROWS - IDX_HEADS * IDX_DIM - IDX_HEADS), F32)], axis=1).T.astype(BF16)
        a, q, k, ki, vt, it = _inproj(xb, wn, wt, idx_kn_g[l].reshape(1, -1), idx_kn_b[l].reshape(1, -1))
        y_attn = _attn(it, ki, q, k, vt, B, S)
        x1 = _mixout(a, y_attn, xf, conv_w[l], _block_diag(w_pool[l]).astype(BF16),
                     pool_scale[l].reshape(1, -1), w_out[l].astype(BF16),
                     ln1_g[l].reshape(1, -1), ln1_b[l].reshape(1, -1), S)
        idxT, gateT, cnt = _router(x1, router_w[l].T, router_b[l].reshape(-1, 1))
        destT, meta = _ranks(idxT, cnt, n_tiles_pad)
        xs = _scatter(meta, x1, destT, n_rows_alloc)
        ys = _gmm(meta, xs, w_gu, b_gu.reshape(depth, N_EXPERTS, 1, -1), w_down,
                  b_down.reshape(depth, N_EXPERTS, 1, -1), n_tiles, l)
        xf, xb = _combine(x1, destT, gateT, ys, ln2_g[l].reshape(1, -1), ln2_b[l].reshape(1, -1))
    return xf.reshape(B, S, D)
```

```python
import functools

import jax
import jax.numpy as jnp
from jax import lax
from jax.experimental import pallas as pl
from jax.experimental.pallas import tpu as pltpu

F32 = jnp.float32
BF16 = jnp.bfloat16
I32 = jnp.int32

D_MODEL = 1024
CONV_DIM = 256
CONV_WIDTH = 3
POOL_WINDOWS = (2, 4, 8, 16)
POOL_GROUP = 64
POOL_DIM = 256
ATTN_HEADS = 8
HEAD_DIM = 64
ATTN_DIM = 512
IDX_HEADS = 8
IDX_DIM = 32
INDEX_TOPK_MAX = 256
Q_BLOCK = 128
N_EXPERTS = 32
TOP_K = 4
D_FF = 1024
SWIGLU_ALPHA = 1.702
SWIGLU_LIMIT = 7.0
LN_EPS = 1e-5
DEPTH = 2
DN_ALPHA = (2.0 * DEPTH) ** 0.25

OFF_P = 3 * CONV_DIM
OFF_Q = OFF_P + POOL_DIM
OFF_K = OFF_Q + ATTN_DIM
OFF_V = OFF_K + ATTN_DIM
OFF_QI = OFF_V + ATTN_DIM
OFF_KI = OFF_QI + IDX_HEADS * IDX_DIM
OFF_WI = OFF_KI + IDX_DIM
D_IN = OFF_WI + IDX_HEADS

LANES = 128
SUBLANES = 8
VMEM_LIMIT = 56 * 1024 * 1024
KEY_CHUNK = 512
TIE_GROUP = 128
HALO = 16
TM_PROJ = 512
TM_MIX = 1024
TM_ROUTE = 1024
TM_MOE = 512
TM_COMB = 1024
IDXT_ROWS = IDX_HEADS * IDX_DIM + 16
WT_ROWS = ATTN_DIM + IDXT_ROWS
INT_MIN = -(2 ** 31)
NEG_BIG = -1e30


def _cparams(sem):
    return pltpu.CompilerParams(dimension_semantics=sem, vmem_limit_bytes=VMEM_LIMIT)


def _log2(n):
    k = n.bit_length() - 1
    assert 1 << k == n
    return k


def _dot(a, b):
    return jnp.dot(a, b, preferred_element_type=F32)


def _dot_nt(a, b):
    return lax.dot_general(a, b, (((1,), (1,)), ((), ())), preferred_element_type=F32)


def _tree(x, op):
    parts = [x[j * SUBLANES:(j + 1) * SUBLANES, :] for j in range(x.shape[0] // SUBLANES)]
    while len(parts) > 1:
        nxt = [op(parts[j], parts[j + 1]) for j in range(0, len(parts) - 1, 2)]
        if len(parts) % 2:
            nxt.append(parts[-1])
        parts = nxt
    return parts[0]


def _layernorm_rows(z, g, b):
    mu = jnp.mean(z, axis=-1, keepdims=True)
    d = z - mu
    var = jnp.mean(d * d, axis=-1, keepdims=True)
    return d * lax.rsqrt(var + LN_EPS) * g + b


def _inproj_kernel(x_ref, wn_ref, wt_ref, kg_ref, kb_ref,
                   a_ref, q_ref, k_ref, ki_ref, vt_ref, it_ref):
    x = x_ref[...].astype(BF16)
    a_ref[...] = _dot(x, wn_ref[:, 0:OFF_Q])
    q_ref[...] = (_dot(x, wn_ref[:, OFF_Q:OFF_K]) * (HEAD_DIM ** -0.5)).astype(BF16)
    k_ref[...] = _dot(x, wn_ref[:, OFF_K:OFF_V]).astype(BF16)
    ki = _dot(x, wn_ref[:, OFF_V:OFF_V + LANES])[:, 0:IDX_DIM]
    ki_ref[...] = _layernorm_rows(ki, kg_ref[...], kb_ref[...])
    t = _dot_nt(wt_ref[...], x)
    for c in range(TM_PROJ // KEY_CHUNK):
        vt_ref[c] = t[0:ATTN_DIM, c * KEY_CHUNK:(c + 1) * KEY_CHUNK].astype(BF16)
    it_ref[...] = t[ATTN_DIM:WT_ROWS, :]


def _inproj(xb, wn, wt, kg, kb):
    T = xb.shape[0]
    nt = T // TM_PROJ
    cpt = TM_PROJ // KEY_CHUNK
    full = lambda i: (0, 0)
    return pl.pallas_call(
        _inproj_kernel,
        grid=(nt,),
        in_specs=[
            pl.BlockSpec((TM_PROJ, D_MODEL), lambda i: (i, 0)),
            pl.BlockSpec(wn.shape, full),
            pl.BlockSpec(wt.shape, full),
            pl.BlockSpec((1, IDX_DIM), full),
            pl.BlockSpec((1, IDX_DIM), full),
        ],
        out_specs=[
            pl.BlockSpec((TM_PROJ, OFF_Q), lambda i: (i, 0)),
            pl.BlockSpec((TM_PROJ, ATTN_DIM), lambda i: (i, 0)),
            pl.BlockSpec((TM_PROJ, ATTN_DIM), lambda i: (i, 0)),
            pl.BlockSpec((TM_PROJ, IDX_DIM), lambda i: (i, 0)),
            pl.BlockSpec((cpt, ATTN_DIM, KEY_CHUNK), lambda i: (i, 0, 0)),
            pl.BlockSpec((IDXT_ROWS, TM_PROJ), lambda i: (0, i)),
        ],
        out_shape=[
            jax.ShapeDtypeStruct((T, OFF_Q), F32),
            jax.ShapeDtypeStruct((T, ATTN_DIM), BF16),
            jax.ShapeDtypeStruct((T, ATTN_DIM), BF16),
            jax.ShapeDtypeStruct((T, IDX_DIM), F32),
            jax.ShapeDtypeStruct((T // KEY_CHUNK, ATTN_DIM, KEY_CHUNK), BF16),
            jax.ShapeDtypeStruct((IDXT_ROWS, T), F32),
        ],
        compiler_params=_cparams(("arbitrary",)),
        name="inproj",
    )(xb, wn, wt, kg, kb)


FIELD_BITS = 8
DIGIT_BITS = FIELD_BITS - 1
FIELDS = 32 // FIELD_BITS
DIGIT_SHIFTS = (25, 18, 11, 4, 0)
EARLY_EXIT_FROM = 4
FIELD_ONES = 0x01010101
FIELD_GUARDS = 0x80808080 - (1 << 32)


def _pack_fields(f):
    q = f.shape[0] // FIELDS
    w = f[0:q]
    for j in range(1, FIELDS):
        w = w | lax.shift_left(f[j * q:(j + 1) * q], FIELD_BITS * j)
    return w | jnp.int32(FIELD_GUARDS)


def _count_fields_ge(words_ref, nch, cand):
    wq = 2 * (KEY_CHUNK // FIELDS)
    assert words_ref.shape[0] // SUBLANES < (1 << FIELD_BITS)
    cvec = cand * jnp.int32(FIELD_ONES)

    def body(c, acc):
        base = pl.multiple_of(c * wq, wq)
        w = words_ref[pl.ds(base, wq), :]
        hit = lax.shift_right_logical(w - cvec, DIGIT_BITS) & jnp.int32(FIELD_ONES)
        return acc + _tree(hit, jnp.add)

    acc = lax.fori_loop(0, lax.shift_right_logical(nch + 1, 1), body,
                        jnp.zeros((SUBLANES, words_ref.shape[1]), I32))
    tot = acc & 255
    for j in range(1, FIELDS):
        tot = tot + (lax.shift_right_logical(acc, FIELD_BITS * j) & 255)
    return jnp.sum(tot.astype(F32), axis=0, keepdims=True)


def _topk_threshold(keys_ref, words_ref, nch, n_sel, n_causal):
    lanes = keys_ref.shape[1]
    wq = KEY_CHUNK // FIELDS
    all_taken = n_causal <= n_sel

    def run_digit(idx, state):
        prefix, want, bucket = state
        shift = DIGIT_SHIFTS[idx]
        width = (DIGIT_SHIFTS[idx - 1] if idx else 32) - shift
        if idx:
            def prep(c, carry):
                key = keys_ref[pl.ds(pl.multiple_of(c * KEY_CHUNK, KEY_CHUNK), KEY_CHUNK), :]
                digit = (key >> shift) & jnp.int32((1 << width) - 1)
                fields = jnp.where((key >> (shift + width)) == prefix, digit, 0)
                words_ref[pl.ds(pl.multiple_of(c * wq, wq), wq), :] = _pack_fields(fields)
                return carry

            lax.fori_loop(0, nch, prep, 0)
        dig = jnp.zeros((1, lanes), I32)
        at_dig = bucket
        above = jnp.zeros((1, lanes), F32)
        for bit in reversed(range(width)):
            cand = dig + (1 << bit)
            cnt = _count_fields_ge(words_ref, nch, cand)
            ok = cnt >= want
            dig = jnp.where(ok, cand, dig)
            at_dig = jnp.where(ok, cnt, at_dig)
            above = jnp.where(ok, above, cnt)
        prefix = (lax.shift_left(prefix, width) | dig) if idx else dig - (1 << (DIGIT_BITS - 1))
        return prefix, want - above, at_dig - above

    def finish(state, low_bits):
        prefix, want, bucket = state
        tau = jnp.where(all_taken, jnp.int32(INT_MIN + 1), lax.shift_left(prefix, low_bits))
        return tau, want, jnp.where(all_taken, 0.0, bucket - want)

    def settled(state):
        _, want, bucket = state
        return jnp.min(jnp.where((bucket == want) | all_taken, 1.0, 0.0)) > 0.0

    def from_digit(idx, state):
        if idx == len(DIGIT_SHIFTS):
            return finish(state, 0)
        if idx < EARLY_EXIT_FROM:
            return from_digit(idx + 1, run_digit(idx, state))
        return lax.cond(settled(state),
                        lambda s: finish(s, DIGIT_SHIFTS[idx - 1]),
                        lambda s: from_digit(idx + 1, run_digit(idx, s)), state)

    state = (jnp.zeros((1, lanes), I32), jnp.full((1, lanes), float(n_sel), F32), n_causal.astype(F32))
    return from_digit(0, state)


def _attn_kernel(n_sel, it_ref, ki_ref, q_ref, k_ref, vt_ref, o_ref,
                 keys_ref, words_ref, rhs_ref, acc_ref, lg_ref):
    i = pl.program_id(1)
    nch = lax.shift_right_logical(i * Q_BLOCK + Q_BLOCK + KEY_CHUNK - 1, _log2(KEY_CHUNK))
    q0 = i * Q_BLOCK
    KC = KEY_CHUNK
    row = lax.broadcasted_iota(I32, (KC, Q_BLOCK), 0)
    lane = lax.broadcasted_iota(I32, (KC, Q_BLOCK), 1)
    qpos = q0 + lane

    qcat = jnp.concatenate(
        [it_ref[h * IDX_DIM:(h + 1) * IDX_DIM, :] for h in range(IDX_HEADS)], axis=1).astype(BF16)
    w_all = it_ref[IDX_HEADS * IDX_DIM:IDX_HEADS * IDX_DIM + IDX_HEADS, :] * (
        (IDX_HEADS ** -0.5) * (IDX_DIM ** -0.5))

    def score_chunk(c, has_future_keys):
        base = pl.multiple_of(c * KC, KC)
        kic = ki_ref[pl.ds(base, KC), :].astype(BF16)
        s = _dot(kic, qcat)
        sc = jnp.zeros((KC, Q_BLOCK), F32)
        for h in range(IDX_HEADS):
            sc = sc + jnp.maximum(s[:, h * Q_BLOCK:(h + 1) * Q_BLOCK], 0.0) * w_all[h:h + 1, :]
        bits = lax.bitcast_convert_type(sc, I32)
        key = bits ^ ((bits >> 31) & jnp.int32(0x7FFFFFFF))
        key = jnp.where(bits == jnp.int32(INT_MIN), jnp.int32(0), key)
        top = (key >> DIGIT_SHIFTS[0]) + (1 << (DIGIT_BITS - 1))
        if has_future_keys:
            causal = base + row <= qpos
            key = jnp.where(causal, key, jnp.int32(INT_MIN))
            top = jnp.where(causal, top, 0)
        keys_ref[pl.ds(base, KC), :] = key
        words_ref[pl.ds(pl.multiple_of(c * (KC // FIELDS), KC // FIELDS), KC // FIELDS), :] = _pack_fields(top)

    def score_body(c, carry):
        score_chunk(c, False)
        return carry

    lax.fori_loop(0, nch - 1, score_body, 0)
    score_chunk(nch - 1, True)

    @pl.when(lax.rem(nch, 2) == 1)
    def _():
        wq = KC // FIELDS
        words_ref[pl.ds(pl.multiple_of(nch * wq, wq), wq), :] = _pack_fields(jnp.zeros((KC, Q_BLOCK), I32))

    n_causal = q0 + lax.broadcasted_iota(I32, (1, Q_BLOCK), 1) + 1
    tau, keep, excess = _topk_threshold(keys_ref, words_ref, nch, n_sel, n_causal)

    @pl.when(jnp.max(excess) > 0.0)
    def _():
        g = TIE_GROUP
        r = lax.broadcasted_iota(I32, (g, g), 0)
        cc = lax.broadcasted_iota(I32, (g, g), 1)
        tri = (cc <= r).astype(BF16)

        def retire(c, before):
            base = pl.multiple_of(c * KC, KC)
            blk = keys_ref[pl.ds(base, KC), :]
            tied = blk == tau
            ones = jnp.where(tied, 1.0, 0.0).astype(BF16)
            local = [_dot(tri, ones[j * g:(j + 1) * g, :]) for j in range(KC // g)]
            upto = []
            for lj in local:
                upto.append(lj + before)
                before = before + lj[g - 1:g, :]
            upto = jnp.concatenate(upto, axis=0)
            keys_ref[pl.ds(base, KC), :] = jnp.where(tied & (upto > keep), jnp.int32(INT_MIN), blk)
            return before

        lax.fori_loop(0, nch, retire, jnp.zeros((1, Q_BLOCK), F32))

    q = q_ref[...]
    lane_q = lax.broadcasted_iota(I32, (Q_BLOCK, 2 * HEAD_DIM), 1)
    for p in range(ATTN_HEADS // 2):
        qp = q[:, p * 2 * HEAD_DIM:(p + 1) * 2 * HEAD_DIM]
        rhs_ref[p, 0:Q_BLOCK, :] = jnp.where(lane_q < HEAD_DIM, qp, jnp.zeros_like(qp))
        rhs_ref[p, Q_BLOCK:2 * Q_BLOCK, :] = jnp.where(lane_q >= HEAD_DIM, qp, jnp.zeros_like(qp))
    acc_ref[...] = jnp.zeros_like(acc_ref)

    def stage_a_prep(c):
        base = pl.multiple_of(c * KC, KC)
        tk = (base + row - (q0 + Q_BLOCK - 1)).astype(F32)
        tkm = jnp.where(keys_ref[pl.ds(base, KC), :] >= tau, tk, NEG_BIG)
        return tkm, k_ref[pl.ds(base, KC), :]

    def stage_a_pair(c, p, tkm, kc):
        l2 = _dot_nt(kc[:, p * 2 * HEAD_DIM:(p + 1) * 2 * HEAD_DIM], rhs_ref[p])
        mcs = []
        for hh in range(2):
            h = 2 * p + hh
            slope = 2.0 ** (-8.0 * (h + 1) / ATTN_HEADS)
            lg = l2[:, hh * Q_BLOCK:(hh + 1) * Q_BLOCK] + slope * tkm
            lg_ref[h] = lg
            mcs.append(jnp.max(_tree(lg, jnp.maximum), axis=0, keepdims=True))
        return mcs

    def stage_b_pair(c, p, m_new, alpha):
        sums = []
        for hh in range(2):
            h = 2 * p + hh
            pr = jnp.exp(lg_ref[h] - m_new[h:h + 1, :])
            sums.append(jnp.sum(_tree(pr, jnp.add), axis=0, keepdims=True))
            pv = _dot(vt_ref[c, h * HEAD_DIM:(h + 1) * HEAD_DIM, :], pr.astype(BF16))
            acc_ref[h * HEAD_DIM:(h + 1) * HEAD_DIM, :] = (
                acc_ref[h * HEAD_DIM:(h + 1) * HEAD_DIM, :] * alpha[h:h + 1, :] + pv)
        return sums

    def stage_b_all(c, m_old, l_old, mc):
        m_new = jnp.maximum(m_old, mc)
        alpha = jnp.exp(m_old - m_new)
        return m_new, alpha

    tkm0, kc0 = stage_a_prep(0)
    mc0 = jnp.concatenate(sum([stage_a_pair(0, p, tkm0, kc0) for p in range(ATTN_HEADS // 2)], []), axis=0)

    def attn_body(c, carry):
        m_old, l_old, mc_prev = carry
        m_new, alpha = stage_b_all(c - 1, m_old, l_old, mc_prev)
        tkm, kc = stage_a_prep(c)
        mcs, sums = [], []
        for p in range(ATTN_HEADS // 2):
            sums += stage_b_pair(c - 1, p, m_new, alpha)
            mcs += stage_a_pair(c, p, tkm, kc)
        return m_new, l_old * alpha + jnp.concatenate(sums, axis=0), jnp.concatenate(mcs, axis=0)

    m0 = jnp.full((ATTN_HEADS, Q_BLOCK), NEG_BIG, F32)
    l0 = jnp.zeros((ATTN_HEADS, Q_BLOCK), F32)
    m_old, l_old, mc_prev = lax.fori_loop(1, nch, attn_body, (m0, l0, mc0))
    m_new, alpha = stage_b_all(nch - 1, m_old, l_old, mc_prev)
    sums = sum([stage_b_pair(nch - 1, p, m_new, alpha) for p in range(ATTN_HEADS // 2)], [])
    l_fin = l_old * alpha + jnp.concatenate(sums, axis=0)

    outs = []
    for h in range(ATTN_HEADS):
        outs.append(acc_ref[h * HEAD_DIM:(h + 1) * HEAD_DIM, :] / l_fin[h:h + 1, :])
    o_ref[...] = jnp.concatenate(outs, axis=0).T


def _attn(it, ki, q, k, vt, B, S):
    T = B * S
    nb = S // Q_BLOCK
    n_sel = min(INDEX_TOPK_MAX, S // 4)
    return pl.pallas_call(
        functools.partial(_attn_kernel, n_sel),
        grid=(B, nb),
        in_specs=[
            pl.BlockSpec((IDXT_ROWS, Q_BLOCK), lambda b, i: (0, b * nb + i)),
            pl.BlockSpec((S, IDX_DIM), lambda b, i: (b, 0)),
            pl.BlockSpec((Q_BLOCK, ATTN_DIM), lambda b, i: (b * nb + i, 0)),
            pl.BlockSpec((S, ATTN_DIM), lambda b, i: (b, 0)),
            pl.BlockSpec((S // KEY_CHUNK, ATTN_DIM, KEY_CHUNK), lambda b, i: (b, 0, 0)),
        ],
        out_specs=pl.BlockSpec((Q_BLOCK, ATTN_DIM), lambda b, i: (b * nb + i, 0)),
        out_shape=jax.ShapeDtypeStruct((T, ATTN_DIM), F32),
        scratch_shapes=[
            pltpu.VMEM((S, Q_BLOCK), I32),
            pltpu.VMEM((-(-(S // KEY_CHUNK) // 2) * 2 * (KEY_CHUNK // FIELDS), Q_BLOCK), I32),
            pltpu.VMEM((ATTN_HEADS // 2, 2 * Q_BLOCK, 2 * HEAD_DIM), BF16),
            pltpu.VMEM((ATTN_DIM, Q_BLOCK), F32),
            pltpu.VMEM((ATTN_HEADS, KEY_CHUNK, Q_BLOCK), F32),
        ],
        compiler_params=_cparams(("arbitrary", "arbitrary")),
        name="attn",
    )(it, ki, q, k, vt)


def _mixout_kernel(tiles_per_seq, a_ref, halo_ref, y_ref, x_ref, cw_ref, wpool_ref, ps_ref,
                   wout_ref, g_ref, b_ref, rw_ref, rb_ref, o_ref, idx_ref, gate_ref, cnt_ref, ext_ref):
    i = pl.program_id(0)
    tm = TM_MIX
    first = lax.rem(i, tiles_per_seq) == 0
    halo = jnp.where(first, 0.0, halo_ref[...])
    a = a_ref[...]
    h_c, gb_c, gc_c, p_c = (a[:, j * CONV_DIM:(j + 1) * CONV_DIM] for j in range(4))

    t0 = 2 * HALO
    n_ext = t0 + tm
    ext_ref[0:HALO, :] = jnp.zeros((HALO, CONV_DIM), F32)

    ext_ref[HALO:t0, :] = halo[:, 2 * CONV_DIM:3 * CONV_DIM] * halo[:, 0:CONV_DIM]
    u = gc_c * h_c
    ext_ref[t0:n_ext, :] = u
    cw = cw_ref[...]
    conv = cw[2:3, :] * u
    conv = conv + cw[1:2, :] * ext_ref[t0 - 1:n_ext - 1, :]
    conv = conv + cw[0:1, :] * ext_ref[t0 - 2:n_ext - 2, :]
    y_conv = gb_c * conv

    ext_ref[HALO:t0, :] = halo[:, OFF_P:OFF_P + POOL_DIM]
    ext_ref[t0:n_ext, :] = p_c
    sums = {}
    step = 1
    while step < POOL_WINDOWS[-1]:
        cur = ext_ref[HALO:n_ext, :] + ext_ref[HALO - step:n_ext - step, :]
        ext_ref[HALO:n_ext, :] = cur
        step *= 2
        sums[step] = ext_ref[t0:n_ext, :]
    tpos = (lax.rem(i, tiles_per_seq) * tm + lax.broadcasted_iota(I32, (tm, POOL_DIM), 0) + 1).astype(F32)
    grp = lax.shift_right_logical(lax.broadcasted_iota(I32, (tm, POOL_DIM), 1), _log2(POOL_GROUP))
    mean = jnp.zeros((tm, POOL_DIM), F32)
    for gi, w in enumerate(POOL_WINDOWS):
        mean = jnp.where(grp == gi, sums[w] / jnp.minimum(tpos, float(w)), mean)
    mixed = mean - p_c
    y_pool = _dot(mixed.astype(BF16), wpool_ref[...]) * ps_ref[...]

    mix = _dot(y_conv.astype(BF16), wout_ref[0:CONV_DIM, :])
    mix = mix + _dot(y_pool.astype(BF16), wout_ref[CONV_DIM:CONV_DIM + POOL_DIM, :])
    mix = mix + _dot(y_ref[...].astype(BF16), wout_ref[CONV_DIM + POOL_DIM:D_MODEL, :])
    z = DN_ALPHA * x_ref[...] + mix
    out = _layernorm_rows(z, g_ref[...], b_ref[...])
    o_ref[...] = out
    _route(out, rw_ref, rb_ref, idx_ref, gate_ref, cnt_ref)


def _mixout(a, y_attn, x, cw, wpool_bd, ps, wout, g, b, rwT, rb, S):
    T = a.shape[0]
    nt = T // TM_MIX
    tps = S // TM_MIX
    hb = TM_MIX // HALO
    full = lambda i: (0, 0)
    return pl.pallas_call(
        functools.partial(_mixout_kernel, tps),
        grid=(nt,),
        in_specs=[
            pl.BlockSpec((TM_MIX, OFF_Q), lambda i: (i, 0)),
            pl.BlockSpec((HALO, OFF_Q), lambda i: (jnp.maximum(i * hb - 1, 0), 0)),
            pl.BlockSpec((TM_MIX, ATTN_DIM), lambda i: (i, 0)),
            pl.BlockSpec((TM_MIX, D_MODEL), lambda i: (i, 0)),
            pl.BlockSpec((CONV_WIDTH, CONV_DIM), full),
            pl.BlockSpec((POOL_DIM, POOL_DIM), full),
            pl.BlockSpec((1, POOL_DIM), full),
            pl.BlockSpec((D_MODEL, D_MODEL), full),
            pl.BlockSpec((1, D_MODEL), full),
            pl.BlockSpec((1, D_MODEL), full),
            pl.BlockSpec((N_EXPERTS, D_MODEL), full),
            pl.BlockSpec((N_EXPERTS, 1), full),
        ],
        out_specs=[
            pl.BlockSpec((TM_MIX, D_MODEL), lambda i: (i, 0)),
            pl.BlockSpec((TOP_K, TM_MIX), lambda i: (0, i)),
            pl.BlockSpec((SUBLANES, TM_MIX), lambda i: (0, i)),
            pl.BlockSpec((N_EXPERTS, LANES), full),
        ],
        out_shape=[
            jax.ShapeDtypeStruct((T, D_MODEL), F32),
            jax.ShapeDtypeStruct((TOP_K, T), I32),
            jax.ShapeDtypeStruct((SUBLANES, T), F32),
            jax.ShapeDtypeStruct((N_EXPERTS, LANES), F32),
        ],
        scratch_shapes=[pltpu.VMEM((2 * HALO + TM_MIX, CONV_DIM), F32)],
        compiler_params=_cparams(("arbitrary",)),
        name="mixout",
    )(a, a, y_attn, x, cw, wpool_bd, ps, wout, g, b, rwT, rb)


def _route(x, rw_ref, rb_ref, idx_ref, gate_ref, cnt_ref):
    i = pl.program_id(0)
    tm = x.shape[0]
    w = rw_ref[...]
    xh = x.astype(BF16)
    xl = (x - xh.astype(F32)).astype(BF16)
    wh = w.astype(BF16)
    wl = (w - wh.astype(F32)).astype(BF16)
    logits = (_dot_nt(wh, xh) + (_dot_nt(wh, xl) + _dot_nt(wl, xh))) + rb_ref[...]
    erow = lax.broadcasted_iota(I32, (N_EXPERTS, tm), 0).astype(F32)
    work = logits
    vals, idxs = [], []
    multi = jnp.zeros((N_EXPERTS, tm), F32)
    for _ in range(TOP_K):
        mx = jnp.max(work, axis=0, keepdims=True)
        pick = jnp.min(jnp.where(work == mx, erow, float(N_EXPERTS)), axis=0, keepdims=True)
        hit = erow == pick
        work = jnp.where(hit, -jnp.inf, work)
        multi = multi + hit.astype(F32)
        vals.append(mx)
        idxs.append(pick)
    es = [jnp.exp(v - vals[0]) for v in vals]
    den = es[0] + es[1] + es[2] + es[3]
    idx_ref[...] = jnp.concatenate(idxs, axis=0).astype(I32)
    gate_ref[...] = jnp.concatenate([e / den for e in es] + [jnp.zeros((SUBLANES - TOP_K, tm), F32)], axis=0)
    part = multi[:, 0:LANES]
    for j in range(1, tm // LANES):
        part = part + multi[:, j * LANES:(j + 1) * LANES]

    @pl.when(i == 0)
    def _():
        cnt_ref[...] = jnp.zeros_like(cnt_ref)

    cnt_ref[...] += part


def _ranks_kernel(n_tiles_pad, idx_ref, cnt_ref, dest_ref, meta_ref, tri_ref, start_ref, carry_ref):
    i = pl.program_id(0)
    tm = TM_ROUTE
    erow = lax.broadcasted_iota(I32, (N_EXPERTS, LANES), 0)
    elane = lax.broadcasted_iota(I32, (N_EXPERTS, LANES), 1)

    @pl.when(i == 0)
    def _():
        cnt = jnp.sum(cnt_ref[...], axis=1, keepdims=True)
        cnt_i = jnp.broadcast_to(cnt, (N_EXPERTS, LANES)).astype(I32)
        padded = lax.shift_left(lax.shift_right_logical(cnt_i + (TM_MOE - 1), _log2(TM_MOE)), _log2(TM_MOE))
        r = lax.broadcasted_iota(I32, (N_EXPERTS, N_EXPERTS), 0)
        c = lax.broadcasted_iota(I32, (N_EXPERTS, N_EXPERTS), 1)
        low = (c <= r).astype(F32)
        pad_end = lax.dot_general(low, padded.astype(F32), (((1,), (0,)), ((), ())),
                                  precision=lax.Precision.HIGHEST, preferred_element_type=F32)
        pad_start = pad_end - padded.astype(F32)
        start_ref[...] = pad_start
        carry_ref[...] = jnp.zeros_like(carry_ref)
        a = lax.broadcasted_iota(I32, (tm, tm), 0)
        bcol = lax.broadcasted_iota(I32, (tm, tm), 1)
        tri_ref[...] = (a < bcol).astype(BF16)
        ntp = n_tiles_pad
        tstart = (lax.broadcasted_iota(I32, (N_EXPERTS, ntp), 1) * TM_MOE).astype(F32)
        pe = jnp.concatenate([pad_end] * (ntp // LANES), axis=1)
        texp = jnp.sum((pe <= tstart).astype(F32), axis=0, keepdims=True)
        texp = jnp.minimum(texp, float(N_EXPERTS - 1)).astype(I32)
        total = jnp.max(pad_end, axis=0, keepdims=True)
        n_used = lax.shift_right_logical(total.astype(I32), _log2(TM_MOE))
        zstart = jnp.sum(jnp.where(erow == elane, pad_start + cnt_i.astype(F32), 0.0),
                         axis=0, keepdims=True).astype(I32)
        zlen = jnp.sum(jnp.where(erow == elane, (padded - cnt_i).astype(F32), 0.0),
                       axis=0, keepdims=True).astype(I32)
        etiles = lax.shift_right_logical(
            jnp.sum(jnp.where(erow == elane, padded.astype(F32), 0.0), axis=0, keepdims=True).astype(I32),
            _log2(TM_MOE))
        lanes_pad = jnp.zeros((1, ntp - LANES), I32)
        meta_ref[...] = jnp.concatenate(
            [texp,
             jnp.concatenate([n_used, lanes_pad], axis=1),
             jnp.concatenate([zstart, lanes_pad], axis=1),
             jnp.concatenate([zlen, lanes_pad], axis=1),
             jnp.concatenate([etiles, lanes_pad], axis=1),
             jnp.zeros((SUBLANES - 5, ntp), I32)], axis=0)

    idx = idx_ref[...]
    erow_t = lax.broadcasted_iota(I32, (N_EXPERTS, tm), 0)
    hits = [erow_t == idx[k:k + 1, :] for k in range(TOP_K)]
    multi = hits[0].astype(F32)
    for k in range(1, TOP_K):
        multi = multi + hits[k].astype(F32)
    prefix = _dot(multi.astype(BF16), tri_ref[...])
    base = jnp.concatenate([carry_ref[...] + start_ref[...]] * (tm // LANES), axis=1)
    tot = prefix + base
    dest_ref[...] = jnp.concatenate(
        [jnp.sum(jnp.where(hits[k], tot, 0.0), axis=0, keepdims=True) for k in range(TOP_K)],
        axis=0).astype(I32)
    carry_ref[...] += jnp.broadcast_to(jnp.sum(multi, axis=1, keepdims=True), (N_EXPERTS, LANES))


def _ranks(idxT, cnt, n_tiles_pad):
    T = idxT.shape[1]
    nt = T // TM_ROUTE
    full = lambda i: (0, 0)
    return pl.pallas_call(
        functools.partial(_ranks_kernel, n_tiles_pad),
        grid=(nt,),
        in_specs=[
            pl.BlockSpec((TOP_K, TM_ROUTE), lambda i: (0, i)),
            pl.BlockSpec((N_EXPERTS, LANES), full),
        ],
        out_specs=[
            pl.BlockSpec((TOP_K, TM_ROUTE), lambda i: (0, i)),
            pl.BlockSpec((SUBLANES, n_tiles_pad), full),
        ],
        out_shape=[
            jax.ShapeDtypeStruct((TOP_K, T), I32),
            jax.ShapeDtypeStruct((SUBLANES, n_tiles_pad), I32),
        ],
        scratch_shapes=[
            pltpu.VMEM((TM_ROUTE, TM_ROUTE), BF16),
            pltpu.VMEM((N_EXPERTS, LANES), F32),
            pltpu.VMEM((N_EXPERTS, LANES), F32),
        ],
        compiler_params=_cparams(("arbitrary",)),
        name="ranks",
    )(idxT, cnt)


RT = D_MODEL // LANES
DMA_UNROLL = 8
SCATTER_PARTS = 4
COMBINE_PARTS = 4


def _row_copy(src_ref, s, dst_ref, d, sem):
    return pltpu.make_async_copy(src_ref.at[pl.ds(pl.multiple_of(s * RT, RT), RT)],
                                 dst_ref.at[pl.ds(pl.multiple_of(d * RT, RT), RT)], sem)


def _to_row_tiled(dst_ref, val, rows):
    for s in range(RT):
        dst_ref[pl.ds(s, rows, stride=RT), :] = val[:, s * LANES:(s + 1) * LANES]


def _from_row_tiled(src_ref, rows):
    return [src_ref[pl.ds(s, rows, stride=RT), :] for s in range(RT)]


def _scatter_kernel(meta_ref, x_ref, dest_ref, xs_ref, stage_ref, zero_ref, sem):
    i = pl.program_id(0)
    tm = TM_ROUTE

    @pl.when(i == 0)
    def _():
        zero_ref[...] = jnp.zeros_like(zero_ref)

        def zcopy(row, nrows):
            off = pl.multiple_of(row * RT, RT)
            return pltpu.make_async_copy(zero_ref.at[pl.ds(0, nrows * RT)],
                                         xs_ref.at[pl.ds(off, nrows * RT)], sem)

        def pad_fill(wait):
            def body(e, c):
                row = meta_ref[2, e]
                plen = meta_ref[3, e]
                for bit in reversed(range(_log2(TM_MOE))):
                    size = 1 << bit
                    has = (plen & size) != 0

                    @pl.when(has)
                    def _():
                        cp = zcopy(row, size)
                        cp.wait() if wait else cp.start()

                    row = row + jnp.where(has, size, 0)
                return c
            lax.fori_loop(0, N_EXPERTS, body, 0)

        def tail_fill(wait):
            def body(j, c):
                cp = zcopy(j * TM_MOE, TM_MOE)
                cp.wait() if wait else cp.start()
                return c
            lax.fori_loop(meta_ref[1, 0], xs_ref.shape[0] // (TM_MOE * RT), body, 0)

        pad_fill(False)
        tail_fill(False)
        pad_fill(True)
        tail_fill(True)

    def start(t, c):
        for k in range(TOP_K):
            _row_copy(stage_ref, t, xs_ref, dest_ref[k, t], sem).start(priority=k % 2)
        return c

    part = tm // SCATTER_PARTS
    for p in range(SCATTER_PARTS):
        for s in range(RT):
            stage_ref[pl.ds(p * part * RT + s, part, stride=RT), :] = (
                x_ref[p * part:(p + 1) * part, s * LANES:(s + 1) * LANES])
        lax.fori_loop(p * part, (p + 1) * part, start, 0, unroll=DMA_UNROLL)
    for k in range(TOP_K):
        pltpu.make_async_copy(stage_ref, xs_ref.at[pl.ds(0, tm * RT)], sem).wait()


def _scatter(meta, x1, destT, n_rows_alloc):
    T = x1.shape[0]
    nt = T // TM_ROUTE
    return pl.pallas_call(
        _scatter_kernel,
        grid_spec=pltpu.PrefetchScalarGridSpec(
            num_scalar_prefetch=1,
            grid=(nt,),
            in_specs=[
                pl.BlockSpec((TM_ROUTE, D_MODEL), lambda i, m: (i, 0)),
                pl.BlockSpec((TOP_K, TM_ROUTE), lambda i, m: (0, i), memory_space=pltpu.SMEM),
            ],
            out_specs=pl.BlockSpec(memory_space=pl.ANY),
            scratch_shapes=[
                pltpu.VMEM((TM_ROUTE * RT, LANES), F32),
                pltpu.VMEM((TM_MOE * RT, LANES), F32),
                pltpu.SemaphoreType.DMA(()),
            ],
        ),
        out_shape=jax.ShapeDtypeStruct((n_rows_alloc * RT, LANES), F32),
        compiler_params=_cparams(("arbitrary",)),
        name="scatter",
    )(meta, x1, destT)


def _gmm_kernel(layer, meta_ref, xs_ref, wgu_hbm, bgu_ref, wd_hbm, bd_ref, ys_ref,
                wgu_f, wd_f, wgu_b, wd_b, lhs_ref, act_ref, grp_ref, sems):
    j = pl.program_id(0)
    n_used = meta_ref[1, 0]
    e_now = meta_ref[0, j]
    e_prev = meta_ref[0, jnp.maximum(j - 1, 0)]
    used = j < n_used

    def weight_copies(e, slot):
        return (pltpu.make_async_copy(wgu_hbm.at[layer, e], wgu_f.at[slot], sems.at[0, slot]),
                pltpu.make_async_copy(wd_hbm.at[layer, e], wd_f.at[slot], sems.at[1, slot]))

    @pl.when(used & (j == 0))
    def _():
        grp_ref[0] = 0
        for cp in weight_copies(e_now, 0):
            cp.start()

    @pl.when(used & ((j == 0) | (e_now != e_prev)))
    def _():
        slot = grp_ref[0]
        j_next = j + meta_ref[4, e_now]

        @pl.when(j_next < n_used)
        def _():
            for cp in weight_copies(meta_ref[0, jnp.minimum(j_next, meta_ref.shape[1] - 1)], 1 - slot):
                cp.start()

        for cp in weight_copies(e_now, slot):
            cp.wait()
        wgu_b[...] = wgu_f[slot].astype(BF16)
        wd_b[...] = wd_f[slot].astype(BF16)
        grp_ref[0] = 1 - slot

    @pl.when(used)
    def _():
        for s, piece in enumerate(_from_row_tiled(xs_ref, TM_MOE)):
            lhs_ref[:, s * LANES:(s + 1) * LANES] = piece.astype(BF16)
        x = lhs_ref[...]
        nc = 512
        for c in range(D_FF // nc):
            gate = _dot(x, wgu_b[:, c * nc:(c + 1) * nc]) + bgu_ref[0, 0, :, c * nc:(c + 1) * nc]
            up = (_dot(x, wgu_b[:, D_FF + c * nc:D_FF + (c + 1) * nc])
                  + bgu_ref[0, 0, :, D_FF + c * nc:D_FF + (c + 1) * nc])
            gate = jnp.minimum(gate, SWIGLU_LIMIT)
            up = jnp.clip(up, -SWIGLU_LIMIT, SWIGLU_LIMIT)
            act = gate * jax.nn.sigmoid(SWIGLU_ALPHA * gate) * (up + 1.0)
            act_ref[:, c * nc:(c + 1) * nc] = act.astype(BF16)
        _to_row_tiled(ys_ref, _dot(act_ref[...], wd_b[...]) + bd_ref[0, 0], TM_MOE)

    @pl.when(jnp.logical_not(used))
    def _():
        ys_ref[...] = jnp.zeros_like(ys_ref)


def _gmm(meta, xs, w_gu, b_gu, w_down, b_down, n_tiles, layer):
    last = lambda m: jnp.maximum(m[1, 0] - 1, 0)
    return pl.pallas_call(
        functools.partial(_gmm_kernel, layer),
        grid_spec=pltpu.PrefetchScalarGridSpec(
            num_scalar_prefetch=1,
            grid=(n_tiles,),
            in_specs=[
                pl.BlockSpec((TM_MOE * RT, LANES), lambda j, m: (jnp.minimum(j, last(m)), 0)),
                pl.BlockSpec(memory_space=pl.ANY),
                pl.BlockSpec((1, 1, 1, 2 * D_FF), lambda j, m: (layer, m[0, j], 0, 0)),
                pl.BlockSpec(memory_space=pl.ANY),
                pl.BlockSpec((1, 1, 1, D_MODEL), lambda j, m: (layer, m[0, j], 0, 0)),
            ],
            out_specs=pl.BlockSpec((TM_MOE * RT, LANES), lambda j, m: (j, 0)),
            scratch_shapes=[
                pltpu.VMEM((2, D_MODEL, 2 * D_FF), F32),
                pltpu.VMEM((2, D_FF, D_MODEL), F32),
                pltpu.VMEM((D_MODEL, 2 * D_FF), BF16),
                pltpu.VMEM((D_FF, D_MODEL), BF16),
                pltpu.VMEM((TM_MOE, D_MODEL), BF16),
                pltpu.VMEM((TM_MOE, D_FF), BF16),
                pltpu.SMEM((1,), I32),
                pltpu.SemaphoreType.DMA((2, 2)),
            ],
        ),
        out_shape=jax.ShapeDtypeStruct((n_tiles * TM_MOE * RT, LANES), F32),
        compiler_params=_cparams(("arbitrary",)),
        name="gmm",
    )(meta, xs, w_gu, b_gu, w_down, b_down)


def _combine_kernel(x_ref, dest_ref, gate_ref, ys_ref, g_ref, b_ref, o_ref, ob_ref, buf_ref, sem):
    tm = TM_COMB
    part = tm // COMBINE_PARTS

    for p in range(COMBINE_PARTS):
        def start(t, c, p=p):
            for k in range(TOP_K):
                _row_copy(ys_ref, dest_ref[k, t], buf_ref.at[k], t, sem.at[p]).start(priority=k % 2)
            return c

        lax.fori_loop(p * part, (p + 1) * part, start, 0, unroll=DMA_UNROLL)
    gates = jnp.concatenate(
        [gate_ref[...], jnp.zeros((LANES - SUBLANES, tm), F32)], axis=0).T
    for p in range(COMBINE_PARTS):
        rows = slice(p * part, (p + 1) * part)
        for k in range(TOP_K):
            pltpu.make_async_copy(ys_ref.at[pl.ds(0, part * RT)],
                                  buf_ref.at[k, pl.ds(p * part * RT, part * RT)], sem.at[p]).wait()
        pieces = None
        for k in range(TOP_K):
            gk = gates[rows, k:k + 1]
            scaled = [buf_ref[k, pl.ds(p * part * RT + s, part, stride=RT), :] * gk for s in range(RT)]
            pieces = scaled if pieces is None else [a + r for a, r in zip(pieces, scaled)]
        z = DN_ALPHA * x_ref[rows, :] + jnp.concatenate(pieces, axis=1)
        out = _layernorm_rows(z, g_ref[...], b_ref[...])
        o_ref[rows, :] = out
        ob_ref[rows, :] = out.astype(BF16)


def _combine(x1, destT, gateT, ys, g, b):
    T = x1.shape[0]
    nt = T // TM_COMB
    full = lambda i: (0, 0)
    return pl.pallas_call(
        _combine_kernel,
        grid=(nt,),
        in_specs=[
            pl.BlockSpec((TM_COMB, D_MODEL), lambda i: (i, 0)),
            pl.BlockSpec((TOP_K, TM_COMB), lambda i: (0, i), memory_space=pltpu.SMEM),
            pl.BlockSpec((SUBLANES, TM_COMB), lambda i: (0, i)),
            pl.BlockSpec(memory_space=pl.ANY),
            pl.BlockSpec((1, D_MODEL), full),
            pl.BlockSpec((1, D_MODEL), full),
        ],
        out_specs=[
            pl.BlockSpec((TM_COMB, D_MODEL), lambda i: (i, 0)),
            pl.BlockSpec((TM_COMB, D_MODEL), lambda i: (i, 0)),
        ],
        out_shape=[
            jax.ShapeDtypeStruct((T, D_MODEL), F32),
            jax.ShapeDtypeStruct((T, D_MODEL), BF16),
        ],
        scratch_shapes=[
            pltpu.VMEM((TOP_K, TM_COMB * RT, LANES), F32),
            pltpu.SemaphoreType.DMA((COMBINE_PARTS,)),
        ],
        compiler_params=_cparams(("arbitrary",)),
        name="combine",
    )(x1, destT, gateT, ys, g, b)


def _block_diag(w):
    g, c, _ = w.shape
    out = jnp.zeros((g * c, g * c), w.dtype)
    for i in range(g):
        out = out.at[i * c:(i + 1) * c, i * c:(i + 1) * c].set(w[i])
    return out


def kernel(x, w_in, conv_w, w_pool, pool_scale, idx_kn_g, idx_kn_b, w_out, ln1_g, ln1_b, router_w,
           router_b, w_gu, b_gu, w_down, b_down, ln2_g, ln2_b):
    B, S, D = x.shape
    T = B * S
    depth = w_in.shape[0]
    n_pairs = T * TOP_K
    n_tiles = (n_pairs + N_EXPERTS * (TM_MOE - 1)) // TM_MOE + 1
    n_tiles_pad = ((n_tiles + LANES - 1) // LANES) * LANES
    n_rows_alloc = (n_tiles + 1) * TM_MOE

    xf = x.reshape(T, D)
    xb = xf
    for l in range(depth):
        wl = w_in[l]
        wn = jnp.concatenate(
            [wl[:, 0:OFF_V], wl[:, OFF_KI:OFF_WI], jnp.zeros((D, LANES - IDX_DIM), F32)], axis=1).astype(BF16)
        wt = jnp.concatenate(
            [wl[:, OFF_V:OFF_QI], wl[:, OFF_QI:OFF_KI], wl[:, OFF_WI:D_IN],
             jnp.zeros((D, IDXT_ROWS - IDX_HEADS * IDX_DIM - IDX_HEADS), F32)], axis=1).T.astype(BF16)
        a, q, k, ki, vt, it = _inproj(xb, wn, wt, idx_kn_g[l].reshape(1, -1), idx_kn_b[l].reshape(1, -1))
        y_attn = _attn(it, ki, q, k, vt, B, S)
        x1, idxT, gateT, cnt = _mixout(
            a, y_attn, xf, conv_w[l], _block_diag(w_pool[l]).astype(BF16),
            pool_scale[l].reshape(1, -1), w_out[l].astype(BF16),
            ln1_g[l].reshape(1, -1), ln1_b[l].reshape(1, -1),
            router_w[l].T, router_b[l].reshape(-1, 1), S)
        destT, meta = _ranks(idxT, cnt, n_tiles_pad)
        xs = _scatter(meta, x1, destT, n_rows_alloc)
        ys = _gmm(meta, xs, w_gu, b_gu.reshape(depth, N_EXPERTS, 1, -1), w_down,
                  b_down.reshape(depth, N_EXPERTS, 1, -1), n_tiles, l)
        xf, xb = _combine(x1, destT, gateT, ys, ln2_g[l].reshape(1, -1), ln2_b[l].reshape(1, -1))
    return xf.reshape(B, S, D)
```
